```python
import jax, jax.numpy as jnp
from jax import lax
import numpy as np

D_MODEL = 1024
BATCH = 32
SEQ = 2048
DEPTH = 4

HEAD_DIM = 64
DIL_PATTERNS = ((128, 1), (512, 4), (2048, 16))
DIL_HEADS = 4
N_DIL_SUB = DIL_HEADS * len(DIL_PATTERNS)
DIL_QKV = N_DIL_SUB * HEAD_DIM
DIL_OUT = DIL_HEADS * HEAD_DIM
DIL_BLOCK = 64
WIN_HALF = 128
WIN_BLOCK = 128
WIN_Q = D_MODEL
WIN_Q_HEADS = WIN_Q // HEAD_DIM
WIN_KV_HEADS = 4
WIN_KV = WIN_KV_HEADS * HEAD_DIM
N_BRANCH = 2
SPLITS = (DIL_QKV, 2 * DIL_QKV, 3 * DIL_QKV,
          3 * DIL_QKV + WIN_Q, 3 * DIL_QKV + WIN_Q + WIN_KV, 3 * DIL_QKV + WIN_Q + 2 * WIN_KV)
IN_COLS = 3 * DIL_QKV + WIN_Q + 2 * WIN_KV + N_BRANCH * D_MODEL
N_EXPERTS = 16
CAPACITY_FACTOR = 2
D_EXPERT = 1024
RMS_EPS = 1e-6
NEG_INF = -1e30

kernel_name = "hybrid_dilated_window_ec_moe_encoder"


def rmsnorm(x, g):
    xf = x.astype(jnp.float32)
    y = xf * lax.rsqrt(jnp.mean(xf * xf, axis=-1, keepdims=True) + RMS_EPS)
    return (y * g.astype(jnp.float32)).astype(x.dtype)


def alibi_slopes(n):
    return 2.0 ** (-8.0 * jnp.arange(1, n + 1, dtype=jnp.float32) / n)


def banded_attention(q, k, v, half, blk, slopes, pos_scale, sink=None):
    N, L, H, dh = q.shape
    Hk = k.shape[2]
    G = H // Hk
    nb = -(-L // blk)
    Lp = nb * blk
    pad = Lp - L
    qb = jnp.pad(q, ((0, 0), (0, pad), (0, 0), (0, 0))).reshape(N, nb, blk, Hk, G, dh)

    def key_blocks(t):
        tp = jnp.pad(t, ((0, 0), (blk, pad + blk), (0, 0), (0, 0)))
        return jnp.concatenate([
            tp[:, :Lp].reshape(N, nb, blk, Hk, dh),
            tp[:, blk:blk + Lp].reshape(N, nb, blk, Hk, dh),
            tp[:, 2 * blk:].reshape(N, nb, blk, Hk, dh)], axis=2)

    kb = key_blocks(k)
    vb = key_blocks(v)
    a = jnp.arange(blk)[:, None]
    b = jnp.arange(3 * blk)[None, :]
    rel = b - blk - a
    kpos = (jnp.arange(nb)[:, None, None] - 1) * blk + b[None]
    mask = (jnp.abs(rel) <= half)[None] & (kpos >= 0) & (kpos < L)
    bias = -slopes.astype(jnp.float32).reshape(Hk, G)[:, :, None, None] * \
        (pos_scale * jnp.abs(rel)).astype(jnp.float32)[None, None]
    s = jnp.einsum('nibkgd,nijkd->nikgbj', qb, kb).astype(jnp.float32) * (dh ** -0.5) + bias
    s = jnp.where(mask[None, :, None, None], s, NEG_INF)
    m = jnp.max(s, axis=-1)
    if sink is not None:
        sk = sink.astype(jnp.float32).reshape(Hk, G)[:, :, None]
        m = jnp.maximum(m, sk)
    p = jnp.exp(s - m[..., None])
    denom = jnp.sum(p, axis=-1)
    if sink is not None:
        denom = denom + jnp.exp(sk - m)
    p = p / denom[..., None]
    o = jnp.einsum('nikgbj,nijkd->nibkgd', p.astype(v.dtype), vb)
    o = o.reshape(N, Lp, H, dh)[:, :L]
    lse = jnp.moveaxis(m + jnp.log(denom), -1, 2).reshape(N, Lp, H)[:, :L]
    return o, lse


def dilated_mixture(q, k, v):
    B, S = q.shape[:2]
    slopes = alibi_slopes(N_DIL_SUB)
    outs, lses = [], []
    for g, (w, r) in enumerate(DIL_PATTERNS):
        sl = slice(g * DIL_HEADS, (g + 1) * DIL_HEADS)

        def strided(t):
            return t[:, :, sl].reshape(B, S // r, r, DIL_HEADS, HEAD_DIM).transpose(0, 2, 1, 3, 4) \
                .reshape(B * r, S // r, DIL_HEADS, HEAD_DIM)

        o, lse = banded_attention(strided(q), strided(k), strided(v), w // (2 * r), DIL_BLOCK,
                                  slopes[sl], r)
        outs.append(o.reshape(B, r, S // r, DIL_HEADS, HEAD_DIM).transpose(0, 2, 1, 3, 4)
                    .reshape(B, S, DIL_HEADS, HEAD_DIM))
        lses.append(lse.reshape(B, r, S // r, DIL_HEADS).transpose(0, 2, 1, 3).reshape(B, S, DIL_HEADS))
    wts = jax.nn.softmax(jnp.stack(lses, axis=0), axis=0)
    return jnp.sum(wts[..., None].astype(q.dtype) * jnp.stack(outs, axis=0), axis=0)


def hybrid_mixer(h, w_in, w_branch_a, w_branch_b, b_gate, sink_logit, w_out):
    B, S, D = h.shape
    proj = jnp.einsum('bsd,dc->bsc', h, w_in)
    qa, ka, va, qw, kw, vw, gates = jnp.split(proj, SPLITS, axis=-1)
    oa = dilated_mixture(qa.reshape(B, S, N_DIL_SUB, HEAD_DIM),
                         ka.reshape(B, S, N_DIL_SUB, HEAD_DIM),
                         va.reshape(B, S, N_DIL_SUB, HEAD_DIM))
    ya = jnp.einsum('bsc,cd->bsd', oa.reshape(B, S, DIL_OUT), w_branch_a)
    ob, _ = banded_attention(qw.reshape(B, S, WIN_Q_HEADS, HEAD_DIM),
                             kw.reshape(B, S, WIN_KV_HEADS, HEAD_DIM),
                             vw.reshape(B, S, WIN_KV_HEADS, HEAD_DIM),
                             WIN_HALF, WIN_BLOCK, alibi_slopes(WIN_Q_HEADS), 1, sink_logit)
    yb = jnp.einsum('bsc,cd->bsd', ob.reshape(B, S, WIN_Q), w_branch_b)
    g = jax.nn.sigmoid((gates + b_gate).astype(jnp.float32)).astype(h.dtype).reshape(B, S, N_BRANCH, D)
    merged = g[:, :, 0] * ya + g[:, :, 1] * yb
    return jnp.einsum('bsd,de->bse', merged, w_out)


def expert_choice_ffn(h, w_router, w_gate, w_up, w_down):
    B, S, D = h.shape
    cap = CAPACITY_FACTOR * S // N_EXPERTS
    aff = jax.nn.softmax(jnp.einsum('bsd,de->bse', h, w_router).astype(jnp.float32), axis=-1)
    gate, idx = lax.top_k(jnp.swapaxes(aff, 1, 2), cap)
    bidx = jnp.arange(B)[:, None, None]
    xe = h[bidx, idx]
    a = jnp.einsum('becd,edf->becf', xe, w_gate)
    u = jnp.einsum('becd,edf->becf', xe, w_up)
    ye = jnp.einsum('becf,efd->becd', jax.nn.silu(a) * u, w_down)
    return jnp.zeros_like(h).at[bidx, idx].add(gate[..., None].astype(h.dtype) * ye)


def setup_inputs(seed: int = 0) -> dict:
    key = jax.random.key(seed)
    ks = jax.random.split(key, 14)
    f = jnp.float32
    return {
        "x": jax.random.normal(ks[0], (BATCH, SEQ, D_MODEL), f),
        "norm_mix": 1.0 + 0.02 * jax.random.normal(ks[1], (DEPTH, D_MODEL), f),
        "w_in": jax.random.normal(ks[2], (DEPTH, D_MODEL, IN_COLS), f) * D_MODEL ** -0.5,
        "w_branch_a": jax.random.normal(ks[3], (DEPTH, DIL_OUT, D_MODEL), f) * DIL_OUT ** -0.5,
        "w_branch_b": jax.random.normal(ks[4], (DEPTH, WIN_Q, D_MODEL), f) * WIN_Q ** -0.5,
        "b_gate": 0.1 * jax.random.normal(ks[5], (DEPTH, N_BRANCH * D_MODEL), f),
        "sink_logit": 0.5 * jax.random.normal(ks[6], (DEPTH, WIN_Q_HEADS), f),
        "w_out": jax.random.normal(ks[7], (DEPTH, D_MODEL, D_MODEL), f) * D_MODEL ** -0.5,
        "norm_ffn": 1.0 + 0.02 * jax.random.normal(ks[8], (DEPTH, D_MODEL), f),
        "w_router": jax.random.normal(ks[9], (DEPTH, D_MODEL, N_EXPERTS), f) * D_MODEL ** -0.5,
        "w_expert_gate": jax.random.normal(ks[10], (DEPTH, N_EXPERTS, D_MODEL, D_EXPERT), f) * D_MODEL ** -0.5,
        "w_expert_up": jax.random.normal(ks[11], (DEPTH, N_EXPERTS, D_MODEL, D_EXPERT), f) * D_MODEL ** -0.5,
        "w_expert_down": jax.random.normal(ks[12], (DEPTH, N_EXPERTS, D_EXPERT, D_MODEL), f) * D_EXPERT ** -0.5,
        "norm_final": 1.0 + 0.02 * jax.random.normal(ks[13], (D_MODEL,), f),
    }


def reference(x, norm_mix, w_in, w_branch_a, w_branch_b, b_gate, sink_logit, w_out,
              norm_ffn, w_router, w_expert_gate, w_expert_up, w_expert_down, norm_final):
    h = x
    for l in range(DEPTH):
        h = h + hybrid_mixer(rmsnorm(h, norm_mix[l]), w_in[l], w_branch_a[l], w_branch_b[l],
                             b_gate[l], sink_logit[l], w_out[l])
        h = h + expert_choice_ffn(rmsnorm(h, norm_ffn[l]), w_router[l], w_expert_gate[l],
                                  w_expert_up[l], w_expert_down[l])
    return rmsnorm(h, norm_final)
```

```python
import functools

import numpy as np
import jax
import jax.numpy as jnp
from jax import lax
from jax.experimental import pallas as pl
from jax.experimental.pallas import tpu as pltpu

D_MODEL = 1024
HEAD_DIM = 64
DIL_PATTERNS = ((128, 1), (512, 4), (2048, 16))
DIL_HEADS = 4
N_DIL_SUB = DIL_HEADS * len(DIL_PATTERNS)
DIL_QKV = N_DIL_SUB * HEAD_DIM
DIL_OUT = DIL_HEADS * HEAD_DIM
DIL_HALF = 64
WIN_HALF = 128
WIN_Q = D_MODEL
WIN_Q_HEADS = WIN_Q // HEAD_DIM
WIN_KV_HEADS = 4
WIN_KV = WIN_KV_HEADS * HEAD_DIM
WIN_GROUP = WIN_Q_HEADS // WIN_KV_HEADS
N_EXPERTS = 16
CAPACITY_FACTOR = 2
RMS_EPS = 1e-6
NEG_INF = -1e30

GROUP_COLS = 3 * DIL_OUT
Q_TILE = 128
ROW_TILE = 512
LANES = 128
VMEM_LIMIT = 56 * 1024 * 1024

F32 = jnp.float32
BF16 = jnp.bfloat16


def _alibi_slopes(n):
    return [float(2.0 ** (-8.0 * i / n)) for i in range(1, n + 1)]


def _rms(x, g):
    return x * lax.rsqrt(jnp.mean(x * x, axis=-1, keepdims=True) + RMS_EPS) * g


def _dot(a, b):
    return jnp.dot(a, b, preferred_element_type=F32)


def _dot_nt(a, b):
    return lax.dot_general(a, b, (((1,), (1,)), ((), ())), preferred_element_type=F32)


_PROJ_SPLITS = (GROUP_COLS, GROUP_COLS, GROUP_COLS, WIN_Q, 2 * WIN_KV)
_PROJ_COLS = sum(_PROJ_SPLITS) + 2 * D_MODEL


def _norm_proj_kernel(has_add, *refs):
    if has_add:
        x_ref, add_ref, g_ref, w_ref, bg_ref, h_ref, *outs = refs
        x = x_ref[...] + add_ref[...]
        h_ref[...] = x
    else:
        x_ref, g_ref, w_ref, bg_ref, *outs = refs
        x = x_ref[...]
    gate_ref = outs[-1]
    xn = _rms(x, g_ref[...]).astype(BF16)
    c0 = 0
    for ref, cw in zip(outs[:-1], _PROJ_SPLITS):
        for j in range(0, cw, 512):
            jw = min(512, cw - j)
            ref[:, j:j + jw] = _dot(xn, w_ref[:, c0 + j:c0 + j + jw]).astype(BF16)
        c0 += cw
    for j in range(0, 2 * D_MODEL, 512):
        z = _dot(xn, w_ref[:, c0 + j:c0 + j + 512]) + bg_ref[:, j:j + 512]
        gate_ref[:, j:j + 512] = jax.nn.sigmoid(z).astype(BF16)


def _norm_proj(x, add, g, w, bg):
    m = x.shape[0]
    has_add = add is not None
    row = lambda c: pl.BlockSpec((ROW_TILE, c), lambda i: (i, 0))
    full = lambda a: pl.BlockSpec(a.shape, lambda i: (0, 0))
    ins = [x] + ([add] if has_add else []) + [g, w, bg]
    in_specs = [row(D_MODEL)] * (2 if has_add else 1) + [full(g), full(w), full(bg)]
    out_cols = list(_PROJ_SPLITS) + [2 * D_MODEL]
    out_shape = [jax.ShapeDtypeStruct((m, c), BF16) for c in out_cols]
    out_specs = [row(c) for c in out_cols]
    if has_add:
        out_shape = [jax.ShapeDtypeStruct((m, D_MODEL), F32)] + out_shape
        out_specs = [row(D_MODEL)] + out_specs
    return pl.pallas_call(
        functools.partial(_norm_proj_kernel, has_add),
        grid=(m // ROW_TILE,),
        in_specs=in_specs,
        out_specs=out_specs,
        out_shape=out_shape,
        compiler_params=pltpu.CompilerParams(
            dimension_semantics=("arbitrary",), vmem_limit_bytes=VMEM_LIMIT),
    )(*ins)


def _abs_distance(q0, ks, tq, w):
    col = lax.broadcasted_iota(jnp.int32, (tq, w), 1)
    row = lax.broadcasted_iota(jnp.int32, (tq, w), 0)
    return jnp.abs(col - row + (ks - q0)).astype(F32)


def _attend(q, k, v, bias, sink):
    s = _dot_nt(q, k) + bias
    m = jnp.max(s, axis=-1, keepdims=True)
    if sink is not None:
        m = jnp.maximum(m, sink)
    p = jnp.exp(s - m)
    l = jnp.sum(p, axis=-1, keepdims=True)
    if sink is not None:
        l = l + jnp.exp(sink - m)
    o = _dot(p.astype(BF16), v) / l
    return o, m + jnp.log(l)


def _dilated_kernel(seq, win, slopes, a_ref, o_ref, lse_ref):
    n_tiles = seq // Q_TILE
    pad = (win - Q_TILE) // 2

    def tile(t, carry):
        q0 = pl.multiple_of(t * Q_TILE, Q_TILE)
        ks = pl.multiple_of(jnp.clip(q0 - pad, 0, seq - win), DIL_HALF)
        dist = _abs_distance(q0, ks, Q_TILE, win)
        inside = dist <= float(DIL_HALF)
        for h in range(DIL_HEADS):
            c = h * HEAD_DIM
            bias = jnp.where(inside, dist * (-slopes[h]), NEG_INF)
            q = a_ref[pl.ds(q0, Q_TILE), c:c + HEAD_DIM]
            k = a_ref[pl.ds(ks, win), DIL_OUT + c:DIL_OUT + c + HEAD_DIM]
            v = a_ref[pl.ds(ks, win), 2 * DIL_OUT + c:2 * DIL_OUT + c + HEAD_DIM]
            o, lse = _attend(q, k, v, bias, None)
            o_ref[pl.ds(q0, Q_TILE), c:c + HEAD_DIM] = o.astype(BF16)
            lse_ref[pl.ds(q0, Q_TILE), c:c + HEAD_DIM] = jnp.broadcast_to(lse, (Q_TILE, HEAD_DIM))
        return carry

    if n_tiles == 1:
        tile(0, 0)
    else:
        lax.fori_loop(0, n_tiles, tile, 0)


def _dilated_attention(a, batch, seq_full, group):
    _, r = DIL_PATTERNS[group]
    seq = seq_full // r
    win = min(seq, Q_TILE + 2 * DIL_HALF)
    all_slopes = _alibi_slopes(N_DIL_SUB)
    slopes = [s * r for s in all_slopes[group * DIL_HEADS:(group + 1) * DIL_HEADS]]
    a3 = a.reshape(batch, seq, r * GROUP_COLS)
    o, lse = pl.pallas_call(
        functools.partial(_dilated_kernel, seq, win, slopes),
        grid=(batch, r),
        in_specs=[pl.BlockSpec((None, seq, GROUP_COLS), lambda b, c: (b, 0, c))],
        out_specs=[pl.BlockSpec((None, seq, DIL_OUT), lambda b, c: (b, 0, c)),
                   pl.BlockSpec((None, seq, DIL_OUT), lambda b, c: (b, 0, c))],
        out_shape=[jax.ShapeDtypeStruct((batch, seq, r * DIL_OUT), BF16),
                   jax.ShapeDtypeStruct((batch, seq, r * DIL_OUT), F32)],
        compiler_params=pltpu.CompilerParams(
            dimension_semantics=("arbitrary", "arbitrary"), vmem_limit_bytes=VMEM_LIMIT),
    )(a3)
    return o.reshape(batch * seq_full, DIL_OUT), lse.reshape(batch * seq_full, DIL_OUT)


def _window_kernel(seq, slopes, q_ref, kv_ref, sink_ref, o_ref):
    win = Q_TILE + 2 * WIN_HALF
    n_tiles = seq // Q_TILE

    def tile(t, carry):
        q0 = pl.multiple_of(t * Q_TILE, Q_TILE)
        ks = pl.multiple_of(jnp.clip(q0 - WIN_HALF, 0, seq - win), Q_TILE)
        dist = _abs_distance(q0, ks, Q_TILE, win)
        inside = dist <= float(WIN_HALF)
        for h in range(WIN_Q_HEADS):
            kc = (h // WIN_GROUP) * HEAD_DIM
            bias = jnp.where(inside, dist * (-slopes[h]), NEG_INF)
            q = q_ref[pl.ds(q0, Q_TILE), h * HEAD_DIM:(h + 1) * HEAD_DIM]
            k = kv_ref[pl.ds(ks, win), kc:kc + HEAD_DIM]
            v = kv_ref[pl.ds(ks, win), WIN_KV + kc:WIN_KV + kc + HEAD_DIM]
            o, _ = _attend(q, k, v, bias, sink_ref[0:1, h:h + 1])
            o_ref[pl.ds(q0, Q_TILE), h * HEAD_DIM:(h + 1) * HEAD_DIM] = o.astype(BF16)
        return carry

    lax.fori_loop(0, n_tiles, tile, 0)


def _window_attention(q, kv, sink, batch, seq):
    slopes = _alibi_slopes(WIN_Q_HEADS)
    o = pl.pallas_call(
        functools.partial(_window_kernel, seq, slopes),
        grid=(batch,),
        in_specs=[pl.BlockSpec((None, seq, WIN_Q), lambda b: (b, 0, 0)),
                  pl.BlockSpec((None, seq, 2 * WIN_KV), lambda b: (b, 0, 0)),
                  pl.BlockSpec((1, WIN_Q_HEADS), lambda b: (0, 0))],
        out_specs=pl.BlockSpec((None, seq, WIN_Q), lambda b: (b, 0, 0)),
        out_shape=jax.ShapeDtypeStruct((batch, seq, WIN_Q), BF16),
        compiler_params=pltpu.CompilerParams(
            dimension_semantics=("arbitrary",), vmem_limit_bytes=VMEM_LIMIT),
    )(q.reshape(batch, seq, WIN_Q), kv.reshape(batch, seq, 2 * WIN_KV), sink)
    return o.reshape(batch * seq, WIN_Q)


def _merge_kernel(o0, o1, o2, l0, l1, l2, ow_ref, gate_ref, h_ref, wa_ref, wb_ref, wo_ref, out_ref):
    lses = [l0[...], l1[...], l2[...]]
    mx = jnp.maximum(jnp.maximum(lses[0], lses[1]), lses[2])
    es = [jnp.exp(l - mx) for l in lses]
    num = es[0] * o0[...].astype(F32) + es[1] * o1[...].astype(F32) + es[2] * o2[...].astype(F32)
    oa = (num / (es[0] + es[1] + es[2])).astype(BF16)
    ya = _dot(oa, wa_ref[...])
    yb = _dot(ow_ref[...], wb_ref[...])
    merged = gate_ref[:, :D_MODEL].astype(F32) * ya + gate_ref[:, D_MODEL:].astype(F32) * yb
    out_ref[...] = h_ref[...] + _dot(merged.astype(BF16), wo_ref[...])


def _merge_out(os_, lses, ow, gates, h, wa, wb, wo):
    m = h.shape[0]
    row = lambda c: pl.BlockSpec((ROW_TILE, c), lambda i: (i, 0))
    full = lambda a: pl.BlockSpec(a.shape, lambda i: (0, 0))
    return pl.pallas_call(
        _merge_kernel,
        grid=(m // ROW_TILE,),
        in_specs=[row(DIL_OUT)] * 6 + [row(WIN_Q), row(2 * D_MODEL), row(D_MODEL),
                                        full(wa), full(wb), full(wo)],
        out_specs=row(D_MODEL),
        out_shape=jax.ShapeDtypeStruct((m, D_MODEL), F32),
        compiler_params=pltpu.CompilerParams(
            dimension_semantics=("arbitrary",), vmem_limit_bytes=VMEM_LIMIT),
    )(*os_, *lses, ow, gates, h, wa, wb, wo)


CUM_CHUNK = 256


def _prefix_exclusive(mask_f, tri):
    e, s = mask_f.shape
    carry = jnp.zeros((e, 1), F32)
    parts = []
    for j in range(0, s, CUM_CHUNK):
        blk = mask_f[:, j:j + CUM_CHUNK]
        parts.append(_dot(blk.astype(BF16), tri) + carry)
        carry = carry + jnp.sum(blk, axis=-1, keepdims=True)
    return jnp.concatenate(parts, axis=-1)


def _route_kernel(cap, h_ref, g_ref, whi_ref, wlo_ref, hn_ref, rrow_ref, rcol_ref, gcol_ref):
    seq = h_ref.shape[0]
    hn = _rms(h_ref[...], g_ref[...])
    hn_hi = hn.astype(BF16)
    hn_lo = (hn - hn_hi.astype(F32)).astype(BF16)
    hn_ref[...] = hn_hi
    logits = (_dot_nt(whi_ref[...], hn_hi) + _dot_nt(whi_ref[...], hn_lo)
              + _dot_nt(wlo_ref[...], hn_hi))
    mx = jnp.max(logits, axis=0, keepdims=True)
    ex = jnp.exp(logits - mx)
    aff = ex / jnp.sum(ex, axis=0, keepdims=True)
    bits = pltpu.bitcast(aff, jnp.int32)

    def search(_, c):
        lo, hi = c
        mid = lo + (hi - lo) // 2
        cnt = jnp.sum(jnp.where(bits >= mid, 1.0, 0.0), axis=-1, keepdims=True)
        ok = cnt >= float(cap)
        return jnp.where(ok, mid, lo), jnp.where(ok, hi, mid)

    lo0 = jnp.zeros((N_EXPERTS, 1), jnp.int32)
    hi0 = jnp.full((N_EXPERTS, 1), 0x3F800001, jnp.int32)
    thr, _ = lax.fori_loop(0, 31, search, (lo0, hi0))

    r_i = lax.broadcasted_iota(jnp.int32, (CUM_CHUNK, CUM_CHUNK), 0)
    c_i = lax.broadcasted_iota(jnp.int32, (CUM_CHUNK, CUM_CHUNK), 1)
    tri = jnp.where(r_i < c_i, 1.0, 0.0).astype(BF16)
    gt = jnp.where(bits > thr, 1.0, 0.0)
    eq = jnp.where(bits == thr, 1.0, 0.0)
    need = float(cap) - jnp.sum(gt, axis=-1, keepdims=True)
    sel = gt + eq * jnp.where(_prefix_exclusive(eq, tri) < need, 1.0, 0.0)
    rank = jnp.where(sel > 0.0, _prefix_exclusive(sel, tri), -1.0)
    for e in range(N_EXPERTS):
        rrow_ref[e] = rank[e:e + 1, :]
    pad = jnp.full((LANES - N_EXPERTS, seq), -1.0, F32)
    rcol_ref[...] = jnp.concatenate([rank, pad], axis=0).T
    gcol_ref[...] = jnp.concatenate([aff, pad], axis=0).T


def _route(h, g, w_hi, w_lo, batch, seq, cap):
    return pl.pallas_call(
        functools.partial(_route_kernel, cap),
        grid=(batch,),
        in_specs=[pl.BlockSpec((None, seq, D_MODEL), lambda b: (b, 0, 0)),
                  pl.BlockSpec((1, D_MODEL), lambda b: (0, 0)),
                  pl.BlockSpec((N_EXPERTS, D_MODEL), lambda b: (0, 0)),
                  pl.BlockSpec((N_EXPERTS, D_MODEL), lambda b: (0, 0))],
        out_specs=[pl.BlockSpec((None, seq, D_MODEL), lambda b: (b, 0, 0)),
                   pl.BlockSpec((None, N_EXPERTS, 1, seq), lambda b: (b, 0, 0, 0)),
                   pl.BlockSpec((None, seq, LANES), lambda b: (b, 0, 0)),
                   pl.BlockSpec((None, seq, LANES), lambda b: (b, 0, 0))],
        out_shape=[jax.ShapeDtypeStruct((batch, seq, D_MODEL), BF16),
                   jax.ShapeDtypeStruct((batch, N_EXPERTS, 1, seq), F32),
                   jax.ShapeDtypeStruct((batch, seq, LANES), F32),
                   jax.ShapeDtypeStruct((batch, seq, LANES), F32)],
        compiler_params=pltpu.CompilerParams(
            dimension_semantics=("arbitrary",), vmem_limit_bytes=VMEM_LIMIT),
    )(h.reshape(batch, seq, D_MODEL), g, w_hi, w_lo)


SCATTER_TILE = 512


def _expert_kernel(cap, hn_ref, rrow_ref, rcol_ref, gcol_ref, wg_ref, wu_ref, wd_ref, out_ref):
    e = pl.program_id(1)
    seq = hn_ref.shape[0]

    @pl.when(e == 0)
    def _():
        out_ref[...] = jnp.zeros_like(out_ref)

    slot_rows = lax.broadcasted_iota(jnp.int32, (cap, seq), 0).astype(F32)
    pick = jnp.where(rrow_ref[e] == slot_rows, 1.0, 0.0).astype(BF16)
    xe = _dot(pick, hn_ref[...]).astype(BF16)
    a = _dot(xe, wg_ref[...])
    u = _dot(xe, wu_ref[...])
    y = _dot((jax.nn.silu(a) * u).astype(BF16), wd_ref[...]).astype(BF16)

    lane = lax.broadcasted_iota(jnp.int32, (SCATTER_TILE, LANES), 1)
    slot_cols = lax.broadcasted_iota(jnp.int32, (SCATTER_TILE, cap), 1).astype(F32)
    for j in range(0, seq, SCATTER_TILE):
        mine = lane == e
        rank = jnp.sum(jnp.where(mine, rcol_ref[j:j + SCATTER_TILE, :], 0.0), axis=-1, keepdims=True)
        gate = jnp.sum(jnp.where(mine, gcol_ref[j:j + SCATTER_TILE, :], 0.0), axis=-1, keepdims=True)
        put = jnp.where(rank == slot_cols, 1.0, 0.0).astype(BF16)
        out_ref[j:j + SCATTER_TILE, :] += gate * _dot(put, y)


def _experts(hn, rrow, rcol, gcol, wg, wu, wd, cap):
    batch, seq, _ = hn.shape
    d_exp = wg.shape[-1]
    return pl.pallas_call(
        functools.partial(_expert_kernel, cap),
        grid=(batch, N_EXPERTS),
        in_specs=[pl.BlockSpec((None, seq, D_MODEL), lambda b, e: (b, 0, 0)),
                  pl.BlockSpec((None, N_EXPERTS, 1, seq), lambda b, e: (b, 0, 0, 0)),
                  pl.BlockSpec((None, seq, LANES), lambda b, e: (b, 0, 0)),
                  pl.BlockSpec((None, seq, LANES), lambda b, e: (b, 0, 0)),
                  pl.BlockSpec((None, D_MODEL, d_exp), lambda b, e: (e, 0, 0)),
                  pl.BlockSpec((None, D_MODEL, d_exp), lambda b, e: (e, 0, 0)),
                  pl.BlockSpec((None, d_exp, D_MODEL), lambda b, e: (e, 0, 0))],
        out_specs=pl.BlockSpec((None, seq, D_MODEL), lambda b, e: (b, 0, 0)),
        out_shape=jax.ShapeDtypeStruct((batch, seq, D_MODEL), F32),
        compiler_params=pltpu.CompilerParams(
            dimension_semantics=("arbitrary", "arbitrary"), vmem_limit_bytes=VMEM_LIMIT),
    )(hn, rrow, rcol, gcol, wg, wu, wd)


def _final_kernel(x_ref, add_ref, g_ref, o_ref):
    o_ref[...] = _rms(x_ref[...] + add_ref[...], g_ref[...])


def _final_norm(x, add, g):
    m = x.shape[0]
    row = pl.BlockSpec((ROW_TILE, D_MODEL), lambda i: (i, 0))
    return pl.pallas_call(
        _final_kernel,
        grid=(m // ROW_TILE,),
        in_specs=[row, row, pl.BlockSpec((1, D_MODEL), lambda i: (0, 0))],
        out_specs=row,
        out_shape=jax.ShapeDtypeStruct((m, D_MODEL), F32),
        compiler_params=pltpu.CompilerParams(dimension_semantics=("arbitrary",)),
    )(x, add, g)


def _arrange_w_in(w):
    scale = HEAD_DIM ** -0.5
    qa, ka, va = w[:, :DIL_QKV], w[:, DIL_QKV:2 * DIL_QKV], w[:, 2 * DIL_QKV:3 * DIL_QKV]
    rest = w[:, 3 * DIL_QKV:]
    parts = []
    for g in range(len(DIL_PATTERNS)):
        sl = slice(g * DIL_OUT, (g + 1) * DIL_OUT)
        parts += [qa[:, sl] * scale, ka[:, sl], va[:, sl]]
    parts += [rest[:, :WIN_Q] * scale, rest[:, WIN_Q:]]
    return jnp.concatenate(parts, axis=1).astype(BF16)


def kernel(x, norm_mix, w_in, w_branch_a, w_branch_b, b_gate, sink_logit, w_out, norm_ffn, w_router,
           w_expert_gate, w_expert_up, w_expert_down, norm_final):
    batch, seq, d = x.shape
    depth = w_in.shape[0]
    cap = CAPACITY_FACTOR * seq // N_EXPERTS
    m = batch * seq
    h = x.reshape(m, d)
    moe = None
    for l in range(depth):
        res = _norm_proj(h, moe, norm_mix[l][None, :], _arrange_w_in(w_in[l]), b_gate[l][None, :])
        if moe is not None:
            h, res = res[0], res[1:]
        a0, a1, a2, qw, kvw, gates = res
        os_, lses = [], []
        for g, a in enumerate((a0, a1, a2)):
            o, lse = _dilated_attention(a, batch, seq, g)
            os_.append(o)
            lses.append(lse)
        ow = _window_attention(qw, kvw, sink_logit[l][None, :], batch, seq)
        h = _merge_out(os_, lses, ow, gates, h, w_branch_a[l].astype(BF16),
                       w_branch_b[l].astype(BF16), w_out[l].astype(BF16))
        wr = w_router[l].T
        wr_hi = wr.astype(BF16)
        wr_lo = (wr - wr_hi.astype(F32)).astype(BF16)
        hn, rrow, rcol, gcol = _route(h, norm_ffn[l][None, :], wr_hi, wr_lo, batch, seq, cap)
        moe = _experts(hn, rrow, rcol, gcol, w_expert_gate[l].astype(BF16),
                       w_expert_up[l].astype(BF16), w_expert_down[l].astype(BF16), cap)
        moe = moe.reshape(m, d)
    return _final_norm(h, moe, norm_final[None, :]).reshape(batch, seq, d)
```

```python
import functools

import jax
import jax.numpy as jnp
from jax import lax
from jax.experimental import pallas as pl
from jax.experimental.pallas import tpu as pltpu

D_MODEL = 1024
HEAD_DIM = 64
DIL_PATTERNS = ((128, 1), (512, 4), (2048, 16))
DIL_HEADS = 4
N_DIL_SUB = DIL_HEADS * len(DIL_PATTERNS)
DIL_QKV = N_DIL_SUB * HEAD_DIM
DIL_OUT = DIL_HEADS * HEAD_DIM
DIL_HALF = 64
WIN_HALF = 128
WIN_Q = D_MODEL
WIN_Q_HEADS = WIN_Q // HEAD_DIM
WIN_KV_HEADS = 4
WIN_KV = WIN_KV_HEADS * HEAD_DIM
N_EXPERTS = 16
CAPACITY_FACTOR = 2
RMS_EPS = 1e-6
NEG_INF = -1e30
LOG2E = 1.4426950408889634

LANES = 128
GROUP_COLS = 3 * DIL_OUT
DIL_PAIRS = DIL_OUT // LANES
WIN_DUP = WIN_KV_HEADS * LANES
Q_TILE = 128
ROW_TILE = 512
VMEM_LIMIT = 56 * 1024 * 1024

F32 = jnp.float32
BF16 = jnp.bfloat16


def _alibi_slopes(n):
    return [float(2.0 ** (-8.0 * i / n)) for i in range(1, n + 1)]


def _rms(x, g):
    return x * lax.rsqrt(jnp.mean(x * x, axis=-1, keepdims=True) + RMS_EPS) * g


def _dot(a, b):
    return jnp.dot(a, b, preferred_element_type=F32)


def _dot_nt(a, b):
    return lax.dot_general(a, b, (((1,), (1,)), ((), ())), preferred_element_type=F32)


def _left_lanes():
    return lax.broadcasted_iota(jnp.int32, (1, LANES), 1) < HEAD_DIM


_C_A = (0, GROUP_COLS, 2 * GROUP_COLS)
_C_QW = 3 * GROUP_COLS
_C_KD = _C_QW + WIN_Q
_C_VD = _C_KD + WIN_DUP
_C_GATE = _C_VD + WIN_DUP
_PROJ_COLS = _C_GATE + 2 * D_MODEL
_COL_CHUNK = 512


def _proj_store(ref, xb, w_ref, c0, cw):
    for j in range(0, cw, _COL_CHUNK):
        jw = min(_COL_CHUNK, cw - j)
        ref[:, j:j + jw] = _dot(xb, w_ref[:, c0 + j:c0 + j + jw]).astype(BF16)


def _norm_proj_kernel(has_add, *refs):
    if has_add:
        x_ref, add_ref, g_ref, w_ref, bg_ref, h_ref, *outs = refs
        x = x_ref[...] + add_ref[...]
        h_ref[...] = x
    else:
        x_ref, g_ref, w_ref, bg_ref, *outs = refs
        x = x_ref[...]
    a0_ref, a1_ref, a2_ref, qw_ref, kd_ref, vd_ref, gate_ref, xs_ref = outs
    xn = _rms(x, g_ref[...])
    n_lane_tiles = D_MODEL // LANES
    for j in range(n_lane_tiles):
        xs_ref[j] = xn[:, j * LANES:(j + 1) * LANES]
    xb = xn.astype(BF16)
    _proj_store(a0_ref, xb, w_ref, _C_A[0], GROUP_COLS)
    for grp, ref in ((1, a1_ref), (2, a2_ref)):
        r = DIL_PATTERNS[grp][1]
        n = ROW_TILE // r
        xp = jnp.concatenate(
            [jnp.concatenate([xs_ref[j, pl.ds(c, n, stride=r), :] for j in range(n_lane_tiles)], axis=1)
             for c in range(r)], axis=0).astype(BF16)
        res = _dot(xp, w_ref[:, _C_A[grp]:_C_A[grp] + GROUP_COLS])
        for c in range(r):
            ref[:, c * GROUP_COLS:(c + 1) * GROUP_COLS] = res[c * n:(c + 1) * n].astype(BF16)
    _proj_store(qw_ref, xb, w_ref, _C_QW, WIN_Q)
    _proj_store(kd_ref, xb, w_ref, _C_KD, WIN_DUP)
    _proj_store(vd_ref, xb, w_ref, _C_VD, WIN_DUP)
    for j in range(0, 2 * D_MODEL, _COL_CHUNK):
        z = _dot(xb, w_ref[:, _C_GATE + j:_C_GATE + j + _COL_CHUNK]) + bg_ref[:, j:j + _COL_CHUNK]
        gate_ref[:, j:j + _COL_CHUNK] = jax.nn.sigmoid(z).astype(BF16)


def _norm_proj(x, add, g, w, bg):
    m = x.shape[0]
    has_add = add is not None
    row = lambda c: pl.BlockSpec((ROW_TILE, c), lambda i: (i, 0))
    full = lambda a: pl.BlockSpec(a.shape, lambda i: (0, 0), pipeline_mode=pl.Buffered(1))
    ins = [x] + ([add] if has_add else []) + [g, w, bg]
    in_specs = [row(D_MODEL)] * (2 if has_add else 1) + [full(g), full(w), full(bg)]
    out_shape, out_specs = [], []
    if has_add:
        out_shape.append(jax.ShapeDtypeStruct((m, D_MODEL), F32))
        out_specs.append(row(D_MODEL))
    for _, r in DIL_PATTERNS:
        out_shape.append(jax.ShapeDtypeStruct((m // r, r * GROUP_COLS), BF16))
        out_specs.append(pl.BlockSpec((ROW_TILE // r, r * GROUP_COLS), lambda i: (i, 0)))
    for c in (WIN_Q, WIN_DUP, WIN_DUP, 2 * D_MODEL):
        out_shape.append(jax.ShapeDtypeStruct((m, c), BF16))
        out_specs.append(row(c))
    return pl.pallas_call(
        functools.partial(_norm_proj_kernel, has_add),
        grid=(m // ROW_TILE,),
        in_specs=in_specs,
        out_specs=out_specs,
        out_shape=out_shape,
        scratch_shapes=[pltpu.VMEM((D_MODEL // LANES, ROW_TILE, LANES), F32)],
        compiler_params=pltpu.CompilerParams(
            dimension_semantics=("arbitrary",), vmem_limit_bytes=VMEM_LIMIT),
        name="norm_proj",
    )(*ins)


def _band_bias(tq, win, off, half, slopes):
    col = lax.broadcasted_iota(jnp.int32, (tq, win), 1)
    row = lax.broadcasted_iota(jnp.int32, (tq, win), 0)
    dist = jnp.abs(col - row + off).astype(F32)
    inside = dist <= float(half)
    return jnp.concatenate([jnp.where(inside, dist * (-s * LOG2E), NEG_INF) for s in slopes], axis=0)


def _ones_blockdiag(win):
    left = lax.broadcasted_iota(jnp.int32, (2 * win, LANES), 1) < HEAD_DIM
    top = jnp.where(lax.broadcasted_iota(jnp.int32, (2 * win, LANES), 0) < win, 1.0, 0.0)
    return jnp.where(left, top, 1.0 - top).astype(BF16)


def _pair_attention(q_tiles, k_win, v_win, bias, ones_bd, sink_col):
    tq = q_tiles[0].shape[0]
    n = len(q_tiles)
    left = _left_lanes()
    zero = jnp.zeros((), BF16)
    rows = []
    for q in q_tiles:
        rows += [jnp.where(left, q, zero), jnp.where(left, zero, q)]
    s = _dot_nt(jnp.concatenate(rows, axis=0), k_win) + bias
    m = jnp.max(s, axis=-1, keepdims=True)
    if sink_col is not None:
        m = jnp.maximum(m, sink_col)
    p = jnp.exp2(s - m).astype(BF16)
    pc = jnp.concatenate(
        [jnp.concatenate([p[2 * i * tq:(2 * i + 1) * tq], p[(2 * i + 1) * tq:(2 * i + 2) * tq]], axis=1)
         for i in range(n)], axis=0)
    vbd = jnp.concatenate([jnp.where(left, v_win, zero), jnp.where(left, zero, v_win)], axis=0)
    on = _dot(pc, jnp.concatenate([vbd, ones_bd], axis=1))
    num, den = on[:, :LANES], on[:, LANES:]
    if sink_col is not None:
        e = jnp.exp2(sink_col - m)
        den = den + jnp.concatenate(
            [jnp.where(left, e[2 * i * tq:(2 * i + 1) * tq], e[(2 * i + 1) * tq:(2 * i + 2) * tq])
             for i in range(n)], axis=0)
    return num, den, m


def _tile_variant(t, n_tiles):
    return jnp.where(t == 0, 0, jnp.where(t == n_tiles - 1, 2, 1))


def _dil_geometry(seq_full, grp):
    r = DIL_PATTERNS[grp][1]
    seq = seq_full // r
    win = min(seq, Q_TILE + 2 * DIL_HALF)
    n_tiles = seq // Q_TILE
    pad = (win - Q_TILE) // 2
    offs = [0] if n_tiles == 1 else [0, -pad, -2 * pad]
    return r, seq, win, n_tiles, pad, offs


def _dilated_kernel(seq_full, a0_ref, a1_ref, a2_ref, out_ref, o_nat, l_nat, b0, b1, b2, ones_ref, ones2_ref):
    a_refs = (a0_ref, a1_ref, a2_ref)
    bias_refs = (b0, b1, b2)
    slopes = _alibi_slopes(N_DIL_SUB)

    @pl.when(pl.program_id(0) == 0)
    def _():
        for grp in range(3):
            r, seq, win, n_tiles, pad, offs = _dil_geometry(seq_full, grp)
            for v, off in enumerate(offs):
                for pr in range(DIL_PAIRS):
                    hs = slopes[grp * DIL_HEADS + 2 * pr:grp * DIL_HEADS + 2 * pr + 2]
                    bias_refs[grp][v, pr] = _band_bias(Q_TILE, win, off, DIL_HALF, [s * r for s in hs])
        ones_ref[...] = _ones_blockdiag(_dil_geometry(seq_full, 0)[2])
        ones2_ref[...] = _ones_blockdiag(_dil_geometry(seq_full, 2)[2])

    left = _left_lanes()
    for grp in range(3):
        r, seq, win, n_tiles, pad, offs = _dil_geometry(seq_full, grp)
        a_ref = a_refs[grp]
        ones_bd_ref = ones2_ref if grp == 2 else ones_ref
        for c in range(r):
            base = c * GROUP_COLS

            def tile(t, carry, c=c, base=base, grp=grp, r=r, seq=seq, win=win, n_tiles=n_tiles, pad=pad,
                     a_ref=a_ref, ones_bd_ref=ones_bd_ref):
                if n_tiles == 1:
                    q0, ks, var = 0, 0, 0
                else:
                    q0 = pl.multiple_of(t * Q_TILE, Q_TILE)
                    ks = pl.multiple_of(jnp.clip(q0 - pad, 0, seq - win), DIL_HALF)
                    var = _tile_variant(t, n_tiles)
                for pr in range(DIL_PAIRS):
                    lo = base + pr * LANES
                    q = a_ref[pl.ds(q0, Q_TILE), lo:lo + LANES]
                    k = a_ref[pl.ds(ks, win), DIL_OUT + lo:DIL_OUT + lo + LANES]
                    v = a_ref[pl.ds(ks, win), 2 * DIL_OUT + lo:2 * DIL_OUT + lo + LANES]
                    num, den, m = _pair_attention([q], k, v, bias_refs[grp][var, pr], ones_bd_ref[...], None)
                    o = num / den
                    lse = jnp.where(left, m[:Q_TILE], m[Q_TILE:]) + jnp.log2(den)
                    if r == 1:
                        rows = pl.ds(q0, Q_TILE)
                    else:
                        rows = pl.ds(c + r * q0, Q_TILE, stride=r)
                    o_nat[grp, pr, rows, :] = o
                    l_nat[grp, pr, rows, :] = lse
                return carry

            if n_tiles == 1:
                tile(0, 0)
            else:
                lax.fori_loop(0, n_tiles, tile, 0)

    def combine(i, carry):
        rows = pl.ds(pl.multiple_of(i * ROW_TILE, ROW_TILE), ROW_TILE)
        for pr in range(DIL_PAIRS):
            ls = [l_nat[g, pr, rows, :] for g in range(3)]
            mx = jnp.maximum(jnp.maximum(ls[0], ls[1]), ls[2])
            es = [jnp.exp2(l - mx) for l in ls]
            num = es[0] * o_nat[0, pr, rows, :] + es[1] * o_nat[1, pr, rows, :] + es[2] * o_nat[2, pr, rows, :]
            out_ref[rows, pr * LANES:(pr + 1) * LANES] = (num / (es[0] + es[1] + es[2])).astype(BF16)
        return carry

    lax.fori_loop(0, seq_full // ROW_TILE, combine, 0)


def _dilated_attention(a0, a1, a2, batch, seq_full):
    views, in_specs, bias_shapes = [], [], []
    for grp, a in enumerate((a0, a1, a2)):
        r, seq, win, n_tiles, pad, offs = _dil_geometry(seq_full, grp)
        views.append(a.reshape(batch, seq, r * GROUP_COLS))
        in_specs.append(pl.BlockSpec((None, seq, r * GROUP_COLS), lambda b: (b, 0, 0)))
        bias_shapes.append(pltpu.VMEM((len(offs), DIL_PAIRS, 2 * Q_TILE, win), F32))
    win0 = _dil_geometry(seq_full, 0)[2]
    win2 = _dil_geometry(seq_full, 2)[2]
    return pl.pallas_call(
        functools.partial(_dilated_kernel, seq_full),
        grid=(batch,),
        in_specs=in_specs,
        out_specs=pl.BlockSpec((None, seq_full, DIL_OUT), lambda b: (b, 0, 0)),
        out_shape=jax.ShapeDtypeStruct((batch, seq_full, DIL_OUT), BF16),
        scratch_shapes=[pltpu.VMEM((3, DIL_PAIRS, seq_full, LANES), F32),
                        pltpu.VMEM((3, DIL_PAIRS, seq_full, LANES), F32)]
        + bias_shapes + [pltpu.VMEM((2 * win0, LANES), BF16), pltpu.VMEM((2 * win2, LANES), BF16)],
        compiler_params=pltpu.CompilerParams(
            dimension_semantics=("arbitrary",), vmem_limit_bytes=VMEM_LIMIT),
        name="dilated",
    )(*views)


WIN_WINDOW = Q_TILE + 2 * WIN_HALF


def _window_kernel(seq, q_ref, kd_ref, vd_ref, sink_ref, o_ref, bias_ref, ones_ref):
    n_tiles = seq // Q_TILE
    slopes = _alibi_slopes(WIN_Q_HEADS)
    group = WIN_Q_HEADS // WIN_KV_HEADS

    @pl.when(pl.program_id(0) == 0)
    def _():
        for v, off in enumerate((0, -WIN_HALF, -2 * WIN_HALF)):
            for kv in range(WIN_KV_HEADS):
                bias_ref[v, kv] = _band_bias(Q_TILE, WIN_WINDOW, off, WIN_HALF,
                                             slopes[kv * group:(kv + 1) * group])
        ones_ref[...] = _ones_blockdiag(WIN_WINDOW)

    def tile(t, carry):
        q0 = pl.multiple_of(t * Q_TILE, Q_TILE)
        ks = pl.multiple_of(jnp.clip(q0 - WIN_HALF, 0, seq - WIN_WINDOW), Q_TILE)
        var = _tile_variant(t, n_tiles)
        for kv in range(WIN_KV_HEADS):
            qs = [q_ref[pl.ds(q0, Q_TILE), (2 * kv + i) * LANES:(2 * kv + i + 1) * LANES] for i in range(2)]
            k = kd_ref[pl.ds(ks, WIN_WINDOW), kv * LANES:(kv + 1) * LANES]
            v = vd_ref[pl.ds(ks, WIN_WINDOW), kv * LANES:(kv + 1) * LANES]
            sink_col = jnp.concatenate(
                [jnp.broadcast_to(sink_ref[0:1, h:h + 1] * LOG2E, (Q_TILE, 1))
                 for h in range(kv * group, (kv + 1) * group)], axis=0)
            num, den, _ = _pair_attention(qs, k, v, bias_ref[var, kv], ones_ref[...], sink_col)
            o = (num / den).astype(BF16)
            for i in range(2):
                o_ref[pl.ds(q0, Q_TILE), (2 * kv + i) * LANES:(2 * kv + i + 1) * LANES] = \
                    o[i * Q_TILE:(i + 1) * Q_TILE]
        return carry

    lax.fori_loop(0, n_tiles, tile, 0)


def _window_attention(q, kd, vd, sink, batch, seq):
    return pl.pallas_call(
        functools.partial(_window_kernel, seq),
        grid=(batch,),
        in_specs=[pl.BlockSpec((None, seq, WIN_Q), lambda b: (b, 0, 0)),
                  pl.BlockSpec((None, seq, WIN_DUP), lambda b: (b, 0, 0)),
                  pl.BlockSpec((None, seq, WIN_DUP), lambda b: (b, 0, 0)),
                  pl.BlockSpec((1, WIN_Q_HEADS), lambda b: (0, 0))],
        out_specs=pl.BlockSpec((None, seq, WIN_Q), lambda b: (b, 0, 0)),
        out_shape=jax.ShapeDtypeStruct((batch, seq, WIN_Q), BF16),
        scratch_shapes=[pltpu.VMEM((3, WIN_KV_HEADS, 4 * Q_TILE, WIN_WINDOW), F32),
                        pltpu.VMEM((2 * WIN_WINDOW, LANES), BF16)],
        compiler_params=pltpu.CompilerParams(
            dimension_semantics=("arbitrary",), vmem_limit_bytes=VMEM_LIMIT),
        name="window",
    )(q.reshape(batch, seq, WIN_Q), kd.reshape(batch, seq, WIN_DUP), vd.reshape(batch, seq, WIN_DUP), sink)


def _merge_kernel(oa_ref, ow_ref, gate_ref, h_ref, wa_ref, wb_ref, wo_ref, out_ref):
    ya = _dot(oa_ref[...], wa_ref[...])
    yb = _dot(ow_ref[...], wb_ref[...])
    merged = gate_ref[:, :D_MODEL].astype(F32) * ya + gate_ref[:, D_MODEL:].astype(F32) * yb
    out_ref[...] = h_ref[...] + _dot(merged.astype(BF16), wo_ref[...])


def _merge_out(oa, ow, gates, h, wa, wb, wo):
    m = h.shape[0]
    row = lambda c: pl.BlockSpec((ROW_TILE, c), lambda i: (i, 0))
    full = lambda a: pl.BlockSpec(a.shape, lambda i: (0, 0), pipeline_mode=pl.Buffered(1))
    return pl.pallas_call(
        _merge_kernel,
        grid=(m // ROW_TILE,),
        in_specs=[row(DIL_OUT), row(WIN_Q), row(2 * D_MODEL), row(D_MODEL), full(wa), full(wb), full(wo)],
        out_specs=row(D_MODEL),
        out_shape=jax.ShapeDtypeStruct((m, D_MODEL), F32),
        compiler_params=pltpu.CompilerParams(
            dimension_semantics=("arbitrary",), vmem_limit_bytes=VMEM_LIMIT),
        name="merge_out",
    )(oa, ow, gates, h, wa, wb, wo)


CUM_CHUNK = 256


def _prefix_exclusive(mask_f, tri):
    e, s = mask_f.shape
    carry = jnp.zeros((e, 1), F32)
    parts = []
    for j in range(0, s, CUM_CHUNK):
        blk = mask_f[:, j:j + CUM_CHUNK]
        parts.append(_dot(blk.astype(BF16), tri) + carry)
        carry = carry + jnp.sum(blk, axis=-1, keepdims=True)
    return jnp.concatenate(parts, axis=-1)


def _route_kernel(cap, h_ref, g_ref, whi_ref, wlo_ref, hn_ref, rrow_ref, rcol_ref, gcol_ref):
    seq = h_ref.shape[0]
    hn = _rms(h_ref[...], g_ref[...])
    hn_hi = hn.astype(BF16)
    hn_lo = (hn - hn_hi.astype(F32)).astype(BF16)
    hn_ref[...] = hn_hi
    logits = (_dot_nt(whi_ref[...], hn_hi) + _dot_nt(whi_ref[...], hn_lo)
              + _dot_nt(wlo_ref[...], hn_hi))
    mx = jnp.max(logits, axis=0, keepdims=True)
    ex = jnp.exp(logits - mx)
    aff = ex / jnp.sum(ex, axis=0, keepdims=True)
    bits = pltpu.bitcast(aff, jnp.int32)

    def search(_, c):
        lo, hi = c
        mid = lo + ((hi - lo) >> 1)
        cnt = jnp.sum(jnp.where(bits >= mid, 1.0, 0.0), axis=-1, keepdims=True)
        ok = cnt >= float(cap)
        return jnp.where(ok, mid, lo), jnp.where(ok, hi, mid)

    lo0 = jnp.zeros((N_EXPERTS, 1), jnp.int32)
    hi0 = jnp.full((N_EXPERTS, 1), 0x3F800001, jnp.int32)
    thr, _ = lax.fori_loop(0, 31, search, (lo0, hi0))

    r_i = lax.broadcasted_iota(jnp.int32, (CUM_CHUNK, CUM_CHUNK), 0)
    c_i = lax.broadcasted_iota(jnp.int32, (CUM_CHUNK, CUM_CHUNK), 1)
    tri = jnp.where(r_i < c_i, 1.0, 0.0).astype(BF16)
    gt = jnp.where(bits > thr, 1.0, 0.0)
    eq = jnp.where(bits == thr, 1.0, 0.0)
    need = float(cap) - jnp.sum(gt, axis=-1, keepdims=True)
    sel = gt + eq * jnp.where(_prefix_exclusive(eq, tri) < need, 1.0, 0.0)
    rank = jnp.where(sel > 0.0, _prefix_exclusive(sel, tri), -1.0)
    for e in range(N_EXPERTS):
        rrow_ref[e] = rank[e:e + 1, :]
    pad = jnp.full((LANES - N_EXPERTS, seq), -1.0, F32)
    rcol_ref[...] = jnp.concatenate([rank, pad], axis=0).T
    gcol_ref[...] = jnp.concatenate([aff, pad], axis=0).T


def _route(h, g, w_hi, w_lo, batch, seq, cap):
    return pl.pallas_call(
        functools.partial(_route_kernel, cap),
        grid=(batch,),
        in_specs=[pl.BlockSpec((None, seq, D_MODEL), lambda b: (b, 0, 0)),
                  pl.BlockSpec((1, D_MODEL), lambda b: (0, 0)),
                  pl.BlockSpec((N_EXPERTS, D_MODEL), lambda b: (0, 0)),
                  pl.BlockSpec((N_EXPERTS, D_MODEL), lambda b: (0, 0))],
        out_specs=[pl.BlockSpec((None, seq, D_MODEL), lambda b: (b, 0, 0)),
                   pl.BlockSpec((None, N_EXPERTS, 1, seq), lambda b: (b, 0, 0, 0)),
                   pl.BlockSpec((None, seq, LANES), lambda b: (b, 0, 0)),
                   pl.BlockSpec((None, seq, LANES), lambda b: (b, 0, 0))],
        out_shape=[jax.ShapeDtypeStruct((batch, seq, D_MODEL), BF16),
                   jax.ShapeDtypeStruct((batch, N_EXPERTS, 1, seq), F32),
                   jax.ShapeDtypeStruct((batch, seq, LANES), F32),
                   jax.ShapeDtypeStruct((batch, seq, LANES), F32)],
        compiler_params=pltpu.CompilerParams(
            dimension_semantics=("arbitrary",), vmem_limit_bytes=VMEM_LIMIT),
        name="route",
    )(h.reshape(batch, seq, D_MODEL), g, w_hi, w_lo)


SCATTER_TILE = 512


def _expert_kernel(cap, hn_ref, rrow_ref, rcol_ref, gcol_ref, wg_ref, wu_ref, wd_ref, out_ref):
    e = pl.program_id(1)
    seq = hn_ref.shape[0]

    @pl.when(e == 0)
    def _():
        out_ref[...] = jnp.zeros_like(out_ref)

    slot_rows = lax.broadcasted_iota(jnp.int32, (cap, seq), 0).astype(F32)
    pick = jnp.where(rrow_ref[e] == slot_rows, 1.0, 0.0).astype(BF16)
    xe = _dot(pick, hn_ref[...]).astype(BF16)
    a = _dot(xe, wg_ref[...])
    u = _dot(xe, wu_ref[...])
    y = _dot((jax.nn.silu(a) * u).astype(BF16), wd_ref[...]).astype(BF16)

    lane = lax.broadcasted_iota(jnp.int32, (SCATTER_TILE, LANES), 1)
    slot_cols = lax.broadcasted_iota(jnp.int32, (SCATTER_TILE, cap), 1).astype(F32)
    for j in range(0, seq, SCATTER_TILE):
        mine = lane == e
        rank = jnp.sum(jnp.where(mine, rcol_ref[j:j + SCATTER_TILE, :], 0.0), axis=-1, keepdims=True)
        gate = jnp.sum(jnp.where(mine, gcol_ref[j:j + SCATTER_TILE, :], 0.0), axis=-1, keepdims=True)
        put = jnp.where(rank == slot_cols, 1.0, 0.0).astype(BF16)
        out_ref[j:j + SCATTER_TILE, :] += gate * _dot(put, y)


def _experts(hn, rrow, rcol, gcol, wg, wu, wd, cap):
    batch, seq, _ = hn.shape
    d_exp = wg.shape[-1]
    return pl.pallas_call(
        functools.partial(_expert_kernel, cap),
        grid=(batch, N_EXPERTS),
        in_specs=[pl.BlockSpec((None, seq, D_MODEL), lambda b, e: (b, 0, 0)),
                  pl.BlockSpec((None, N_EXPERTS, 1, seq), lambda b, e: (b, 0, 0, 0)),
                  pl.BlockSpec((None, seq, LANES), lambda b, e: (b, 0, 0)),
                  pl.BlockSpec((None, seq, LANES), lambda b, e: (b, 0, 0)),
                  pl.BlockSpec((None, D_MODEL, d_exp), lambda b, e: (e, 0, 0)),
                  pl.BlockSpec((None, D_MODEL, d_exp), lambda b, e: (e, 0, 0)),
                  pl.BlockSpec((None, d_exp, D_MODEL), lambda b, e: (e, 0, 0))],
        out_specs=pl.BlockSpec((None, seq, D_MODEL), lambda b, e: (b, 0, 0)),
        out_shape=jax.ShapeDtypeStruct((batch, seq, D_MODEL), F32),
        compiler_params=pltpu.CompilerParams(
            dimension_semantics=("arbitrary", "arbitrary"), vmem_limit_bytes=VMEM_LIMIT),
        name="experts",
    )(hn, rrow, rcol, gcol, wg, wu, wd)


def _final_kernel(x_ref, add_ref, g_ref, o_ref):
    o_ref[...] = _rms(x_ref[...] + add_ref[...], g_ref[...])


def _final_norm(x, add, g):
    m = x.shape[0]
    row = pl.BlockSpec((ROW_TILE, D_MODEL), lambda i: (i, 0))
    return pl.pallas_call(
        _final_kernel,
        grid=(m // ROW_TILE,),
        in_specs=[row, row, pl.BlockSpec((1, D_MODEL), lambda i: (0, 0))],
        out_specs=row,
        out_shape=jax.ShapeDtypeStruct((m, D_MODEL), F32),
        compiler_params=pltpu.CompilerParams(dimension_semantics=("arbitrary",)),
        name="final_norm",
    )(x, add, g)


def _arrange_w_in(w):
    scale = LOG2E * HEAD_DIM ** -0.5
    qa, ka, va = w[:, :DIL_QKV], w[:, DIL_QKV:2 * DIL_QKV], w[:, 2 * DIL_QKV:3 * DIL_QKV]
    rest = w[:, 3 * DIL_QKV:]
    parts = []
    for g in range(len(DIL_PATTERNS)):
        sl = slice(g * DIL_OUT, (g + 1) * DIL_OUT)
        parts += [qa[:, sl] * scale, ka[:, sl], va[:, sl]]
    parts.append(rest[:, :WIN_Q] * scale)
    for base in (WIN_Q, WIN_Q + WIN_KV):
        for kv in range(WIN_KV_HEADS):
            blk = rest[:, base + kv * HEAD_DIM:base + (kv + 1) * HEAD_DIM]
            parts += [blk, blk]
    parts.append(rest[:, WIN_Q + 2 * WIN_KV:])
    return jnp.concatenate(parts, axis=1).astype(BF16)


def kernel(x, norm_mix, w_in, w_branch_a, w_branch_b, b_gate, sink_logit, w_out, norm_ffn, w_router,
           w_expert_gate, w_expert_up, w_expert_down, norm_final):
    batch, seq, d = x.shape
    depth = w_in.shape[0]
    cap = CAPACITY_FACTOR * seq // N_EXPERTS
    m = batch * seq
    h = x.reshape(m, d)
    moe = None
    for l in range(depth):
        res = _norm_proj(h, moe, norm_mix[l][None, :], _arrange_w_in(w_in[l]), b_gate[l][None, :])
        if moe is not None:
            h, res = res[0], res[1:]
        a0, a1, a2, qw, kd, vd, gates = res
        oa = _dilated_attention(a0, a1, a2, batch, seq).reshape(m, DIL_OUT)
        ow = _window_attention(qw, kd, vd, sink_logit[l][None, :], batch, seq).reshape(m, WIN_Q)
        h = _merge_out(oa, ow, gates, h, w_branch_a[l].astype(BF16),
                       w_branch_b[l].astype(BF16), w_out[l].astype(BF16))
        wr = w_router[l].T
        wr_hi = wr.astype(BF16)
        wr_lo = (wr - wr_hi.astype(F32)).astype(BF16)
        hn, rrow, rcol, gcol = _route(h, norm_ffn[l][None, :], wr_hi, wr_lo, batch, seq, cap)
        moe = _experts(hn, rrow, rcol, gcol, w_expert_gate[l].astype(BF16),
                       w_expert_up[l].astype(BF16), w_expert_down[l].astype(BF16), cap)
        moe = moe.reshape(m, d)
    return _final_norm(h, moe, norm_final[None, :]).reshape(batch, seq, d)
```

```python
import functools

import jax
import jax.numpy as jnp
from jax import lax
from jax.experimental import pallas as pl
from jax.experimental.pallas import tpu as pltpu

D_MODEL = 1024
HEAD_DIM = 64
DIL_PATTERNS = ((128, 1), (512, 4), (2048, 16))
DIL_HEADS = 4
N_DIL_SUB = DIL_HEADS * len(DIL_PATTERNS)
DIL_QKV = N_DIL_SUB * HEAD_DIM
DIL_OUT = DIL_HEADS * HEAD_DIM
DIL_HALF = 64
WIN_HALF = 128
WIN_Q = D_MODEL
WIN_Q_HEADS = WIN_Q // HEAD_DIM
WIN_KV_HEADS = 4
WIN_KV = WIN_KV_HEADS * HEAD_DIM
N_EXPERTS = 16
CAPACITY_FACTOR = 2
RMS_EPS = 1e-6
NEG_INF = -1e30
LOG2E = 1.4426950408889634

LANES = 128
GROUP_COLS = 3 * DIL_OUT
DIL_PAIRS = DIL_OUT // LANES
WIN_UNITS = WIN_KV_HEADS // 2
WIN_UNIT_PAIRS = WIN_Q_HEADS // (2 * WIN_UNITS)
Q_TILE = 128
ROW_TILE = 512
VMEM_LIMIT = 56 * 1024 * 1024

F32 = jnp.float32
BF16 = jnp.bfloat16


def _alibi_slopes(n):
    return [float(2.0 ** (-8.0 * i / n)) for i in range(1, n + 1)]


def _rms(x, g):
    return x * lax.rsqrt(jnp.mean(x * x, axis=-1, keepdims=True) + RMS_EPS) * g


def _dot(a, b):
    return jnp.dot(a, b, preferred_element_type=F32)


def _dot_nt(a, b):
    return lax.dot_general(a, b, (((1,), (1,)), ((), ())), preferred_element_type=F32)


def _left_lanes():
    return lax.broadcasted_iota(jnp.int32, (1, LANES), 1) < HEAD_DIM


_C_A = (0, GROUP_COLS, 2 * GROUP_COLS)
_C_QW = 3 * GROUP_COLS
_C_KV = _C_QW + WIN_Q
_C_GATE = _C_KV + 2 * WIN_KV
_PROJ_COLS = _C_GATE + 2 * D_MODEL
_COL_CHUNK = 512


def _proj_store(ref, xb, w_ref, c0, cw):
    for j in range(0, cw, _COL_CHUNK):
        jw = min(_COL_CHUNK, cw - j)
        ref[:, j:j + jw] = _dot(xb, w_ref[:, c0 + j:c0 + j + jw]).astype(BF16)


def _norm_proj_kernel(x_ref, g_ref, w_ref, bg_ref, a0_ref, a1_ref, a2_ref, qw_ref, kv_ref, gate_ref, xs_ref):
    xn = _rms(x_ref[...], g_ref[...])
    n_lane_tiles = D_MODEL // LANES
    for j in range(n_lane_tiles):
        xs_ref[j] = xn[:, j * LANES:(j + 1) * LANES]
    xb = xn.astype(BF16)
    _proj_store(a0_ref, xb, w_ref, _C_A[0], GROUP_COLS)
    for grp, ref in ((1, a1_ref), (2, a2_ref)):
        r = DIL_PATTERNS[grp][1]
        n = ROW_TILE // r
        xp = jnp.concatenate(
            [jnp.concatenate([xs_ref[j, pl.ds(c, n, stride=r), :] for j in range(n_lane_tiles)], axis=1)
             for c in range(r)], axis=0).astype(BF16)
        res = _dot(xp, w_ref[:, _C_A[grp]:_C_A[grp] + GROUP_COLS])
        for c in range(r):
            ref[:, c * GROUP_COLS:(c + 1) * GROUP_COLS] = res[c * n:(c + 1) * n].astype(BF16)
    _proj_store(qw_ref, xb, w_ref, _C_QW, WIN_Q)
    _proj_store(kv_ref, xb, w_ref, _C_KV, 2 * WIN_KV)
    for j in range(0, 2 * D_MODEL, _COL_CHUNK):
        z = _dot(xb, w_ref[:, _C_GATE + j:_C_GATE + j + _COL_CHUNK]) + bg_ref[:, j:j + _COL_CHUNK]
        gate_ref[:, j:j + _COL_CHUNK] = jax.nn.sigmoid(z).astype(BF16)


def _norm_proj(x, g, w, bg):
    m = x.shape[0]
    row = lambda c: pl.BlockSpec((ROW_TILE, c), lambda i: (i, 0))
    full = lambda a: pl.BlockSpec(a.shape, lambda i: (0, 0), pipeline_mode=pl.Buffered(1))
    in_specs = [row(D_MODEL), full(g), full(w), full(bg)]
    out_shape, out_specs = [], []
    for _, r in DIL_PATTERNS:
        out_shape.append(jax.ShapeDtypeStruct((m // r, r * GROUP_COLS), BF16))
        out_specs.append(pl.BlockSpec((ROW_TILE // r, r * GROUP_COLS), lambda i: (i, 0)))
    for c in (WIN_Q, 2 * WIN_KV, 2 * D_MODEL):
        out_shape.append(jax.ShapeDtypeStruct((m, c), BF16))
        out_specs.append(row(c))
    return pl.pallas_call(
        _norm_proj_kernel,
        grid=(m // ROW_TILE,),
        in_specs=in_specs,
        out_specs=out_specs,
        out_shape=out_shape,
        scratch_shapes=[pltpu.VMEM((D_MODEL // LANES, ROW_TILE, LANES), F32)],
        compiler_params=pltpu.CompilerParams(
            dimension_semantics=("arbitrary",), vmem_limit_bytes=VMEM_LIMIT),
        name="norm_proj",
    )(x, g, w, bg)


def _band_bias(tq, win, off, half, slopes):
    col = lax.broadcasted_iota(jnp.int32, (tq, win), 1)
    row = lax.broadcasted_iota(jnp.int32, (tq, win), 0)
    dist = jnp.abs(col - row + off).astype(F32)
    inside = dist <= float(half)
    return jnp.concatenate([jnp.where(inside, dist * (-s * LOG2E), NEG_INF) for s in slopes], axis=0)


def _ones_blockdiag(win):
    left = lax.broadcasted_iota(jnp.int32, (2 * win, LANES), 1) < HEAD_DIM
    top = jnp.where(lax.broadcasted_iota(jnp.int32, (2 * win, LANES), 0) < win, 1.0, 0.0)
    return jnp.where(left, top, 1.0 - top).astype(BF16)


def _pair_attention(q_tiles, k_win, v_win, bias, ones_bd, sink_rows):
    tq = q_tiles[0].shape[0]
    n = len(q_tiles)
    win = k_win.shape[0]
    left = _left_lanes()
    zero = jnp.zeros((), BF16)
    rows = []
    for q in q_tiles:
        rows += [jnp.where(left, q, zero), jnp.where(left, zero, q)]
    s = _dot_nt(jnp.concatenate(rows, axis=0), k_win) + bias
    m = jnp.broadcast_to(jnp.max(s, axis=-1, keepdims=True), (2 * n * tq, LANES))
    if sink_rows is not None:
        m = jnp.maximum(m, sink_rows)
    p = jnp.exp2(s - jnp.concatenate([m] * (win // LANES), axis=1)).astype(BF16)
    pair = lambda x: jnp.concatenate(
        [jnp.where(left, x[2 * i * tq:(2 * i + 1) * tq], x[(2 * i + 1) * tq:(2 * i + 2) * tq])
         for i in range(n)], axis=0)
    pc = jnp.concatenate(
        [jnp.concatenate([p[2 * i * tq:(2 * i + 1) * tq], p[(2 * i + 1) * tq:(2 * i + 2) * tq]], axis=1)
         for i in range(n)], axis=0)
    vbd = jnp.concatenate([jnp.where(left, v_win, zero), jnp.where(left, zero, v_win)], axis=0)
    on = _dot(pc, jnp.concatenate([vbd, ones_bd], axis=1))
    num, den = on[:, :LANES], on[:, LANES:]
    if sink_rows is not None:
        den = den + pair(jnp.exp2(sink_rows - m))
    return num, den, pair(m)


def _tile_variant(t, n_tiles):
    return jnp.where(t == 0, 0, jnp.where(t == n_tiles - 1, 2, 1))


def _dil_geometry(seq_full, grp):
    r = DIL_PATTERNS[grp][1]
    seq = seq_full // r
    win = min(seq, Q_TILE + 2 * DIL_HALF)
    n_tiles = seq // Q_TILE
    pad = (win - Q_TILE) // 2
    offs = [0] if n_tiles == 1 else [0, -pad, -2 * pad]
    return r, seq, win, n_tiles, pad, offs


def _dilated_kernel(seq_full, a0_ref, a1_ref, a2_ref, out_ref, o_nat, l_nat, b0, b1, b2, ones_ref, ones2_ref):
    a_refs = (a0_ref, a1_ref, a2_ref)
    bias_refs = (b0, b1, b2)
    slopes = _alibi_slopes(N_DIL_SUB)

    @pl.when(pl.program_id(0) == 0)
    def _():
        for grp in range(3):
            r, seq, win, n_tiles, pad, offs = _dil_geometry(seq_full, grp)
            for v, off in enumerate(offs):
                for pr in range(DIL_PAIRS):
                    hs = slopes[grp * DIL_HEADS + 2 * pr:grp * DIL_HEADS + 2 * pr + 2]
                    bias_refs[grp][v, pr] = _band_bias(Q_TILE, win, off, DIL_HALF, [s * r for s in hs])
        ones_ref[...] = _ones_blockdiag(_dil_geometry(seq_full, 0)[2])
        ones2_ref[...] = _ones_blockdiag(_dil_geometry(seq_full, 2)[2])

    for grp in range(3):
        r, seq, win, n_tiles, pad, offs = _dil_geometry(seq_full, grp)
        a_ref = a_refs[grp]
        ones_bd_ref = ones2_ref if grp == 2 else ones_ref
        for c in range(r):
            base = c * GROUP_COLS

            def tile(t, carry, c=c, base=base, grp=grp, r=r, seq=seq, win=win, n_tiles=n_tiles, pad=pad,
                     a_ref=a_ref, ones_bd_ref=ones_bd_ref):
                if n_tiles == 1:
                    q0, ks, var = 0, 0, 0
                else:
                    q0 = pl.multiple_of(t * Q_TILE, Q_TILE)
                    ks = pl.multiple_of(jnp.clip(q0 - pad, 0, seq - win), DIL_HALF)
                    var = _tile_variant(t, n_tiles)
                for pr in range(DIL_PAIRS):
                    lo = base + pr * LANES
                    q = a_ref[pl.ds(q0, Q_TILE), lo:lo + LANES]
                    k = a_ref[pl.ds(ks, win), DIL_OUT + lo:DIL_OUT + lo + LANES]
                    v = a_ref[pl.ds(ks, win), 2 * DIL_OUT + lo:2 * DIL_OUT + lo + LANES]
                    num, den, m = _pair_attention([q], k, v, bias_refs[grp][var, pr], ones_bd_ref[...], None)
                    o = num / den
                    lse = m + jnp.log2(den)
                    if r == 1:
                        rows = pl.ds(q0, Q_TILE)
                    else:
                        rows = pl.ds(c + r * q0, Q_TILE, stride=r)
                    o_nat[grp, pr, rows, :] = o
                    l_nat[grp, pr, rows, :] = lse
                return carry

            if n_tiles == 1:
                tile(0, 0)
            else:
                lax.fori_loop(0, n_tiles, tile, 0)

    def combine(i, carry):
        rows = pl.ds(pl.multiple_of(i * ROW_TILE, ROW_TILE), ROW_TILE)
        for pr in range(DIL_PAIRS):
            ls = [l_nat[g, pr, rows, :] for g in range(3)]
            mx = jnp.maximum(jnp.maximum(ls[0], ls[1]), ls[2])
            es = [jnp.exp2(l - mx) for l in ls]
            num = es[0] * o_nat[0, pr, rows, :] + es[1] * o_nat[1, pr, rows, :] + es[2] * o_nat[2, pr, rows, :]
            out_ref[rows, pr * LANES:(pr + 1) * LANES] = (num / (es[0] + es[1] + es[2])).astype(BF16)
        return carry

    lax.fori_loop(0, seq_full // ROW_TILE, combine, 0)


def _dilated_attention(a0, a1, a2, batch, seq_full):
    views, in_specs, bias_shapes = [], [], []
    for grp, a in enumerate((a0, a1, a2)):
        r, seq, win, n_tiles, pad, offs = _dil_geometry(seq_full, grp)
        views.append(a.reshape(batch, seq, r * GROUP_COLS))
        in_specs.append(pl.BlockSpec((None, seq, r * GROUP_COLS), lambda b: (b, 0, 0)))
        bias_shapes.append(pltpu.VMEM((len(offs), DIL_PAIRS, 2 * Q_TILE, win), F32))
    win0 = _dil_geometry(seq_full, 0)[2]
    win2 = _dil_geometry(seq_full, 2)[2]
    return pl.pallas_call(
        functools.partial(_dilated_kernel, seq_full),
        grid=(batch,),
        in_specs=in_specs,
        out_specs=pl.BlockSpec((None, seq_full, DIL_OUT), lambda b: (b, 0, 0)),
        out_shape=jax.ShapeDtypeStruct((batch, seq_full, DIL_OUT), BF16),
        scratch_shapes=[pltpu.VMEM((3, DIL_PAIRS, seq_full, LANES), F32),
                        pltpu.VMEM((3, DIL_PAIRS, seq_full, LANES), F32)]
        + bias_shapes + [pltpu.VMEM((2 * win0, LANES), BF16), pltpu.VMEM((2 * win2, LANES), BF16)],
        compiler_params=pltpu.CompilerParams(
            dimension_semantics=("arbitrary",), vmem_limit_bytes=VMEM_LIMIT),
        name="dilated",
    )(*views)


WIN_WINDOW = Q_TILE + 2 * WIN_HALF


def _window_pair_heads():
    group = WIN_Q_HEADS // WIN_KV_HEADS
    pairs = []
    for u in range(WIN_UNITS):
        for i in range(WIN_UNIT_PAIRS):
            pairs.append((2 * u * group + i, (2 * u + 1) * group + i))
    return pairs


def _window_kernel(seq, q_ref, kv_ref, sink_ref, o_ref, bias_ref, ones_ref, sink_rows_ref):
    n_tiles = seq // Q_TILE
    slopes = _alibi_slopes(WIN_Q_HEADS)
    pair_heads = _window_pair_heads()
    unit_rows = 2 * WIN_UNIT_PAIRS * Q_TILE

    @pl.when(pl.program_id(0) == 0)
    def _():
        for u in range(WIN_UNITS):
            heads = [h for pr in pair_heads[u * WIN_UNIT_PAIRS:(u + 1) * WIN_UNIT_PAIRS] for h in pr]
            for v, off in enumerate((0, -WIN_HALF, -2 * WIN_HALF)):
                bias_ref[v, u] = _band_bias(Q_TILE, WIN_WINDOW, off, WIN_HALF, [slopes[h] for h in heads])
            sink_rows_ref[u] = jnp.concatenate(
                [jnp.broadcast_to(sink_ref[0:1, h:h + 1] * LOG2E, (Q_TILE, LANES)) for h in heads], axis=0)
        ones_ref[...] = _ones_blockdiag(WIN_WINDOW)

    def tile(t, carry):
        q0 = pl.multiple_of(t * Q_TILE, Q_TILE)
        ks = pl.multiple_of(jnp.clip(q0 - WIN_HALF, 0, seq - WIN_WINDOW), Q_TILE)
        var = _tile_variant(t, n_tiles)
        for u in range(WIN_UNITS):
            cols = [(u * WIN_UNIT_PAIRS + i) * LANES for i in range(WIN_UNIT_PAIRS)]
            qs = [q_ref[pl.ds(q0, Q_TILE), c:c + LANES] for c in cols]
            k = kv_ref[pl.ds(ks, WIN_WINDOW), u * LANES:(u + 1) * LANES]
            v = kv_ref[pl.ds(ks, WIN_WINDOW), WIN_KV + u * LANES:WIN_KV + (u + 1) * LANES]
            num, den, _ = _pair_attention(qs, k, v, bias_ref[var, u], ones_ref[...], sink_rows_ref[u])
            o = (num / den).astype(BF16)
            for i, c in enumerate(cols):
                o_ref[pl.ds(q0, Q_TILE), c:c + LANES] = o[i * Q_TILE:(i + 1) * Q_TILE]
        return carry

    lax.fori_loop(0, n_tiles, tile, 0)


def _window_attention(q, kv, sink, batch, seq):
    unit_rows = 2 * WIN_UNIT_PAIRS * Q_TILE
    return pl.pallas_call(
        functools.partial(_window_kernel, seq),
        grid=(batch,),
        in_specs=[pl.BlockSpec((None, seq, WIN_Q), lambda b: (b, 0, 0)),
                  pl.BlockSpec((None, seq, 2 * WIN_KV), lambda b: (b, 0, 0)),
                  pl.BlockSpec((1, WIN_Q_HEADS), lambda b: (0, 0))],
        out_specs=pl.BlockSpec((None, seq, WIN_Q), lambda b: (b, 0, 0)),
        out_shape=jax.ShapeDtypeStruct((batch, seq, WIN_Q), BF16),
        scratch_shapes=[pltpu.VMEM((3, WIN_UNITS, unit_rows, WIN_WINDOW), F32),
                        pltpu.VMEM((2 * WIN_WINDOW, LANES), BF16),
                        pltpu.VMEM((WIN_UNITS, unit_rows, LANES), F32)],
        compiler_params=pltpu.CompilerParams(
            dimension_semantics=("arbitrary",), vmem_limit_bytes=VMEM_LIMIT),
        name="window",
    )(q.reshape(batch, seq, WIN_Q), kv.reshape(batch, seq, 2 * WIN_KV), sink)


def _merge_kernel(oa_ref, ow_ref, gate_ref, h_ref, wa_ref, wb_ref, wo_ref, out_ref):
    ya = _dot(oa_ref[...], wa_ref[...])
    yb = _dot(ow_ref[...], wb_ref[...])
    merged = gate_ref[:, :D_MODEL].astype(F32) * ya + gate_ref[:, D_MODEL:].astype(F32) * yb
    out_ref[...] = h_ref[...] + _dot(merged.astype(BF16), wo_ref[...])


def _merge_out(oa, ow, gates, h, wa, wb, wo):
    m = h.shape[0]
    row = lambda c: pl.BlockSpec((ROW_TILE, c), lambda i: (i, 0))
    full = lambda a: pl.BlockSpec(a.shape, lambda i: (0, 0), pipeline_mode=pl.Buffered(1))
    return pl.pallas_call(
        _merge_kernel,
        grid=(m // ROW_TILE,),
        in_specs=[row(DIL_OUT), row(WIN_Q), row(2 * D_MODEL), row(D_MODEL), full(wa), full(wb), full(wo)],
        out_specs=row(D_MODEL),
        out_shape=jax.ShapeDtypeStruct((m, D_MODEL), F32),
        compiler_params=pltpu.CompilerParams(
            dimension_semantics=("arbitrary",), vmem_limit_bytes=VMEM_LIMIT),
        name="merge_out",
    )(oa, ow, gates, h, wa, wb, wo)


CUM_CHUNK = 256


def _prefix_exclusive(mask_f, tri):
    e, s = mask_f.shape
    carry = jnp.zeros((e, 1), F32)
    parts = []
    for j in range(0, s, CUM_CHUNK):
        blk = mask_f[:, j:j + CUM_CHUNK]
        parts.append(_dot(blk.astype(BF16), tri) + carry)
        carry = carry + jnp.sum(blk, axis=-1, keepdims=True)
    return jnp.concatenate(parts, axis=-1)


def _route_kernel(cap, h_ref, g_ref, whi_ref, wlo_ref, hn_ref, rrow_ref, rcol_ref, gcol_ref):
    seq = h_ref.shape[0]
    hn = _rms(h_ref[...], g_ref[...])
    hn_hi = hn.astype(BF16)
    hn_lo = (hn - hn_hi.astype(F32)).astype(BF16)
    hn_ref[...] = hn_hi
    logits = (_dot_nt(whi_ref[...], hn_hi) + _dot_nt(whi_ref[...], hn_lo)
              + _dot_nt(wlo_ref[...], hn_hi))
    mx = jnp.max(logits, axis=0, keepdims=True)
    ex = jnp.exp(logits - mx)
    aff = ex / jnp.sum(ex, axis=0, keepdims=True)
    bits = pltpu.bitcast(aff, jnp.int32)

    def search(_, c):
        lo, hi = c
        mid = lo + ((hi - lo) >> 1)
        cnt = jnp.sum(jnp.where(bits >= mid, 1.0, 0.0), axis=-1, keepdims=True)
        ok = cnt >= float(cap)
        return jnp.where(ok, mid, lo), jnp.where(ok, hi, mid)

    lo0 = jnp.zeros((N_EXPERTS, 1), jnp.int32)
    hi0 = jnp.full((N_EXPERTS, 1), 0x3F800001, jnp.int32)
    thr, _ = lax.fori_loop(0, 31, search, (lo0, hi0))

    r_i = lax.broadcasted_iota(jnp.int32, (CUM_CHUNK, CUM_CHUNK), 0)
    c_i = lax.broadcasted_iota(jnp.int32, (CUM_CHUNK, CUM_CHUNK), 1)
    tri = jnp.where(r_i < c_i, 1.0, 0.0).astype(BF16)
    gt = jnp.where(bits > thr, 1.0, 0.0)
    eq = jnp.where(bits == thr, 1.0, 0.0)
    need = float(cap) - jnp.sum(gt, axis=-1, keepdims=True)
    sel = gt + eq * jnp.where(_prefix_exclusive(eq, tri) < need, 1.0, 0.0)
    rank = jnp.where(sel > 0.0, _prefix_exclusive(sel, tri), -1.0)
    for e in range(N_EXPERTS):
        rrow_ref[e] = rank[e:e + 1, :]
    pad = jnp.full((LANES - N_EXPERTS, seq), -1.0, F32)
    rcol_ref[...] = jnp.concatenate([rank, pad], axis=0).T
    gcol_ref[...] = jnp.concatenate([aff, pad], axis=0).T


def _route(h, g, w_hi, w_lo, batch, seq, cap):
    return pl.pallas_call(
        functools.partial(_route_kernel, cap),
        grid=(batch,),
        in_specs=[pl.BlockSpec((None, seq, D_MODEL), lambda b: (b, 0, 0)),
                  pl.BlockSpec((1, D_MODEL), lambda b: (0, 0)),
                  pl.BlockSpec((N_EXPERTS, D_MODEL), lambda b: (0, 0)),
                  pl.BlockSpec((N_EXPERTS, D_MODEL), lambda b: (0, 0))],
        out_specs=[pl.BlockSpec((None, seq, D_MODEL), lambda b: (b, 0, 0)),
                   pl.BlockSpec((None, N_EXPERTS, 1, seq), lambda b: (b, 0, 0, 0)),
                   pl.BlockSpec((None, seq, LANES), lambda b: (b, 0, 0)),
                   pl.BlockSpec((None, seq, LANES), lambda b: (b, 0, 0))],
        out_shape=[jax.ShapeDtypeStruct((batch, seq, D_MODEL), BF16),
                   jax.ShapeDtypeStruct((batch, N_EXPERTS, 1, seq), F32),
                   jax.ShapeDtypeStruct((batch, seq, LANES), F32),
                   jax.ShapeDtypeStruct((batch, seq, LANES), F32)],
        compiler_params=pltpu.CompilerParams(
            dimension_semantics=("arbitrary",), vmem_limit_bytes=VMEM_LIMIT),
        name="route",
    )(h.reshape(batch, seq, D_MODEL), g, w_hi, w_lo)


SCATTER_TILE = 512


def _expert_kernel(cap, h_hbm, hn_ref, rrow_ref, rcol_ref, gcol_ref, wg_ref, wu_ref, wd_ref, out_ref, sem):
    b = pl.program_id(0)
    e = pl.program_id(1)
    seq = hn_ref.shape[0]
    residual_copy = pltpu.make_async_copy(h_hbm.at[pl.ds(b, 1)], out_ref, sem)

    @pl.when(e == 0)
    def _():
        residual_copy.start()

    slot_rows = lax.broadcasted_iota(jnp.int32, (cap, seq), 0).astype(F32)
    pick = jnp.where(rrow_ref[e] == slot_rows, 1.0, 0.0).astype(BF16)
    xe = _dot(pick, hn_ref[...]).astype(BF16)
    a = _dot(xe, wg_ref[...])
    u = _dot(xe, wu_ref[...])
    y = _dot((jax.nn.silu(a) * u).astype(BF16), wd_ref[...]).astype(BF16)

    @pl.when(e == 0)
    def _():
        residual_copy.wait()

    lane =lax.broadcasted_iota(jnp.int32, (SCATTER_TILE, LANES), 1)
    slot_cols = lax.broadcasted_iota(jnp.int32, (SCATTER_TILE, cap), 1).astype(F32)
    for j in range(0, seq, SCATTER_TILE):
        mine = lane == e
        rank = jnp.sum(jnp.where(mine, rcol_ref[j:j + SCATTER_TILE, :], 0.0), axis=-1, keepdims=True)
        gate = jnp.sum(jnp.where(mine, gcol_ref[j:j + SCATTER_TILE, :], 0.0), axis=-1, keepdims=True)
        put = jnp.where(rank == slot_cols, 1.0, 0.0).astype(BF16)
        out_ref[0, j:j + SCATTER_TILE, :] += gate * _dot(put, y)


def _experts(h, hn, rrow, rcol, gcol, wg, wu, wd, cap):
    batch, seq, _ = hn.shape
    d_exp = wg.shape[-1]
    return pl.pallas_call(
        functools.partial(_expert_kernel, cap),
        grid=(batch, N_EXPERTS),
        in_specs=[pl.BlockSpec(memory_space=pl.ANY),
                  pl.BlockSpec((None, seq, D_MODEL), lambda b, e: (b, 0, 0)),
                  pl.BlockSpec((None, N_EXPERTS, 1, seq), lambda b, e: (b, 0, 0, 0)),
                  pl.BlockSpec((None, seq, LANES), lambda b, e: (b, 0, 0)),
                  pl.BlockSpec((None, seq, LANES), lambda b, e: (b, 0, 0)),
                  pl.BlockSpec((None, D_MODEL, d_exp), lambda b, e: (e, 0, 0)),
                  pl.BlockSpec((None, D_MODEL, d_exp), lambda b, e: (e, 0, 0)),
                  pl.BlockSpec((None, d_exp, D_MODEL), lambda b, e: (e, 0, 0))],
        out_specs=pl.BlockSpec((1, seq, D_MODEL), lambda b, e: (b, 0, 0)),
        out_shape=jax.ShapeDtypeStruct((batch, seq, D_MODEL), F32),
        scratch_shapes=[pltpu.SemaphoreType.DMA(())],
        compiler_params=pltpu.CompilerParams(
            dimension_semantics=("arbitrary", "arbitrary"), vmem_limit_bytes=VMEM_LIMIT),
        name="experts",
    )(h.reshape(batch, seq, D_MODEL), hn, rrow, rcol, gcol, wg, wu, wd)


def _final_kernel(x_ref, g_ref, o_ref):
    o_ref[...] = _rms(x_ref[...], g_ref[...])


def _final_norm(x, g):
    m = x.shape[0]
    row = pl.BlockSpec((ROW_TILE, D_MODEL), lambda i: (i, 0))
    return pl.pallas_call(
        _final_kernel,
        grid=(m // ROW_TILE,),
        in_specs=[row, pl.BlockSpec((1, D_MODEL), lambda i: (0, 0))],
        out_specs=row,
        out_shape=jax.ShapeDtypeStruct((m, D_MODEL), F32),
        compiler_params=pltpu.CompilerParams(dimension_semantics=("arbitrary",)),
        name="final_norm",
    )(x, g)


def _arrange_w_in(w):
    scale = LOG2E * HEAD_DIM ** -0.5
    qa, ka, va = w[:, :DIL_QKV], w[:, DIL_QKV:2 * DIL_QKV], w[:, 2 * DIL_QKV:3 * DIL_QKV]
    rest = w[:, 3 * DIL_QKV:]
    parts = []
    for g in range(len(DIL_PATTERNS)):
        sl = slice(g * DIL_OUT, (g + 1) * DIL_OUT)
        parts += [qa[:, sl] * scale, ka[:, sl], va[:, sl]]
    for pr in _window_pair_heads():
        parts += [rest[:, h * HEAD_DIM:(h + 1) * HEAD_DIM] * scale for h in pr]
    parts.append(rest[:, WIN_Q:])
    return jnp.concatenate(parts, axis=1).astype(BF16)


def _arrange_w_branch_b(w):
    return jnp.concatenate([w[h * HEAD_DIM:(h + 1) * HEAD_DIM] for pr in _window_pair_heads() for h in pr],
                           axis=0).astype(BF16)


def kernel(x, norm_mix, w_in, w_branch_a, w_branch_b, b_gate, sink_logit, w_out, norm_ffn, w_router,
           w_expert_gate, w_expert_up, w_expert_down, norm_final):
    batch, seq, d = x.shape
    depth = w_in.shape[0]
    cap = CAPACITY_FACTOR * seq // N_EXPERTS
    m = batch * seq
    h = x.reshape(m, d)
    for l in range(depth):
        a0, a1, a2, qw, kv, gates = _norm_proj(h, norm_mix[l][None, :], _arrange_w_in(w_in[l]),
                                               b_gate[l][None, :])
        oa = _dilated_attention(a0, a1, a2, batch, seq).reshape(m, DIL_OUT)
        ow = _window_attention(qw, kv, sink_logit[l][None, :], batch, seq).reshape(m, WIN_Q)
        h = _merge_out(oa, ow, gates, h, w_branch_a[l].astype(BF16),
                       _arrange_w_branch_b(w_branch_b[l]), w_out[l].astype(BF16))
        wr = w_router[l].T
        wr_hi = wr.astype(BF16)
        wr_lo = (wr - wr_hi.astype(F32)).astype(BF16)
        hn, rrow, rcol, gcol = _route(h, norm_ffn[l][None, :], wr_hi, wr_lo, batch, seq, cap)
        h = _experts(h, hn, rrow, rcol, gcol, w_expert_gate[l].astype(BF16),
                     w_expert_up[l].astype(BF16), w_expert_down[l].astype(BF16), cap).reshape(m, d)
    return _final_norm(h, norm_final[None, :]).reshape(batch, seq, d)
```

```python
import functools

import jax
import jax.numpy as jnp
from jax import lax
from jax.experimental import pallas as pl
from jax.experimental.pallas import tpu as pltpu

D_MODEL = 1024
HEAD_DIM = 64
DIL_PATTERNS = ((128, 1), (512, 4), (2048, 16))
DIL_HEADS = 4
N_DIL_SUB = DIL_HEADS * len(DIL_PATTERNS)
DIL_QKV = N_DIL_SUB * HEAD_DIM
DIL_OUT = DIL_HEADS * HEAD_DIM
DIL_HALF = 64
WIN_HALF = 128
WIN_Q = D_MODEL
WIN_Q_HEADS = WIN_Q // HEAD_DIM
WIN_KV_HEADS = 4
WIN_KV = WIN_KV_HEADS * HEAD_DIM
N_EXPERTS = 16
CAPACITY_FACTOR = 2
RMS_EPS = 1e-6
NEG_INF = -1e30
LOG2E = 1.4426950408889634

LANES = 128
GROUP_COLS = 3 * DIL_OUT
DIL_PAIRS = DIL_OUT // LANES
WIN_UNITS = WIN_KV_HEADS // 2
WIN_UNIT_PAIRS = WIN_Q_HEADS // (2 * WIN_UNITS)
Q_TILE = 128
ROW_TILE = 512
VMEM_LIMIT = 56 * 1024 * 1024

F32 = jnp.float32
BF16 = jnp.bfloat16


def _alibi_slopes(n):
    return [float(2.0 ** (-8.0 * i / n)) for i in range(1, n + 1)]


def _rms(x, g):
    return x * lax.rsqrt(jnp.mean(x * x, axis=-1, keepdims=True) + RMS_EPS) * g


def _dot(a, b):
    return jnp.dot(a, b, preferred_element_type=F32)


def _dot_nt(a, b):
    return lax.dot_general(a, b, (((1,), (1,)), ((), ())), preferred_element_type=F32)


def _left_lanes():
    return lax.broadcasted_iota(jnp.int32, (1, LANES), 1) < HEAD_DIM


_C_A = (0, GROUP_COLS, 2 * GROUP_COLS)
_C_QW = 3 * GROUP_COLS
_C_KV = _C_QW + WIN_Q
_C_GATE = _C_KV + 2 * WIN_KV
_PROJ_COLS = _C_GATE + 2 * D_MODEL
_COL_CHUNK = 512


def _proj_store(ref, xb, w_ref, c0, cw):
    for j in range(0, cw, _COL_CHUNK):
        jw = min(_COL_CHUNK, cw - j)
        ref[:, j:j + jw] = _dot(xb, w_ref[:, c0 + j:c0 + j + jw]).astype(BF16)


def _norm_proj_kernel(x_ref, g_ref, w_ref, bg_ref, a0_ref, a1_ref, a2_ref, qw_ref, kv_ref, gate_ref, xs_ref):
    xn = _rms(x_ref[...], g_ref[...])
    n_lane_tiles = D_MODEL // LANES
    for j in range(n_lane_tiles):
        xs_ref[j] = xn[:, j * LANES:(j + 1) * LANES]
    xb = xn.astype(BF16)
    _proj_store(a0_ref, xb, w_ref, _C_A[0], GROUP_COLS)
    for grp, ref in ((1, a1_ref), (2, a2_ref)):
        r = DIL_PATTERNS[grp][1]
        n = ROW_TILE // r
        xp = jnp.concatenate(
            [jnp.concatenate([xs_ref[j, pl.ds(c, n, stride=r), :] for j in range(n_lane_tiles)], axis=1)
             for c in range(r)], axis=0).astype(BF16)
        res = _dot(xp, w_ref[:, _C_A[grp]:_C_A[grp] + GROUP_COLS])
        for c in range(r):
            ref[:, c * GROUP_COLS:(c + 1) * GROUP_COLS] = res[c * n:(c + 1) * n].astype(BF16)
    _proj_store(qw_ref, xb, w_ref, _C_QW, WIN_Q)
    _proj_store(kv_ref, xb, w_ref, _C_KV, 2 * WIN_KV)
    for j in range(0, 2 * D_MODEL, _COL_CHUNK):
        z = _dot(xb, w_ref[:, _C_GATE + j:_C_GATE + j + _COL_CHUNK]) + bg_ref[:, j:j + _COL_CHUNK]
        gate_ref[:, j:j + _COL_CHUNK] = jax.nn.sigmoid(z).astype(BF16)


def _norm_proj(x, g, w, bg):
    m = x.shape[0]
    row = lambda c: pl.BlockSpec((ROW_TILE, c), lambda i: (i, 0))
    full = lambda a: pl.BlockSpec(a.shape, lambda i: (0, 0), pipeline_mode=pl.Buffered(1))
    in_specs = [row(D_MODEL), full(g), full(w), full(bg)]
    out_shape, out_specs = [], []
    for _, r in DIL_PATTERNS:
        out_shape.append(jax.ShapeDtypeStruct((m // r, r * GROUP_COLS), BF16))
        out_specs.append(pl.BlockSpec((ROW_TILE // r, r * GROUP_COLS), lambda i: (i, 0)))
    for c in (WIN_Q, 2 * WIN_KV, 2 * D_MODEL):
        out_shape.append(jax.ShapeDtypeStruct((m, c), BF16))
        out_specs.append(row(c))
    return pl.pallas_call(
        _norm_proj_kernel,
        grid=(m // ROW_TILE,),
        in_specs=in_specs,
        out_specs=out_specs,
        out_shape=out_shape,
        scratch_shapes=[pltpu.VMEM((D_MODEL // LANES, ROW_TILE, LANES), F32)],
        compiler_params=pltpu.CompilerParams(
            dimension_semantics=("arbitrary",), vmem_limit_bytes=VMEM_LIMIT),
        name="norm_proj",
    )(x, g, w, bg)


def _band_bias(tq, win, off, half, slopes):
    col = lax.broadcasted_iota(jnp.int32, (tq, win), 1)
    row = lax.broadcasted_iota(jnp.int32, (tq, win), 0)
    dist = jnp.abs(col - row + off).astype(F32)
    inside = dist <= float(half)
    return jnp.concatenate([jnp.where(inside, dist * (-s * LOG2E), NEG_INF) for s in slopes], axis=0)


def _ones_blockdiag(win):
    left = lax.broadcasted_iota(jnp.int32, (2 * win, LANES), 1) < HEAD_DIM
    top = jnp.where(lax.broadcasted_iota(jnp.int32, (2 * win, LANES), 0) < win, 1.0, 0.0)
    return jnp.where(left, top, 1.0 - top).astype(BF16)


def _pair_attention(q_tiles, k_win, v_win, bias, ones_bd, sink_rows):
    tq = q_tiles[0].shape[0]
    n = len(q_tiles)
    win = k_win.shape[0]
    left = _left_lanes()
    zero = jnp.zeros((), BF16)
    rows = []
    for q in q_tiles:
        rows += [jnp.where(left, q, zero), jnp.where(left, zero, q)]
    s = _dot_nt(jnp.concatenate(rows, axis=0), k_win) + bias
    m = jnp.broadcast_to(jnp.max(s, axis=-1, keepdims=True), (2 * n * tq, LANES))
    if sink_rows is not None:
        m = jnp.maximum(m, sink_rows)
    p = jnp.exp2(s - jnp.concatenate([m] * (win // LANES), axis=1)).astype(BF16)
    pair = lambda x: jnp.concatenate(
        [jnp.where(left, x[2 * i * tq:(2 * i + 1) * tq], x[(2 * i + 1) * tq:(2 * i + 2) * tq])
         for i in range(n)], axis=0)
    pc = jnp.concatenate(
        [jnp.concatenate([p[2 * i * tq:(2 * i + 1) * tq], p[(2 * i + 1) * tq:(2 * i + 2) * tq]], axis=1)
         for i in range(n)], axis=0)
    vbd = jnp.concatenate([jnp.where(left, v_win, zero), jnp.where(left, zero, v_win)], axis=0)
    on = _dot(pc, jnp.concatenate([vbd, ones_bd], axis=1))
    num, den = on[:, :LANES], on[:, LANES:]
    if sink_rows is not None:
        den = den + pair(jnp.exp2(sink_rows - m))
    return num, den, pair(m)


def _tile_variant(t, n_tiles):
    return jnp.where(t == 0, 0, jnp.where(t == n_tiles - 1, 2, 1))


def _dil_geometry(seq_full, grp):
    r = DIL_PATTERNS[grp][1]
    seq = seq_full // r
    win = min(seq, Q_TILE + 2 * DIL_HALF)
    n_tiles = seq // Q_TILE
    pad = (win - Q_TILE) // 2
    offs = [0] if n_tiles == 1 else [0, -pad, -2 * pad]
    return r, seq, win, n_tiles, pad, offs


def _dilated_kernel(seq_full, a0_ref, a1_ref, a2_ref, out_ref, o_nat, l_nat, b0, b1, b2, ones_ref, ones2_ref):
    a_refs = (a0_ref, a1_ref, a2_ref)
    bias_refs = (b0, b1, b2)
    slopes = _alibi_slopes(N_DIL_SUB)

    @pl.when(pl.program_id(0) == 0)
    def _():
        for grp in range(3):
            r, seq, win, n_tiles, pad, offs = _dil_geometry(seq_full, grp)
            for v, off in enumerate(offs):
                for pr in range(DIL_PAIRS):
                    hs = slopes[grp * DIL_HEADS + 2 * pr:grp * DIL_HEADS + 2 * pr + 2]
                    bias_refs[grp][v, pr] = _band_bias(Q_TILE, win, off, DIL_HALF, [s * r for s in hs])
        ones_ref[...] = _ones_blockdiag(_dil_geometry(seq_full, 0)[2])
        ones2_ref[...] = _ones_blockdiag(_dil_geometry(seq_full, 2)[2])

    for grp in range(3):
        r, seq, win, n_tiles, pad, offs = _dil_geometry(seq_full, grp)
        a_ref = a_refs[grp]
        ones_bd_ref = ones2_ref if grp == 2 else ones_ref
        for c in range(r):
            base = c * GROUP_COLS

            def tile(t, carry, c=c, base=base, grp=grp, r=r, seq=seq, win=win, n_tiles=n_tiles, pad=pad,
                     a_ref=a_ref, ones_bd_ref=ones_bd_ref):
                if n_tiles == 1:
                    q0, ks, var = 0, 0, 0
                else:
                    q0 = pl.multiple_of(t * Q_TILE, Q_TILE)
                    ks = pl.multiple_of(jnp.clip(q0 - pad, 0, seq - win), DIL_HALF)
                    var = _tile_variant(t, n_tiles)
                for pr in range(DIL_PAIRS):
                    lo = base + pr * LANES
                    q = a_ref[pl.ds(q0, Q_TILE), lo:lo + LANES]
                    k = a_ref[pl.ds(ks, win), DIL_OUT + lo:DIL_OUT + lo + LANES]
                    v = a_ref[pl.ds(ks, win), 2 * DIL_OUT + lo:2 * DIL_OUT + lo + LANES]
                    num, den, m = _pair_attention([q], k, v, bias_refs[grp][var, pr], ones_bd_ref[...], None)
                    o = num / den
                    lse = m + jnp.log2(den)
                    if r == 1:
                        rows = pl.ds(q0, Q_TILE)
                    else:
                        rows = pl.ds(c + r * q0, Q_TILE, stride=r)
                    o_nat[grp, pr, rows, :] = o
                    l_nat[grp, pr, rows, :] = lse
                return carry

            if n_tiles == 1:
                tile(0, 0)
            else:
                lax.fori_loop(0, n_tiles, tile, 0)

    def combine(i, carry):
        rows = pl.ds(pl.multiple_of(i * ROW_TILE, ROW_TILE), ROW_TILE)
        for pr in range(DIL_PAIRS):
            ls = [l_nat[g, pr, rows, :] for g in range(3)]
            mx = jnp.maximum(jnp.maximum(ls[0], ls[1]), ls[2])
            es = [jnp.exp2(l - mx) for l in ls]
            num = es[0] * o_nat[0, pr, rows, :] + es[1] * o_nat[1, pr, rows, :] + es[2] * o_nat[2, pr, rows, :]
            out_ref[rows, pr * LANES:(pr + 1) * LANES] = (num / (es[0] + es[1] + es[2])).astype(BF16)
        return carry

    lax.fori_loop(0, seq_full // ROW_TILE, combine, 0)


def _dilated_attention(a0, a1, a2, batch, seq_full):
    views, in_specs, bias_shapes = [], [], []
    for grp, a in enumerate((a0, a1, a2)):
        r, seq, win, n_tiles, pad, offs = _dil_geometry(seq_full, grp)
        views.append(a.reshape(batch, seq, r * GROUP_COLS))
        in_specs.append(pl.BlockSpec((None, seq, r * GROUP_COLS), lambda b: (b, 0, 0)))
        bias_shapes.append(pltpu.VMEM((len(offs), DIL_PAIRS, 2 * Q_TILE, win), F32))
    win0 = _dil_geometry(seq_full, 0)[2]
    win2 = _dil_geometry(seq_full, 2)[2]
    return pl.pallas_call(
        functools.partial(_dilated_kernel, seq_full),
        grid=(batch,),
        in_specs=in_specs,
        out_specs=pl.BlockSpec((None, seq_full, DIL_OUT), lambda b: (b, 0, 0)),
        out_shape=jax.ShapeDtypeStruct((batch, seq_full, DIL_OUT), BF16),
        scratch_shapes=[pltpu.VMEM((3, DIL_PAIRS, seq_full, LANES), F32),
                        pltpu.VMEM((3, DIL_PAIRS, seq_full, LANES), F32)]
        + bias_shapes + [pltpu.VMEM((2 * win0, LANES), BF16), pltpu.VMEM((2 * win2, LANES), BF16)],
        compiler_params=pltpu.CompilerParams(
            dimension_semantics=("arbitrary",), vmem_limit_bytes=VMEM_LIMIT),
        name="dilated",
    )(*views)


WIN_WINDOW = Q_TILE + 2 * WIN_HALF


def _window_pair_heads():
    group = WIN_Q_HEADS // WIN_KV_HEADS
    pairs = []
    for u in range(WIN_UNITS):
        for i in range(WIN_UNIT_PAIRS):
            pairs.append((2 * u * group + i, (2 * u + 1) * group + i))
    return pairs


def _window_kernel(seq, q_ref, kv_ref, sink_ref, o_ref, bias_ref, ones_ref, sink_rows_ref):
    n_tiles = seq // Q_TILE
    slopes = _alibi_slopes(WIN_Q_HEADS)
    pair_heads = _window_pair_heads()
    unit_rows = 2 * WIN_UNIT_PAIRS * Q_TILE

    @pl.when(pl.program_id(0) == 0)
    def _():
        for u in range(WIN_UNITS):
            heads = [h for pr in pair_heads[u * WIN_UNIT_PAIRS:(u + 1) * WIN_UNIT_PAIRS] for h in pr]
            for v, off in enumerate((0, -WIN_HALF, -2 * WIN_HALF)):
                bias_ref[v, u] = _band_bias(Q_TILE, WIN_WINDOW, off, WIN_HALF, [slopes[h] for h in heads])
            sink_rows_ref[u] = jnp.concatenate(
                [jnp.broadcast_to(sink_ref[0:1, h:h + 1] * LOG2E, (Q_TILE, LANES)) for h in heads], axis=0)
        ones_ref[...] = _ones_blockdiag(WIN_WINDOW)

    def tile(t, carry):
        q0 = pl.multiple_of(t * Q_TILE, Q_TILE)
        ks = pl.multiple_of(jnp.clip(q0 - WIN_HALF, 0, seq - WIN_WINDOW), Q_TILE)
        var = _tile_variant(t, n_tiles)
        for u in range(WIN_UNITS):
            cols = [(u * WIN_UNIT_PAIRS + i) * LANES for i in range(WIN_UNIT_PAIRS)]
            qs = [q_ref[pl.ds(q0, Q_TILE), c:c + LANES] for c in cols]
            k = kv_ref[pl.ds(ks, WIN_WINDOW), u * LANES:(u + 1) * LANES]
            v = kv_ref[pl.ds(ks, WIN_WINDOW), WIN_KV + u * LANES:WIN_KV + (u + 1) * LANES]
            num, den, _ = _pair_attention(qs, k, v, bias_ref[var, u], ones_ref[...], sink_rows_ref[u])
            o = (num / den).astype(BF16)
            for i, c in enumerate(cols):
                o_ref[pl.ds(q0, Q_TILE), c:c + LANES] = o[i * Q_TILE:(i + 1) * Q_TILE]
        return carry

    lax.fori_loop(0, n_tiles, tile, 0)


def _window_attention(q, kv, sink, batch, seq):
    unit_rows = 2 * WIN_UNIT_PAIRS * Q_TILE
    return pl.pallas_call(
        functools.partial(_window_kernel, seq),
        grid=(batch,),
        in_specs=[pl.BlockSpec((None, seq, WIN_Q), lambda b: (b, 0, 0)),
                  pl.BlockSpec((None, seq, 2 * WIN_KV), lambda b: (b, 0, 0)),
                  pl.BlockSpec((1, WIN_Q_HEADS), lambda b: (0, 0))],
        out_specs=pl.BlockSpec((None, seq, WIN_Q), lambda b: (b, 0, 0)),
        out_shape=jax.ShapeDtypeStruct((batch, seq, WIN_Q), BF16),
        scratch_shapes=[pltpu.VMEM((3, WIN_UNITS, unit_rows, WIN_WINDOW), F32),
                        pltpu.VMEM((2 * WIN_WINDOW, LANES), BF16),
                        pltpu.VMEM((WIN_UNITS, unit_rows, LANES), F32)],
        compiler_params=pltpu.CompilerParams(
            dimension_semantics=("arbitrary",), vmem_limit_bytes=VMEM_LIMIT),
        name="window",
    )(q.reshape(batch, seq, WIN_Q), kv.reshape(batch, seq, 2 * WIN_KV), sink)


def _merge_kernel(oa_ref, ow_ref, gate_ref, h_ref, wa_ref, wb_ref, wo_ref, out_ref):
    ya = _dot(oa_ref[...], wa_ref[...])
    yb = _dot(ow_ref[...], wb_ref[...])
    merged = gate_ref[:, :D_MODEL].astype(F32) * ya + gate_ref[:, D_MODEL:].astype(F32) * yb
    out_ref[...] = h_ref[...] + _dot(merged.astype(BF16), wo_ref[...])


def _merge_out(oa, ow, gates, h, wa, wb, wo):
    m = h.shape[0]
    row = lambda c: pl.BlockSpec((ROW_TILE, c), lambda i: (i, 0))
    full = lambda a: pl.BlockSpec(a.shape, lambda i: (0, 0), pipeline_mode=pl.Buffered(1))
    return pl.pallas_call(
        _merge_kernel,
        grid=(m // ROW_TILE,),
        in_specs=[row(DIL_OUT), row(WIN_Q), row(2 * D_MODEL), row(D_MODEL), full(wa), full(wb), full(wo)],
        out_specs=row(D_MODEL),
        out_shape=jax.ShapeDtypeStruct((m, D_MODEL), F32),
        compiler_params=pltpu.CompilerParams(
            dimension_semantics=("arbitrary",), vmem_limit_bytes=VMEM_LIMIT),
        name="merge_out",
    )(oa, ow, gates, h, wa, wb, wo)


TOKEN_TILE = 256
CUM_CHUNK = TOKEN_TILE
SLOT_WINDOW = 64
SLOT_ALIGN = 16


def _prefix_exclusive(mask_f, tri):
    e, s = mask_f.shape
    carry = jnp.zeros((e, 1), F32)
    parts, carries = [], []
    for j in range(0, s, CUM_CHUNK):
        blk = mask_f[:, j:j + CUM_CHUNK]
        carries.append(carry)
        parts.append(_dot(blk.astype(BF16), tri) + carry)
        carry = carry + jnp.sum(blk, axis=-1, keepdims=True)
    return jnp.concatenate(parts, axis=-1), carries + [carry]


def _route_kernel(cap, h_ref, g_ref, whi_ref, wlo_ref, hn_ref, rank_ref, aff_ref, starts_ref):
    seq = h_ref.shape[0]
    hn = _rms(h_ref[...], g_ref[...])
    hn_hi = hn.astype(BF16)
    hn_lo = (hn - hn_hi.astype(F32)).astype(BF16)
    hn_ref[...] = hn_hi
    logits = (_dot_nt(whi_ref[...], hn_hi) + _dot_nt(whi_ref[...], hn_lo)
              + _dot_nt(wlo_ref[...], hn_hi))
    mx = jnp.max(logits, axis=0, keepdims=True)
    ex = jnp.exp(logits - mx)
    aff = ex / jnp.sum(ex, axis=0, keepdims=True)
    bits = pltpu.bitcast(aff, jnp.int32)

    def search(_, c):
        lo, hi = c
        mid = lo + ((hi - lo) >> 1)
        cnt = jnp.sum(jnp.where(bits >= mid, 1.0, 0.0), axis=-1, keepdims=True)
        ok = cnt >= float(cap)
        return jnp.where(ok, mid, lo), jnp.where(ok, hi, mid)

    lo0 = jnp.zeros((N_EXPERTS, 1), jnp.int32)
    hi0 = jnp.full((N_EXPERTS, 1), 0x3F800001, jnp.int32)
    thr, _ = lax.fori_loop(0, 31, search, (lo0, hi0))

    r_i = lax.broadcasted_iota(jnp.int32, (CUM_CHUNK, CUM_CHUNK), 0)
    c_i = lax.broadcasted_iota(jnp.int32, (CUM_CHUNK, CUM_CHUNK), 1)
    tri = jnp.where(r_i < c_i, 1.0, 0.0).astype(BF16)
    gt = jnp.where(bits > thr, 1.0, 0.0)
    eq = jnp.where(bits == thr, 1.0, 0.0)
    need = float(cap) - jnp.sum(gt, axis=-1, keepdims=True)
    tie_rank, _ = _prefix_exclusive(eq, tri)
    sel = gt + eq * jnp.where(tie_rank < need, 1.0, 0.0)
    slot, starts = _prefix_exclusive(sel, tri)
    rank = jnp.where(sel > 0.0, slot, -1.0)
    for t in range(seq // TOKEN_TILE):
        rank_ref[t] = rank[:, t * TOKEN_TILE:(t + 1) * TOKEN_TILE]
        aff_ref[t] = aff[:, t * TOKEN_TILE:(t + 1) * TOKEN_TILE]
    lane = lax.broadcasted_iota(jnp.int32, (N_EXPERTS, LANES), 1)
    acc = jnp.zeros((N_EXPERTS, LANES), F32)
    for t, c in enumerate(starts):
        acc = acc + jnp.where(lane == t, c, 0.0)
    starts_ref[...] = acc


def _route(h, g, w_hi, w_lo, batch, seq, cap):
    n_tiles = seq // TOKEN_TILE
    return pl.pallas_call(
        functools.partial(_route_kernel, cap),
        grid=(batch,),
        in_specs=[pl.BlockSpec((None, seq, D_MODEL), lambda b: (b, 0, 0)),
                  pl.BlockSpec((1, D_MODEL), lambda b: (0, 0)),
                  pl.BlockSpec((N_EXPERTS, D_MODEL), lambda b: (0, 0)),
                  pl.BlockSpec((N_EXPERTS, D_MODEL), lambda b: (0, 0))],
        out_specs=[pl.BlockSpec((None, seq, D_MODEL), lambda b: (b, 0, 0)),
                   pl.BlockSpec((None, n_tiles, N_EXPERTS, TOKEN_TILE), lambda b: (b, 0, 0, 0)),
                   pl.BlockSpec((None, n_tiles, N_EXPERTS, TOKEN_TILE), lambda b: (b, 0, 0, 0)),
                   pl.BlockSpec((None, N_EXPERTS, LANES), lambda b: (b, 0, 0))],
        out_shape=[jax.ShapeDtypeStruct((batch, seq, D_MODEL), BF16),
                   jax.ShapeDtypeStruct((batch, n_tiles, N_EXPERTS, TOKEN_TILE), F32),
                   jax.ShapeDtypeStruct((batch, n_tiles, N_EXPERTS, TOKEN_TILE), F32),
                   jax.ShapeDtypeStruct((batch, N_EXPERTS, LANES), F32)],
        compiler_params=pltpu.CompilerParams(
            dimension_semantics=("arbitrary",), vmem_limit_bytes=VMEM_LIMIT),
        name="route",
    )(h.reshape(batch, seq, D_MODEL), g, w_hi, w_lo)


def _slot_rows(cap):
    return cap + SLOT_WINDOW


def _window_starts(win_ref, b, t, p, cap):
    base = b * (N_EXPERTS * LANES) + t
    return [pl.multiple_of(jnp.minimum(win_ref[base + e * LANES] + p * SLOT_WINDOW, cap), SLOT_ALIGN)
            for e in range(N_EXPERTS)]


def _window_hits(rank_tile, starts):
    rows = lax.broadcasted_iota(jnp.int32, (SLOT_WINDOW, TOKEN_TILE), 0).astype(F32)
    return [(rank_tile[e:e + 1, :] - starts[e].astype(F32)) == rows for e in range(N_EXPERTS)]


def _one_hot(hits):
    return jnp.concatenate([jnp.where(h, 1.0, 0.0).astype(BF16) for h in hits], axis=0)


def _gather_kernel(cap, win_ref, npass_ref, hn_ref, rank_ref, aff_ref, xe_ref, gate_ref):
    b = pl.program_id(0)
    xe_ref[...] = jnp.zeros_like(xe_ref)
    gate_ref[...] = jnp.zeros_like(gate_ref)
    for t in range(hn_ref.shape[0] // TOKEN_TILE):

        def one_pass(p, carry, t=t):
            starts = _window_starts(win_ref, b, t, p, cap)
            hits = _window_hits(rank_ref[t], starts)
            rows = _dot(_one_hot(hits), hn_ref[t * TOKEN_TILE:(t + 1) * TOKEN_TILE, :]).astype(BF16)
            aff_tile = aff_ref[t]
            for e in range(N_EXPERTS):
                win = pl.ds(starts[e], SLOT_WINDOW)
                xe_ref[e, win, :] += rows[e * SLOT_WINDOW:(e + 1) * SLOT_WINDOW]
                g = jnp.sum(jnp.where(hits[e], aff_tile[e:e + 1, :], 0.0), axis=-1, keepdims=True)
                gate_ref[e, win, :] += jnp.broadcast_to(g, (SLOT_WINDOW, LANES))
            return carry

        lax.fori_loop(0, npass_ref[b * LANES + t], one_pass, 0)


def _gather(hn, rank, aff, win, npass, cap):
    batch, seq, _ = hn.shape
    n_tiles = seq // TOKEN_TILE
    grid_spec = pltpu.PrefetchScalarGridSpec(
        num_scalar_prefetch=2,
        grid=(batch,),
        in_specs=[pl.BlockSpec((None, seq, D_MODEL), lambda b, *_: (b, 0, 0)),
                  pl.BlockSpec((None, n_tiles, N_EXPERTS, TOKEN_TILE), lambda b, *_: (b, 0, 0, 0)),
                  pl.BlockSpec((None, n_tiles, N_EXPERTS, TOKEN_TILE), lambda b, *_: (b, 0, 0, 0))],
        out_specs=[pl.BlockSpec((None, N_EXPERTS, _slot_rows(cap), D_MODEL), lambda b, *_: (b, 0, 0, 0)),
                   pl.BlockSpec((None, N_EXPERTS, _slot_rows(cap), LANES), lambda b, *_: (b, 0, 0, 0))])
    return pl.pallas_call(
        functools.partial(_gather_kernel, cap),
        grid_spec=grid_spec,
        out_shape=[jax.ShapeDtypeStruct((batch, N_EXPERTS, _slot_rows(cap), D_MODEL), BF16),
                   jax.ShapeDtypeStruct((batch, N_EXPERTS, _slot_rows(cap), LANES), F32)],
        compiler_params=pltpu.CompilerParams(
            dimension_semantics=("arbitrary",), vmem_limit_bytes=VMEM_LIMIT),
        name="gather",
    )(win, npass, hn, rank, aff)


FFN_SEQS = 4


def _ffn_kernel(cap, xe_ref, gate_ref, wg_ref, wu_ref, wd_ref, y_ref):
    d = xe_ref.shape[-1]
    xe = xe_ref[:, :cap, :].reshape(FFN_SEQS * cap, d)
    a = _dot(xe, wg_ref[...])
    u = _dot(xe, wu_ref[...])
    y = _dot((jax.nn.silu(a) * u).astype(BF16), wd_ref[...])
    gate = gate_ref[:, :cap, :].reshape(FFN_SEQS * cap, LANES)
    y = y * jnp.concatenate([gate] * (d // LANES), axis=1)
    y_ref[:, :cap, :] = y.astype(BF16).reshape(FFN_SEQS, cap, d)
    y_ref[:, cap:, :] = jnp.zeros((FFN_SEQS, SLOT_WINDOW, d), BF16)


def _ffn(xe, gate, wg, wu, wd, cap):
    batch = xe.shape[0]
    d_exp = wg.shape[-1]
    slots = pl.BlockSpec((FFN_SEQS, None, _slot_rows(cap), D_MODEL), lambda e, i: (i, e, 0, 0))
    return pl.pallas_call(
        functools.partial(_ffn_kernel, cap),
        grid=(N_EXPERTS, batch // FFN_SEQS),
        in_specs=[slots,
                  pl.BlockSpec((FFN_SEQS, None, _slot_rows(cap), LANES), lambda e, i: (i, e, 0, 0)),
                  pl.BlockSpec((None, D_MODEL, d_exp), lambda e, i: (e, 0, 0)),
                  pl.BlockSpec((None, D_MODEL, d_exp), lambda e, i: (e, 0, 0)),
                  pl.BlockSpec((None, d_exp, D_MODEL), lambda e, i: (e, 0, 0))],
        out_specs=slots,
        out_shape=jax.ShapeDtypeStruct(xe.shape, BF16),
        compiler_params=pltpu.CompilerParams(
            dimension_semantics=("arbitrary", "arbitrary"), vmem_limit_bytes=VMEM_LIMIT),
        name="ffn",
    )(xe, gate, wg, wu, wd)


def _scatter_kernel(cap, win_ref, npass_ref, y_ref, rank_ref, h_ref, out_ref):
    b = pl.program_id(0)
    t = pl.program_id(1)
    out_ref[...] = h_ref[...]

    def one_pass(p, carry):
        starts = _window_starts(win_ref, b, t, p, cap)
        put = _one_hot(_window_hits(rank_ref[...], starts))
        yw = jnp.concatenate([y_ref[e, pl.ds(starts[e], SLOT_WINDOW), :] for e in range(N_EXPERTS)], axis=0)
        out_ref[...] += lax.dot_general(put, yw, (((0,), (0,)), ((), ())), preferred_element_type=F32)
        return carry

    lax.fori_loop(0, npass_ref[b * LANES + t], one_pass, 0)


def _scatter(y, rank, h, win, npass, cap):
    batch, n_tiles = rank.shape[:2]
    grid_spec = pltpu.PrefetchScalarGridSpec(
        num_scalar_prefetch=2,
        grid=(batch, n_tiles),
        in_specs=[pl.BlockSpec((None, N_EXPERTS, _slot_rows(cap), D_MODEL), lambda b, t, *_: (b, 0, 0, 0)),
                  pl.BlockSpec((None, None, N_EXPERTS, TOKEN_TILE), lambda b, t, *_: (b, t, 0, 0)),
                  pl.BlockSpec((None, TOKEN_TILE, D_MODEL), lambda b, t, *_: (b, t, 0))],
        out_specs=pl.BlockSpec((None, TOKEN_TILE, D_MODEL), lambda b, t, *_: (b, t, 0)))
    return pl.pallas_call(
        functools.partial(_scatter_kernel, cap),
        grid_spec=grid_spec,
        out_shape=jax.ShapeDtypeStruct(h.shape, F32),
        compiler_params=pltpu.CompilerParams(
            dimension_semantics=("arbitrary", "arbitrary"), vmem_limit_bytes=VMEM_LIMIT),
        name="scatter",
    )(win, npass, y, rank, h)


def _slot_windows(starts, n_tiles, cap):
    s = starts.astype(jnp.int32)
    first = (s // SLOT_ALIGN) * SLOT_ALIGN
    span = s[:, :, 1:n_tiles + 1] - first[:, :, :n_tiles]
    npass = jnp.maximum(jnp.max(-(-span // SLOT_WINDOW), axis=1), 1)
    npass = jnp.pad(npass, ((0, 0), (0, LANES - n_tiles)))
    return first.reshape(-1), npass.reshape(-1)


def _final_kernel(x_ref, g_ref, o_ref):
    o_ref[...] = _rms(x_ref[...], g_ref[...])


def _final_norm(x, g):
    m = x.shape[0]
    row = pl.BlockSpec((ROW_TILE, D_MODEL), lambda i: (i, 0))
    return pl.pallas_call(
        _final_kernel,
        grid=(m // ROW_TILE,),
        in_specs=[row, pl.BlockSpec((1, D_MODEL), lambda i: (0, 0))],
        out_specs=row,
        out_shape=jax.ShapeDtypeStruct((m, D_MODEL), F32),
        compiler_params=pltpu.CompilerParams(dimension_semantics=("arbitrary",)),
        name="final_norm",
    )(x, g)


def _arrange_w_in(w):
    scale = LOG2E * HEAD_DIM ** -0.5
    qa, ka, va = w[:, :DIL_QKV], w[:, DIL_QKV:2 * DIL_QKV], w[:, 2 * DIL_QKV:3 * DIL_QKV]
    rest = w[:, 3 * DIL_QKV:]
    parts = []
    for g in range(len(DIL_PATTERNS)):
        sl = slice(g * DIL_OUT, (g + 1) * DIL_OUT)
        parts += [qa[:, sl] * scale, ka[:, sl], va[:, sl]]
    for pr in _window_pair_heads():
        parts += [rest[:, h * HEAD_DIM:(h + 1) * HEAD_DIM] * scale for h in pr]
    parts.append(rest[:, WIN_Q:])
    return jnp.concatenate(parts, axis=1).astype(BF16)


def _arrange_w_branch_b(w):
    return jnp.concatenate([w[h * HEAD_DIM:(h + 1) * HEAD_DIM] for pr in _window_pair_heads() for h in pr],
                           axis=0).astype(BF16)


def kernel(x, norm_mix, w_in, w_branch_a, w_branch_b, b_gate, sink_logit, w_out, norm_ffn, w_router,
           w_expert_gate, w_expert_up, w_expert_down, norm_final):
    batch, seq, d = x.shape
    depth = w_in.shape[0]
    cap = CAPACITY_FACTOR * seq // N_EXPERTS
    m = batch * seq
    h = x.reshape(m, d)
    for l in range(depth):
        a0, a1, a2, qw, kv, gates = _norm_proj(h, norm_mix[l][None, :], _arrange_w_in(w_in[l]),
                                               b_gate[l][None, :])
        oa = _dilated_attention(a0, a1, a2, batch, seq).reshape(m, DIL_OUT)
        ow = _window_attention(qw, kv, sink_logit[l][None, :], batch, seq).reshape(m, WIN_Q)
        h = _merge_out(oa, ow, gates, h, w_branch_a[l].astype(BF16),
                       _arrange_w_branch_b(w_branch_b[l]), w_out[l].astype(BF16))
        wr = w_router[l].T
        wr_hi = wr.astype(BF16)
        wr_lo = (wr - wr_hi.astype(F32)).astype(BF16)
        hn, rank, aff, starts = _route(h, norm_ffn[l][None, :], wr_hi, wr_lo, batch, seq, cap)
        win, npass = _slot_windows(starts, seq // TOKEN_TILE, cap)
        xe, gate = _gather(hn, rank, aff, win, npass, cap)
        y = _ffn(xe, gate, w_expert_gate[l].astype(BF16), w_expert_up[l].astype(BF16),
                 w_expert_down[l].astype(BF16), cap)
        h = _scatter(y, rank, h.reshape(batch, seq, d), win, npass, cap).reshape(m, d)
    return _final_norm(h, norm_final[None, :]).reshape(batch, seq, d)
```

```python
import functools

import jax
import jax.numpy as jnp
from jax import lax
from jax.experimental import pallas as pl
from jax.experimental.pallas import tpu as pltpu

D_MODEL = 1024
HEAD_DIM = 64
DIL_PATTERNS = ((128, 1), (512, 4), (2048, 16))
DIL_HEADS = 4
N_DIL_SUB = DIL_HEADS * len(DIL_PATTERNS)
DIL_QKV = N_DIL_SUB * HEAD_DIM
DIL_OUT = DIL_HEADS * HEAD_DIM
DIL_HALF = 64
WIN_HALF = 128
WIN_Q = D_MODEL
WIN_Q_HEADS = WIN_Q // HEAD_DIM
WIN_KV_HEADS = 4
WIN_KV = WIN_KV_HEADS * HEAD_DIM
N_EXPERTS = 16
CAPACITY_FACTOR = 2
RMS_EPS = 1e-6
NEG_INF = -1e30
LOG2E = 1.4426950408889634

LANES = 128
GROUP_COLS = 3 * DIL_OUT
DIL_PAIRS = DIL_OUT // LANES
WIN_UNITS = WIN_KV_HEADS // 2
WIN_UNIT_PAIRS = WIN_Q_HEADS // (2 * WIN_UNITS)
Q_TILE = 128
DIL_INFLIGHT = 8
ROW_TILE = 512
VMEM_LIMIT = 56 * 1024 * 1024

F32 = jnp.float32
BF16 = jnp.bfloat16


def _alibi_slopes(n):
    return [float(2.0 ** (-8.0 * i / n)) for i in range(1, n + 1)]


def _rms(x, g):
    return x * lax.rsqrt(jnp.mean(x * x, axis=-1, keepdims=True) + RMS_EPS) * g


def _dot(a, b):
    return jnp.dot(a, b, preferred_element_type=F32)


def _dot_nt(a, b):
    return lax.dot_general(a, b, (((1,), (1,)), ((), ())), preferred_element_type=F32)


def _left_lanes():
    return lax.broadcasted_iota(jnp.int32, (1, LANES), 1) < HEAD_DIM


_C_A = (0, GROUP_COLS, 2 * GROUP_COLS)
_C_QW = 3 * GROUP_COLS
_C_KV = _C_QW + WIN_Q
_C_GATE = _C_KV + 2 * WIN_KV
_PROJ_COLS = _C_GATE + 2 * D_MODEL
_COL_CHUNK = 512


def _proj_store(ref, xb, w_ref, c0, cw):
    for j in range(0, cw, _COL_CHUNK):
        jw = min(_COL_CHUNK, cw - j)
        ref[:, j:j + jw] = _dot(xb, w_ref[:, c0 + j:c0 + j + jw]).astype(BF16)


def _norm_proj_kernel(x_ref, g_ref, w_ref, bg_ref, a0_ref, a1_ref, a2_ref, qw_ref, kv_ref, gate_ref, xs_ref):
    xn = _rms(x_ref[...], g_ref[...])
    n_lane_tiles = D_MODEL // LANES
    for j in range(n_lane_tiles):
        xs_ref[j] = xn[:, j * LANES:(j + 1) * LANES]
    xb = xn.astype(BF16)
    _proj_store(a0_ref, xb, w_ref, _C_A[0], GROUP_COLS)
    for grp, ref in ((1, a1_ref), (2, a2_ref)):
        r = DIL_PATTERNS[grp][1]
        n = ROW_TILE // r
        xp = jnp.concatenate(
            [jnp.concatenate([xs_ref[j, pl.ds(c, n, stride=r), :] for j in range(n_lane_tiles)], axis=1)
             for c in range(r)], axis=0).astype(BF16)
        res = _dot(xp, w_ref[:, _C_A[grp]:_C_A[grp] + GROUP_COLS])
        for c in range(r):
            ref[:, c * GROUP_COLS:(c + 1) * GROUP_COLS] = res[c * n:(c + 1) * n].astype(BF16)
    _proj_store(qw_ref, xb, w_ref, _C_QW, WIN_Q)
    _proj_store(kv_ref, xb, w_ref, _C_KV, 2 * WIN_KV)
    for j in range(0, 2 * D_MODEL, _COL_CHUNK):
        z = _dot(xb, w_ref[:, _C_GATE + j:_C_GATE + j + _COL_CHUNK]) + bg_ref[:, j:j + _COL_CHUNK]
        gate_ref[:, j:j + _COL_CHUNK] = jax.nn.sigmoid(z).astype(BF16)


def _norm_proj(x, g, w, bg):
    m = x.shape[0]
    row = lambda c: pl.BlockSpec((ROW_TILE, c), lambda i: (i, 0))
    full = lambda a: pl.BlockSpec(a.shape, lambda i: (0, 0), pipeline_mode=pl.Buffered(1))
    in_specs = [row(D_MODEL), full(g), full(w), full(bg)]
    out_shape, out_specs = [], []
    for _, r in DIL_PATTERNS:
        out_shape.append(jax.ShapeDtypeStruct((m // r, r * GROUP_COLS), BF16))
        out_specs.append(pl.BlockSpec((ROW_TILE // r, r * GROUP_COLS), lambda i: (i, 0)))
    for c in (WIN_Q, 2 * WIN_KV, 2 * D_MODEL):
        out_shape.append(jax.ShapeDtypeStruct((m, c), BF16))
        out_specs.append(row(c))
    return pl.pallas_call(
        _norm_proj_kernel,
        grid=(m // ROW_TILE,),
        in_specs=in_specs,
        out_specs=out_specs,
        out_shape=out_shape,
        scratch_shapes=[pltpu.VMEM((D_MODEL // LANES, ROW_TILE, LANES), F32)],
        compiler_params=pltpu.CompilerParams(
            dimension_semantics=("arbitrary",), vmem_limit_bytes=VMEM_LIMIT),
        name="norm_proj",
    )(x, g, w, bg)


def _band_bias(tq, win, off, half, slopes):
    col = lax.broadcasted_iota(jnp.int32, (tq, win), 1)
    row = lax.broadcasted_iota(jnp.int32, (tq, win), 0)
    dist = jnp.abs(col - row + off).astype(F32)
    inside = dist <= float(half)
    return jnp.concatenate([jnp.where(inside, dist * (-s * LOG2E), NEG_INF) for s in slopes], axis=0)


def _ones_blockdiag(win):
    left = lax.broadcasted_iota(jnp.int32, (2 * win, LANES), 1) < HEAD_DIM
    top = jnp.where(lax.broadcasted_iota(jnp.int32, (2 * win, LANES), 0) < win, 1.0, 0.0)
    return jnp.where(left, top, 1.0 - top).astype(BF16)


def _pair_scores(q_tiles, k_win, bias):
    left = _left_lanes()
    zero = jnp.zeros((), BF16)
    rows = []
    for q in q_tiles:
        rows += [jnp.where(left, q, zero), jnp.where(left, zero, q)]
    return _dot_nt(jnp.concatenate(rows, axis=0), k_win) + bias


def _pair_values(s_ref, v_win, ones_bd, sink_ref, p_ref, aux_ref):
    win = v_win.shape[0]
    n = s_ref.shape[0] // (2 * Q_TILE)
    left = _left_lanes()
    zero = jnp.zeros((), BF16)
    for i in range(n):
        ms = []
        for side in range(2):
            rows = pl.ds((2 * i + side) * Q_TILE, Q_TILE)
            s = s_ref[rows, :]
            m = jnp.broadcast_to(jnp.max(s, axis=-1, keepdims=True), (Q_TILE, LANES))
            if sink_ref is not None:
                m = jnp.maximum(m, sink_ref[rows, :])
            p = jnp.exp2(s - jnp.concatenate([m] * (win // LANES), axis=1)).astype(BF16)
            p_ref[i * Q_TILE:(i + 1) * Q_TILE, side * win:(side + 1) * win] = p
            ms.append(m)
        m_pair = jnp.where(left, ms[0], ms[1])
        if sink_ref is not None:
            sink_pair = jnp.where(left, sink_ref[pl.ds(2 * i * Q_TILE, Q_TILE), :],
                                  sink_ref[pl.ds((2 * i + 1) * Q_TILE, Q_TILE), :])
            aux_ref[i * Q_TILE:(i + 1) * Q_TILE, :] = jnp.exp2(sink_pair - m_pair)
        else:
            aux_ref[i * Q_TILE:(i + 1) * Q_TILE, :] = m_pair
    vbd = jnp.concatenate([jnp.where(left, v_win, zero), jnp.where(left, zero, v_win)], axis=0)
    on = _dot(p_ref[...], jnp.concatenate([vbd, ones_bd], axis=1))
    return on[:, :LANES], on[:, LANES:], aux_ref[...]


def _tile_variant(t, n_tiles):
    return jnp.where(t == 0, 0, jnp.where(t == n_tiles - 1, 2, 1))


def _pipelined_tiles(n_tiles, score_stage, value_stage):
    score_stage(0, 0)

    def body(i, carry):
        t = 2 * i
        score_stage(t + 1, 1)
        value_stage(t, 0)
        score_stage(jnp.minimum(t + 2, n_tiles - 1), 0)
        value_stage(t + 1, 1)
        return carry

    lax.fori_loop(0, n_tiles // 2, body, 0)


def _dil_geometry(seq_full, grp):
    r = DIL_PATTERNS[grp][1]
    seq = seq_full // r
    win = min(seq, Q_TILE + 2 * DIL_HALF)
    n_tiles = seq // Q_TILE
    pad = (win - Q_TILE) // 2
    offs = [0] if n_tiles == 1 else [0, -pad, -2 * pad]
    return r, seq, win, n_tiles, pad, offs


def _dilated_kernel(seq_full, a0_ref, a1_ref, a2_ref, out_ref, o_nat, l_nat, b0, b1, b2, ones_ref, ones2_ref,
                    s_wide, s_narrow, p_wide, p_narrow, aux_ref):
    a_refs = (a0_ref, a1_ref, a2_ref)
    bias_refs = (b0, b1, b2)
    s_refs = (s_wide, s_wide, s_narrow)
    p_refs = (p_wide, p_wide, p_narrow)
    slopes = _alibi_slopes(N_DIL_SUB)

    @pl.when(pl.program_id(0) == 0)
    def _():
        for grp in range(3):
            r, seq, win, n_tiles, pad, offs = _dil_geometry(seq_full, grp)
            for v, off in enumerate(offs):
                for pr in range(DIL_PAIRS):
                    hs = slopes[grp * DIL_HEADS + 2 * pr:grp * DIL_HEADS + 2 * pr + 2]
                    bias_refs[grp][v, pr] = _band_bias(Q_TILE, win, off, DIL_HALF, [s * r for s in hs])
        ones_ref[...] = _ones_blockdiag(_dil_geometry(seq_full, 0)[2])
        ones2_ref[...] = _ones_blockdiag(_dil_geometry(seq_full, 2)[2])

    def pair_tile(grp, c, t, pr, slot):
        r, seq, win, n_tiles, pad, offs = _dil_geometry(seq_full, grp)
        a_ref = a_refs[grp]
        ones_bd_ref = ones2_ref if grp == 2 else ones_ref
        if n_tiles == 1:
            q0, ks, var = 0, 0, 0
        else:
            q0 = pl.multiple_of(t * Q_TILE, Q_TILE)
            ks = pl.multiple_of(jnp.clip(q0 - pad, 0, seq - win), DIL_HALF)
            var = _tile_variant(t, n_tiles)
        lo = c * GROUP_COLS + pr * LANES
        q = a_ref[pl.ds(q0, Q_TILE), lo:lo + LANES]
        k = a_ref[pl.ds(ks, win), DIL_OUT + lo:DIL_OUT + lo + LANES]
        v = a_ref[pl.ds(ks, win), 2 * DIL_OUT + lo:2 * DIL_OUT + lo + LANES]
        s_ref = s_refs[grp].at[slot]
        s_ref[...] = _pair_scores([q], k, bias_refs[grp][var, pr])
        num, den, m = _pair_values(s_ref, v, ones_bd_ref[...], None, p_refs[grp].at[slot], aux_ref.at[slot])
        rows = pl.ds(q0, Q_TILE) if r == 1 else pl.ds(c + r * q0, Q_TILE, stride=r)
        o_nat[grp, pr, rows, :] = num / den
        l_nat[grp, pr, rows, :] = m + jnp.log2(den)

    for grp in range(3):
        r, seq, win, n_tiles, pad, offs = _dil_geometry(seq_full, grp)
        per_body = DIL_INFLIGHT // DIL_PAIRS
        if n_tiles == 1:
            for c in range(r):
                for pr in range(DIL_PAIRS):
                    pair_tile(grp, c, 0, pr, c * DIL_PAIRS + pr)
        elif r >= per_body:
            for c0 in range(0, r, per_body):
                def body(t, carry, grp=grp, c0=c0):
                    for j in range(per_body):
                        for pr in range(DIL_PAIRS):
                            pair_tile(grp, c0 + j, t, pr, j * DIL_PAIRS + pr)
                    return carry
                lax.fori_loop(0, n_tiles, body, 0)
        else:
            tiles_per_body = per_body // r
            def body(i, carry, grp=grp, r=r, tiles_per_body=tiles_per_body):
                for j in range(tiles_per_body):
                    for c in range(r):
                        for pr in range(DIL_PAIRS):
                            pair_tile(grp, c, i * tiles_per_body + j, pr, (j * r + c) * DIL_PAIRS + pr)
                return carry
            lax.fori_loop(0, n_tiles // tiles_per_body, body, 0)

    def combine(i, carry):
        rows = pl.ds(pl.multiple_of(i * ROW_TILE, ROW_TILE), ROW_TILE)
        for pr in range(DIL_PAIRS):
            ls = [l_nat[g, pr, rows, :] for g in range(3)]
            mx = jnp.maximum(jnp.maximum(ls[0], ls[1]), ls[2])
            es = [jnp.exp2(l - mx) for l in ls]
            num = es[0] * o_nat[0, pr, rows, :] + es[1] * o_nat[1, pr, rows, :] + es[2] * o_nat[2, pr, rows, :]
            out_ref[rows, pr * LANES:(pr + 1) * LANES] = (num / (es[0] + es[1] + es[2])).astype(BF16)
        return carry

    lax.fori_loop(0, seq_full // ROW_TILE, combine, 0)


def _dilated_attention(a0, a1, a2, batch, seq_full):
    views, in_specs, bias_shapes = [], [], []
    for grp, a in enumerate((a0, a1, a2)):
        r, seq, win, n_tiles, pad, offs = _dil_geometry(seq_full, grp)
        views.append(a.reshape(batch, seq, r * GROUP_COLS))
        in_specs.append(pl.BlockSpec((None, seq, r * GROUP_COLS), lambda b: (b, 0, 0)))
        bias_shapes.append(pltpu.VMEM((len(offs), DIL_PAIRS, 2 * Q_TILE, win), F32))
    win0 = _dil_geometry(seq_full, 0)[2]
    win2 = _dil_geometry(seq_full, 2)[2]
    n_straight = DIL_PATTERNS[2][1] * DIL_PAIRS
    return pl.pallas_call(
        functools.partial(_dilated_kernel, seq_full),
        grid=(batch,),
        in_specs=in_specs,
        out_specs=pl.BlockSpec((None, seq_full, DIL_OUT), lambda b: (b, 0, 0)),
        out_shape=jax.ShapeDtypeStruct((batch, seq_full, DIL_OUT), BF16),
        scratch_shapes=[pltpu.VMEM((3, DIL_PAIRS, seq_full, LANES), F32),
                        pltpu.VMEM((3, DIL_PAIRS, seq_full, LANES), F32)]
        + bias_shapes + [pltpu.VMEM((2 * win0, LANES), BF16), pltpu.VMEM((2 * win2, LANES), BF16)]
        + [pltpu.VMEM((n, 2 * Q_TILE, w), F32) for n, w in ((DIL_INFLIGHT, win0), (n_straight, win2))]
        + [pltpu.VMEM((n, Q_TILE, 2 * w), BF16) for n, w in ((DIL_INFLIGHT, win0), (n_straight, win2))]
        + [pltpu.VMEM((max(DIL_INFLIGHT, n_straight), Q_TILE, LANES), F32)],
        compiler_params=pltpu.CompilerParams(
            dimension_semantics=("arbitrary",), vmem_limit_bytes=VMEM_LIMIT),
        name="dilated",
    )(*views)


WIN_WINDOW = Q_TILE + 2 * WIN_HALF


def _window_pair_heads():
    group = WIN_Q_HEADS // WIN_KV_HEADS
    pairs = []
    for u in range(WIN_UNITS):
        for i in range(WIN_UNIT_PAIRS):
            pairs.append((2 * u * group + i, (2 * u + 1) * group + i))
    return pairs


def _window_kernel(seq, q_ref, kv_ref, sink_ref, o_ref, bias_ref, ones_ref, sink_rows_ref, s_ref, p_ref,
                   aux_ref):
    n_tiles = seq // Q_TILE
    slopes = _alibi_slopes(WIN_Q_HEADS)
    pair_heads = _window_pair_heads()
    unit_rows = 2 * WIN_UNIT_PAIRS * Q_TILE

    @pl.when(pl.program_id(0) == 0)
    def _():
        for u in range(WIN_UNITS):
            heads = [h for pr in pair_heads[u * WIN_UNIT_PAIRS:(u + 1) * WIN_UNIT_PAIRS] for h in pr]
            for v, off in enumerate((0, -WIN_HALF, -2 * WIN_HALF)):
                bias_ref[v, u] = _band_bias(Q_TILE, WIN_WINDOW, off, WIN_HALF, [slopes[h] for h in heads])
            sink_rows_ref[u] = jnp.concatenate(
                [jnp.broadcast_to(sink_ref[0:1, h:h + 1] * LOG2E, (Q_TILE, LANES)) for h in heads], axis=0)
        ones_ref[...] = _ones_blockdiag(WIN_WINDOW)

    def rows_of(t):
        q0 = pl.multiple_of(t * Q_TILE, Q_TILE)
        ks = pl.multiple_of(jnp.clip(q0 - WIN_HALF, 0, seq - WIN_WINDOW), Q_TILE)
        return q0, ks

    def unit_cols(u):
        return [(u * WIN_UNIT_PAIRS + i) * LANES for i in range(WIN_UNIT_PAIRS)]

    def score_stage(t, k_set):
        q0, ks = rows_of(t)
        var = _tile_variant(t, n_tiles)
        for u in range(WIN_UNITS):
            qs = [q_ref[pl.ds(q0, Q_TILE), c:c + LANES] for c in unit_cols(u)]
            k = kv_ref[pl.ds(ks, WIN_WINDOW), u * LANES:(u + 1) * LANES]
            s_ref[k_set * WIN_UNITS + u] = _pair_scores(qs, k, bias_ref[var, u])

    def value_stage(t, k_set):
        q0, ks = rows_of(t)
        for u in range(WIN_UNITS):
            slot = k_set * WIN_UNITS + u
            v = kv_ref[pl.ds(ks, WIN_WINDOW), WIN_KV + u * LANES:WIN_KV + (u + 1) * LANES]
            num, den, sink_term = _pair_values(s_ref.at[slot], v, ones_ref[...], sink_rows_ref.at[u],
                                               p_ref.at[slot], aux_ref.at[slot])
            o = (num / (den + sink_term)).astype(BF16)
            for i, c in enumerate(unit_cols(u)):
                o_ref[pl.ds(q0, Q_TILE), c:c + LANES] = o[i * Q_TILE:(i + 1) * Q_TILE]

    _pipelined_tiles(n_tiles, score_stage, value_stage)


def _window_attention(q, kv, sink, batch, seq):
    unit_rows = 2 * WIN_UNIT_PAIRS * Q_TILE
    return pl.pallas_call(
        functools.partial(_window_kernel, seq),
        grid=(batch,),
        in_specs=[pl.BlockSpec((None, seq, WIN_Q), lambda b: (b, 0, 0)),
                  pl.BlockSpec((None, seq, 2 * WIN_KV), lambda b: (b, 0, 0)),
                  pl.BlockSpec((1, WIN_Q_HEADS), lambda b: (0, 0))],
        out_specs=pl.BlockSpec((None, seq, WIN_Q), lambda b: (b, 0, 0)),
        out_shape=jax.ShapeDtypeStruct((batch, seq, WIN_Q), BF16),
        scratch_shapes=[pltpu.VMEM((3, WIN_UNITS, unit_rows, WIN_WINDOW), F32),
                        pltpu.VMEM((2 * WIN_WINDOW, LANES), BF16),
                        pltpu.VMEM((WIN_UNITS, unit_rows, LANES), F32),
                        pltpu.VMEM((2 * WIN_UNITS, unit_rows, WIN_WINDOW), F32),
                        pltpu.VMEM((2 * WIN_UNITS, unit_rows // 2, 2 * WIN_WINDOW), BF16),
                        pltpu.VMEM((2 * WIN_UNITS, unit_rows // 2, LANES), F32)],
        compiler_params=pltpu.CompilerParams(
            dimension_semantics=("arbitrary",), vmem_limit_bytes=VMEM_LIMIT),
        name="window",
    )(q.reshape(batch, seq, WIN_Q), kv.reshape(batch, seq, 2 * WIN_KV), sink)


def _merge_kernel(oa_ref, ow_ref, gate_ref, h_ref, wa_ref, wb_ref, wo_ref, out_ref):
    ya = _dot(oa_ref[...], wa_ref[...])
    yb = _dot(ow_ref[...], wb_ref[...])
    merged = gate_ref[:, :D_MODEL].astype(F32) * ya + gate_ref[:, D_MODEL:].astype(F32) * yb
    out_ref[...] = h_ref[...] + _dot(merged.astype(BF16), wo_ref[...])


def _merge_out(oa, ow, gates, h, wa, wb, wo):
    m = h.shape[0]
    row = lambda c: pl.BlockSpec((ROW_TILE, c), lambda i: (i, 0))
    full = lambda a: pl.BlockSpec(a.shape, lambda i: (0, 0), pipeline_mode=pl.Buffered(1))
    return pl.pallas_call(
        _merge_kernel,
        grid=(m // ROW_TILE,),
        in_specs=[row(DIL_OUT), row(WIN_Q), row(2 * D_MODEL), row(D_MODEL), full(wa), full(wb), full(wo)],
        out_specs=row(D_MODEL),
        out_shape=jax.ShapeDtypeStruct((m, D_MODEL), F32),
        compiler_params=pltpu.CompilerParams(
            dimension_semantics=("arbitrary",), vmem_limit_bytes=VMEM_LIMIT),
        name="merge_out",
    )(oa, ow, gates, h, wa, wb, wo)


TOKEN_TILE = 256
CUM_CHUNK = TOKEN_TILE
SLOT_WINDOW = 64
SLOT_ALIGN = 16


def _prefix_exclusive(mask_f, tri):
    e, s = mask_f.shape
    carry = jnp.zeros((e, 1), F32)
    parts, carries = [], []
    for j in range(0, s, CUM_CHUNK):
        blk = mask_f[:, j:j + CUM_CHUNK]
        carries.append(carry)
        parts.append(_dot(blk.astype(BF16), tri) + carry)
        carry = carry + jnp.sum(blk, axis=-1, keepdims=True)
    return jnp.concatenate(parts, axis=-1), carries + [carry]


def _route_kernel(cap, h_ref, g_ref, whi_ref, wlo_ref, hn_ref, rank_ref, aff_ref, starts_ref):
    seq = h_ref.shape[0]
    hn = _rms(h_ref[...], g_ref[...])
    hn_hi = hn.astype(BF16)
    hn_lo = (hn - hn_hi.astype(F32)).astype(BF16)
    hn_ref[...] = hn_hi
    logits = (_dot_nt(whi_ref[...], hn_hi) + _dot_nt(whi_ref[...], hn_lo)
              + _dot_nt(wlo_ref[...], hn_hi))
    mx = jnp.max(logits, axis=0, keepdims=True)
    ex = jnp.exp(logits - mx)
    aff = ex / jnp.sum(ex, axis=0, keepdims=True)
    bits = pltpu.bitcast(aff, jnp.int32)

    def search(_, c):
        lo, hi = c
        mid = lo + ((hi - lo) >> 1)
        cnt = jnp.sum(jnp.where(bits >= mid, 1.0, 0.0), axis=-1, keepdims=True)
        ok = cnt >= float(cap)
        return jnp.where(ok, mid, lo), jnp.where(ok, hi, mid)

    lo0 = jnp.zeros((N_EXPERTS, 1), jnp.int32)
    hi0 = jnp.full((N_EXPERTS, 1), 0x3F800001, jnp.int32)
    thr, _ = lax.fori_loop(0, 31, search, (lo0, hi0))

    r_i = lax.broadcasted_iota(jnp.int32, (CUM_CHUNK, CUM_CHUNK), 0)
    c_i = lax.broadcasted_iota(jnp.int32, (CUM_CHUNK, CUM_CHUNK), 1)
    tri = jnp.where(r_i < c_i, 1.0, 0.0).astype(BF16)
    gt = jnp.where(bits > thr, 1.0, 0.0)
    eq = jnp.where(bits == thr, 1.0, 0.0)
    need = float(cap) - jnp.sum(gt, axis=-1, keepdims=True)
    tie_rank, _ = _prefix_exclusive(eq, tri)
    sel = gt + eq * jnp.where(tie_rank < need, 1.0, 0.0)
    slot, starts = _prefix_exclusive(sel, tri)
    rank = jnp.where(sel > 0.0, slot, -1.0)
    for t in range(seq // TOKEN_TILE):
        rank_ref[t] = rank[:, t * TOKEN_TILE:(t + 1) * TOKEN_TILE]
        aff_ref[t] = aff[:, t * TOKEN_TILE:(t + 1) * TOKEN_TILE]
    lane = lax.broadcasted_iota(jnp.int32, (N_EXPERTS, LANES), 1)
    acc = jnp.zeros((N_EXPERTS, LANES), F32)
    for t, c in enumerate(starts):
        acc = acc + jnp.where(lane == t, c, 0.0)
    starts_ref[...] = acc


def _route(h, g, w_hi, w_lo, batch, seq, cap):
    n_tiles = seq // TOKEN_TILE
    return pl.pallas_call(
        functools.partial(_route_kernel, cap),
        grid=(batch,),
        in_specs=[pl.BlockSpec((None, seq, D_MODEL), lambda b: (b, 0, 0)),
                  pl.BlockSpec((1, D_MODEL), lambda b: (0, 0)),
                  pl.BlockSpec((N_EXPERTS, D_MODEL), lambda b: (0, 0)),
                  pl.BlockSpec((N_EXPERTS, D_MODEL), lambda b: (0, 0))],
        out_specs=[pl.BlockSpec((None, seq, D_MODEL), lambda b: (b, 0, 0)),
                   pl.BlockSpec((None, n_tiles, N_EXPERTS, TOKEN_TILE), lambda b: (b, 0, 0, 0)),
                   pl.BlockSpec((None, n_tiles, N_EXPERTS, TOKEN_TILE), lambda b: (b, 0, 0, 0)),
                   pl.BlockSpec((None, N_EXPERTS, LANES), lambda b: (b, 0, 0))],
        out_shape=[jax.ShapeDtypeStruct((batch, seq, D_MODEL), BF16),
                   jax.ShapeDtypeStruct((batch, n_tiles, N_EXPERTS, TOKEN_TILE), F32),
                   jax.ShapeDtypeStruct((batch, n_tiles, N_EXPERTS, TOKEN_TILE), F32),
                   jax.ShapeDtypeStruct((batch, N_EXPERTS, LANES), F32)],
        compiler_params=pltpu.CompilerParams(
            dimension_semantics=("arbitrary",), vmem_limit_bytes=VMEM_LIMIT),
        name="route",
    )(h.reshape(batch, seq, D_MODEL), g, w_hi, w_lo)


def _slot_rows(cap):
    return cap + SLOT_WINDOW


def _window_starts(win_ref, b, t, p, cap):
    base = b * (N_EXPERTS * LANES) + t
    return [pl.multiple_of(jnp.minimum(win_ref[base + e * LANES] + p * SLOT_WINDOW, cap), SLOT_ALIGN)
            for e in range(N_EXPERTS)]


def _window_hits(rank_tile, starts):
    rows = lax.broadcasted_iota(jnp.int32, (SLOT_WINDOW, TOKEN_TILE), 0).astype(F32)
    return [(rank_tile[e:e + 1, :] - starts[e].astype(F32)) == rows for e in range(N_EXPERTS)]


def _one_hot(hits):
    return jnp.concatenate([jnp.where(h, 1.0, 0.0).astype(BF16) for h in hits], axis=0)


def _gather_kernel(cap, win_ref, npass_ref, hn_ref, rank_ref, aff_ref, xe_ref, gate_ref):
    b = pl.program_id(0)
    xe_ref[...] = jnp.zeros_like(xe_ref)
    gate_ref[...] = jnp.zeros_like(gate_ref)
    for t in range(hn_ref.shape[0] // TOKEN_TILE):

        def one_pass(p, carry, t=t):
            starts = _window_starts(win_ref, b, t, p, cap)
            hits = _window_hits(rank_ref[t], starts)
            rows = _dot(_one_hot(hits), hn_ref[t * TOKEN_TILE:(t + 1) * TOKEN_TILE, :]).astype(BF16)
            aff_tile = aff_ref[t]
            for e in range(N_EXPERTS):
                win = pl.ds(starts[e], SLOT_WINDOW)
                xe_ref[e, win, :] += rows[e * SLOT_WINDOW:(e + 1) * SLOT_WINDOW]
                g = jnp.sum(jnp.where(hits[e], aff_tile[e:e + 1, :], 0.0), axis=-1, keepdims=True)
                gate_ref[e, win, :] += jnp.broadcast_to(g, (SLOT_WINDOW, LANES))
            return carry

        lax.fori_loop(0, npass_ref[b * LANES + t], one_pass, 0)


def _gather(hn, rank, aff, win, npass, cap):
    batch, seq, _ = hn.shape
    n_tiles = seq // TOKEN_TILE
    grid_spec = pltpu.PrefetchScalarGridSpec(
        num_scalar_prefetch=2,
        grid=(batch,),
        in_specs=[pl.BlockSpec((None, seq, D_MODEL), lambda b, *_: (b, 0, 0)),
                  pl.BlockSpec((None, n_tiles, N_EXPERTS, TOKEN_TILE), lambda b, *_: (b, 0, 0, 0)),
                  pl.BlockSpec((None, n_tiles, N_EXPERTS, TOKEN_TILE), lambda b, *_: (b, 0, 0, 0))],
        out_specs=[pl.BlockSpec((None, N_EXPERTS, _slot_rows(cap), D_MODEL), lambda b, *_: (b, 0, 0, 0)),
                   pl.BlockSpec((None, N_EXPERTS, _slot_rows(cap), LANES), lambda b, *_: (b, 0, 0, 0))])
    return pl.pallas_call(
        functools.partial(_gather_kernel, cap),
        grid_spec=grid_spec,
        out_shape=[jax.ShapeDtypeStruct((batch, N_EXPERTS, _slot_rows(cap), D_MODEL), BF16),
                   jax.ShapeDtypeStruct((batch, N_EXPERTS, _slot_rows(cap), LANES), F32)],
        compiler_params=pltpu.CompilerParams(
            dimension_semantics=("arbitrary",), vmem_limit_bytes=VMEM_LIMIT),
        name="gather",
    )(win, npass, hn, rank, aff)


FFN_SEQS = 4


def _ffn_kernel(cap, xe_ref, gate_ref, wg_ref, wu_ref, wd_ref, y_ref):
    d = xe_ref.shape[-1]
    xe = xe_ref[:, :cap, :].reshape(FFN_SEQS * cap, d)
    a = _dot(xe, wg_ref[...])
    u = _dot(xe, wu_ref[...])
    y = _dot((jax.nn.silu(a) * u).astype(BF16), wd_ref[...])
    gate = gate_ref[:, :cap, :].reshape(FFN_SEQS * cap, LANES)
    y = y * jnp.concatenate([gate] * (d // LANES), axis=1)
    y_ref[:, :cap, :] = y.astype(BF16).reshape(FFN_SEQS, cap, d)
    y_ref[:, cap:, :] = jnp.zeros((FFN_SEQS, SLOT_WINDOW, d), BF16)


def _ffn(xe, gate, wg, wu, wd, cap):
    batch = xe.shape[0]
    d_exp = wg.shape[-1]
    slots = pl.BlockSpec((FFN_SEQS, None, _slot_rows(cap), D_MODEL), lambda e, i: (i, e, 0, 0))
    return pl.pallas_call(
        functools.partial(_ffn_kernel, cap),
        grid=(N_EXPERTS, batch // FFN_SEQS),
        in_specs=[slots,
                  pl.BlockSpec((FFN_SEQS, None, _slot_rows(cap), LANES), lambda e, i: (i, e, 0, 0)),
                  pl.BlockSpec((None, D_MODEL, d_exp), lambda e, i: (e, 0, 0)),
                  pl.BlockSpec((None, D_MODEL, d_exp), lambda e, i: (e, 0, 0)),
                  pl.BlockSpec((None, d_exp, D_MODEL), lambda e, i: (e, 0, 0))],
        out_specs=slots,
        out_shape=jax.ShapeDtypeStruct(xe.shape, BF16),
        compiler_params=pltpu.CompilerParams(
            dimension_semantics=("arbitrary", "arbitrary"), vmem_limit_bytes=VMEM_LIMIT),
        name="ffn",
    )(xe, gate, wg, wu, wd)


def _scatter_kernel(cap, win_ref, npass_ref, y_ref, rank_ref, h_ref, out_ref):
    b = pl.program_id(0)
    t = pl.program_id(1)
    out_ref[...] = h_ref[...]

    def one_pass(p, carry):
        starts = _window_starts(win_ref, b, t, p, cap)
        put = _one_hot(_window_hits(rank_ref[...], starts))
        yw = jnp.concatenate([y_ref[e, pl.ds(starts[e], SLOT_WINDOW), :] for e in range(N_EXPERTS)], axis=0)
        out_ref[...] += lax.dot_general(put, yw, (((0,), (0,)), ((), ())), preferred_element_type=F32)
        return carry

    lax.fori_loop(0, npass_ref[b * LANES + t], one_pass, 0)


def _scatter(y, rank, h, win, npass, cap):
    batch, n_tiles = rank.shape[:2]
    grid_spec = pltpu.PrefetchScalarGridSpec(
        num_scalar_prefetch=2,
        grid=(batch, n_tiles),
        in_specs=[pl.BlockSpec((None, N_EXPERTS, _slot_rows(cap), D_MODEL), lambda b, t, *_: (b, 0, 0, 0)),
                  pl.BlockSpec((None, None, N_EXPERTS, TOKEN_TILE), lambda b, t, *_: (b, t, 0, 0)),
                  pl.BlockSpec((None, TOKEN_TILE, D_MODEL), lambda b, t, *_: (b, t, 0))],
        out_specs=pl.BlockSpec((None, TOKEN_TILE, D_MODEL), lambda b, t, *_: (b, t, 0)))
    return pl.pallas_call(
        functools.partial(_scatter_kernel, cap),
        grid_spec=grid_spec,
        out_shape=jax.ShapeDtypeStruct(h.shape, F32),
        compiler_params=pltpu.CompilerParams(
            dimension_semantics=("arbitrary", "arbitrary"), vmem_limit_bytes=VMEM_LIMIT),
        name="scatter",
    )(win, npass, y, rank, h)


def _slot_windows(starts, n_tiles, cap):
    s = starts.astype(jnp.int32)
    first = (s // SLOT_ALIGN) * SLOT_ALIGN
    span = s[:, :, 1:n_tiles + 1] - first[:, :, :n_tiles]
    npass = jnp.maximum(jnp.max(-(-span // SLOT_WINDOW), axis=1), 1)
    npass = jnp.pad(npass, ((0, 0), (0, LANES - n_tiles)))
    return first.reshape(-1), npass.reshape(-1)


def _final_kernel(x_ref, g_ref, o_ref):
    o_ref[...] = _rms(x_ref[...], g_ref[...])


def _final_norm(x, g):
    m = x.shape[0]
    row = pl.BlockSpec((ROW_TILE, D_MODEL), lambda i: (i, 0))
    return pl.pallas_call(
        _final_kernel,
        grid=(m // ROW_TILE,),
        in_specs=[row, pl.BlockSpec((1, D_MODEL), lambda i: (0, 0))],
        out_specs=row,
        out_shape=jax.ShapeDtypeStruct((m, D_MODEL), F32),
        compiler_params=pltpu.CompilerParams(dimension_semantics=("arbitrary",)),
        name="final_norm",
    )(x, g)


def _arrange_w_in(w):
    scale = LOG2E * HEAD_DIM ** -0.5
    qa, ka, va = w[:, :DIL_QKV], w[:, DIL_QKV:2 * DIL_QKV], w[:, 2 * DIL_QKV:3 * DIL_QKV]
    rest = w[:, 3 * DIL_QKV:]
    parts = []
    for g in range(len(DIL_PATTERNS)):
        sl = slice(g * DIL_OUT, (g + 1) * DIL_OUT)
        parts += [qa[:, sl] * scale, ka[:, sl], va[:, sl]]
    for pr in _window_pair_heads():
        parts += [rest[:, h * HEAD_DIM:(h + 1) * HEAD_DIM] * scale for h in pr]
    parts.append(rest[:, WIN_Q:])
    return jnp.concatenate(parts, axis=1).astype(BF16)


def _arrange_w_branch_b(w):
    return jnp.concatenate([w[h * HEAD_DIM:(h + 1) * HEAD_DIM] for pr in _window_pair_heads() for h in pr],
                           axis=0).astype(BF16)


def kernel(x, norm_mix, w_in, w_branch_a, w_branch_b, b_gate, sink_logit, w_out, norm_ffn, w_router,
           w_expert_gate, w_expert_up, w_expert_down, norm_final):
    batch, seq, d = x.shape
    depth = w_in.shape[0]
    cap = CAPACITY_FACTOR * seq // N_EXPERTS
    m = batch * seq
    h = x.reshape(m, d)
    for l in range(depth):
        a0, a1, a2, qw, kv, gates = _norm_proj(h, norm_mix[l][None, :], _arrange_w_in(w_in[l]),
                                               b_gate[l][None, :])
        oa = _dilated_attention(a0, a1, a2, batch, seq).reshape(m, DIL_OUT)
        ow = _window_attention(qw, kv, sink_logit[l][None, :], batch, seq).reshape(m, WIN_Q)
        h = _merge_out(oa, ow, gates, h, w_branch_a[l].astype(BF16),
                       _arrange_w_branch_b(w_branch_b[l]), w_out[l].astype(BF16))
        wr = w_router[l].T
        wr_hi = wr.astype(BF16)
        wr_lo = (wr - wr_hi.astype(F32)).astype(BF16)
        hn, rank, aff, starts = _route(h, norm_ffn[l][None, :], wr_hi, wr_lo, batch, seq, cap)
        win, npass = _slot_windows(starts, seq // TOKEN_TILE, cap)
        xe, gate = _gather(hn, rank, aff, win, npass, cap)
        y = _ffn(xe, gate, w_expert_gate[l].astype(BF16), w_expert_up[l].astype(BF16),
                 w_expert_down[l].astype(BF16), cap)
        h = _scatter(y, rank, h.reshape(batch, seq, d), win, npass, cap).reshape(m, d)
    return _final_norm(h, norm_final[None, :]).reshape(batch, seq, d)
```

```python
import functools

import jax
import jax.numpy as jnp
from jax import lax
from jax.experimental import pallas as pl
from jax.experimental.pallas import tpu as pltpu

D_MODEL = 1024
HEAD_DIM = 64
DIL_PATTERNS = ((128, 1), (512, 4), (2048, 16))
DIL_HEADS = 4
N_DIL_SUB = DIL_HEADS * len(DIL_PATTERNS)
DIL_QKV = N_DIL_SUB * HEAD_DIM
DIL_OUT = DIL_HEADS * HEAD_DIM
DIL_HALF = 64
WIN_HALF = 128
WIN_Q = D_MODEL
WIN_Q_HEADS = WIN_Q // HEAD_DIM
WIN_KV_HEADS = 4
WIN_KV = WIN_KV_HEADS * HEAD_DIM
N_EXPERTS = 16
CAPACITY_FACTOR = 2
RMS_EPS = 1e-6
NEG_INF = -1e30
LOG2E = 1.4426950408889634

LANES = 128
GROUP_COLS = 3 * DIL_OUT
DIL_PAIRS = DIL_OUT // LANES
WIN_UNITS = WIN_KV_HEADS // 2
WIN_UNIT_PAIRS = WIN_Q_HEADS // (2 * WIN_UNITS)
Q_TILE = 128
DIL_INFLIGHT = 8
ROW_TILE = 512
VMEM_LIMIT = 56 * 1024 * 1024

F32 = jnp.float32
BF16 = jnp.bfloat16


def _alibi_slopes(n):
    return [float(2.0 ** (-8.0 * i / n)) for i in range(1, n + 1)]


def _rms(x, g):
    return x * lax.rsqrt(jnp.mean(x * x, axis=-1, keepdims=True) + RMS_EPS) * g


def _dot(a, b):
    return jnp.dot(a, b, preferred_element_type=F32)


def _dot_nt(a, b):
    return lax.dot_general(a, b, (((1,), (1,)), ((), ())), preferred_element_type=F32)


def _left_lanes():
    return lax.broadcasted_iota(jnp.int32, (1, LANES), 1) < HEAD_DIM


_C_A = (0, GROUP_COLS, 2 * GROUP_COLS)
_C_QW = 3 * GROUP_COLS
_C_KV = _C_QW + WIN_Q
_COL_CHUNK = 512


def _proj_store(ref, xb, w_ref, c0, cw):
    for j in range(0, cw, _COL_CHUNK):
        jw = min(_COL_CHUNK, cw - j)
        ref[:, j:j + jw] = _dot(xb, w_ref[:, c0 + j:c0 + j + jw]).astype(BF16)


def _norm_proj_kernel(x_ref, g_ref, w_ref, a0_ref, a1_ref, a2_ref, qw_ref, kv_ref, xs_ref):
    xn = _rms(x_ref[...], g_ref[...])
    n_lane_tiles = D_MODEL // LANES
    for j in range(n_lane_tiles):
        xs_ref[j] = xn[:, j * LANES:(j + 1) * LANES]
    xb = xn.astype(BF16)
    _proj_store(a0_ref, xb, w_ref, _C_A[0], GROUP_COLS)
    for grp, ref in ((1, a1_ref), (2, a2_ref)):
        r = DIL_PATTERNS[grp][1]
        n = ROW_TILE // r
        xp = jnp.concatenate(
            [jnp.concatenate([xs_ref[j, pl.ds(c, n, stride=r), :] for j in range(n_lane_tiles)], axis=1)
             for c in range(r)], axis=0).astype(BF16)
        res = _dot(xp, w_ref[:, _C_A[grp]:_C_A[grp] + GROUP_COLS])
        for c in range(r):
            ref[:, c * GROUP_COLS:(c + 1) * GROUP_COLS] = res[c * n:(c + 1) * n].astype(BF16)
    _proj_store(qw_ref, xb, w_ref, _C_QW, WIN_Q)
    _proj_store(kv_ref, xb, w_ref, _C_KV, 2 * WIN_KV)


def _norm_proj(x, g, w):
    m = x.shape[0]
    row = lambda c: pl.BlockSpec((ROW_TILE, c), lambda i: (i, 0))
    full = lambda a: pl.BlockSpec(a.shape, lambda i: (0, 0), pipeline_mode=pl.Buffered(1))
    in_specs = [row(D_MODEL), full(g), full(w)]
    out_shape, out_specs = [], []
    for _, r in DIL_PATTERNS:
        out_shape.append(jax.ShapeDtypeStruct((m // r, r * GROUP_COLS), BF16))
        out_specs.append(pl.BlockSpec((ROW_TILE // r, r * GROUP_COLS), lambda i: (i, 0)))
    for c in (WIN_Q, 2 * WIN_KV):
        out_shape.append(jax.ShapeDtypeStruct((m, c), BF16))
        out_specs.append(row(c))
    return pl.pallas_call(
        _norm_proj_kernel,
        grid=(m // ROW_TILE,),
        in_specs=in_specs,
        out_specs=out_specs,
        out_shape=out_shape,
        scratch_shapes=[pltpu.VMEM((D_MODEL // LANES, ROW_TILE, LANES), F32)],
        compiler_params=pltpu.CompilerParams(
            dimension_semantics=("arbitrary",), vmem_limit_bytes=VMEM_LIMIT),
        name="norm_proj",
    )(x, g, w)


def _band_bias(tq, win, off, half, slopes):
    col = lax.broadcasted_iota(jnp.int32, (tq, win), 1)
    row = lax.broadcasted_iota(jnp.int32, (tq, win), 0)
    dist = jnp.abs(col - row + off).astype(F32)
    inside = dist <= float(half)
    return jnp.concatenate([jnp.where(inside, dist * (-s * LOG2E), NEG_INF) for s in slopes], axis=0)


def _ones_blockdiag(win):
    left = lax.broadcasted_iota(jnp.int32, (2 * win, LANES), 1) < HEAD_DIM
    top = jnp.where(lax.broadcasted_iota(jnp.int32, (2 * win, LANES), 0) < win, 1.0, 0.0)
    return jnp.where(left, top, 1.0 - top).astype(BF16)


def _pair_scores(q_tiles, k_win, bias):
    left = _left_lanes()
    zero = jnp.zeros((), BF16)
    rows = []
    for q in q_tiles:
        rows += [jnp.where(left, q, zero), jnp.where(left, zero, q)]
    return _dot_nt(jnp.concatenate(rows, axis=0), k_win) + bias


def _pair_values(s_ref, v_win, ones_bd, sink_ref, p_ref, aux_ref):
    win = v_win.shape[0]
    n = s_ref.shape[0] // (2 * Q_TILE)
    left = _left_lanes()
    zero = jnp.zeros((), BF16)
    for i in range(n):
        ms = []
        for side in range(2):
            rows = pl.ds((2 * i + side) * Q_TILE, Q_TILE)
            s = s_ref[rows, :]
            m = jnp.broadcast_to(jnp.max(s, axis=-1, keepdims=True), (Q_TILE, LANES))
            if sink_ref is not None:
                m = jnp.maximum(m, sink_ref[rows, :])
            p = jnp.exp2(s - jnp.concatenate([m] * (win // LANES), axis=1)).astype(BF16)
            p_ref[i * Q_TILE:(i + 1) * Q_TILE, side * win:(side + 1) * win] = p
            ms.append(m)
        m_pair = jnp.where(left, ms[0], ms[1])
        if sink_ref is not None:
            sink_pair = jnp.where(left, sink_ref[pl.ds(2 * i * Q_TILE, Q_TILE), :],
                                  sink_ref[pl.ds((2 * i + 1) * Q_TILE, Q_TILE), :])
            aux_ref[i * Q_TILE:(i + 1) * Q_TILE, :] = jnp.exp2(sink_pair - m_pair)
        else:
            aux_ref[i * Q_TILE:(i + 1) * Q_TILE, :] = m_pair
    vbd = jnp.concatenate([jnp.where(left, v_win, zero), jnp.where(left, zero, v_win)], axis=0)
    on = _dot(p_ref[...], jnp.concatenate([vbd, ones_bd], axis=1))
    return on[:, :LANES], on[:, LANES:], aux_ref[...]


def _tile_variant(t, n_tiles):
    return jnp.where(t == 0, 0, jnp.where(t == n_tiles - 1, 2, 1))


def _pipelined_tiles(n_tiles, score_stage, value_stage):
    score_stage(0, 0)

    def body(i, carry):
        t = 2 * i
        score_stage(t + 1, 1)
        value_stage(t, 0)
        score_stage(jnp.minimum(t + 2, n_tiles - 1), 0)
        value_stage(t + 1, 1)
        return carry

    lax.fori_loop(0, n_tiles // 2, body, 0)


def _dil_geometry(seq_full, grp):
    r = DIL_PATTERNS[grp][1]
    seq = seq_full // r
    win = min(seq, Q_TILE + 2 * DIL_HALF)
    n_tiles = seq // Q_TILE
    pad = (win - Q_TILE) // 2
    offs = [0] if n_tiles == 1 else [0, -pad, -2 * pad]
    return r, seq, win, n_tiles, pad, offs


def _dilated_kernel(seq_full, a0_ref, a1_ref, a2_ref, out_ref, o_nat, l_nat, b0, b1, b2, ones_ref, ones2_ref,
                    s_wide, s_narrow, p_wide, p_narrow, aux_ref):
    a_refs = (a0_ref, a1_ref, a2_ref)
    bias_refs = (b0, b1, b2)
    s_refs = (s_wide, s_wide, s_narrow)
    p_refs = (p_wide, p_wide, p_narrow)
    slopes = _alibi_slopes(N_DIL_SUB)

    @pl.when(pl.program_id(0) == 0)
    def _():
        for grp in range(3):
            r, seq, win, n_tiles, pad, offs = _dil_geometry(seq_full, grp)
            for v, off in enumerate(offs):
                for pr in range(DIL_PAIRS):
                    hs = slopes[grp * DIL_HEADS + 2 * pr:grp * DIL_HEADS + 2 * pr + 2]
                    bias_refs[grp][v, pr] = _band_bias(Q_TILE, win, off, DIL_HALF, [s * r for s in hs])
        ones_ref[...] = _ones_blockdiag(_dil_geometry(seq_full, 0)[2])
        ones2_ref[...] = _ones_blockdiag(_dil_geometry(seq_full, 2)[2])

    def pair_tile(grp, c, t, pr, slot):
        r, seq, win, n_tiles, pad, offs = _dil_geometry(seq_full, grp)
        a_ref = a_refs[grp]
        ones_bd_ref = ones2_ref if grp == 2 else ones_ref
        if n_tiles == 1:
            q0, ks, var = 0, 0, 0
        else:
            q0 = pl.multiple_of(t * Q_TILE, Q_TILE)
            ks = pl.multiple_of(jnp.clip(q0 - pad, 0, seq - win), DIL_HALF)
            var = _tile_variant(t, n_tiles)
        lo = c * GROUP_COLS + pr * LANES
        q = a_ref[pl.ds(q0, Q_TILE), lo:lo + LANES]
        k = a_ref[pl.ds(ks, win), DIL_OUT + lo:DIL_OUT + lo + LANES]
        v = a_ref[pl.ds(ks, win), 2 * DIL_OUT + lo:2 * DIL_OUT + lo + LANES]
        s_ref = s_refs[grp].at[slot]
        s_ref[...] = _pair_scores([q], k, bias_refs[grp][var, pr])
        num, den, m = _pair_values(s_ref, v, ones_bd_ref[...], None, p_refs[grp].at[slot], aux_ref.at[slot])
        rows = pl.ds(q0, Q_TILE) if r == 1 else pl.ds(c + r * q0, Q_TILE, stride=r)
        o_nat[grp, pr, rows, :] = num / den
        l_nat[grp, pr, rows, :] = m + jnp.log2(den)

    for grp in range(3):
        r, seq, win, n_tiles, pad, offs = _dil_geometry(seq_full, grp)
        per_body = DIL_INFLIGHT // DIL_PAIRS
        if n_tiles == 1:
            for c in range(r):
                for pr in range(DIL_PAIRS):
                    pair_tile(grp, c, 0, pr, c * DIL_PAIRS + pr)
        elif r >= per_body:
            for c0 in range(0, r, per_body):
                def body(t, carry, grp=grp, c0=c0):
                    for j in range(per_body):
                        for pr in range(DIL_PAIRS):
                            pair_tile(grp, c0 + j, t, pr, j * DIL_PAIRS + pr)
                    return carry
                lax.fori_loop(0, n_tiles, body, 0)
        else:
            tiles_per_body = per_body // r
            def body(i, carry, grp=grp, r=r, tiles_per_body=tiles_per_body):
                for j in range(tiles_per_body):
                    for c in range(r):
                        for pr in range(DIL_PAIRS):
                            pair_tile(grp, c, i * tiles_per_body + j, pr, (j * r + c) * DIL_PAIRS + pr)
                return carry
            lax.fori_loop(0, n_tiles // tiles_per_body, body, 0)

    def combine(i, carry):
        rows = pl.ds(pl.multiple_of(i * ROW_TILE, ROW_TILE), ROW_TILE)
        for pr in range(DIL_PAIRS):
            ls = [l_nat[g, pr, rows, :] for g in range(3)]
            mx = jnp.maximum(jnp.maximum(ls[0], ls[1]), ls[2])
            es = [jnp.exp2(l - mx) for l in ls]
            num = es[0] * o_nat[0, pr, rows, :] + es[1] * o_nat[1, pr, rows, :] + es[2] * o_nat[2, pr, rows, :]
            out_ref[rows, pr * LANES:(pr + 1) * LANES] = (num / (es[0] + es[1] + es[2])).astype(BF16)
        return carry

    lax.fori_loop(0, seq_full // ROW_TILE, combine, 0)


def _dilated_attention(a0, a1, a2, batch, seq_full):
    views, in_specs, bias_shapes = [], [], []
    for grp, a in enumerate((a0, a1, a2)):
        r, seq, win, n_tiles, pad, offs = _dil_geometry(seq_full, grp)
        views.append(a.reshape(batch, seq, r * GROUP_COLS))
        in_specs.append(pl.BlockSpec((None, seq, r * GROUP_COLS), lambda b: (b, 0, 0)))
        bias_shapes.append(pltpu.VMEM((len(offs), DIL_PAIRS, 2 * Q_TILE, win), F32))
    win0 = _dil_geometry(seq_full, 0)[2]
    win2 = _dil_geometry(seq_full, 2)[2]
    n_straight = DIL_PATTERNS[2][1] * DIL_PAIRS
    return pl.pallas_call(
        functools.partial(_dilated_kernel, seq_full),
        grid=(batch,),
        in_specs=in_specs,
        out_specs=pl.BlockSpec((None, seq_full, DIL_OUT), lambda b: (b, 0, 0)),
        out_shape=jax.ShapeDtypeStruct((batch, seq_full, DIL_OUT), BF16),
        scratch_shapes=[pltpu.VMEM((3, DIL_PAIRS, seq_full, LANES), F32),
                        pltpu.VMEM((3, DIL_PAIRS, seq_full, LANES), F32)]
        + bias_shapes + [pltpu.VMEM((2 * win0, LANES), BF16), pltpu.VMEM((2 * win2, LANES), BF16)]
        + [pltpu.VMEM((n, 2 * Q_TILE, w), F32) for n, w in ((DIL_INFLIGHT, win0), (n_straight, win2))]
        + [pltpu.VMEM((n, Q_TILE, 2 * w), BF16) for n, w in ((DIL_INFLIGHT, win0), (n_straight, win2))]
        + [pltpu.VMEM((max(DIL_INFLIGHT, n_straight), Q_TILE, LANES), F32)],
        compiler_params=pltpu.CompilerParams(
            dimension_semantics=("arbitrary",), vmem_limit_bytes=VMEM_LIMIT),
        name="dilated",
    )(*views)


WIN_WINDOW = Q_TILE + 2 * WIN_HALF


def _window_pair_heads():
    group = WIN_Q_HEADS // WIN_KV_HEADS
    pairs = []
    for u in range(WIN_UNITS):
        for i in range(WIN_UNIT_PAIRS):
            pairs.append((2 * u * group + i, (2 * u + 1) * group + i))
    return pairs


def _window_kernel(seq, q_ref, kv_ref, sink_ref, o_ref, bias_ref, ones_ref, sink_rows_ref, s_ref, p_ref,
                   aux_ref):
    n_tiles = seq // Q_TILE
    slopes = _alibi_slopes(WIN_Q_HEADS)
    pair_heads = _window_pair_heads()
    unit_rows = 2 * WIN_UNIT_PAIRS * Q_TILE

    @pl.when(pl.program_id(0) == 0)
    def _():
        for u in range(WIN_UNITS):
            heads = [h for pr in pair_heads[u * WIN_UNIT_PAIRS:(u + 1) * WIN_UNIT_PAIRS] for h in pr]
            for v, off in enumerate((0, -WIN_HALF, -2 * WIN_HALF)):
                bias_ref[v, u] = _band_bias(Q_TILE, WIN_WINDOW, off, WIN_HALF, [slopes[h] for h in heads])
            sink_rows_ref[u] = jnp.concatenate(
                [jnp.broadcast_to(sink_ref[0:1, h:h + 1] * LOG2E, (Q_TILE, LANES)) for h in heads], axis=0)
        ones_ref[...] = _ones_blockdiag(WIN_WINDOW)

    def rows_of(t):
        q0 = pl.multiple_of(t * Q_TILE, Q_TILE)
        ks = pl.multiple_of(jnp.clip(q0 - WIN_HALF, 0, seq - WIN_WINDOW), Q_TILE)
        return q0, ks

    def unit_cols(u):
        return [(u * WIN_UNIT_PAIRS + i) * LANES for i in range(WIN_UNIT_PAIRS)]

    def score_stage(t, k_set):
        q0, ks = rows_of(t)
        var = _tile_variant(t, n_tiles)
        for u in range(WIN_UNITS):
            qs = [q_ref[pl.ds(q0, Q_TILE), c:c + LANES] for c in unit_cols(u)]
            k = kv_ref[pl.ds(ks, WIN_WINDOW), u * LANES:(u + 1) * LANES]
            s_ref[k_set * WIN_UNITS + u] = _pair_scores(qs, k, bias_ref[var, u])

    def value_stage(t, k_set):
        q0, ks = rows_of(t)
        for u in range(WIN_UNITS):
            slot = k_set * WIN_UNITS + u
            v = kv_ref[pl.ds(ks, WIN_WINDOW), WIN_KV + u * LANES:WIN_KV + (u + 1) * LANES]
            num, den, sink_term = _pair_values(s_ref.at[slot], v, ones_ref[...], sink_rows_ref.at[u],
                                               p_ref.at[slot], aux_ref.at[slot])
            o = (num / (den + sink_term)).astype(BF16)
            for i, c in enumerate(unit_cols(u)):
                o_ref[pl.ds(q0, Q_TILE), c:c + LANES] = o[i * Q_TILE:(i + 1) * Q_TILE]

    _pipelined_tiles(n_tiles, score_stage, value_stage)


def _window_attention(q, kv, sink, batch, seq):
    unit_rows = 2 * WIN_UNIT_PAIRS * Q_TILE
    return pl.pallas_call(
        functools.partial(_window_kernel, seq),
        grid=(batch,),
        in_specs=[pl.BlockSpec((None, seq, WIN_Q), lambda b: (b, 0, 0)),
                  pl.BlockSpec((None, seq, 2 * WIN_KV), lambda b: (b, 0, 0)),
                  pl.BlockSpec((1, WIN_Q_HEADS), lambda b: (0, 0))],
        out_specs=pl.BlockSpec((None, seq, WIN_Q), lambda b: (b, 0, 0)),
        out_shape=jax.ShapeDtypeStruct((batch, seq, WIN_Q), BF16),
        scratch_shapes=[pltpu.VMEM((3, WIN_UNITS, unit_rows, WIN_WINDOW), F32),
                        pltpu.VMEM((2 * WIN_WINDOW, LANES), BF16),
                        pltpu.VMEM((WIN_UNITS, unit_rows, LANES), F32),
                        pltpu.VMEM((2 * WIN_UNITS, unit_rows, WIN_WINDOW), F32),
                        pltpu.VMEM((2 * WIN_UNITS, unit_rows // 2, 2 * WIN_WINDOW), BF16),
                        pltpu.VMEM((2 * WIN_UNITS, unit_rows // 2, LANES), F32)],
        compiler_params=pltpu.CompilerParams(
            dimension_semantics=("arbitrary",), vmem_limit_bytes=VMEM_LIMIT),
        name="window",
    )(q.reshape(batch, seq, WIN_Q), kv.reshape(batch, seq, 2 * WIN_KV), sink)


def _merge_kernel(oa_ref, ow_ref, h_ref, g_ref, wgate_ref, bgate_ref, wa_ref, wb_ref, wo_ref, out_ref):
    h = h_ref[...]
    xb = _rms(h, g_ref[...]).astype(BF16)
    merged = None
    for br, (o_ref, w_ref) in enumerate(((oa_ref, wa_ref), (ow_ref, wb_ref))):
        cols = slice(br * D_MODEL, (br + 1) * D_MODEL)
        gate = jax.nn.sigmoid(_dot(xb, wgate_ref[:, cols]) + bgate_ref[:, cols])
        term = gate * _dot(o_ref[...], w_ref[...])
        merged = term if merged is None else merged + term
    out_ref[...] = h + _dot(merged.astype(BF16), wo_ref[...])


def _merge_out(oa, ow, h, g, wgate, bgate, wa, wb, wo):
    m = h.shape[0]
    row = lambda c: pl.BlockSpec((ROW_TILE, c), lambda i: (i, 0))
    full = lambda a: pl.BlockSpec(a.shape, lambda i: (0, 0), pipeline_mode=pl.Buffered(1))
    return pl.pallas_call(
        _merge_kernel,
        grid=(m // ROW_TILE,),
        in_specs=[row(DIL_OUT), row(WIN_Q), row(D_MODEL), full(g), full(wgate), full(bgate),
                  full(wa), full(wb), full(wo)],
        out_specs=row(D_MODEL),
        out_shape=jax.ShapeDtypeStruct((m, D_MODEL), F32),
        compiler_params=pltpu.CompilerParams(
            dimension_semantics=("arbitrary",), vmem_limit_bytes=VMEM_LIMIT),
        name="merge_out",
    )(oa, ow, h, g, wgate, bgate, wa, wb, wo)


TOKEN_TILE = 256
CUM_CHUNK = TOKEN_TILE
SLOT_WINDOW = 64
SLOT_ALIGN = 16


def _prefix_exclusive(mask_f, tri):
    e, s = mask_f.shape
    carry = jnp.zeros((e, 1), F32)
    parts, carries = [], []
    for j in range(0, s, CUM_CHUNK):
        blk = mask_f[:, j:j + CUM_CHUNK]
        carries.append(carry)
        parts.append(_dot(blk.astype(BF16), tri) + carry)
        carry = carry + jnp.sum(blk, axis=-1, keepdims=True)
    return jnp.concatenate(parts, axis=-1), carries + [carry]


def _route_kernel(cap, h_ref, g_ref, whi_ref, wlo_ref, hn_ref, rank_ref, aff_ref, starts_ref):
    seq = h_ref.shape[0]
    hn = _rms(h_ref[...], g_ref[...])
    hn_hi = hn.astype(BF16)
    hn_lo = (hn - hn_hi.astype(F32)).astype(BF16)
    hn_ref[...] = hn_hi
    logits = (_dot_nt(whi_ref[...], hn_hi) + _dot_nt(whi_ref[...], hn_lo)
              + _dot_nt(wlo_ref[...], hn_hi))
    mx = jnp.max(logits, axis=0, keepdims=True)
    ex = jnp.exp(logits - mx)
    aff = ex / jnp.sum(ex, axis=0, keepdims=True)
    bits = pltpu.bitcast(aff, jnp.int32)

    def enough(t):
        return jnp.sum(jnp.where(bits >= t, 1.0, 0.0), axis=-1, keepdims=True) >= float(cap)

    def search4(_, c):
        lo, hi = c
        q = (hi - lo) >> 2
        m1, m2, m3 = lo + q, lo + 2 * q, lo + 3 * q
        ok1, ok2, ok3 = enough(m1), enough(m2), enough(m3)
        return (jnp.where(ok3, m3, jnp.where(ok2, m2, jnp.where(ok1, m1, lo))),
                jnp.where(ok3, hi, jnp.where(ok2, m3, jnp.where(ok1, m2, m1))))

    def search2(_, c):
        lo, hi = c
        mid = lo + ((hi - lo) >> 1)
        ok = enough(mid)
        return jnp.where(ok, mid, lo), jnp.where(ok, hi, mid)

    lo0 = jnp.zeros((N_EXPERTS, 1), jnp.int32)
    hi0 = jnp.full((N_EXPERTS, 1), 0x3F800001, jnp.int32)
    thr, _ = lax.fori_loop(0, 3, search2, lax.fori_loop(0, 15, search4, (lo0, hi0)))

    r_i = lax.broadcasted_iota(jnp.int32, (CUM_CHUNK, CUM_CHUNK), 0)
    c_i = lax.broadcasted_iota(jnp.int32, (CUM_CHUNK, CUM_CHUNK), 1)
    tri = jnp.where(r_i < c_i, 1.0, 0.0).astype(BF16)
    gt = jnp.where(bits > thr, 1.0, 0.0)
    eq = jnp.where(bits == thr, 1.0, 0.0)
    need = float(cap) - jnp.sum(gt, axis=-1, keepdims=True)
    tie_rank, _ = _prefix_exclusive(eq, tri)
    sel = gt + eq * jnp.where(tie_rank < need, 1.0, 0.0)
    slot, starts = _prefix_exclusive(sel, tri)
    rank = jnp.where(sel > 0.0, slot, -1.0)
    for t in range(seq // TOKEN_TILE):
        rank_ref[t] = rank[:, t * TOKEN_TILE:(t + 1) * TOKEN_TILE]
        aff_ref[t] = aff[:, t * TOKEN_TILE:(t + 1) * TOKEN_TILE]
    lane = lax.broadcasted_iota(jnp.int32, (N_EXPERTS, LANES), 1)
    acc = jnp.zeros((N_EXPERTS, LANES), F32)
    for t, c in enumerate(starts):
        acc = acc + jnp.where(lane == t, c, 0.0)
    starts_ref[...] = acc


def _route(h, g, w_hi, w_lo, batch, seq, cap):
    n_tiles = seq // TOKEN_TILE
    return pl.pallas_call(
        functools.partial(_route_kernel, cap),
        grid=(batch,),
        in_specs=[pl.BlockSpec((None, seq, D_MODEL), lambda b: (b, 0, 0)),
                  pl.BlockSpec((1, D_MODEL), lambda b: (0, 0)),
                  pl.BlockSpec((N_EXPERTS, D_MODEL), lambda b: (0, 0)),
                  pl.BlockSpec((N_EXPERTS, D_MODEL), lambda b: (0, 0))],
        out_specs=[pl.BlockSpec((None, seq, D_MODEL), lambda b: (b, 0, 0)),
                   pl.BlockSpec((None, n_tiles, N_EXPERTS, TOKEN_TILE), lambda b: (b, 0, 0, 0)),
                   pl.BlockSpec((None, n_tiles, N_EXPERTS, TOKEN_TILE), lambda b: (b, 0, 0, 0)),
                   pl.BlockSpec((None, N_EXPERTS, LANES), lambda b: (b, 0, 0))],
        out_shape=[jax.ShapeDtypeStruct((batch, seq, D_MODEL), BF16),
                   jax.ShapeDtypeStruct((batch, n_tiles, N_EXPERTS, TOKEN_TILE), F32),
                   jax.ShapeDtypeStruct((batch, n_tiles, N_EXPERTS, TOKEN_TILE), F32),
                   jax.ShapeDtypeStruct((batch, N_EXPERTS, LANES), F32)],
        compiler_params=pltpu.CompilerParams(
            dimension_semantics=("arbitrary",), vmem_limit_bytes=VMEM_LIMIT),
        name="route",
    )(h.reshape(batch, seq, D_MODEL), g, w_hi, w_lo)


def _slot_rows(cap):
    return cap + SLOT_WINDOW


def _window_starts(win_ref, b, t, p, cap):
    base = b * (N_EXPERTS * LANES) + t
    return [pl.multiple_of(jnp.minimum(win_ref[base + e * LANES] + p * SLOT_WINDOW, cap), SLOT_ALIGN)
            for e in range(N_EXPERTS)]


def _window_hits(rank_tile, starts):
    rows = lax.broadcasted_iota(jnp.int32, (SLOT_WINDOW, TOKEN_TILE), 0).astype(F32)
    return [(rank_tile[e:e + 1, :] - starts[e].astype(F32)) == rows for e in range(N_EXPERTS)]


def _one_hot(hits):
    return jnp.concatenate([jnp.where(h, 1.0, 0.0).astype(BF16) for h in hits], axis=0)


def _gather_kernel(cap, win_ref, npass_ref, hn_ref, rank_ref, aff_ref, xe_ref, gate_ref):
    b = pl.program_id(0)
    xe_ref[...] = jnp.zeros_like(xe_ref)
    gate_ref[...] = jnp.zeros_like(gate_ref)
    for t in range(hn_ref.shape[0] // TOKEN_TILE):

        def one_pass(p, carry, t=t):
            starts = _window_starts(win_ref, b, t, p, cap)
            hits = _window_hits(rank_ref[t], starts)
            rows = _dot(_one_hot(hits), hn_ref[t * TOKEN_TILE:(t + 1) * TOKEN_TILE, :]).astype(BF16)
            aff_tile = aff_ref[t]
            for e in range(N_EXPERTS):
                win = pl.ds(starts[e], SLOT_WINDOW)
                xe_ref[e, win, :] += rows[e * SLOT_WINDOW:(e + 1) * SLOT_WINDOW]
                g = jnp.sum(jnp.where(hits[e], aff_tile[e:e + 1, :], 0.0), axis=-1, keepdims=True)
                gate_ref[e, win, :] += jnp.broadcast_to(g, (SLOT_WINDOW, LANES))
            return carry

        lax.fori_loop(0, npass_ref[b * LANES + t], one_pass, 0)


def _gather(hn, rank, aff, win, npass, cap):
    batch, seq, _ = hn.shape
    n_tiles = seq // TOKEN_TILE
    grid_spec = pltpu.PrefetchScalarGridSpec(
        num_scalar_prefetch=2,
        grid=(batch,),
        in_specs=[pl.BlockSpec((None, seq, D_MODEL), lambda b, *_: (b, 0, 0)),
                  pl.BlockSpec((None, n_tiles, N_EXPERTS, TOKEN_TILE), lambda b, *_: (b, 0, 0, 0)),
                  pl.BlockSpec((None, n_tiles, N_EXPERTS, TOKEN_TILE), lambda b, *_: (b, 0, 0, 0))],
        out_specs=[pl.BlockSpec((None, N_EXPERTS, _slot_rows(cap), D_MODEL), lambda b, *_: (b, 0, 0, 0)),
                   pl.BlockSpec((None, N_EXPERTS, _slot_rows(cap), LANES), lambda b, *_: (b, 0, 0, 0))])
    return pl.pallas_call(
        functools.partial(_gather_kernel, cap),
        grid_spec=grid_spec,
        out_shape=[jax.ShapeDtypeStruct((batch, N_EXPERTS, _slot_rows(cap), D_MODEL), BF16),
                   jax.ShapeDtypeStruct((batch, N_EXPERTS, _slot_rows(cap), LANES), F32)],
        compiler_params=pltpu.CompilerParams(
            dimension_semantics=("arbitrary",), vmem_limit_bytes=VMEM_LIMIT),
        name="gather",
    )(win, npass, hn, rank, aff)


FFN_SEQS = 4


def _ffn_kernel(cap, xe_ref, gate_ref, wg_ref, wu_ref, wd_ref, y_ref):
    d = xe_ref.shape[-1]
    xe = xe_ref[:, :cap, :].reshape(FFN_SEQS * cap, d)
    a = _dot(xe, wg_ref[...])
    u = _dot(xe, wu_ref[...])
    y = _dot((jax.nn.silu(a) * u).astype(BF16), wd_ref[...])
    gate = gate_ref[:, :cap, :].reshape(FFN_SEQS * cap, LANES)
    y = y * jnp.concatenate([gate] * (d // LANES), axis=1)
    y_ref[:, :cap, :] = y.astype(BF16).reshape(FFN_SEQS, cap, d)
    y_ref[:, cap:, :] = jnp.zeros((FFN_SEQS, SLOT_WINDOW, d), BF16)


def _ffn(xe, gate, wg, wu, wd, layer, cap):
    batch = xe.shape[0]
    d_exp = wg.shape[-1]
    slots = pl.BlockSpec((FFN_SEQS, None, _slot_rows(cap), D_MODEL), lambda e, i: (i, e, 0, 0))
    return pl.pallas_call(
        functools.partial(_ffn_kernel, cap),
        grid=(N_EXPERTS, batch // FFN_SEQS),
        in_specs=[slots,
                  pl.BlockSpec((FFN_SEQS, None, _slot_rows(cap), LANES), lambda e, i: (i, e, 0, 0)),
                  pl.BlockSpec((None, None, D_MODEL, d_exp), lambda e, i: (layer, e, 0, 0)),
                  pl.BlockSpec((None, None, D_MODEL, d_exp), lambda e, i: (layer, e, 0, 0)),
                  pl.BlockSpec((None, None, d_exp, D_MODEL), lambda e, i: (layer, e, 0, 0))],
        out_specs=slots,
        out_shape=jax.ShapeDtypeStruct(xe.shape, BF16),
        compiler_params=pltpu.CompilerParams(
            dimension_semantics=("arbitrary", "arbitrary"), vmem_limit_bytes=VMEM_LIMIT),
        name="ffn",
    )(xe, gate, wg, wu, wd)


def _scatter_kernel(cap, win_ref, npass_ref, y_ref, rank_ref, h_ref, out_ref):
    b = pl.program_id(0)
    t = pl.program_id(1)
    out_ref[...] = h_ref[...]

    def one_pass(p, carry):
        starts = _window_starts(win_ref, b, t, p, cap)
        put = _one_hot(_window_hits(rank_ref[...], starts))
        yw = jnp.concatenate([y_ref[e, pl.ds(starts[e], SLOT_WINDOW), :] for e in range(N_EXPERTS)], axis=0)
        out_ref[...] += lax.dot_general(put, yw, (((0,), (0,)), ((), ())), preferred_element_type=F32)
        return carry

    lax.fori_loop(0, npass_ref[b * LANES + t], one_pass, 0)


def _scatter(y, rank, h, win, npass, cap):
    batch, n_tiles = rank.shape[:2]
    grid_spec = pltpu.PrefetchScalarGridSpec(
        num_scalar_prefetch=2,
        grid=(batch, n_tiles),
        in_specs=[pl.BlockSpec((None, N_EXPERTS, _slot_rows(cap), D_MODEL), lambda b, t, *_: (b, 0, 0, 0)),
                  pl.BlockSpec((None, None, N_EXPERTS, TOKEN_TILE), lambda b, t, *_: (b, t, 0, 0)),
                  pl.BlockSpec((None, TOKEN_TILE, D_MODEL), lambda b, t, *_: (b, t, 0))],
        out_specs=pl.BlockSpec((None, TOKEN_TILE, D_MODEL), lambda b, t, *_: (b, t, 0)))
    return pl.pallas_call(
        functools.partial(_scatter_kernel, cap),
        grid_spec=grid_spec,
        out_shape=jax.ShapeDtypeStruct(h.shape, F32),
        compiler_params=pltpu.CompilerParams(
            dimension_semantics=("arbitrary", "arbitrary"), vmem_limit_bytes=VMEM_LIMIT),
        name="scatter",
    )(win, npass, y, rank, h)


def _slot_windows(starts, n_tiles, cap):
    s = starts.astype(jnp.int32)
    first = (s // SLOT_ALIGN) * SLOT_ALIGN
    span = s[:, :, 1:n_tiles + 1] - first[:, :, :n_tiles]
    npass = jnp.maximum(jnp.max(-(-span // SLOT_WINDOW), axis=1), 1)
    npass = jnp.pad(npass, ((0, 0), (0, LANES - n_tiles)))
    return first.reshape(-1), npass.reshape(-1)


def _final_kernel(x_ref, g_ref, o_ref):
    o_ref[...] = _rms(x_ref[...], g_ref[...])


def _final_norm(x, g):
    m = x.shape[0]
    row = pl.BlockSpec((ROW_TILE, D_MODEL), lambda i: (i, 0))
    return pl.pallas_call(
        _final_kernel,
        grid=(m // ROW_TILE,),
        in_specs=[row, pl.BlockSpec((1, D_MODEL), lambda i: (0, 0))],
        out_specs=row,
        out_shape=jax.ShapeDtypeStruct((m, D_MODEL), F32),
        compiler_params=pltpu.CompilerParams(dimension_semantics=("arbitrary",)),
        name="final_norm",
    )(x, g)


def _arrange_w_in(w):
    scale = LOG2E * HEAD_DIM ** -0.5
    qa, ka, va = w[:, :DIL_QKV], w[:, DIL_QKV:2 * DIL_QKV], w[:, 2 * DIL_QKV:3 * DIL_QKV]
    rest = w[:, 3 * DIL_QKV:]
    parts = []
    for g in range(len(DIL_PATTERNS)):
        sl = slice(g * DIL_OUT, (g + 1) * DIL_OUT)
        parts += [qa[:, sl] * scale, ka[:, sl], va[:, sl]]
    for pr in _window_pair_heads():
        parts += [rest[:, h * HEAD_DIM:(h + 1) * HEAD_DIM] * scale for h in pr]
    parts.append(rest[:, WIN_Q:WIN_Q + 2 * WIN_KV])
    return jnp.concatenate(parts, axis=1).astype(BF16), rest[:, WIN_Q + 2 * WIN_KV:].astype(BF16)


def _arrange_w_branch_b(w):
    return jnp.concatenate([w[h * HEAD_DIM:(h + 1) * HEAD_DIM] for pr in _window_pair_heads() for h in pr],
                           axis=0).astype(BF16)


def kernel(x, norm_mix, w_in, w_branch_a, w_branch_b, b_gate, sink_logit, w_out, norm_ffn, w_router,
           w_expert_gate, w_expert_up, w_expert_down, norm_final):
    batch, seq, d = x.shape
    depth = w_in.shape[0]
    cap = CAPACITY_FACTOR * seq // N_EXPERTS
    m = batch * seq
    h = x.reshape(m, d)
    wg_all, wu_all, wd_all = (w.astype(BF16) for w in (w_expert_gate, w_expert_up, w_expert_down))
    for l in range(depth):
        w_attn, w_gates = _arrange_w_in(w_in[l])
        a0, a1, a2, qw, kv = _norm_proj(h, norm_mix[l][None, :], w_attn)
        oa = _dilated_attention(a0, a1, a2, batch, seq).reshape(m, DIL_OUT)
        ow = _window_attention(qw, kv, sink_logit[l][None, :], batch, seq).reshape(m, WIN_Q)
        h = _merge_out(oa, ow, h, norm_mix[l][None, :], w_gates, b_gate[l][None, :],
                       w_branch_a[l].astype(BF16), _arrange_w_branch_b(w_branch_b[l]), w_out[l].astype(BF16))
        wr = w_router[l].T
        wr_hi = wr.astype(BF16)
        wr_lo = (wr - wr_hi.astype(F32)).astype(BF16)
        hn, rank, aff, starts = _route(h, norm_ffn[l][None, :], wr_hi, wr_lo, batch, seq, cap)
        win, npass = _slot_windows(starts, seq // TOKEN_TILE, cap)
        xe, gate = _gather(hn, rank, aff, win, npass, cap)
        y = _ffn(xe, gate, wg_all, wu_all, wd_all, l, cap)
        h = _scatter(y, rank, h.reshape(batch, seq, d), win, npass, cap).reshape(m, d)
    return _final_norm(h, norm_final[None, :]).reshape(batch, seq, d)
```

```python
import functools

import jax
import jax.numpy as jnp
from jax import lax
from jax.experimental import pallas as pl
from jax.experimental.pallas import tpu as pltpu

D_MODEL = 1024
HEAD_DIM = 64
DIL_PATTERNS = ((128, 1), (512, 4), (2048, 16))
DIL_HEADS = 4
N_DIL_SUB = DIL_HEADS * len(DIL_PATTERNS)
DIL_QKV = N_DIL_SUB * HEAD_DIM
DIL_OUT = DIL_HEADS * HEAD_DIM
DIL_HALF = 64
WIN_HALF = 128
WIN_Q = D_MODEL
WIN_Q_HEADS = WIN_Q // HEAD_DIM
WIN_KV_HEADS = 4
WIN_KV = WIN_KV_HEADS * HEAD_DIM
N_EXPERTS = 16
CAPACITY_FACTOR = 2
RMS_EPS = 1e-6
NEG_INF = -1e30
LOG2E = 1.4426950408889634

LANES = 128
GROUP_COLS = 3 * DIL_OUT
DIL_PAIRS = DIL_OUT // LANES
WIN_UNITS = WIN_KV_HEADS // 2
WIN_UNIT_PAIRS = WIN_Q_HEADS // (2 * WIN_UNITS)
Q_TILE = 128
DIL_INFLIGHT = 8
ROW_TILE = 512
VMEM_LIMIT = 56 * 1024 * 1024

F32 = jnp.float32
BF16 = jnp.bfloat16


def _alibi_slopes(n):
    return [float(2.0 ** (-8.0 * i / n)) for i in range(1, n + 1)]


def _rms(x, g):
    return x * lax.rsqrt(jnp.mean(x * x, axis=-1, keepdims=True) + RMS_EPS) * g


def _dot(a, b):
    return jnp.dot(a, b, preferred_element_type=F32)


def _dot_nt(a, b):
    return lax.dot_general(a, b, (((1,), (1,)), ((), ())), preferred_element_type=F32)


def _left_lanes():
    return lax.broadcasted_iota(jnp.int32, (1, LANES), 1) < HEAD_DIM


_C_A = (0, GROUP_COLS, 2 * GROUP_COLS)
_C_QW = 3 * GROUP_COLS
_C_KV = _C_QW + WIN_Q
_COL_CHUNK = 512


def _proj_store(ref, xb, w_ref, c0, cw):
    for j in range(0, cw, _COL_CHUNK):
        jw = min(_COL_CHUNK, cw - j)
        ref[:, j:j + jw] = _dot(xb, w_ref[:, c0 + j:c0 + j + jw]).astype(BF16)


def _norm_proj_kernel(x_ref, g_ref, w_ref, a0_ref, a1_ref, a2_ref, qw_ref, kv_ref, xs_ref):
    _project(x_ref[...], g_ref, w_ref, a0_ref, a1_ref, a2_ref, qw_ref, kv_ref, xs_ref)


def _project(x, g_ref, w_ref, a0_ref, a1_ref, a2_ref, qw_ref, kv_ref, xs_ref):
    xn = _rms(x, g_ref[...])
    n_lane_tiles = D_MODEL // LANES
    for j in range(n_lane_tiles):
        xs_ref[j] = xn[:, j * LANES:(j + 1) * LANES]
    xb = xn.astype(BF16)
    _proj_store(a0_ref, xb, w_ref, _C_A[0], GROUP_COLS)
    for grp, ref in ((1, a1_ref), (2, a2_ref)):
        r = DIL_PATTERNS[grp][1]
        n = ROW_TILE // r
        xp = jnp.concatenate(
            [jnp.concatenate([xs_ref[j, pl.ds(c, n, stride=r), :] for j in range(n_lane_tiles)], axis=1)
             for c in range(r)], axis=0).astype(BF16)
        res = _dot(xp, w_ref[:, _C_A[grp]:_C_A[grp] + GROUP_COLS])
        for c in range(r):
            ref[:, c * GROUP_COLS:(c + 1) * GROUP_COLS] = res[c * n:(c + 1) * n].astype(BF16)
    _proj_store(qw_ref, xb, w_ref, _C_QW, WIN_Q)
    _proj_store(kv_ref, xb, w_ref, _C_KV, 2 * WIN_KV)


def _proj_outputs(m, row_index):
    out_shape, out_specs = [], []
    for _, r in DIL_PATTERNS:
        out_shape.append(jax.ShapeDtypeStruct((m // r, r * GROUP_COLS), BF16))
        out_specs.append(pl.BlockSpec((ROW_TILE // r, r * GROUP_COLS), row_index))
    for c in (WIN_Q, 2 * WIN_KV):
        out_shape.append(jax.ShapeDtypeStruct((m, c), BF16))
        out_specs.append(pl.BlockSpec((ROW_TILE, c), row_index))
    return out_shape, out_specs


def _norm_proj(x, g, w):
    m = x.shape[0]
    row = lambda c: pl.BlockSpec((ROW_TILE, c), lambda i: (i, 0))
    full = lambda a: pl.BlockSpec(a.shape, lambda i: (0, 0), pipeline_mode=pl.Buffered(1))
    in_specs = [row(D_MODEL), full(g), full(w)]
    out_shape, out_specs = _proj_outputs(m, lambda i: (i, 0))
    return pl.pallas_call(
        _norm_proj_kernel,
        grid=(m // ROW_TILE,),
        in_specs=in_specs,
        out_specs=out_specs,
        out_shape=out_shape,
        scratch_shapes=[pltpu.VMEM((D_MODEL // LANES, ROW_TILE, LANES), F32)],
        compiler_params=pltpu.CompilerParams(
            dimension_semantics=("arbitrary",), vmem_limit_bytes=VMEM_LIMIT),
        name="norm_proj",
    )(x, g, w)


def _band_bias(tq, win, off, half, slopes):
    col = lax.broadcasted_iota(jnp.int32, (tq, win), 1)
    row = lax.broadcasted_iota(jnp.int32, (tq, win), 0)
    dist = jnp.abs(col - row + off).astype(F32)
    inside = dist <= float(half)
    return jnp.concatenate([jnp.where(inside, dist * (-s * LOG2E), NEG_INF) for s in slopes], axis=0)


def _ones_blockdiag(win):
    left = lax.broadcasted_iota(jnp.int32, (2 * win, LANES), 1) < HEAD_DIM
    top = jnp.where(lax.broadcasted_iota(jnp.int32, (2 * win, LANES), 0) < win, 1.0, 0.0)
    return jnp.where(left, top, 1.0 - top).astype(BF16)


def _pair_scores(q_tiles, k_win, bias):
    left = _left_lanes()
    zero = jnp.zeros((), BF16)
    rows = []
    for q in q_tiles:
        rows += [jnp.where(left, q, zero), jnp.where(left, zero, q)]
    return _dot_nt(jnp.concatenate(rows, axis=0), k_win) + bias


def _pair_values(s_ref, v_win, ones_bd, sink_ref, p_ref, aux_ref):
    win = v_win.shape[0]
    n = s_ref.shape[0] // (2 * Q_TILE)
    left = _left_lanes()
    zero = jnp.zeros((), BF16)
    for i in range(n):
        ms = []
        for side in range(2):
            rows = pl.ds((2 * i + side) * Q_TILE, Q_TILE)
            s = s_ref[rows, :]
            m = jnp.broadcast_to(jnp.max(s, axis=-1, keepdims=True), (Q_TILE, LANES))
            if sink_ref is not None:
                m = jnp.maximum(m, sink_ref[rows, :])
            p = jnp.exp2(s - jnp.concatenate([m] * (win // LANES), axis=1)).astype(BF16)
            p_ref[i * Q_TILE:(i + 1) * Q_TILE, side * win:(side + 1) * win] = p
            ms.append(m)
        m_pair = jnp.where(left, ms[0], ms[1])
        if sink_ref is not None:
            sink_pair = jnp.where(left, sink_ref[pl.ds(2 * i * Q_TILE, Q_TILE), :],
                                  sink_ref[pl.ds((2 * i + 1) * Q_TILE, Q_TILE), :])
            aux_ref[i * Q_TILE:(i + 1) * Q_TILE, :] = jnp.exp2(sink_pair - m_pair)
        else:
            aux_ref[i * Q_TILE:(i + 1) * Q_TILE, :] = m_pair
    vbd = jnp.concatenate([jnp.where(left, v_win, zero), jnp.where(left, zero, v_win)], axis=0)
    on = _dot(p_ref[...], jnp.concatenate([vbd, ones_bd], axis=1))
    return on[:, :LANES], on[:, LANES:], aux_ref[...]


def _tile_variant(t, n_tiles):
    return jnp.where(t == 0, 0, jnp.where(t == n_tiles - 1, 2, 1))


def _pipelined_tiles(n_tiles, score_stage, value_stage):
    score_stage(0, 0)

    def body(i, carry):
        t = 2 * i
        score_stage(t + 1, 1)
        value_stage(t, 0)
        score_stage(jnp.minimum(t + 2, n_tiles - 1), 0)
        value_stage(t + 1, 1)
        return carry

    lax.fori_loop(0, n_tiles // 2, body, 0)


def _dil_geometry(seq_full, grp):
    r = DIL_PATTERNS[grp][1]
    seq = seq_full // r
    win = min(seq, Q_TILE + 2 * DIL_HALF)
    n_tiles = seq // Q_TILE
    pad = (win - Q_TILE) // 2
    offs = [0] if n_tiles == 1 else [0, -pad, -2 * pad]
    return r, seq, win, n_tiles, pad, offs


def _dilated_kernel(seq_full, a0_ref, a1_ref, a2_ref, out_ref, o_nat, l_nat, b0, b1, b2, ones_ref, ones2_ref,
                    s_wide, s_narrow, p_wide, p_narrow, aux_ref):
    a_refs = (a0_ref, a1_ref, a2_ref)
    bias_refs = (b0, b1, b2)
    s_refs = (s_wide, s_wide, s_narrow)
    p_refs = (p_wide, p_wide, p_narrow)
    slopes = _alibi_slopes(N_DIL_SUB)

    @pl.when(pl.program_id(0) == 0)
    def _():
        for grp in range(3):
            r, seq, win, n_tiles, pad, offs = _dil_geometry(seq_full, grp)
            for v, off in enumerate(offs):
                for pr in range(DIL_PAIRS):
                    hs = slopes[grp * DIL_HEADS + 2 * pr:grp * DIL_HEADS + 2 * pr + 2]
                    bias_refs[grp][v, pr] = _band_bias(Q_TILE, win, off, DIL_HALF, [s * r for s in hs])
        ones_ref[...] = _ones_blockdiag(_dil_geometry(seq_full, 0)[2])
        ones2_ref[...] = _ones_blockdiag(_dil_geometry(seq_full, 2)[2])

    def pair_tile(grp, c, t, pr, slot):
        r, seq, win, n_tiles, pad, offs = _dil_geometry(seq_full, grp)
        a_ref = a_refs[grp]
        ones_bd_ref = ones2_ref if grp == 2 else ones_ref
        if n_tiles == 1:
            q0, ks, var = 0, 0, 0
        else:
            q0 = pl.multiple_of(t * Q_TILE, Q_TILE)
            ks = pl.multiple_of(jnp.clip(q0 - pad, 0, seq - win), DIL_HALF)
            var = _tile_variant(t, n_tiles)
        lo = c * GROUP_COLS + pr * LANES
        q = a_ref[pl.ds(q0, Q_TILE), lo:lo + LANES]
        k = a_ref[pl.ds(ks, win), DIL_OUT + lo:DIL_OUT + lo + LANES]
        v = a_ref[pl.ds(ks, win), 2 * DIL_OUT + lo:2 * DIL_OUT + lo + LANES]
        s_ref = s_refs[grp].at[slot]
        s_ref[...] = _pair_scores([q], k, bias_refs[grp][var, pr])
        num, den, m = _pair_values(s_ref, v, ones_bd_ref[...], None, p_refs[grp].at[slot], aux_ref.at[slot])
        rows = pl.ds(q0, Q_TILE) if r == 1 else pl.ds(c + r * q0, Q_TILE, stride=r)
        o_nat[grp, pr, rows, :] = num / den
        l_nat[grp, pr, rows, :] = m + jnp.log2(den)

    for grp in range(3):
        r, seq, win, n_tiles, pad, offs = _dil_geometry(seq_full, grp)
        per_body = DIL_INFLIGHT // DIL_PAIRS
        if n_tiles == 1:
            for c in range(r):
                for pr in range(DIL_PAIRS):
                    pair_tile(grp, c, 0, pr, c * DIL_PAIRS + pr)
        elif r >= per_body:
            for c0 in range(0, r, per_body):
                def body(t, carry, grp=grp, c0=c0):
                    for j in range(per_body):
                        for pr in range(DIL_PAIRS):
                            pair_tile(grp, c0 + j, t, pr, j * DIL_PAIRS + pr)
                    return carry
                lax.fori_loop(0, n_tiles, body, 0)
        else:
            tiles_per_body = per_body // r
            def body(i, carry, grp=grp, r=r, tiles_per_body=tiles_per_body):
                for j in range(tiles_per_body):
                    for c in range(r):
                        for pr in range(DIL_PAIRS):
                            pair_tile(grp, c, i * tiles_per_body + j, pr, (j * r + c) * DIL_PAIRS + pr)
                return carry
            lax.fori_loop(0, n_tiles // tiles_per_body, body, 0)

    def combine(i, carry):
        rows = pl.ds(pl.multiple_of(i * ROW_TILE, ROW_TILE), ROW_TILE)
        for pr in range(DIL_PAIRS):
            ls = [l_nat[g, pr, rows, :] for g in range(3)]
            mx = jnp.maximum(jnp.maximum(ls[0], ls[1]), ls[2])
            es = [jnp.exp2(l - mx) for l in ls]
            num = es[0] * o_nat[0, pr, rows, :] + es[1] * o_nat[1, pr, rows, :] + es[2] * o_nat[2, pr, rows, :]
            out_ref[rows, pr * LANES:(pr + 1) * LANES] = (num / (es[0] + es[1] + es[2])).astype(BF16)
        return carry

    lax.fori_loop(0, seq_full // ROW_TILE, combine, 0)


def _dilated_attention(a0, a1, a2, batch, seq_full):
    views, in_specs, bias_shapes = [], [], []
    for grp, a in enumerate((a0, a1, a2)):
        r, seq, win, n_tiles, pad, offs = _dil_geometry(seq_full, grp)
        views.append(a.reshape(batch, seq, r * GROUP_COLS))
        in_specs.append(pl.BlockSpec((None, seq, r * GROUP_COLS), lambda b: (b, 0, 0)))
        bias_shapes.append(pltpu.VMEM((len(offs), DIL_PAIRS, 2 * Q_TILE, win), F32))
    win0 = _dil_geometry(seq_full, 0)[2]
    win2 = _dil_geometry(seq_full, 2)[2]
    n_straight = DIL_PATTERNS[2][1] * DIL_PAIRS
    return pl.pallas_call(
        functools.partial(_dilated_kernel, seq_full),
        grid=(batch,),
        in_specs=in_specs,
        out_specs=pl.BlockSpec((None, seq_full, DIL_OUT), lambda b: (b, 0, 0)),
        out_shape=jax.ShapeDtypeStruct((batch, seq_full, DIL_OUT), BF16),
        scratch_shapes=[pltpu.VMEM((3, DIL_PAIRS, seq_full, LANES), F32),
                        pltpu.VMEM((3, DIL_PAIRS, seq_full, LANES), F32)]
        + bias_shapes + [pltpu.VMEM((2 * win0, LANES), BF16), pltpu.VMEM((2 * win2, LANES), BF16)]
        + [pltpu.VMEM((n, 2 * Q_TILE, w), F32) for n, w in ((DIL_INFLIGHT, win0), (n_straight, win2))]
        + [pltpu.VMEM((n, Q_TILE, 2 * w), BF16) for n, w in ((DIL_INFLIGHT, win0), (n_straight, win2))]
        + [pltpu.VMEM((max(DIL_INFLIGHT, n_straight), Q_TILE, LANES), F32)],
        compiler_params=pltpu.CompilerParams(
            dimension_semantics=("arbitrary",), vmem_limit_bytes=VMEM_LIMIT),
        name="dilated",
    )(*views)


WIN_WINDOW = Q_TILE + 2 * WIN_HALF


def _window_pair_heads():
    group = WIN_Q_HEADS // WIN_KV_HEADS
    pairs = []
    for u in range(WIN_UNITS):
        for i in range(WIN_UNIT_PAIRS):
            pairs.append((2 * u * group + i, (2 * u + 1) * group + i))
    return pairs


def _window_kernel(seq, q_ref, kv_ref, sink_ref, o_ref, bias_ref, ones_ref, sink_rows_ref, s_ref, p_ref,
                   aux_ref):
    n_tiles = seq // Q_TILE
    slopes = _alibi_slopes(WIN_Q_HEADS)
    pair_heads = _window_pair_heads()
    unit_rows = 2 * WIN_UNIT_PAIRS * Q_TILE

    @pl.when(pl.program_id(0) == 0)
    def _():
        for u in range(WIN_UNITS):
            heads = [h for pr in pair_heads[u * WIN_UNIT_PAIRS:(u + 1) * WIN_UNIT_PAIRS] for h in pr]
            for v, off in enumerate((0, -WIN_HALF, -2 * WIN_HALF)):
                bias_ref[v, u] = _band_bias(Q_TILE, WIN_WINDOW, off, WIN_HALF, [slopes[h] for h in heads])
            sink_rows_ref[u] = jnp.concatenate(
                [jnp.broadcast_to(sink_ref[0:1, h:h + 1] * LOG2E, (Q_TILE, LANES)) for h in heads], axis=0)
        ones_ref[...] = _ones_blockdiag(WIN_WINDOW)

    def rows_of(t):
        q0 = pl.multiple_of(t * Q_TILE, Q_TILE)
        ks = pl.multiple_of(jnp.clip(q0 - WIN_HALF, 0, seq - WIN_WINDOW), Q_TILE)
        return q0, ks

    def unit_cols(u):
        return [(u * WIN_UNIT_PAIRS + i) * LANES for i in range(WIN_UNIT_PAIRS)]

    def score_stage(t, k_set):
        q0, ks = rows_of(t)
        var = _tile_variant(t, n_tiles)
        for u in range(WIN_UNITS):
            qs = [q_ref[pl.ds(q0, Q_TILE), c:c + LANES] for c in unit_cols(u)]
            k = kv_ref[pl.ds(ks, WIN_WINDOW), u * LANES:(u + 1) * LANES]
            s_ref[k_set * WIN_UNITS + u] = _pair_scores(qs, k, bias_ref[var, u])

    def value_stage(t, k_set):
        q0, ks = rows_of(t)
        for u in range(WIN_UNITS):
            slot = k_set * WIN_UNITS + u
            v = kv_ref[pl.ds(ks, WIN_WINDOW), WIN_KV + u * LANES:WIN_KV + (u + 1) * LANES]
            num, den, sink_term = _pair_values(s_ref.at[slot], v, ones_ref[...], sink_rows_ref.at[u],
                                               p_ref.at[slot], aux_ref.at[slot])
            o = (num / (den + sink_term)).astype(BF16)
            for i, c in enumerate(unit_cols(u)):
                o_ref[pl.ds(q0, Q_TILE), c:c + LANES] = o[i * Q_TILE:(i + 1) * Q_TILE]

    _pipelined_tiles(n_tiles, score_stage, value_stage)


def _window_attention(q, kv, sink, batch, seq):
    unit_rows = 2 * WIN_UNIT_PAIRS * Q_TILE
    return pl.pallas_call(
        functools.partial(_window_kernel, seq),
        grid=(batch,),
        in_specs=[pl.BlockSpec((None, seq, WIN_Q), lambda b: (b, 0, 0)),
                  pl.BlockSpec((None, seq, 2 * WIN_KV), lambda b: (b, 0, 0)),
                  pl.BlockSpec((1, WIN_Q_HEADS), lambda b: (0, 0))],
        out_specs=pl.BlockSpec((None, seq, WIN_Q), lambda b: (b, 0, 0)),
        out_shape=jax.ShapeDtypeStruct((batch, seq, WIN_Q), BF16),
        scratch_shapes=[pltpu.VMEM((3, WIN_UNITS, unit_rows, WIN_WINDOW), F32),
                        pltpu.VMEM((2 * WIN_WINDOW, LANES), BF16),
                        pltpu.VMEM((WIN_UNITS, unit_rows, LANES), F32),
                        pltpu.VMEM((2 * WIN_UNITS, unit_rows, WIN_WINDOW), F32),
                        pltpu.VMEM((2 * WIN_UNITS, unit_rows // 2, 2 * WIN_WINDOW), BF16),
                        pltpu.VMEM((2 * WIN_UNITS, unit_rows // 2, LANES), F32)],
        compiler_params=pltpu.CompilerParams(
            dimension_semantics=("arbitrary",), vmem_limit_bytes=VMEM_LIMIT),
        name="window",
    )(q.reshape(batch, seq, WIN_Q), kv.reshape(batch, seq, 2 * WIN_KV), sink)


def _merge_kernel(oa_ref, ow_ref, h_ref, g_ref, wgate_ref, bgate_ref, wa_ref, wb_ref, wo_ref, out_ref):
    h = h_ref[...]
    xb = _rms(h, g_ref[...]).astype(BF16)
    merged = None
    for br, (o_ref, w_ref) in enumerate(((oa_ref, wa_ref), (ow_ref, wb_ref))):
        cols = slice(br * D_MODEL, (br + 1) * D_MODEL)
        gate = jax.nn.sigmoid(_dot(xb, wgate_ref[:, cols]) + bgate_ref[:, cols])
        term = gate * _dot(o_ref[...], w_ref[...])
        merged = term if merged is None else merged + term
    out_ref[...] = h + _dot(merged.astype(BF16), wo_ref[...])


def _merge_out(oa, ow, h, g, wgate, bgate, wa, wb, wo):
    m = h.shape[0]
    row = lambda c: pl.BlockSpec((ROW_TILE, c), lambda i: (i, 0))
    full = lambda a: pl.BlockSpec(a.shape, lambda i: (0, 0), pipeline_mode=pl.Buffered(1))
    return pl.pallas_call(
        _merge_kernel,
        grid=(m // ROW_TILE,),
        in_specs=[row(DIL_OUT), row(WIN_Q), row(D_MODEL), full(g), full(wgate), full(bgate),
                  full(wa), full(wb), full(wo)],
        out_specs=row(D_MODEL),
        out_shape=jax.ShapeDtypeStruct((m, D_MODEL), F32),
        compiler_params=pltpu.CompilerParams(
            dimension_semantics=("arbitrary",), vmem_limit_bytes=VMEM_LIMIT),
        name="merge_out",
    )(oa, ow, h, g, wgate, bgate, wa, wb, wo)


TOKEN_TILE = 256
CUM_CHUNK = TOKEN_TILE
SLOT_WINDOW = 64
SLOT_ALIGN = 16


def _prefix_exclusive(mask_f, tri):
    e, s = mask_f.shape
    carry = jnp.zeros((e, 1), F32)
    parts, carries = [], []
    for j in range(0, s, CUM_CHUNK):
        blk = mask_f[:, j:j + CUM_CHUNK]
        carries.append(carry)
        parts.append(_dot(blk.astype(BF16), tri) + carry)
        carry = carry + jnp.sum(blk, axis=-1, keepdims=True)
    return jnp.concatenate(parts, axis=-1), carries + [carry]


def _route_kernel(cap, h_ref, g_ref, whi_ref, wlo_ref, hn_ref, rank_ref, aff_ref, starts_ref):
    seq = h_ref.shape[0]
    hn = _rms(h_ref[...], g_ref[...])
    hn_hi = hn.astype(BF16)
    hn_lo = (hn - hn_hi.astype(F32)).astype(BF16)
    hn_ref[...] = hn_hi
    logits = (_dot_nt(whi_ref[...], hn_hi) + _dot_nt(whi_ref[...], hn_lo)
              + _dot_nt(wlo_ref[...], hn_hi))
    mx = jnp.max(logits, axis=0, keepdims=True)
    ex = jnp.exp(logits - mx)
    aff = ex / jnp.sum(ex, axis=0, keepdims=True)
    bits = pltpu.bitcast(aff, jnp.int32)

    def enough(t):
        return jnp.sum(jnp.where(bits >= t, 1.0, 0.0), axis=-1, keepdims=True) >= float(cap)

    def search4(_, c):
        lo, hi = c
        q = (hi - lo) >> 2
        m1, m2, m3 = lo + q, lo + 2 * q, lo + 3 * q
        ok1, ok2, ok3 = enough(m1), enough(m2), enough(m3)
        return (jnp.where(ok3, m3, jnp.where(ok2, m2, jnp.where(ok1, m1, lo))),
                jnp.where(ok3, hi, jnp.where(ok2, m3, jnp.where(ok1, m2, m1))))

    def search2(_, c):
        lo, hi = c
        mid = lo + ((hi - lo) >> 1)
        ok = enough(mid)
        return jnp.where(ok, mid, lo), jnp.where(ok, hi, mid)

    lo0 = jnp.zeros((N_EXPERTS, 1), jnp.int32)
    hi0 = jnp.full((N_EXPERTS, 1), 0x3F800001, jnp.int32)
    thr, _ = lax.fori_loop(0, 3, search2, lax.fori_loop(0, 15, search4, (lo0, hi0)))

    r_i = lax.broadcasted_iota(jnp.int32, (CUM_CHUNK, CUM_CHUNK), 0)
    c_i = lax.broadcasted_iota(jnp.int32, (CUM_CHUNK, CUM_CHUNK), 1)
    tri = jnp.where(r_i < c_i, 1.0, 0.0).astype(BF16)
    gt = jnp.where(bits > thr, 1.0, 0.0)
    eq = jnp.where(bits == thr, 1.0, 0.0)
    need = float(cap) - jnp.sum(gt, axis=-1, keepdims=True)
    tie_rank, _ = _prefix_exclusive(eq, tri)
    sel = gt + eq * jnp.where(tie_rank < need, 1.0, 0.0)
    slot, starts = _prefix_exclusive(sel, tri)
    rank = jnp.where(sel > 0.0, slot, -1.0)
    for t in range(seq // TOKEN_TILE):
        rank_ref[t] = rank[:, t * TOKEN_TILE:(t + 1) * TOKEN_TILE]
        aff_ref[t] = aff[:, t * TOKEN_TILE:(t + 1) * TOKEN_TILE]
    lane = lax.broadcasted_iota(jnp.int32, (N_EXPERTS, LANES), 1)
    acc = jnp.zeros((N_EXPERTS, LANES), F32)
    for t, c in enumerate(starts):
        acc = acc + jnp.where(lane == t, c, 0.0)
    starts_ref[...] = acc


def _route(h, g, w_hi, w_lo, batch, seq, cap):
    n_tiles = seq // TOKEN_TILE
    return pl.pallas_call(
        functools.partial(_route_kernel, cap),
        grid=(batch,),
        in_specs=[pl.BlockSpec((None, seq, D_MODEL), lambda b: (b, 0, 0)),
                  pl.BlockSpec((1, D_MODEL), lambda b: (0, 0)),
                  pl.BlockSpec((N_EXPERTS, D_MODEL), lambda b: (0, 0)),
                  pl.BlockSpec((N_EXPERTS, D_MODEL), lambda b: (0, 0))],
        out_specs=[pl.BlockSpec((None, seq, D_MODEL), lambda b: (b, 0, 0)),
                   pl.BlockSpec((None, n_tiles, N_EXPERTS, TOKEN_TILE), lambda b: (b, 0, 0, 0)),
                   pl.BlockSpec((None, n_tiles, N_EXPERTS, TOKEN_TILE), lambda b: (b, 0, 0, 0)),
                   pl.BlockSpec((None, N_EXPERTS, LANES), lambda b: (b, 0, 0))],
        out_shape=[jax.ShapeDtypeStruct((batch, seq, D_MODEL), BF16),
                   jax.ShapeDtypeStruct((batch, n_tiles, N_EXPERTS, TOKEN_TILE), F32),
                   jax.ShapeDtypeStruct((batch, n_tiles, N_EXPERTS, TOKEN_TILE), F32),
                   jax.ShapeDtypeStruct((batch, N_EXPERTS, LANES), F32)],
        compiler_params=pltpu.CompilerParams(
            dimension_semantics=("arbitrary",), vmem_limit_bytes=VMEM_LIMIT),
        name="route",
    )(h.reshape(batch, seq, D_MODEL), g, w_hi, w_lo)


def _slot_rows(cap):
    return cap + SLOT_WINDOW


def _window_starts(win_ref, b, t, p, cap):
    base = b * (N_EXPERTS * LANES) + t
    return [pl.multiple_of(jnp.minimum(win_ref[base + e * LANES] + p * SLOT_WINDOW, cap), SLOT_ALIGN)
            for e in range(N_EXPERTS)]


def _window_hits(rank_tile, starts):
    rows = lax.broadcasted_iota(jnp.int32, (SLOT_WINDOW, TOKEN_TILE), 0).astype(F32)
    return [(rank_tile[e:e + 1, :] - starts[e].astype(F32)) == rows for e in range(N_EXPERTS)]


def _one_hot(hits):
    return jnp.concatenate([jnp.where(h, 1.0, 0.0).astype(BF16) for h in hits], axis=0)


def _gather_kernel(cap, win_ref, npass_ref, hn_ref, rank_ref, aff_ref, xe_ref, gate_ref):
    b = pl.program_id(0)
    xe_ref[...] = jnp.zeros_like(xe_ref)
    gate_ref[...] = jnp.zeros_like(gate_ref)
    for t in range(hn_ref.shape[0] // TOKEN_TILE):

        def one_pass(p, carry, t=t):
            starts = _window_starts(win_ref, b, t, p, cap)
            hits = _window_hits(rank_ref[t], starts)
            rows = _dot(_one_hot(hits), hn_ref[t * TOKEN_TILE:(t + 1) * TOKEN_TILE, :]).astype(BF16)
            aff_tile = aff_ref[t]
            for e in range(N_EXPERTS):
                win = pl.ds(starts[e], SLOT_WINDOW)
                xe_ref[e, win, :] += rows[e * SLOT_WINDOW:(e + 1) * SLOT_WINDOW]
                g = jnp.sum(jnp.where(hits[e], aff_tile[e:e + 1, :], 0.0), axis=-1, keepdims=True)
                gate_ref[e, win, :] += jnp.broadcast_to(g, (SLOT_WINDOW, LANES))
            return carry

        lax.fori_loop(0, npass_ref[b * LANES + t], one_pass, 0)


def _gather(hn, rank, aff, win, npass, cap):
    batch, seq, _ = hn.shape
    n_tiles = seq // TOKEN_TILE
    grid_spec = pltpu.PrefetchScalarGridSpec(
        num_scalar_prefetch=2,
        grid=(batch,),
        in_specs=[pl.BlockSpec((None, seq, D_MODEL), lambda b, *_: (b, 0, 0)),
                  pl.BlockSpec((None, n_tiles, N_EXPERTS, TOKEN_TILE), lambda b, *_: (b, 0, 0, 0)),
                  pl.BlockSpec((None, n_tiles, N_EXPERTS, TOKEN_TILE), lambda b, *_: (b, 0, 0, 0))],
        out_specs=[pl.BlockSpec((None, N_EXPERTS, _slot_rows(cap), D_MODEL), lambda b, *_: (b, 0, 0, 0)),
                   pl.BlockSpec((None, N_EXPERTS, _slot_rows(cap), LANES), lambda b, *_: (b, 0, 0, 0))])
    return pl.pallas_call(
        functools.partial(_gather_kernel, cap),
        grid_spec=grid_spec,
        out_shape=[jax.ShapeDtypeStruct((batch, N_EXPERTS, _slot_rows(cap), D_MODEL), BF16),
                   jax.ShapeDtypeStruct((batch, N_EXPERTS, _slot_rows(cap), LANES), F32)],
        compiler_params=pltpu.CompilerParams(
            dimension_semantics=("arbitrary",), vmem_limit_bytes=VMEM_LIMIT),
        name="gather",
    )(win, npass, hn, rank, aff)


FFN_SEQS = 4


def _ffn_kernel(cap, xe_ref, gate_ref, wg_ref, wu_ref, wd_ref, y_ref):
    d = xe_ref.shape[-1]
    xe = xe_ref[:, :cap, :].reshape(FFN_SEQS * cap, d)
    a = _dot(xe, wg_ref[...])
    u = _dot(xe, wu_ref[...])
    y = _dot((jax.nn.silu(a) * u).astype(BF16), wd_ref[...])
    gate = gate_ref[:, :cap, :].reshape(FFN_SEQS * cap, LANES)
    y = y * jnp.concatenate([gate] * (d // LANES), axis=1)
    y_ref[:, :cap, :] = y.astype(BF16).reshape(FFN_SEQS, cap, d)
    y_ref[:, cap:, :] = jnp.zeros((FFN_SEQS, SLOT_WINDOW, d), BF16)


def _ffn(xe, gate, wg, wu, wd, layer, cap):
    batch = xe.shape[0]
    d_exp = wg.shape[-1]
    slots = pl.BlockSpec((FFN_SEQS, None, _slot_rows(cap), D_MODEL), lambda e, i: (i, e, 0, 0))
    return pl.pallas_call(
        functools.partial(_ffn_kernel, cap),
        grid=(N_EXPERTS, batch // FFN_SEQS),
        in_specs=[slots,
                  pl.BlockSpec((FFN_SEQS, None, _slot_rows(cap), LANES), lambda e, i: (i, e, 0, 0)),
                  pl.BlockSpec((None, None, D_MODEL, d_exp), lambda e, i: (layer, e, 0, 0)),
                  pl.BlockSpec((None, None, D_MODEL, d_exp), lambda e, i: (layer, e, 0, 0)),
                  pl.BlockSpec((None, None, d_exp, D_MODEL), lambda e, i: (layer, e, 0, 0))],
        out_specs=slots,
        out_shape=jax.ShapeDtypeStruct(xe.shape, BF16),
        compiler_params=pltpu.CompilerParams(
            dimension_semantics=("arbitrary", "arbitrary"), vmem_limit_bytes=VMEM_LIMIT),
        name="ffn",
    )(xe, gate, wg, wu, wd)


TILES_PER_ROW_TILE = ROW_TILE // TOKEN_TILE


def _scatter_kernel(cap, project, win_ref, npass_ref, y_ref, rank_ref, h_ref, g_ref, *rest):
    if project:
        w_ref, out_ref, a0_ref, a1_ref, a2_ref, qw_ref, kv_ref, xs_ref = rest
    else:
        out_ref, = rest
    b = pl.program_id(0)
    j = pl.program_id(1)
    out_ref[...] = h_ref[...]
    for tt in range(TILES_PER_ROW_TILE):
        t = j * TILES_PER_ROW_TILE + tt
        rows = slice(tt * TOKEN_TILE, (tt + 1) * TOKEN_TILE)

        def one_pass(p, carry, t=t, tt=tt, rows=rows):
            starts = _window_starts(win_ref, b, t, p, cap)
            put = _one_hot(_window_hits(rank_ref[tt], starts))
            yw = jnp.concatenate([y_ref[e, pl.ds(starts[e], SLOT_WINDOW), :] for e in range(N_EXPERTS)],
                                 axis=0)
            out_ref[rows, :] += lax.dot_general(put, yw, (((0,), (0,)), ((), ())), preferred_element_type=F32)
            return carry

        lax.fori_loop(0, npass_ref[b * LANES + t], one_pass, 0)
    if project:
        _project(out_ref[...], g_ref, w_ref, a0_ref, a1_ref, a2_ref, qw_ref, kv_ref, xs_ref)
    else:
        out_ref[...] = _rms(out_ref[...], g_ref[...])


def _scatter(y, rank, h, win, npass, cap, g, w_next=None):
    batch, n_tiles = rank.shape[:2]
    seq = h.shape[1]
    steps = n_tiles // TILES_PER_ROW_TILE
    project = w_next is not None
    const = lambda a: pl.BlockSpec(a.shape, lambda b, j, *_: (0, 0), pipeline_mode=pl.Buffered(1))
    in_specs = [pl.BlockSpec((None, N_EXPERTS, _slot_rows(cap), D_MODEL), lambda b, j, *_: (b, 0, 0, 0)),
                pl.BlockSpec((None, TILES_PER_ROW_TILE, N_EXPERTS, TOKEN_TILE), lambda b, j, *_: (b, j, 0, 0)),
                pl.BlockSpec((None, ROW_TILE, D_MODEL), lambda b, j, *_: (b, j, 0)),
                const(g)]
    out_shape = [jax.ShapeDtypeStruct(h.shape, F32)]
    out_specs = [pl.BlockSpec((None, ROW_TILE, D_MODEL), lambda b, j, *_: (b, j, 0))]
    args = [win, npass, y, rank, h, g]
    scratch = []
    if project:
        in_specs.append(const(w_next))
        args.append(w_next)
        proj_shape, proj_specs = _proj_outputs(batch * seq, lambda b, j, *_: (b * steps + j, 0))
        out_shape += proj_shape
        out_specs += proj_specs
        scratch = [pltpu.VMEM((D_MODEL // LANES, ROW_TILE, LANES), F32)]
    grid_spec = pltpu.PrefetchScalarGridSpec(
        num_scalar_prefetch=2, grid=(batch, steps), in_specs=in_specs, out_specs=out_specs,
        scratch_shapes=scratch)
    return pl.pallas_call(
        functools.partial(_scatter_kernel, cap, project),
        grid_spec=grid_spec,
        out_shape=out_shape,
        compiler_params=pltpu.CompilerParams(
            dimension_semantics=("arbitrary", "arbitrary"), vmem_limit_bytes=VMEM_LIMIT),
        name="scatter_proj" if project else "scatter_final",
    )(*args)


def _slot_windows(starts, n_tiles, cap):
    s = starts.astype(jnp.int32)
    first = (s // SLOT_ALIGN) * SLOT_ALIGN
    span = s[:, :, 1:n_tiles + 1] - first[:, :, :n_tiles]
    npass = jnp.maximum(jnp.max(-(-span // SLOT_WINDOW), axis=1), 1)
    npass = jnp.pad(npass, ((0, 0), (0, LANES - n_tiles)))
    return first.reshape(-1), npass.reshape(-1)


def _arrange_w_in(w):
    scale = LOG2E * HEAD_DIM ** -0.5
    qa, ka, va = w[:, :DIL_QKV], w[:, DIL_QKV:2 * DIL_QKV], w[:, 2 * DIL_QKV:3 * DIL_QKV]
    rest = w[:, 3 * DIL_QKV:]
    parts = []
    for g in range(len(DIL_PATTERNS)):
        sl = slice(g * DIL_OUT, (g + 1) * DIL_OUT)
        parts += [qa[:, sl] * scale, ka[:, sl], va[:, sl]]
    for pr in _window_pair_heads():
        parts += [rest[:, h * HEAD_DIM:(h + 1) * HEAD_DIM] * scale for h in pr]
    parts.append(rest[:, WIN_Q:WIN_Q + 2 * WIN_KV])
    return jnp.concatenate(parts, axis=1).astype(BF16), rest[:, WIN_Q + 2 * WIN_KV:].astype(BF16)


def _arrange_w_branch_b(w):
    return jnp.concatenate([w[h * HEAD_DIM:(h + 1) * HEAD_DIM] for pr in _window_pair_heads() for h in pr],
                           axis=0).astype(BF16)


def kernel(x, norm_mix, w_in, w_branch_a, w_branch_b, b_gate, sink_logit, w_out, norm_ffn, w_router,
           w_expert_gate, w_expert_up, w_expert_down, norm_final):
    batch, seq, d = x.shape
    depth = w_in.shape[0]
    cap = CAPACITY_FACTOR * seq // N_EXPERTS
    m = batch * seq
    h = x.reshape(m, d)
    wg_all, wu_all, wd_all = (w.astype(BF16) for w in (w_expert_gate, w_expert_up, w_expert_down))
    w_attn, w_gates = _arrange_w_in(w_in[0])
    proj = _norm_proj(h, norm_mix[0][None, :], w_attn)
    for l in range(depth):
        a0, a1, a2, qw, kv = proj
        oa = _dilated_attention(a0, a1, a2, batch, seq).reshape(m, DIL_OUT)
        ow = _window_attention(qw, kv, sink_logit[l][None, :], batch, seq).reshape(m, WIN_Q)
        h = _merge_out(oa, ow, h, norm_mix[l][None, :], w_gates, b_gate[l][None, :],
                       w_branch_a[l].astype(BF16), _arrange_w_branch_b(w_branch_b[l]), w_out[l].astype(BF16))
        wr = w_router[l].T
        wr_hi = wr.astype(BF16)
        wr_lo = (wr - wr_hi.astype(F32)).astype(BF16)
        hn, rank, aff, starts = _route(h, norm_ffn[l][None, :], wr_hi, wr_lo, batch, seq, cap)
        win, npass = _slot_windows(starts, seq // TOKEN_TILE, cap)
        xe, gate = _gather(hn, rank, aff, win, npass, cap)
        y = _ffn(xe, gate, wg_all, wu_all, wd_all, l, cap)
        h3 = h.reshape(batch, seq, d)
        if l + 1 == depth:
            return _scatter(y, rank, h3, win, npass, cap, norm_final[None, :])[0]
        w_attn, w_gates = _arrange_w_in(w_in[l + 1])
        h3, *proj = _scatter(y, rank, h3, win, npass, cap, norm_mix[l + 1][None, :], w_attn)
        h = h3.reshape(m, d)
```

```python
import functools

import jax
import jax.numpy as jnp
from jax import lax
from jax.experimental import pallas as pl
from jax.experimental.pallas import tpu as pltpu

D_MODEL = 1024
HEAD_DIM = 64
DIL_PATTERNS = ((128, 1), (512, 4), (2048, 16))
DIL_HEADS = 4
N_DIL_SUB = DIL_HEADS * len(DIL_PATTERNS)
DIL_QKV = N_DIL_SUB * HEAD_DIM
DIL_OUT = DIL_HEADS * HEAD_DIM
DIL_HALF = 64
WIN_HALF = 128
WIN_Q = D_MODEL
WIN_Q_HEADS = WIN_Q // HEAD_DIM
WIN_KV_HEADS = 4
WIN_KV = WIN_KV_HEADS * HEAD_DIM
N_EXPERTS = 16
CAPACITY_FACTOR = 2
RMS_EPS = 1e-6
NEG_INF = -1e30
LOG2E = 1.4426950408889634

LANES = 128
GROUP_COLS = 3 * DIL_OUT
DIL_PAIRS = DIL_OUT // LANES
WIN_UNITS = WIN_KV_HEADS // 2
WIN_UNIT_PAIRS = WIN_Q_HEADS // (2 * WIN_UNITS)
Q_TILE = 128
DIL_INFLIGHT = 16
ROW_TILE = 512
VMEM_LIMIT = 56 * 1024 * 1024

F32 = jnp.float32
BF16 = jnp.bfloat16


def _alibi_slopes(n):
    return [float(2.0 ** (-8.0 * i / n)) for i in range(1, n + 1)]


def _rms(x, g):
    return x * lax.rsqrt(jnp.mean(x * x, axis=-1, keepdims=True) + RMS_EPS) * g


def _dot(a, b):
    return jnp.dot(a, b, preferred_element_type=F32)


def _dot_nt(a, b):
    return lax.dot_general(a, b, (((1,), (1,)), ((), ())), preferred_element_type=F32)


def _left_lanes():
    return lax.broadcasted_iota(jnp.int32, (1, LANES), 1) < HEAD_DIM


_C_A = (0, GROUP_COLS, 2 * GROUP_COLS)
_C_QW = 3 * GROUP_COLS
_C_KV = _C_QW + WIN_Q
_COL_CHUNK = 512


def _proj_store(ref, xb, w_ref, c0, cw):
    for j in range(0, cw, _COL_CHUNK):
        jw = min(_COL_CHUNK, cw - j)
        ref[:, j:j + jw] = _dot(xb, w_ref[:, c0 + j:c0 + j + jw]).astype(BF16)


def _norm_proj_kernel(x_ref, g_ref, w_ref, a0_ref, a1_ref, a2_ref, qw_ref, kv_ref, xs_ref):
    _project(x_ref[...], g_ref, w_ref, a0_ref, a1_ref, a2_ref, qw_ref, kv_ref, xs_ref)


def _project(x, g_ref, w_ref, a0_ref, a1_ref, a2_ref, qw_ref, kv_ref, xs_ref):
    xn = _rms(x, g_ref[...])
    n_lane_tiles = D_MODEL // LANES
    for j in range(n_lane_tiles):
        xs_ref[j] = xn[:, j * LANES:(j + 1) * LANES]
    xb = xn.astype(BF16)
    _proj_store(a0_ref, xb, w_ref, _C_A[0], GROUP_COLS)
    for grp, ref in ((1, a1_ref), (2, a2_ref)):
        r = DIL_PATTERNS[grp][1]
        n = ROW_TILE // r
        xp = jnp.concatenate(
            [jnp.concatenate([xs_ref[j, pl.ds(c, n, stride=r), :] for j in range(n_lane_tiles)], axis=1)
             for c in range(r)], axis=0).astype(BF16)
        res = _dot(xp, w_ref[:, _C_A[grp]:_C_A[grp] + GROUP_COLS])
        for c in range(r):
            ref[:, c * GROUP_COLS:(c + 1) * GROUP_COLS] = res[c * n:(c + 1) * n].astype(BF16)
    _proj_store(qw_ref, xb, w_ref, _C_QW, WIN_Q)
    _proj_store(kv_ref, xb, w_ref, _C_KV, 2 * WIN_KV)


def _proj_outputs(m, row_index):
    out_shape, out_specs = [], []
    for _, r in DIL_PATTERNS:
        out_shape.append(jax.ShapeDtypeStruct((m // r, r * GROUP_COLS), BF16))
        out_specs.append(pl.BlockSpec((ROW_TILE // r, r * GROUP_COLS), row_index))
    for c in (WIN_Q, 2 * WIN_KV):
        out_shape.append(jax.ShapeDtypeStruct((m, c), BF16))
        out_specs.append(pl.BlockSpec((ROW_TILE, c), row_index))
    return out_shape, out_specs


def _norm_proj(x, g, w):
    m = x.shape[0]
    row = lambda c: pl.BlockSpec((ROW_TILE, c), lambda i: (i, 0))
    full = lambda a: pl.BlockSpec(a.shape, lambda i: (0, 0), pipeline_mode=pl.Buffered(1))
    in_specs = [row(D_MODEL), full(g), full(w)]
    out_shape, out_specs = _proj_outputs(m, lambda i: (i, 0))
    return pl.pallas_call(
        _norm_proj_kernel,
        grid=(m // ROW_TILE,),
        in_specs=in_specs,
        out_specs=out_specs,
        out_shape=out_shape,
        scratch_shapes=[pltpu.VMEM((D_MODEL // LANES, ROW_TILE, LANES), F32)],
        compiler_params=pltpu.CompilerParams(
            dimension_semantics=("arbitrary",), vmem_limit_bytes=VMEM_LIMIT),
        name="norm_proj",
    )(x, g, w)


def _band_bias(tq, win, off, half, slopes):
    col = lax.broadcasted_iota(jnp.int32, (tq, win), 1)
    row = lax.broadcasted_iota(jnp.int32, (tq, win), 0)
    dist = jnp.abs(col - row + off).astype(F32)
    inside = dist <= float(half)
    return jnp.concatenate([jnp.where(inside, dist * (-s * LOG2E), NEG_INF) for s in slopes], axis=0)


def _ones_blockdiag(win):
    left = lax.broadcasted_iota(jnp.int32, (2 * win, LANES), 1) < HEAD_DIM
    top = jnp.where(lax.broadcasted_iota(jnp.int32, (2 * win, LANES), 0) < win, 1.0, 0.0)
    return jnp.where(left, top, 1.0 - top).astype(BF16)


def _pair_scores(q_tiles, k_win, bias):
    left = _left_lanes()
    zero = jnp.zeros((), BF16)
    rows = []
    for q in q_tiles:
        rows += [jnp.where(left, q, zero), jnp.where(left, zero, q)]
    return _dot_nt(jnp.concatenate(rows, axis=0), k_win) + bias


def _pair_values(s_ref, v_win, ones_bd, sink_ref, p_ref, aux_ref):
    win = v_win.shape[0]
    n = s_ref.shape[0] // (2 * Q_TILE)
    left = _left_lanes()
    zero = jnp.zeros((), BF16)
    for i in range(n):
        ms = []
        for side in range(2):
            rows = pl.ds((2 * i + side) * Q_TILE, Q_TILE)
            s = s_ref[rows, :]
            m = jnp.broadcast_to(jnp.max(s, axis=-1, keepdims=True), (Q_TILE, LANES))
            if sink_ref is not None:
                m = jnp.maximum(m, sink_ref[rows, :])
            p = jnp.exp2(s - jnp.concatenate([m] * (win // LANES), axis=1)).astype(BF16)
            p_ref[i * Q_TILE:(i + 1) * Q_TILE, side * win:(side + 1) * win] = p
            ms.append(m)
        m_pair = jnp.where(left, ms[0], ms[1])
        if sink_ref is not None:
            sink_pair = jnp.where(left, sink_ref[pl.ds(2 * i * Q_TILE, Q_TILE), :],
                                  sink_ref[pl.ds((2 * i + 1) * Q_TILE, Q_TILE), :])
            aux_ref[i * Q_TILE:(i + 1) * Q_TILE, :] = jnp.exp2(sink_pair - m_pair)
        else:
            aux_ref[i * Q_TILE:(i + 1) * Q_TILE, :] = m_pair
    vbd = jnp.concatenate([jnp.where(left, v_win, zero), jnp.where(left, zero, v_win)], axis=0)
    on = _dot(p_ref[...], jnp.concatenate([vbd, ones_bd], axis=1))
    return on[:, :LANES], on[:, LANES:], aux_ref[...]


def _tile_variant(t, n_tiles):
    return jnp.where(t == 0, 0, jnp.where(t == n_tiles - 1, 2, 1))


def _pipelined_tiles(n_tiles, score_stage, value_stage):
    score_stage(0, 0)

    def body(i, carry):
        t = 2 * i
        score_stage(t + 1, 1)
        value_stage(t, 0)
        score_stage(jnp.minimum(t + 2, n_tiles - 1), 0)
        value_stage(t + 1, 1)
        return carry

    lax.fori_loop(0, n_tiles // 2, body, 0)


def _dil_geometry(seq_full, grp):
    r = DIL_PATTERNS[grp][1]
    seq = seq_full // r
    win = min(seq, Q_TILE + 2 * DIL_HALF)
    n_tiles = seq // Q_TILE
    pad = (win - Q_TILE) // 2
    offs = [0] if n_tiles == 1 else [0, -pad, -2 * pad]
    return r, seq, win, n_tiles, pad, offs


def _dilated_kernel(seq_full, a0_ref, a1_ref, a2_ref, out_ref, o_nat, l_nat, b0, b1, b2, ones_ref, ones2_ref,
                    s_wide, s_narrow, p_wide, p_narrow, aux_ref):
    a_refs = (a0_ref, a1_ref, a2_ref)
    bias_refs = (b0, b1, b2)
    s_refs = (s_wide, s_wide, s_narrow)
    p_refs = (p_wide, p_wide, p_narrow)
    slopes = _alibi_slopes(N_DIL_SUB)

    @pl.when(pl.program_id(0) == 0)
    def _():
        for grp in range(3):
            r, seq, win, n_tiles, pad, offs = _dil_geometry(seq_full, grp)
            for v, off in enumerate(offs):
                for pr in range(DIL_PAIRS):
                    hs = slopes[grp * DIL_HEADS + 2 * pr:grp * DIL_HEADS + 2 * pr + 2]
                    bias_refs[grp][v, pr] = _band_bias(Q_TILE, win, off, DIL_HALF, [s * r for s in hs])
        ones_ref[...] = _ones_blockdiag(_dil_geometry(seq_full, 0)[2])
        ones2_ref[...] = _ones_blockdiag(_dil_geometry(seq_full, 2)[2])

    def pair_tile(grp, c, t, pr, slot):
        r, seq, win, n_tiles, pad, offs = _dil_geometry(seq_full, grp)
        a_ref = a_refs[grp]
        ones_bd_ref = ones2_ref if grp == 2 else ones_ref
        if n_tiles == 1:
            q0, ks, var = 0, 0, 0
        else:
            q0 = pl.multiple_of(t * Q_TILE, Q_TILE)
            ks = pl.multiple_of(jnp.clip(q0 - pad, 0, seq - win), DIL_HALF)
            var = _tile_variant(t, n_tiles)
        lo = c * GROUP_COLS + pr * LANES
        q = a_ref[pl.ds(q0, Q_TILE), lo:lo + LANES]
        k = a_ref[pl.ds(ks, win), DIL_OUT + lo:DIL_OUT + lo + LANES]
        v = a_ref[pl.ds(ks, win), 2 * DIL_OUT + lo:2 * DIL_OUT + lo + LANES]
        s_ref = s_refs[grp].at[slot]
        s_ref[...] = _pair_scores([q], k, bias_refs[grp][var, pr])
        num, den, m = _pair_values(s_ref, v, ones_bd_ref[...], None, p_refs[grp].at[slot], aux_ref.at[slot])
        rows = pl.ds(q0, Q_TILE) if r == 1 else pl.ds(c + r * q0, Q_TILE, stride=r)
        o_nat[grp, pr, rows, :] = num / den
        l_nat[grp, pr, rows, :] = m + jnp.log2(den)

    for grp in range(3):
        r, seq, win, n_tiles, pad, offs = _dil_geometry(seq_full, grp)
        per_body = DIL_INFLIGHT // DIL_PAIRS
        if n_tiles == 1:
            for c in range(r):
                for pr in range(DIL_PAIRS):
                    pair_tile(grp, c, 0, pr, c * DIL_PAIRS + pr)
        elif r >= per_body:
            for c0 in range(0, r, per_body):
                def body(t, carry, grp=grp, c0=c0):
                    for j in range(per_body):
                        for pr in range(DIL_PAIRS):
                            pair_tile(grp, c0 + j, t, pr, j * DIL_PAIRS + pr)
                    return carry
                lax.fori_loop(0, n_tiles, body, 0)
        else:
            tiles_per_body = per_body // r
            def body(i, carry, grp=grp, r=r, tiles_per_body=tiles_per_body):
                for j in range(tiles_per_body):
                    for c in range(r):
                        for pr in range(DIL_PAIRS):
                            pair_tile(grp, c, i * tiles_per_body + j, pr, (j * r + c) * DIL_PAIRS + pr)
                return carry
            lax.fori_loop(0, n_tiles // tiles_per_body, body, 0)

    def combine(i, carry):
        rows = pl.ds(pl.multiple_of(i * ROW_TILE, ROW_TILE), ROW_TILE)
        for pr in range(DIL_PAIRS):
            ls = [l_nat[g, pr, rows, :] for g in range(3)]
            mx = jnp.maximum(jnp.maximum(ls[0], ls[1]), ls[2])
            es = [jnp.exp2(l - mx) for l in ls]
            num = es[0] * o_nat[0, pr, rows, :] + es[1] * o_nat[1, pr, rows, :] + es[2] * o_nat[2, pr, rows, :]
            out_ref[rows, pr * LANES:(pr + 1) * LANES] = (num / (es[0] + es[1] + es[2])).astype(BF16)
        return carry

    lax.fori_loop(0, seq_full // ROW_TILE, combine, 0)


def _dilated_attention(a0, a1, a2, batch, seq_full):
    views, in_specs, bias_shapes = [], [], []
    for grp, a in enumerate((a0, a1, a2)):
        r, seq, win, n_tiles, pad, offs = _dil_geometry(seq_full, grp)
        views.append(a.reshape(batch, seq, r * GROUP_COLS))
        in_specs.append(pl.BlockSpec((None, seq, r * GROUP_COLS), lambda b: (b, 0, 0)))
        bias_shapes.append(pltpu.VMEM((len(offs), DIL_PAIRS, 2 * Q_TILE, win), F32))
    win0 = _dil_geometry(seq_full, 0)[2]
    win2 = _dil_geometry(seq_full, 2)[2]
    n_straight = DIL_PATTERNS[2][1] * DIL_PAIRS
    return pl.pallas_call(
        functools.partial(_dilated_kernel, seq_full),
        grid=(batch,),
        in_specs=in_specs,
        out_specs=pl.BlockSpec((None, seq_full, DIL_OUT), lambda b: (b, 0, 0)),
        out_shape=jax.ShapeDtypeStruct((batch, seq_full, DIL_OUT), BF16),
        scratch_shapes=[pltpu.VMEM((3, DIL_PAIRS, seq_full, LANES), F32),
                        pltpu.VMEM((3, DIL_PAIRS, seq_full, LANES), F32)]
        + bias_shapes + [pltpu.VMEM((2 * win0, LANES), BF16), pltpu.VMEM((2 * win2, LANES), BF16)]
        + [pltpu.VMEM((n, 2 * Q_TILE, w), F32) for n, w in ((DIL_INFLIGHT, win0), (n_straight, win2))]
        + [pltpu.VMEM((n, Q_TILE, 2 * w), BF16) for n, w in ((DIL_INFLIGHT, win0), (n_straight, win2))]
        + [pltpu.VMEM((max(DIL_INFLIGHT, n_straight), Q_TILE, LANES), F32)],
        compiler_params=pltpu.CompilerParams(
            dimension_semantics=("arbitrary",), vmem_limit_bytes=VMEM_LIMIT),
        name="dilated",
    )(*views)


WIN_WINDOW = Q_TILE + 2 * WIN_HALF


def _window_pair_heads():
    group = WIN_Q_HEADS // WIN_KV_HEADS
    pairs = []
    for u in range(WIN_UNITS):
        for i in range(WIN_UNIT_PAIRS):
            pairs.append((2 * u * group + i, (2 * u + 1) * group + i))
    return pairs


def _window_kernel(seq, q_ref, kv_ref, sink_ref, o_ref, bias_ref, ones_ref, sink_rows_ref, s_ref, p_ref,
                   aux_ref):
    n_tiles = seq // Q_TILE
    slopes = _alibi_slopes(WIN_Q_HEADS)
    pair_heads = _window_pair_heads()
    unit_rows = 2 * WIN_UNIT_PAIRS * Q_TILE

    @pl.when(pl.program_id(0) == 0)
    def _():
        for u in range(WIN_UNITS):
            heads = [h for pr in pair_heads[u * WIN_UNIT_PAIRS:(u + 1) * WIN_UNIT_PAIRS] for h in pr]
            for v, off in enumerate((0, -WIN_HALF, -2 * WIN_HALF)):
                bias_ref[v, u] = _band_bias(Q_TILE, WIN_WINDOW, off, WIN_HALF, [slopes[h] for h in heads])
            sink_rows_ref[u] = jnp.concatenate(
                [jnp.broadcast_to(sink_ref[0:1, h:h + 1] * LOG2E, (Q_TILE, LANES)) for h in heads], axis=0)
        ones_ref[...] = _ones_blockdiag(WIN_WINDOW)

    def rows_of(t):
        q0 = pl.multiple_of(t * Q_TILE, Q_TILE)
        ks = pl.multiple_of(jnp.clip(q0 - WIN_HALF, 0, seq - WIN_WINDOW), Q_TILE)
        return q0, ks

    def unit_cols(u):
        return [(u * WIN_UNIT_PAIRS + i) * LANES for i in range(WIN_UNIT_PAIRS)]

    def score_stage(t, k_set):
        q0, ks = rows_of(t)
        var = _tile_variant(t, n_tiles)
        for u in range(WIN_UNITS):
            qs = [q_ref[pl.ds(q0, Q_TILE), c:c + LANES] for c in unit_cols(u)]
            k = kv_ref[pl.ds(ks, WIN_WINDOW), u * LANES:(u + 1) * LANES]
            s_ref[k_set * WIN_UNITS + u] = _pair_scores(qs, k, bias_ref[var, u])

    def value_stage(t, k_set):
        q0, ks = rows_of(t)
        for u in range(WIN_UNITS):
            slot = k_set * WIN_UNITS + u
            v = kv_ref[pl.ds(ks, WIN_WINDOW), WIN_KV + u * LANES:WIN_KV + (u + 1) * LANES]
            num, den, sink_term = _pair_values(s_ref.at[slot], v, ones_ref[...], sink_rows_ref.at[u],
                                               p_ref.at[slot], aux_ref.at[slot])
            o = (num / (den + sink_term)).astype(BF16)
            for i, c in enumerate(unit_cols(u)):
                o_ref[pl.ds(q0, Q_TILE), c:c + LANES] = o[i * Q_TILE:(i + 1) * Q_TILE]

    _pipelined_tiles(n_tiles, score_stage, value_stage)


def _window_attention(q, kv, sink, batch, seq):
    unit_rows = 2 * WIN_UNIT_PAIRS * Q_TILE
    return pl.pallas_call(
        functools.partial(_window_kernel, seq),
        grid=(batch,),
        in_specs=[pl.BlockSpec((None, seq, WIN_Q), lambda b: (b, 0, 0)),
                  pl.BlockSpec((None, seq, 2 * WIN_KV), lambda b: (b, 0, 0)),
                  pl.BlockSpec((1, WIN_Q_HEADS), lambda b: (0, 0))],
        out_specs=pl.BlockSpec((None, seq, WIN_Q), lambda b: (b, 0, 0)),
        out_shape=jax.ShapeDtypeStruct((batch, seq, WIN_Q), BF16),
        scratch_shapes=[pltpu.VMEM((3, WIN_UNITS, unit_rows, WIN_WINDOW), F32),
                        pltpu.VMEM((2 * WIN_WINDOW, LANES), BF16),
                        pltpu.VMEM((WIN_UNITS, unit_rows, LANES), F32),
                        pltpu.VMEM((2 * WIN_UNITS, unit_rows, WIN_WINDOW), F32),
                        pltpu.VMEM((2 * WIN_UNITS, unit_rows // 2, 2 * WIN_WINDOW), BF16),
                        pltpu.VMEM((2 * WIN_UNITS, unit_rows // 2, LANES), F32)],
        compiler_params=pltpu.CompilerParams(
            dimension_semantics=("arbitrary",), vmem_limit_bytes=VMEM_LIMIT),
        name="window",
    )(q.reshape(batch, seq, WIN_Q), kv.reshape(batch, seq, 2 * WIN_KV), sink)


def _merge_kernel(oa_ref, ow_ref, h_ref, g_ref, wgate_ref, bgate_ref, wa_ref, wb_ref, wo_ref, out_ref):
    h = h_ref[...]
    xb = _rms(h, g_ref[...]).astype(BF16)
    merged = None
    for br, (o_ref, w_ref) in enumerate(((oa_ref, wa_ref), (ow_ref, wb_ref))):
        cols = slice(br * D_MODEL, (br + 1) * D_MODEL)
        gate = jax.nn.sigmoid(_dot(xb, wgate_ref[:, cols]) + bgate_ref[:, cols])
        term = gate * _dot(o_ref[...], w_ref[...])
        merged = term if merged is None else merged + term
    out_ref[...] = h + _dot(merged.astype(BF16), wo_ref[...])


def _merge_out(oa, ow, h, g, wgate, bgate, wa, wb, wo):
    m = h.shape[0]
    row = lambda c: pl.BlockSpec((ROW_TILE, c), lambda i: (i, 0))
    full = lambda a: pl.BlockSpec(a.shape, lambda i: (0, 0), pipeline_mode=pl.Buffered(1))
    return pl.pallas_call(
        _merge_kernel,
        grid=(m // ROW_TILE,),
        in_specs=[row(DIL_OUT), row(WIN_Q), row(D_MODEL), full(g), full(wgate), full(bgate),
                  full(wa), full(wb), full(wo)],
        out_specs=row(D_MODEL),
        out_shape=jax.ShapeDtypeStruct((m, D_MODEL), F32),
        compiler_params=pltpu.CompilerParams(
            dimension_semantics=("arbitrary",), vmem_limit_bytes=VMEM_LIMIT),
        name="merge_out",
    )(oa, ow, h, g, wgate, bgate, wa, wb, wo)


TOKEN_TILE = 256
CUM_CHUNK = TOKEN_TILE
SLOT_WINDOW = 64
SLOT_ALIGN = 16


def _prefix_exclusive(mask_f, tri):
    e, s = mask_f.shape
    carry = jnp.zeros((e, 1), F32)
    parts, carries = [], []
    for j in range(0, s, CUM_CHUNK):
        blk = mask_f[:, j:j + CUM_CHUNK]
        carries.append(carry)
        parts.append(_dot(blk.astype(BF16), tri) + carry)
        carry = carry + jnp.sum(blk, axis=-1, keepdims=True)
    return jnp.concatenate(parts, axis=-1), carries + [carry]


def _route_kernel(cap, h_ref, g_ref, whi_ref, wlo_ref, hn_ref, rank_ref, aff_ref, starts_ref):
    seq = h_ref.shape[0]
    hn = _rms(h_ref[...], g_ref[...])
    hn_hi = hn.astype(BF16)
    hn_lo = (hn - hn_hi.astype(F32)).astype(BF16)
    hn_ref[...] = hn_hi
    logits = (_dot_nt(whi_ref[...], hn_hi) + _dot_nt(whi_ref[...], hn_lo)
              + _dot_nt(wlo_ref[...], hn_hi))
    mx = jnp.max(logits, axis=0, keepdims=True)
    ex = jnp.exp(logits - mx)
    aff = ex / jnp.sum(ex, axis=0, keepdims=True)
    bits = pltpu.bitcast(aff, jnp.int32)

    def enough(t):
        return jnp.sum(jnp.where(bits >= t, 1.0, 0.0), axis=-1, keepdims=True) >= float(cap)

    def search4(_, c):
        lo, hi = c
        q = (hi - lo) >> 2
        m1, m2, m3 = lo + q, lo + 2 * q, lo + 3 * q
        ok1, ok2, ok3 = enough(m1), enough(m2), enough(m3)
        return (jnp.where(ok3, m3, jnp.where(ok2, m2, jnp.where(ok1, m1, lo))),
                jnp.where(ok3, hi, jnp.where(ok2, m3, jnp.where(ok1, m2, m1))))

    def search2(_, c):
        lo, hi = c
        mid = lo + ((hi - lo) >> 1)
        ok = enough(mid)
        return jnp.where(ok, mid, lo), jnp.where(ok, hi, mid)

    lo0 = jnp.zeros((N_EXPERTS, 1), jnp.int32)
    hi0 = jnp.full((N_EXPERTS, 1), 0x3F800001, jnp.int32)
    thr, _ = lax.fori_loop(0, 3, search2, lax.fori_loop(0, 15, search4, (lo0, hi0)))

    r_i = lax.broadcasted_iota(jnp.int32, (CUM_CHUNK, CUM_CHUNK), 0)
    c_i = lax.broadcasted_iota(jnp.int32, (CUM_CHUNK, CUM_CHUNK), 1)
    tri = jnp.where(r_i < c_i, 1.0, 0.0).astype(BF16)
    gt = jnp.where(bits > thr, 1.0, 0.0)
    eq = jnp.where(bits == thr, 1.0, 0.0)
    need = float(cap) - jnp.sum(gt, axis=-1, keepdims=True)
    tie_rank, _ = _prefix_exclusive(eq, tri)
    sel = gt + eq * jnp.where(tie_rank < need, 1.0, 0.0)
    slot, starts = _prefix_exclusive(sel, tri)
    rank = jnp.where(sel > 0.0, slot, -1.0)
    for t in range(seq // TOKEN_TILE):
        rank_ref[t] = rank[:, t * TOKEN_TILE:(t + 1) * TOKEN_TILE]
        aff_ref[t] = aff[:, t * TOKEN_TILE:(t + 1) * TOKEN_TILE]
    lane = lax.broadcasted_iota(jnp.int32, (N_EXPERTS, LANES), 1)
    acc = jnp.zeros((N_EXPERTS, LANES), F32)
    for t, c in enumerate(starts):
        acc = acc + jnp.where(lane == t, c, 0.0)
    starts_ref[...] = acc


def _route(h, g, w_hi, w_lo, batch, seq, cap):
    n_tiles = seq // TOKEN_TILE
    return pl.pallas_call(
        functools.partial(_route_kernel, cap),
        grid=(batch,),
        in_specs=[pl.BlockSpec((None, seq, D_MODEL), lambda b: (b, 0, 0)),
                  pl.BlockSpec((1, D_MODEL), lambda b: (0, 0)),
                  pl.BlockSpec((N_EXPERTS, D_MODEL), lambda b: (0, 0)),
                  pl.BlockSpec((N_EXPERTS, D_MODEL), lambda b: (0, 0))],
        out_specs=[pl.BlockSpec((None, seq, D_MODEL), lambda b: (b, 0, 0)),
                   pl.BlockSpec((None, n_tiles, N_EXPERTS, TOKEN_TILE), lambda b: (b, 0, 0, 0)),
                   pl.BlockSpec((None, n_tiles, N_EXPERTS, TOKEN_TILE), lambda b: (b, 0, 0, 0)),
                   pl.BlockSpec((None, N_EXPERTS, LANES), lambda b: (b, 0, 0))],
        out_shape=[jax.ShapeDtypeStruct((batch, seq, D_MODEL), BF16),
                   jax.ShapeDtypeStruct((batch, n_tiles, N_EXPERTS, TOKEN_TILE), F32),
                   jax.ShapeDtypeStruct((batch, n_tiles, N_EXPERTS, TOKEN_TILE), F32),
                   jax.ShapeDtypeStruct((batch, N_EXPERTS, LANES), F32)],
        compiler_params=pltpu.CompilerParams(
            dimension_semantics=("arbitrary",), vmem_limit_bytes=VMEM_LIMIT),
        name="route",
    )(h.reshape(batch, seq, D_MODEL), g, w_hi, w_lo)


def _slot_rows(cap):
    return cap + SLOT_WINDOW


def _window_starts(win_ref, b, t, p, cap):
    base = b * (N_EXPERTS * LANES) + t
    return [pl.multiple_of(jnp.minimum(win_ref[base + e * LANES] + p * SLOT_WINDOW, cap), SLOT_ALIGN)
            for e in range(N_EXPERTS)]


def _window_hits(rank_tile, starts):
    rows = lax.broadcasted_iota(jnp.int32, (SLOT_WINDOW, TOKEN_TILE), 0).astype(F32)
    return [(rank_tile[e:e + 1, :] - starts[e].astype(F32)) == rows for e in range(N_EXPERTS)]


def _one_hot(hits):
    return jnp.concatenate([jnp.where(h, 1.0, 0.0).astype(BF16) for h in hits], axis=0)


def _gather_kernel(cap, win_ref, npass_ref, hn_ref, rank_ref, aff_ref, xe_ref, gate_ref):
    b = pl.program_id(0)
    xe_ref[...] = jnp.zeros_like(xe_ref)
    gate_ref[...] = jnp.zeros_like(gate_ref)
    for t in range(hn_ref.shape[0] // TOKEN_TILE):

        def one_pass(p, carry, t=t):
            starts = _window_starts(win_ref, b, t, p, cap)
            hits = _window_hits(rank_ref[t], starts)
            rows = _dot(_one_hot(hits), hn_ref[t * TOKEN_TILE:(t + 1) * TOKEN_TILE, :]).astype(BF16)
            aff_tile = aff_ref[t]
            for e in range(N_EXPERTS):
                win = pl.ds(starts[e], SLOT_WINDOW)
                xe_ref[e, win, :] += rows[e * SLOT_WINDOW:(e + 1) * SLOT_WINDOW]
                g = jnp.sum(jnp.where(hits[e], aff_tile[e:e + 1, :], 0.0), axis=-1, keepdims=True)
                gate_ref[e, win, :] += jnp.broadcast_to(g, (SLOT_WINDOW, LANES))
            return carry

        lax.fori_loop(0, npass_ref[b * LANES + t], one_pass, 0)


def _gather(hn, rank, aff, win, npass, cap):
    batch, seq, _ = hn.shape
    n_tiles = seq // TOKEN_TILE
    grid_spec = pltpu.PrefetchScalarGridSpec(
        num_scalar_prefetch=2,
        grid=(batch,),
        in_specs=[pl.BlockSpec((None, seq, D_MODEL), lambda b, *_: (b, 0, 0)),
                  pl.BlockSpec((None, n_tiles, N_EXPERTS, TOKEN_TILE), lambda b, *_: (b, 0, 0, 0)),
                  pl.BlockSpec((None, n_tiles, N_EXPERTS, TOKEN_TILE), lambda b, *_: (b, 0, 0, 0))],
        out_specs=[pl.BlockSpec((None, N_EXPERTS, _slot_rows(cap), D_MODEL), lambda b, *_: (b, 0, 0, 0)),
                   pl.BlockSpec((None, N_EXPERTS, _slot_rows(cap), LANES), lambda b, *_: (b, 0, 0, 0))])
    return pl.pallas_call(
        functools.partial(_gather_kernel, cap),
        grid_spec=grid_spec,
        out_shape=[jax.ShapeDtypeStruct((batch, N_EXPERTS, _slot_rows(cap), D_MODEL), BF16),
                   jax.ShapeDtypeStruct((batch, N_EXPERTS, _slot_rows(cap), LANES), F32)],
        compiler_params=pltpu.CompilerParams(
            dimension_semantics=("arbitrary",), vmem_limit_bytes=VMEM_LIMIT),
        name="gather",
    )(win, npass, hn, rank, aff)


FFN_SEQS = 4


def _ffn_kernel(cap, xe_ref, gate_ref, wg_ref, wu_ref, wd_ref, y_ref, wg_b, wu_b, wd_b):
    d = xe_ref.shape[-1]

    @pl.when(pl.program_id(1) == 0)
    def _():
        for src, dst in ((wg_ref, wg_b), (wu_ref, wu_b), (wd_ref, wd_b)):
            dst[...] = src[...].astype(BF16)

    xe = xe_ref[:, :cap, :].reshape(FFN_SEQS * cap, d)
    a = _dot(xe, wg_b[...])
    u = _dot(xe, wu_b[...])
    y = _dot((jax.nn.silu(a) * u).astype(BF16), wd_b[...])
    gate = gate_ref[:, :cap, :].reshape(FFN_SEQS * cap, LANES)
    y = y * jnp.concatenate([gate] * (d // LANES), axis=1)
    y_ref[:, :cap, :] = y.astype(BF16).reshape(FFN_SEQS, cap, d)
    y_ref[:, cap:, :] = jnp.zeros((FFN_SEQS, SLOT_WINDOW, d), BF16)


def _ffn(xe, gate, wg, wu, wd, layer, cap):
    batch = xe.shape[0]
    d_exp = wg.shape[-1]
    slots = pl.BlockSpec((FFN_SEQS, None, _slot_rows(cap), D_MODEL), lambda e, i: (i, e, 0, 0))
    return pl.pallas_call(
        functools.partial(_ffn_kernel, cap),
        grid=(N_EXPERTS, batch // FFN_SEQS),
        in_specs=[slots,
                  pl.BlockSpec((FFN_SEQS, None, _slot_rows(cap), LANES), lambda e, i: (i, e, 0, 0)),
                  pl.BlockSpec((None, None, D_MODEL, d_exp), lambda e, i: (layer, e, 0, 0)),
                  pl.BlockSpec((None, None, D_MODEL, d_exp), lambda e, i: (layer, e, 0, 0)),
                  pl.BlockSpec((None, None, d_exp, D_MODEL), lambda e, i: (layer, e, 0, 0))],
        out_specs=slots,
        out_shape=jax.ShapeDtypeStruct(xe.shape, BF16),
        scratch_shapes=[pltpu.VMEM((D_MODEL, d_exp), BF16), pltpu.VMEM((D_MODEL, d_exp), BF16),
                        pltpu.VMEM((d_exp, D_MODEL), BF16)],
        compiler_params=pltpu.CompilerParams(
            dimension_semantics=("arbitrary", "arbitrary"), vmem_limit_bytes=VMEM_LIMIT),
        name="ffn",
    )(xe, gate, wg, wu, wd)


TILES_PER_ROW_TILE = ROW_TILE // TOKEN_TILE


def _scatter_kernel(cap, project, win_ref, npass_ref, y_ref, rank_ref, h_ref, g_ref, *rest):
    if project:
        w_ref, out_ref, a0_ref, a1_ref, a2_ref, qw_ref, kv_ref, xs_ref = rest
    else:
        out_ref, = rest
    b = pl.program_id(0)
    j = pl.program_id(1)
    out_ref[...] = h_ref[...]
    for tt in range(TILES_PER_ROW_TILE):
        t = j * TILES_PER_ROW_TILE + tt
        rows = slice(tt * TOKEN_TILE, (tt + 1) * TOKEN_TILE)

        def one_pass(p, carry, t=t, tt=tt, rows=rows):
            starts = _window_starts(win_ref, b, t, p, cap)
            put = _one_hot(_window_hits(rank_ref[tt], starts))
            yw = jnp.concatenate([y_ref[e, pl.ds(starts[e], SLOT_WINDOW), :] for e in range(N_EXPERTS)],
                                 axis=0)
            out_ref[rows, :] += lax.dot_general(put, yw, (((0,), (0,)), ((), ())), preferred_element_type=F32)
            return carry

        lax.fori_loop(0, npass_ref[b * LANES + t], one_pass, 0)
    if project:
        _project(out_ref[...], g_ref, w_ref, a0_ref, a1_ref, a2_ref, qw_ref, kv_ref, xs_ref)
    else:
        out_ref[...] = _rms(out_ref[...], g_ref[...])


def _scatter(y, rank, h, win, npass, cap, g, w_next=None):
    batch, n_tiles = rank.shape[:2]
    seq = h.shape[1]
    steps = n_tiles // TILES_PER_ROW_TILE
    project = w_next is not None
    const = lambda a: pl.BlockSpec(a.shape, lambda b, j, *_: (0, 0), pipeline_mode=pl.Buffered(1))
    in_specs = [pl.BlockSpec((None, N_EXPERTS, _slot_rows(cap), D_MODEL), lambda b, j, *_: (b, 0, 0, 0)),
                pl.BlockSpec((None, TILES_PER_ROW_TILE, N_EXPERTS, TOKEN_TILE), lambda b, j, *_: (b, j, 0, 0)),
                pl.BlockSpec((None, ROW_TILE, D_MODEL), lambda b, j, *_: (b, j, 0)),
                const(g)]
    out_shape = [jax.ShapeDtypeStruct(h.shape, F32)]
    out_specs = [pl.BlockSpec((None, ROW_TILE, D_MODEL), lambda b, j, *_: (b, j, 0))]
    args = [win, npass, y, rank, h, g]
    scratch = []
    if project:
        in_specs.append(const(w_next))
        args.append(w_next)
        proj_shape, proj_specs = _proj_outputs(batch * seq, lambda b, j, *_: (b * steps + j, 0))
        out_shape += proj_shape
        out_specs += proj_specs
        scratch = [pltpu.VMEM((D_MODEL // LANES, ROW_TILE, LANES), F32)]
    grid_spec = pltpu.PrefetchScalarGridSpec(
        num_scalar_prefetch=2, grid=(batch, steps), in_specs=in_specs, out_specs=out_specs,
        scratch_shapes=scratch)
    return pl.pallas_call(
        functools.partial(_scatter_kernel, cap, project),
        grid_spec=grid_spec,
        out_shape=out_shape,
        compiler_params=pltpu.CompilerParams(
            dimension_semantics=("arbitrary", "arbitrary"), vmem_limit_bytes=VMEM_LIMIT),
        name="scatter_proj" if project else "scatter_final",
    )(*args)


def _slot_windows(starts, n_tiles, cap):
    s = starts.astype(jnp.int32)
    first = (s // SLOT_ALIGN) * SLOT_ALIGN
    span = s[:, :, 1:n_tiles + 1] - first[:, :, :n_tiles]
    npass = jnp.maximum(jnp.max(-(-span // SLOT_WINDOW), axis=1), 1)
    npass = jnp.pad(npass, ((0, 0), (0, LANES - n_tiles)))
    return first.reshape(-1), npass.reshape(-1)


def _arrange_w_in(w):
    scale = LOG2E * HEAD_DIM ** -0.5
    qa, ka, va = w[:, :DIL_QKV], w[:, DIL_QKV:2 * DIL_QKV], w[:, 2 * DIL_QKV:3 * DIL_QKV]
    rest = w[:, 3 * DIL_QKV:]
    parts = []
    for g in range(len(DIL_PATTERNS)):
        sl = slice(g * DIL_OUT, (g + 1) * DIL_OUT)
        parts += [qa[:, sl] * scale, ka[:, sl], va[:, sl]]
    for pr in _window_pair_heads():
        parts += [rest[:, h * HEAD_DIM:(h + 1) * HEAD_DIM] * scale for h in pr]
    parts.append(rest[:, WIN_Q:WIN_Q + 2 * WIN_KV])
    return jnp.concatenate(parts, axis=1).astype(BF16), rest[:, WIN_Q + 2 * WIN_KV:].astype(BF16)


def _arrange_w_branch_b(w):
    return jnp.concatenate([w[h * HEAD_DIM:(h + 1) * HEAD_DIM] for pr in _window_pair_heads() for h in pr],
                           axis=0).astype(BF16)


def kernel(x, norm_mix, w_in, w_branch_a, w_branch_b, b_gate, sink_logit, w_out, norm_ffn, w_router,
           w_expert_gate, w_expert_up, w_expert_down, norm_final):
    batch, seq, d = x.shape
    depth = w_in.shape[0]
    cap = CAPACITY_FACTOR * seq // N_EXPERTS
    m = batch * seq
    h = x.reshape(m, d)
    w_attn, w_gates = _arrange_w_in(w_in[0])
    proj = _norm_proj(h, norm_mix[0][None, :], w_attn)
    for l in range(depth):
        a0, a1, a2, qw, kv = proj
        oa = _dilated_attention(a0, a1, a2, batch, seq).reshape(m, DIL_OUT)
        ow = _window_attention(qw, kv, sink_logit[l][None, :], batch, seq).reshape(m, WIN_Q)
        h = _merge_out(oa, ow, h, norm_mix[l][None, :], w_gates, b_gate[l][None, :],
                       w_branch_a[l].astype(BF16), _arrange_w_branch_b(w_branch_b[l]), w_out[l].astype(BF16))
        wr = w_router[l].T
        wr_hi = wr.astype(BF16)
        wr_lo = (wr - wr_hi.astype(F32)).astype(BF16)
        hn, rank, aff, starts = _route(h, norm_ffn[l][None, :], wr_hi, wr_lo, batch, seq, cap)
        win, npass = _slot_windows(starts, seq // TOKEN_TILE, cap)
        xe, gate = _gather(hn, rank, aff, win, npass, cap)
        y = _ffn(xe, gate, w_expert_gate, w_expert_up, w_expert_down, l, cap)
        h3 = h.reshape(batch, seq, d)
        if l + 1 == depth:
            return _scatter(y, rank, h3, win, npass, cap, norm_final[None, :])[0]
        w_attn, w_gates = _arrange_w_in(w_in[l + 1])
        h3, *proj = _scatter(y, rank, h3, win, npass, cap, norm_mix[l + 1][None, :], w_attn)
        h = h3.reshape(m, d)
```

```python
import functools

import jax
import jax.numpy as jnp
from jax import lax
from jax.experimental import pallas as pl
from jax.experimental.pallas import tpu as pltpu

D_MODEL = 1024
HEAD_DIM = 64
DIL_PATTERNS = ((128, 1), (512, 4), (2048, 16))
DIL_HEADS = 4
N_DIL_SUB = DIL_HEADS * len(DIL_PATTERNS)
DIL_QKV = N_DIL_SUB * HEAD_DIM
DIL_OUT = DIL_HEADS * HEAD_DIM
DIL_HALF = 64
WIN_HALF = 128
WIN_Q = D_MODEL
WIN_Q_HEADS = WIN_Q // HEAD_DIM
WIN_KV_HEADS = 4
WIN_KV = WIN_KV_HEADS * HEAD_DIM
N_EXPERTS = 16
CAPACITY_FACTOR = 2
RMS_EPS = 1e-6
NEG_INF = -1e30
LOG2E = 1.4426950408889634

LANES = 128
GROUP_COLS = 3 * DIL_OUT
DIL_PAIRS = DIL_OUT // LANES
WIN_UNITS = WIN_KV_HEADS // 2
WIN_UNIT_PAIRS = WIN_Q_HEADS // (2 * WIN_UNITS)
Q_TILE = 128
PV_PAIRS = 4
WIN_TILES_PER_BODY = 4
DIL_INFLIGHT = 16
ROW_TILE = 512
VMEM_LIMIT = 56 * 1024 * 1024

F32 = jnp.float32
BF16 = jnp.bfloat16


def _alibi_slopes(n):
    return [float(2.0 ** (-8.0 * i / n)) for i in range(1, n + 1)]


def _rms(x, g):
    return x * lax.rsqrt(jnp.mean(x * x, axis=-1, keepdims=True) + RMS_EPS) * g


def _dot(a, b):
    return jnp.dot(a, b, preferred_element_type=F32)


def _dot_nt(a, b):
    return lax.dot_general(a, b, (((1,), (1,)), ((), ())), preferred_element_type=F32)


def _left_lanes():
    return lax.broadcasted_iota(jnp.int32, (1, LANES), 1) < HEAD_DIM


_C_A = (0, GROUP_COLS, 2 * GROUP_COLS)
_C_QW = 3 * GROUP_COLS
_C_KV = _C_QW + WIN_Q
_COL_CHUNK = 512


def _proj_store(ref, xb, w_ref, c0, cw):
    for j in range(0, cw, _COL_CHUNK):
        jw = min(_COL_CHUNK, cw - j)
        ref[:, j:j + jw] = _dot(xb, w_ref[:, c0 + j:c0 + j + jw]).astype(BF16)


def _norm_proj_kernel(x_ref, g_ref, w_ref, a0_ref, a1_ref, a2_ref, qw_ref, kv_ref, xs_ref):
    _project(x_ref[...], g_ref, w_ref, a0_ref, a1_ref, a2_ref, qw_ref, kv_ref, xs_ref)


def _project(x, g_ref, w_ref, a0_ref, a1_ref, a2_ref, qw_ref, kv_ref, xs_ref):
    xn = _rms(x, g_ref[...])
    n_lane_tiles = D_MODEL // LANES
    for j in range(n_lane_tiles):
        xs_ref[j] = xn[:, j * LANES:(j + 1) * LANES]
    xb = xn.astype(BF16)
    _proj_store(a0_ref, xb, w_ref, _C_A[0], GROUP_COLS)
    for grp, ref in ((1, a1_ref), (2, a2_ref)):
        r = DIL_PATTERNS[grp][1]
        n = ROW_TILE // r
        xp = jnp.concatenate(
            [jnp.concatenate([xs_ref[j, pl.ds(c, n, stride=r), :] for j in range(n_lane_tiles)], axis=1)
             for c in range(r)], axis=0).astype(BF16)
        res = _dot(xp, w_ref[:, _C_A[grp]:_C_A[grp] + GROUP_COLS])
        for c in range(r):
            ref[:, c * GROUP_COLS:(c + 1) * GROUP_COLS] = res[c * n:(c + 1) * n].astype(BF16)
    _proj_store(qw_ref, xb, w_ref, _C_QW, WIN_Q)
    _proj_store(kv_ref, xb, w_ref, _C_KV, 2 * WIN_KV)


def _proj_outputs(m, row_index):
    out_shape, out_specs = [], []
    for _, r in DIL_PATTERNS:
        out_shape.append(jax.ShapeDtypeStruct((m // r, r * GROUP_COLS), BF16))
        out_specs.append(pl.BlockSpec((ROW_TILE // r, r * GROUP_COLS), row_index))
    for c in (WIN_Q, 2 * WIN_KV):
        out_shape.append(jax.ShapeDtypeStruct((m, c), BF16))
        out_specs.append(pl.BlockSpec((ROW_TILE, c), row_index))
    return out_shape, out_specs


def _norm_proj(x, g, w):
    m = x.shape[0]
    row = lambda c: pl.BlockSpec((ROW_TILE, c), lambda i: (i, 0))
    full = lambda a: pl.BlockSpec(a.shape, lambda i: (0, 0), pipeline_mode=pl.Buffered(1))
    in_specs = [row(D_MODEL), full(g), full(w)]
    out_shape, out_specs = _proj_outputs(m, lambda i: (i, 0))
    return pl.pallas_call(
        _norm_proj_kernel,
        grid=(m // ROW_TILE,),
        in_specs=in_specs,
        out_specs=out_specs,
        out_shape=out_shape,
        scratch_shapes=[pltpu.VMEM((D_MODEL // LANES, ROW_TILE, LANES), F32)],
        compiler_params=pltpu.CompilerParams(
            dimension_semantics=("arbitrary",), vmem_limit_bytes=VMEM_LIMIT),
        name="norm_proj",
    )(x, g, w)


def _band_bias(tq, win, off, half, slopes):
    col = lax.broadcasted_iota(jnp.int32, (tq, win), 1)
    row = lax.broadcasted_iota(jnp.int32, (tq, win), 0)
    dist = jnp.abs(col - row + off).astype(F32)
    inside = dist <= float(half)
    return jnp.concatenate([jnp.where(inside, dist * (-s * LOG2E), NEG_INF) for s in slopes], axis=0)


def _ones_blockdiag(win):
    left = lax.broadcasted_iota(jnp.int32, (2 * win, LANES), 1) < HEAD_DIM
    top = jnp.where(lax.broadcasted_iota(jnp.int32, (2 * win, LANES), 0) < win, 1.0, 0.0)
    return jnp.where(left, top, 1.0 - top).astype(BF16)


def _pair_scores(q_tiles, k_win, bias):
    left = _left_lanes()
    zero = jnp.zeros((), BF16)
    rows = []
    for q in q_tiles:
        rows += [jnp.where(left, q, zero), jnp.where(left, zero, q)]
    return _dot_nt(jnp.concatenate(rows, axis=0), k_win) + bias


def _pair_values(s_ref, v_win, ones_bd, sink_ref, p_ref, aux_ref):
    win = v_win.shape[0]
    n = s_ref.shape[0] // (2 * Q_TILE)
    left = _left_lanes()
    zero = jnp.zeros((), BF16)
    vbd = jnp.concatenate([jnp.where(left, v_win, zero), jnp.where(left, zero, v_win)], axis=0)
    rhs = jnp.concatenate([vbd, ones_bd], axis=1)
    outs = []
    for i in range(n):
        ms = []
        for side in range(2):
            rows = pl.ds((2 * i + side) * Q_TILE, Q_TILE)
            s = s_ref[rows, :]
            m = jnp.broadcast_to(jnp.max(s, axis=-1, keepdims=True), (Q_TILE, LANES))
            if sink_ref is not None:
                m = jnp.maximum(m, sink_ref[rows, :])
            p = jnp.exp2(s - jnp.concatenate([m] * (win // LANES), axis=1)).astype(BF16)
            p_ref[i * Q_TILE:(i + 1) * Q_TILE, side * win:(side + 1) * win] = p
            ms.append(m)
        m_pair = jnp.where(left, ms[0], ms[1])
        if sink_ref is not None:
            sink_pair = jnp.where(left, sink_ref[pl.ds(2 * i * Q_TILE, Q_TILE), :],
                                  sink_ref[pl.ds((2 * i + 1) * Q_TILE, Q_TILE), :])
            aux_ref[i * Q_TILE:(i + 1) * Q_TILE, :] = jnp.exp2(sink_pair - m_pair)
        else:
            aux_ref[i * Q_TILE:(i + 1) * Q_TILE, :] = m_pair
        if (i + 1) % PV_PAIRS == 0 or i + 1 == n:
            lo = (i // PV_PAIRS) * PV_PAIRS * Q_TILE
            outs.append(_dot(p_ref[lo:(i + 1) * Q_TILE, :], rhs))
    on = outs[0] if len(outs) == 1 else jnp.concatenate(outs, axis=0)
    return on[:, :LANES], on[:, LANES:], aux_ref[...]


def _tile_variant(t, n_tiles):
    return jnp.where(t == 0, 0, jnp.where(t == n_tiles - 1, 2, 1))


def _pipelined_tiles(n_tiles, per_body, score_stage, value_stage):
    score_stage(0, 0)

    def body(i, carry):
        t = per_body * i
        for j in range(per_body):
            score_stage(jnp.minimum(t + j + 1, n_tiles - 1), (j + 1) % per_body)
            value_stage(t + j, j)
        return carry

    lax.fori_loop(0, n_tiles // per_body, body, 0)


def _dil_geometry(seq_full, grp):
    r = DIL_PATTERNS[grp][1]
    seq = seq_full // r
    win = min(seq, Q_TILE + 2 * DIL_HALF)
    n_tiles = seq // Q_TILE
    pad = (win - Q_TILE) // 2
    offs = [0] if n_tiles == 1 else [0, -pad, -2 * pad]
    return r, seq, win, n_tiles, pad, offs


def _dilated_kernel(seq_full, a0_ref, a1_ref, a2_ref, out_ref, o_nat, l_nat, b0, b1, b2, ones_ref, ones2_ref,
                    s_wide, s_narrow, p_wide, p_narrow, aux_ref):
    a_refs = (a0_ref, a1_ref, a2_ref)
    bias_refs = (b0, b1, b2)
    s_refs = (s_wide, s_wide, s_narrow)
    p_refs = (p_wide, p_wide, p_narrow)
    slopes = _alibi_slopes(N_DIL_SUB)

    @pl.when(pl.program_id(0) == 0)
    def _():
        for grp in range(3):
            r, seq, win, n_tiles, pad, offs = _dil_geometry(seq_full, grp)
            for v, off in enumerate(offs):
                for pr in range(DIL_PAIRS):
                    hs = slopes[grp * DIL_HEADS + 2 * pr:grp * DIL_HEADS + 2 * pr + 2]
                    bias_refs[grp][v, pr] = _band_bias(Q_TILE, win, off, DIL_HALF, [s * r for s in hs])
        ones_ref[...] = _ones_blockdiag(_dil_geometry(seq_full, 0)[2])
        ones2_ref[...] = _ones_blockdiag(_dil_geometry(seq_full, 2)[2])

    def pair_tile(grp, c, t, pr, slot):
        r, seq, win, n_tiles, pad, offs = _dil_geometry(seq_full, grp)
        a_ref = a_refs[grp]
        ones_bd_ref = ones2_ref if grp == 2 else ones_ref
        if n_tiles == 1:
            q0, ks, var = 0, 0, 0
        else:
            q0 = pl.multiple_of(t * Q_TILE, Q_TILE)
            ks = pl.multiple_of(jnp.clip(q0 - pad, 0, seq - win), DIL_HALF)
            var = _tile_variant(t, n_tiles)
        lo = c * GROUP_COLS + pr * LANES
        q = a_ref[pl.ds(q0, Q_TILE), lo:lo + LANES]
        k = a_ref[pl.ds(ks, win), DIL_OUT + lo:DIL_OUT + lo + LANES]
        v = a_ref[pl.ds(ks, win), 2 * DIL_OUT + lo:2 * DIL_OUT + lo + LANES]
        s_ref = s_refs[grp].at[slot]
        s_ref[...] = _pair_scores([q], k, bias_refs[grp][var, pr])
        num, den, m = _pair_values(s_ref, v, ones_bd_ref[...], None, p_refs[grp].at[slot], aux_ref.at[slot])
        rows = pl.ds(q0, Q_TILE) if r == 1 else pl.ds(c + r * q0, Q_TILE, stride=r)
        o_nat[grp, pr, rows, :] = num / den
        l_nat[grp, pr, rows, :] = m + jnp.log2(den)

    for grp in range(3):
        r, seq, win, n_tiles, pad, offs = _dil_geometry(seq_full, grp)
        per_body = DIL_INFLIGHT // DIL_PAIRS
        if n_tiles == 1:
            for c in range(r):
                for pr in range(DIL_PAIRS):
                    pair_tile(grp, c, 0, pr, c * DIL_PAIRS + pr)
        elif r >= per_body:
            for c0 in range(0, r, per_body):
                def body(t, carry, grp=grp, c0=c0):
                    for j in range(per_body):
                        for pr in range(DIL_PAIRS):
                            pair_tile(grp, c0 + j, t, pr, j * DIL_PAIRS + pr)
                    return carry
                lax.fori_loop(0, n_tiles, body, 0)
        else:
            tiles_per_body = per_body // r
            def body(i, carry, grp=grp, r=r, tiles_per_body=tiles_per_body):
                for j in range(tiles_per_body):
                    for c in range(r):
                        for pr in range(DIL_PAIRS):
                            pair_tile(grp, c, i * tiles_per_body + j, pr, (j * r + c) * DIL_PAIRS + pr)
                return carry
            lax.fori_loop(0, n_tiles // tiles_per_body, body, 0)

    def combine(i, carry):
        rows = pl.ds(pl.multiple_of(i * ROW_TILE, ROW_TILE), ROW_TILE)
        for pr in range(DIL_PAIRS):
            ls = [l_nat[g, pr, rows, :] for g in range(3)]
            mx = jnp.maximum(jnp.maximum(ls[0], ls[1]), ls[2])
            es = [jnp.exp2(l - mx) for l in ls]
            num = es[0] * o_nat[0, pr, rows, :] + es[1] * o_nat[1, pr, rows, :] + es[2] * o_nat[2, pr, rows, :]
            out_ref[rows, pr * LANES:(pr + 1) * LANES] = (num / (es[0] + es[1] + es[2])).astype(BF16)
        return carry

    lax.fori_loop(0, seq_full // ROW_TILE, combine, 0)


def _dilated_attention(a0, a1, a2, batch, seq_full):
    views, in_specs, bias_shapes = [], [], []
    for grp, a in enumerate((a0, a1, a2)):
        r, seq, win, n_tiles, pad, offs = _dil_geometry(seq_full, grp)
        views.append(a.reshape(batch, seq, r * GROUP_COLS))
        in_specs.append(pl.BlockSpec((None, seq, r * GROUP_COLS), lambda b: (b, 0, 0)))
        bias_shapes.append(pltpu.VMEM((len(offs), DIL_PAIRS, 2 * Q_TILE, win), F32))
    win0 = _dil_geometry(seq_full, 0)[2]
    win2 = _dil_geometry(seq_full, 2)[2]
    n_straight = DIL_PATTERNS[2][1] * DIL_PAIRS
    return pl.pallas_call(
        functools.partial(_dilated_kernel, seq_full),
        grid=(batch,),
        in_specs=in_specs,
        out_specs=pl.BlockSpec((None, seq_full, DIL_OUT), lambda b: (b, 0, 0)),
        out_shape=jax.ShapeDtypeStruct((batch, seq_full, DIL_OUT), BF16),
        scratch_shapes=[pltpu.VMEM((3, DIL_PAIRS, seq_full, LANES), F32),
                        pltpu.VMEM((3, DIL_PAIRS, seq_full, LANES), F32)]
        + bias_shapes + [pltpu.VMEM((2 * win0, LANES), BF16), pltpu.VMEM((2 * win2, LANES), BF16)]
        + [pltpu.VMEM((n, 2 * Q_TILE, w), F32) for n, w in ((DIL_INFLIGHT, win0), (n_straight, win2))]
        + [pltpu.VMEM((n, Q_TILE, 2 * w), BF16) for n, w in ((DIL_INFLIGHT, win0), (n_straight, win2))]
        + [pltpu.VMEM((max(DIL_INFLIGHT, n_straight), Q_TILE, LANES), F32)],
        compiler_params=pltpu.CompilerParams(
            dimension_semantics=("arbitrary",), vmem_limit_bytes=VMEM_LIMIT),
        name="dilated",
    )(*views)


WIN_WINDOW = Q_TILE + 2 * WIN_HALF


def _window_pair_heads():
    group = WIN_Q_HEADS // WIN_KV_HEADS
    pairs = []
    for u in range(WIN_UNITS):
        for i in range(WIN_UNIT_PAIRS):
            pairs.append((2 * u * group + i, (2 * u + 1) * group + i))
    return pairs


def _window_kernel(seq, q_ref, kv_ref, sink_ref, o_ref, bias_ref, ones_ref, sink_rows_ref, s_ref, p_ref,
                   aux_ref):
    n_tiles = seq // Q_TILE
    slopes = _alibi_slopes(WIN_Q_HEADS)
    pair_heads = _window_pair_heads()
    unit_rows = 2 * WIN_UNIT_PAIRS * Q_TILE

    @pl.when(pl.program_id(0) == 0)
    def _():
        for u in range(WIN_UNITS):
            heads = [h for pr in pair_heads[u * WIN_UNIT_PAIRS:(u + 1) * WIN_UNIT_PAIRS] for h in pr]
            for v, off in enumerate((0, -WIN_HALF, -2 * WIN_HALF)):
                bias_ref[v, u] = _band_bias(Q_TILE, WIN_WINDOW, off, WIN_HALF, [slopes[h] for h in heads])
            sink_rows_ref[u] = jnp.concatenate(
                [jnp.broadcast_to(sink_ref[0:1, h:h + 1] * LOG2E, (Q_TILE, LANES)) for h in heads], axis=0)
        ones_ref[...] = _ones_blockdiag(WIN_WINDOW)

    def rows_of(t):
        q0 = pl.multiple_of(t * Q_TILE, Q_TILE)
        ks = pl.multiple_of(jnp.clip(q0 - WIN_HALF, 0, seq - WIN_WINDOW), Q_TILE)
        return q0, ks

    def unit_cols(u):
        return [(u * WIN_UNIT_PAIRS + i) * LANES for i in range(WIN_UNIT_PAIRS)]

    def score_stage(t, k_set):
        q0, ks = rows_of(t)
        var = _tile_variant(t, n_tiles)
        for u in range(WIN_UNITS):
            qs = [q_ref[pl.ds(q0, Q_TILE), c:c + LANES] for c in unit_cols(u)]
            k = kv_ref[pl.ds(ks, WIN_WINDOW), u * LANES:(u + 1) * LANES]
            s_ref[k_set * WIN_UNITS + u] = _pair_scores(qs, k, bias_ref[var, u])

    def value_stage(t, k_set):
        q0, ks = rows_of(t)
        for u in range(WIN_UNITS):
            slot = k_set * WIN_UNITS + u
            v = kv_ref[pl.ds(ks, WIN_WINDOW), WIN_KV + u * LANES:WIN_KV + (u + 1) * LANES]
            num, den, sink_term = _pair_values(s_ref.at[slot], v, ones_ref[...], sink_rows_ref.at[u],
                                               p_ref.at[slot], aux_ref.at[slot])
            o = (num / (den + sink_term)).astype(BF16)
            for i, c in enumerate(unit_cols(u)):
                o_ref[pl.ds(q0, Q_TILE), c:c + LANES] = o[i * Q_TILE:(i + 1) * Q_TILE]

    _pipelined_tiles(n_tiles, WIN_TILES_PER_BODY, score_stage, value_stage)


def _window_attention(q, kv, sink, batch, seq):
    unit_rows = 2 * WIN_UNIT_PAIRS * Q_TILE
    return pl.pallas_call(
        functools.partial(_window_kernel, seq),
        grid=(batch,),
        in_specs=[pl.BlockSpec((None, seq, WIN_Q), lambda b: (b, 0, 0)),
                  pl.BlockSpec((None, seq, 2 * WIN_KV), lambda b: (b, 0, 0)),
                  pl.BlockSpec((1, WIN_Q_HEADS), lambda b: (0, 0))],
        out_specs=pl.BlockSpec((None, seq, WIN_Q), lambda b: (b, 0, 0)),
        out_shape=jax.ShapeDtypeStruct((batch, seq, WIN_Q), BF16),
        scratch_shapes=[pltpu.VMEM((3, WIN_UNITS, unit_rows, WIN_WINDOW), F32),
                        pltpu.VMEM((2 * WIN_WINDOW, LANES), BF16),
                        pltpu.VMEM((WIN_UNITS, unit_rows, LANES), F32),
                        pltpu.VMEM((WIN_TILES_PER_BODY * WIN_UNITS, unit_rows, WIN_WINDOW), F32),
                        pltpu.VMEM((WIN_TILES_PER_BODY * WIN_UNITS, unit_rows // 2, 2 * WIN_WINDOW), BF16),
                        pltpu.VMEM((WIN_TILES_PER_BODY * WIN_UNITS, unit_rows // 2, LANES), F32)],
        compiler_params=pltpu.CompilerParams(
            dimension_semantics=("arbitrary",), vmem_limit_bytes=VMEM_LIMIT),
        name="window",
    )(q.reshape(batch, seq, WIN_Q), kv.reshape(batch, seq, 2 * WIN_KV), sink)


def _merge_kernel(oa_ref, ow_ref, h_ref, g_ref, wgate_ref, bgate_ref, wa_ref, wb_ref, wo_ref, out_ref):
    h = h_ref[...]
    xb = _rms(h, g_ref[...]).astype(BF16)
    merged = None
    for br, (o_ref, w_ref) in enumerate(((oa_ref, wa_ref), (ow_ref, wb_ref))):
        cols = slice(br * D_MODEL, (br + 1) * D_MODEL)
        gate = jax.nn.sigmoid(_dot(xb, wgate_ref[:, cols]) + bgate_ref[:, cols])
        term = gate * _dot(o_ref[...], w_ref[...])
        merged = term if merged is None else merged + term
    out_ref[...] = h + _dot(merged.astype(BF16), wo_ref[...])


def _merge_out(oa, ow, h, g, wgate, bgate, wa, wb, wo):
    m = h.shape[0]
    row = lambda c: pl.BlockSpec((ROW_TILE, c), lambda i: (i, 0))
    full = lambda a: pl.BlockSpec(a.shape, lambda i: (0, 0), pipeline_mode=pl.Buffered(1))
    return pl.pallas_call(
        _merge_kernel,
        grid=(m // ROW_TILE,),
        in_specs=[row(DIL_OUT), row(WIN_Q), row(D_MODEL), full(g), full(wgate), full(bgate),
                  full(wa), full(wb), full(wo)],
        out_specs=row(D_MODEL),
        out_shape=jax.ShapeDtypeStruct((m, D_MODEL), F32),
        compiler_params=pltpu.CompilerParams(
            dimension_semantics=("arbitrary",), vmem_limit_bytes=VMEM_LIMIT),
        name="merge_out",
    )(oa, ow, h, g, wgate, bgate, wa, wb, wo)


TOKEN_TILE = 256
CUM_CHUNK = TOKEN_TILE
SLOT_WINDOW = 64
SLOT_ALIGN = 16


def _prefix_exclusive(mask_f, tri):
    e, s = mask_f.shape
    carry = jnp.zeros((e, 1), F32)
    parts, carries = [], []
    for j in range(0, s, CUM_CHUNK):
        blk = mask_f[:, j:j + CUM_CHUNK]
        carries.append(carry)
        parts.append(_dot(blk.astype(BF16), tri) + carry)
        carry = carry + jnp.sum(blk, axis=-1, keepdims=True)
    return jnp.concatenate(parts, axis=-1), carries + [carry]


def _route_kernel(cap, h_ref, g_ref, whi_ref, wlo_ref, hn_ref, rank_ref, aff_ref, starts_ref):
    seq = h_ref.shape[0]
    hn = _rms(h_ref[...], g_ref[...])
    hn_hi = hn.astype(BF16)
    hn_lo = (hn - hn_hi.astype(F32)).astype(BF16)
    hn_ref[...] = hn_hi
    both = _dot_nt(jnp.concatenate([whi_ref[...], wlo_ref[...]], axis=0), hn_hi)
    logits = both[:N_EXPERTS] + both[N_EXPERTS:] + _dot_nt(whi_ref[...], hn_lo)
    mx = jnp.max(logits, axis=0, keepdims=True)
    ex = jnp.exp(logits - mx)
    aff = ex / jnp.sum(ex, axis=0, keepdims=True)
    bits = pltpu.bitcast(aff, jnp.int32)

    def enough(t):
        return jnp.sum(jnp.where(bits >= t, 1.0, 0.0), axis=-1, keepdims=True) >= float(cap)

    def search4(_, c):
        lo, hi = c
        q = (hi - lo) >> 2
        m1, m2, m3 = lo + q, lo + 2 * q, lo + 3 * q
        ok1, ok2, ok3 = enough(m1), enough(m2), enough(m3)
        return (jnp.where(ok3, m3, jnp.where(ok2, m2, jnp.where(ok1, m1, lo))),
                jnp.where(ok3, hi, jnp.where(ok2, m3, jnp.where(ok1, m2, m1))))

    def search2(_, c):
        lo, hi = c
        mid = lo + ((hi - lo) >> 1)
        ok = enough(mid)
        return jnp.where(ok, mid, lo), jnp.where(ok, hi, mid)

    lo0 = jnp.zeros((N_EXPERTS, 1), jnp.int32)
    hi0 = jnp.full((N_EXPERTS, 1), 0x3F800001, jnp.int32)
    thr, _ = lax.fori_loop(0, 3, search2, lax.fori_loop(0, 15, search4, (lo0, hi0)))

    r_i = lax.broadcasted_iota(jnp.int32, (CUM_CHUNK, CUM_CHUNK), 0)
    c_i = lax.broadcasted_iota(jnp.int32, (CUM_CHUNK, CUM_CHUNK), 1)
    tri = jnp.where(r_i < c_i, 1.0, 0.0).astype(BF16)
    gt = jnp.where(bits > thr, 1.0, 0.0)
    eq = jnp.where(bits == thr, 1.0, 0.0)
    need = float(cap) - jnp.sum(gt, axis=-1, keepdims=True)
    tie_rank, _ = _prefix_exclusive(eq, tri)
    sel = gt + eq * jnp.where(tie_rank < need, 1.0, 0.0)
    slot, starts = _prefix_exclusive(sel, tri)
    rank = jnp.where(sel > 0.0, slot, -1.0)
    for t in range(seq // TOKEN_TILE):
        rank_ref[t] = rank[:, t * TOKEN_TILE:(t + 1) * TOKEN_TILE]
        aff_ref[t] = aff[:, t * TOKEN_TILE:(t + 1) * TOKEN_TILE]
    lane = lax.broadcasted_iota(jnp.int32, (N_EXPERTS, LANES), 1)
    acc = jnp.zeros((N_EXPERTS, LANES), F32)
    for t, c in enumerate(starts):
        acc = acc + jnp.where(lane == t, c, 0.0)
    starts_ref[...] = acc


def _route(h, g, w_hi, w_lo, batch, seq, cap):
    n_tiles = seq // TOKEN_TILE
    return pl.pallas_call(
        functools.partial(_route_kernel, cap),
        grid=(batch,),
        in_specs=[pl.BlockSpec((None, seq, D_MODEL), lambda b: (b, 0, 0)),
                  pl.BlockSpec((1, D_MODEL), lambda b: (0, 0)),
                  pl.BlockSpec((N_EXPERTS, D_MODEL), lambda b: (0, 0)),
                  pl.BlockSpec((N_EXPERTS, D_MODEL), lambda b: (0, 0))],
        out_specs=[pl.BlockSpec((None, seq, D_MODEL), lambda b: (b, 0, 0)),
                   pl.BlockSpec((None, n_tiles, N_EXPERTS, TOKEN_TILE), lambda b: (b, 0, 0, 0)),
                   pl.BlockSpec((None, n_tiles, N_EXPERTS, TOKEN_TILE), lambda b: (b, 0, 0, 0)),
                   pl.BlockSpec((None, N_EXPERTS, LANES), lambda b: (b, 0, 0))],
        out_shape=[jax.ShapeDtypeStruct((batch, seq, D_MODEL), BF16),
                   jax.ShapeDtypeStruct((batch, n_tiles, N_EXPERTS, TOKEN_TILE), F32),
                   jax.ShapeDtypeStruct((batch, n_tiles, N_EXPERTS, TOKEN_TILE), F32),
                   jax.ShapeDtypeStruct((batch, N_EXPERTS, LANES), F32)],
        compiler_params=pltpu.CompilerParams(
            dimension_semantics=("arbitrary",), vmem_limit_bytes=VMEM_LIMIT),
        name="route",
    )(h.reshape(batch, seq, D_MODEL), g, w_hi, w_lo)


def _slot_rows(cap):
    return cap + SLOT_WINDOW


def _window_starts(win_ref, b, t, p, cap):
    base = b * (N_EXPERTS * LANES) + t
    return [pl.multiple_of(jnp.minimum(win_ref[base + e * LANES] + p * SLOT_WINDOW, cap), SLOT_ALIGN)
            for e in range(N_EXPERTS)]


def _window_hits(rank_tile, starts):
    rows = lax.broadcasted_iota(jnp.int32, (SLOT_WINDOW, TOKEN_TILE), 0).astype(F32)
    return [(rank_tile[e:e + 1, :] - starts[e].astype(F32)) == rows for e in range(N_EXPERTS)]


def _one_hot(hits):
    return jnp.concatenate([jnp.where(h, 1.0, 0.0).astype(BF16) for h in hits], axis=0)


def _gather_kernel(cap, win_ref, npass_ref, hn_ref, rank_ref, aff_ref, xe_ref, gate_ref):
    b = pl.program_id(0)
    xe_ref[...] = jnp.zeros_like(xe_ref)
    gate_ref[...] = jnp.zeros_like(gate_ref)
    for t in range(hn_ref.shape[0] // TOKEN_TILE):

        def one_pass(p, carry, t=t):
            starts = _window_starts(win_ref, b, t, p, cap)
            hits = _window_hits(rank_ref[t], starts)
            rows = _dot(_one_hot(hits), hn_ref[t * TOKEN_TILE:(t + 1) * TOKEN_TILE, :]).astype(BF16)
            aff_tile = aff_ref[t]
            for e in range(N_EXPERTS):
                win = pl.ds(starts[e], SLOT_WINDOW)
                xe_ref[e, win, :] += rows[e * SLOT_WINDOW:(e + 1) * SLOT_WINDOW]
                g = jnp.sum(jnp.where(hits[e], aff_tile[e:e + 1, :], 0.0), axis=-1, keepdims=True)
                gate_ref[e, win, :] += jnp.broadcast_to(g, (SLOT_WINDOW, LANES))
            return carry

        lax.fori_loop(0, npass_ref[b * LANES + t], one_pass, 0)


def _gather(hn, rank, aff, win, npass, cap):
    batch, seq, _ = hn.shape
    n_tiles = seq // TOKEN_TILE
    grid_spec = pltpu.PrefetchScalarGridSpec(
        num_scalar_prefetch=2,
        grid=(batch,),
        in_specs=[pl.BlockSpec((None, seq, D_MODEL), lambda b, *_: (b, 0, 0)),
                  pl.BlockSpec((None, n_tiles, N_EXPERTS, TOKEN_TILE), lambda b, *_: (b, 0, 0, 0)),
                  pl.BlockSpec((None, n_tiles, N_EXPERTS, TOKEN_TILE), lambda b, *_: (b, 0, 0, 0))],
        out_specs=[pl.BlockSpec((None, N_EXPERTS, _slot_rows(cap), D_MODEL), lambda b, *_: (b, 0, 0, 0)),
                   pl.BlockSpec((None, N_EXPERTS, _slot_rows(cap), LANES), lambda b, *_: (b, 0, 0, 0))])
    return pl.pallas_call(
        functools.partial(_gather_kernel, cap),
        grid_spec=grid_spec,
        out_shape=[jax.ShapeDtypeStruct((batch, N_EXPERTS, _slot_rows(cap), D_MODEL), BF16),
                   jax.ShapeDtypeStruct((batch, N_EXPERTS, _slot_rows(cap), LANES), F32)],
        compiler_params=pltpu.CompilerParams(
            dimension_semantics=("arbitrary",), vmem_limit_bytes=VMEM_LIMIT),
        name="gather",
    )(win, npass, hn, rank, aff)


FFN_SEQS = 4


def _ffn_kernel(cap, xe_ref, gate_ref, wg_ref, wu_ref, wd_ref, y_ref, wg_b, wu_b, wd_b):
    d = xe_ref.shape[-1]

    @pl.when(pl.program_id(1) == 0)
    def _():
        for src, dst in ((wg_ref, wg_b), (wu_ref, wu_b), (wd_ref, wd_b)):
            dst[...] = src[...].astype(BF16)

    xe = xe_ref[:, :cap, :].reshape(FFN_SEQS * cap, d)
    a = _dot(xe, wg_b[...])
    u = _dot(xe, wu_b[...])
    y = _dot((jax.nn.silu(a) * u).astype(BF16), wd_b[...])
    gate = gate_ref[:, :cap, :].reshape(FFN_SEQS * cap, LANES)
    y = y * jnp.concatenate([gate] * (d // LANES), axis=1)
    y_ref[:, :cap, :] = y.astype(BF16).reshape(FFN_SEQS, cap, d)
    y_ref[:, cap:, :] = jnp.zeros((FFN_SEQS, SLOT_WINDOW, d), BF16)


def _ffn(xe, gate, wg, wu, wd, layer, cap):
    batch = xe.shape[0]
    d_exp = wg.shape[-1]
    slots = pl.BlockSpec((FFN_SEQS, None, _slot_rows(cap), D_MODEL), lambda e, i: (i, e, 0, 0))
    return pl.pallas_call(
        functools.partial(_ffn_kernel, cap),
        grid=(N_EXPERTS, batch // FFN_SEQS),
        in_specs=[slots,
                  pl.BlockSpec((FFN_SEQS, None, _slot_rows(cap), LANES), lambda e, i: (i, e, 0, 0)),
                  pl.BlockSpec((None, None, D_MODEL, d_exp), lambda e, i: (layer, e, 0, 0)),
                  pl.BlockSpec((None, None, D_MODEL, d_exp), lambda e, i: (layer, e, 0, 0)),
                  pl.BlockSpec((None, None, d_exp, D_MODEL), lambda e, i: (layer, e, 0, 0))],
        out_specs=slots,
        out_shape=jax.ShapeDtypeStruct(xe.shape, BF16),
        scratch_shapes=[pltpu.VMEM((D_MODEL, d_exp), BF16), pltpu.VMEM((D_MODEL, d_exp), BF16),
                        pltpu.VMEM((d_exp, D_MODEL), BF16)],
        compiler_params=pltpu.CompilerParams(
            dimension_semantics=("arbitrary", "arbitrary"), vmem_limit_bytes=VMEM_LIMIT),
        name="ffn",
    )(xe, gate, wg, wu, wd)


TILES_PER_ROW_TILE = ROW_TILE // TOKEN_TILE


def _scatter_kernel(cap, project, win_ref, npass_ref, y_ref, rank_ref, h_ref, g_ref, *rest):
    if project:
        w_ref, out_ref, a0_ref, a1_ref, a2_ref, qw_ref, kv_ref, xs_ref = rest
    else:
        out_ref, = rest
    b = pl.program_id(0)
    j = pl.program_id(1)
    out_ref[...] = h_ref[...]
    for tt in range(TILES_PER_ROW_TILE):
        t = j * TILES_PER_ROW_TILE + tt
        rows = slice(tt * TOKEN_TILE, (tt + 1) * TOKEN_TILE)

        def one_pass(p, carry, t=t, tt=tt, rows=rows):
            starts = _window_starts(win_ref, b, t, p, cap)
            put = _one_hot(_window_hits(rank_ref[tt], starts))
            yw = jnp.concatenate([y_ref[e, pl.ds(starts[e], SLOT_WINDOW), :] for e in range(N_EXPERTS)],
                                 axis=0)
            out_ref[rows, :] += lax.dot_general(put, yw, (((0,), (0,)), ((), ())), preferred_element_type=F32)
            return carry

        lax.fori_loop(0, npass_ref[b * LANES + t], one_pass, 0)
    if project:
        _project(out_ref[...], g_ref, w_ref, a0_ref, a1_ref, a2_ref, qw_ref, kv_ref, xs_ref)
    else:
        out_ref[...] = _rms(out_ref[...], g_ref[...])


def _scatter(y, rank, h, win, npass, cap, g, w_next=None):
    batch, n_tiles = rank.shape[:2]
    seq = h.shape[1]
    steps = n_tiles // TILES_PER_ROW_TILE
    project = w_next is not None
    const = lambda a: pl.BlockSpec(a.shape, lambda b, j, *_: (0, 0), pipeline_mode=pl.Buffered(1))
    in_specs = [pl.BlockSpec((None, N_EXPERTS, _slot_rows(cap), D_MODEL), lambda b, j, *_: (b, 0, 0, 0)),
                pl.BlockSpec((None, TILES_PER_ROW_TILE, N_EXPERTS, TOKEN_TILE), lambda b, j, *_: (b, j, 0, 0)),
                pl.BlockSpec((None, ROW_TILE, D_MODEL), lambda b, j, *_: (b, j, 0)),
                const(g)]
    out_shape = [jax.ShapeDtypeStruct(h.shape, F32)]
    out_specs = [pl.BlockSpec((None, ROW_TILE, D_MODEL), lambda b, j, *_: (b, j, 0))]
    args = [win, npass, y, rank, h, g]
    scratch = []
    if project:
        in_specs.append(const(w_next))
        args.append(w_next)
        proj_shape, proj_specs = _proj_outputs(batch * seq, lambda b, j, *_: (b * steps + j, 0))
        out_shape += proj_shape
        out_specs += proj_specs
        scratch = [pltpu.VMEM((D_MODEL // LANES, ROW_TILE, LANES), F32)]
    grid_spec = pltpu.PrefetchScalarGridSpec(
        num_scalar_prefetch=2, grid=(batch, steps), in_specs=in_specs, out_specs=out_specs,
        scratch_shapes=scratch)
    return pl.pallas_call(
        functools.partial(_scatter_kernel, cap, project),
        grid_spec=grid_spec,
        out_shape=out_shape,
        compiler_params=pltpu.CompilerParams(
            dimension_semantics=("arbitrary", "arbitrary"), vmem_limit_bytes=VMEM_LIMIT),
        name="scatter_proj" if project else "scatter_final",
    )(*args)


def _slot_windows(starts, n_tiles, cap):
    s = starts.astype(jnp.int32)
    first = (s // SLOT_ALIGN) * SLOT_ALIGN
    span = s[:, :, 1:n_tiles + 1] - first[:, :, :n_tiles]
    npass = jnp.maximum(jnp.max(-(-span // SLOT_WINDOW), axis=1), 1)
    npass = jnp.pad(npass, ((0, 0), (0, LANES - n_tiles)))
    return first.reshape(-1), npass.reshape(-1)


def _arrange_w_in(w):
    scale = LOG2E * HEAD_DIM ** -0.5
    qa, ka, va = w[:, :DIL_QKV], w[:, DIL_QKV:2 * DIL_QKV], w[:, 2 * DIL_QKV:3 * DIL_QKV]
    rest = w[:, 3 * DIL_QKV:]
    parts = []
    for g in range(len(DIL_PATTERNS)):
        sl = slice(g * DIL_OUT, (g + 1) * DIL_OUT)
        parts += [qa[:, sl] * scale, ka[:, sl], va[:, sl]]
    for pr in _window_pair_heads():
        parts += [rest[:, h * HEAD_DIM:(h + 1) * HEAD_DIM] * scale for h in pr]
    parts.append(rest[:, WIN_Q:WIN_Q + 2 * WIN_KV])
    return jnp.concatenate(parts, axis=1).astype(BF16), rest[:, WIN_Q + 2 * WIN_KV:].astype(BF16)


def _arrange_w_branch_b(w):
    return jnp.concatenate([w[h * HEAD_DIM:(h + 1) * HEAD_DIM] for pr in _window_pair_heads() for h in pr],
                           axis=0).astype(BF16)


def kernel(x, norm_mix, w_in, w_branch_a, w_branch_b, b_gate, sink_logit, w_out, norm_ffn, w_router,
           w_expert_gate, w_expert_up, w_expert_down, norm_final):
    batch, seq, d = x.shape
    depth = w_in.shape[0]
    cap = CAPACITY_FACTOR * seq // N_EXPERTS
    m = batch * seq
    h = x.reshape(m, d)
    w_attn, w_gates = _arrange_w_in(w_in[0])
    proj = _norm_proj(h, norm_mix[0][None, :], w_attn)
    for l in range(depth):
        a0, a1, a2, qw, kv = proj
        oa = _dilated_attention(a0, a1, a2, batch, seq).reshape(m, DIL_OUT)
        ow = _window_attention(qw, kv, sink_logit[l][None, :], batch, seq).reshape(m, WIN_Q)
        h = _merge_out(oa, ow, h, norm_mix[l][None, :], w_gates, b_gate[l][None, :],
                       w_branch_a[l].astype(BF16), _arrange_w_branch_b(w_branch_b[l]), w_out[l].astype(BF16))
        wr = w_router[l].T
        wr_hi = wr.astype(BF16)
        wr_lo = (wr - wr_hi.astype(F32)).astype(BF16)
        hn, rank, aff, starts = _route(h, norm_ffn[l][None, :], wr_hi, wr_lo, batch, seq, cap)
        win, npass = _slot_windows(starts, seq // TOKEN_TILE, cap)
        xe, gate = _gather(hn, rank, aff, win, npass, cap)
        y = _ffn(xe, gate, w_expert_gate, w_expert_up, w_expert_down, l, cap)
        h3 = h.reshape(batch, seq, d)
        if l + 1 == depth:
            return _scatter(y, rank, h3, win, npass, cap, norm_final[None, :])[0]
        w_attn, w_gates = _arrange_w_in(w_in[l + 1])
        h3, *proj = _scatter(y, rank, h3, win, npass, cap, norm_mix[l + 1][None, :], w_attn)
        h = h3.reshape(m, d)
```

```python
import functools

import jax
import jax.numpy as jnp
from jax import lax
from jax.experimental import pallas as pl
from jax.experimental.pallas import tpu as pltpu

D_MODEL = 1024
HEAD_DIM = 64
DIL_PATTERNS = ((128, 1), (512, 4), (2048, 16))
DIL_HEADS = 4
N_DIL_SUB = DIL_HEADS * len(DIL_PATTERNS)
DIL_QKV = N_DIL_SUB * HEAD_DIM
DIL_OUT = DIL_HEADS * HEAD_DIM
DIL_HALF = 64
WIN_HALF = 128
WIN_Q = D_MODEL
WIN_Q_HEADS = WIN_Q // HEAD_DIM
WIN_KV_HEADS = 4
WIN_KV = WIN_KV_HEADS * HEAD_DIM
N_EXPERTS = 16
CAPACITY_FACTOR = 2
RMS_EPS = 1e-6
NEG_INF = -1e30
LOG2E = 1.4426950408889634

LANES = 128
GROUP_COLS = 3 * DIL_OUT
DIL_PAIRS = DIL_OUT // LANES
WIN_UNITS = WIN_KV_HEADS // 2
WIN_UNIT_PAIRS = WIN_Q_HEADS // (2 * WIN_UNITS)
Q_TILE = 128
WIN_TILES_PER_BODY = 4
DIL_INFLIGHT = 16
ROW_TILE = 512
MERGE_TILE = 1024
VMEM_LIMIT = 56 * 1024 * 1024

F32 = jnp.float32
BF16 = jnp.bfloat16


def _alibi_slopes(n):
    return [float(2.0 ** (-8.0 * i / n)) for i in range(1, n + 1)]


def _rms(x, g):
    return x * lax.rsqrt(jnp.mean(x * x, axis=-1, keepdims=True) + RMS_EPS) * g


def _dot(a, b):
    return jnp.dot(a, b, preferred_element_type=F32)


def _dot_nt(a, b):
    return lax.dot_general(a, b, (((1,), (1,)), ((), ())), preferred_element_type=F32)


def _left_lanes():
    return lax.broadcasted_iota(jnp.int32, (1, LANES), 1) < HEAD_DIM


_C_A = (0, GROUP_COLS, 2 * GROUP_COLS)
_C_QW = 3 * GROUP_COLS
_C_KV = _C_QW + WIN_Q
_COL_CHUNK = 512


def _proj_store(ref, xb, w_ref, c0, cw):
    for j in range(0, cw, _COL_CHUNK):
        jw = min(_COL_CHUNK, cw - j)
        ref[:, j:j + jw] = _dot(xb, w_ref[:, c0 + j:c0 + j + jw]).astype(BF16)


def _norm_proj_kernel(x_ref, g_ref, w_ref, a0_ref, a1_ref, a2_ref, qw_ref, kv_ref, xs_ref):
    _project(x_ref[...], g_ref, w_ref, a0_ref, a1_ref, a2_ref, qw_ref, kv_ref, xs_ref)


def _project(x, g_ref, w_ref, a0_ref, a1_ref, a2_ref, qw_ref, kv_ref, xs_ref):
    xn = _rms(x, g_ref[...])
    n_lane_tiles = D_MODEL // LANES
    for j in range(n_lane_tiles):
        xs_ref[j] = xn[:, j * LANES:(j + 1) * LANES]
    xb = xn.astype(BF16)
    _proj_store(a0_ref, xb, w_ref, _C_A[0], GROUP_COLS)
    for grp, ref in ((1, a1_ref), (2, a2_ref)):
        r = DIL_PATTERNS[grp][1]
        n = ROW_TILE // r
        xp = jnp.concatenate(
            [jnp.concatenate([xs_ref[j, pl.ds(c, n, stride=r), :] for j in range(n_lane_tiles)], axis=1)
             for c in range(r)], axis=0).astype(BF16)
        res = _dot(xp, w_ref[:, _C_A[grp]:_C_A[grp] + GROUP_COLS])
        for c in range(r):
            ref[:, c * GROUP_COLS:(c + 1) * GROUP_COLS] = res[c * n:(c + 1) * n].astype(BF16)
    _proj_store(qw_ref, xb, w_ref, _C_QW, WIN_Q)
    _proj_store(kv_ref, xb, w_ref, _C_KV, 2 * WIN_KV)


def _proj_outputs(m, row_index):
    out_shape, out_specs = [], []
    for _, r in DIL_PATTERNS:
        out_shape.append(jax.ShapeDtypeStruct((m // r, r * GROUP_COLS), BF16))
        out_specs.append(pl.BlockSpec((ROW_TILE // r, r * GROUP_COLS), row_index))
    for c in (WIN_Q, 2 * WIN_KV):
        out_shape.append(jax.ShapeDtypeStruct((m, c), BF16))
        out_specs.append(pl.BlockSpec((ROW_TILE, c), row_index))
    return out_shape, out_specs


def _norm_proj(x, g, w):
    m = x.shape[0]
    row = lambda c: pl.BlockSpec((ROW_TILE, c), lambda i: (i, 0))
    full = lambda a: pl.BlockSpec(a.shape, lambda i: (0, 0), pipeline_mode=pl.Buffered(1))
    in_specs = [row(D_MODEL), full(g), full(w)]
    out_shape, out_specs = _proj_outputs(m, lambda i: (i, 0))
    return pl.pallas_call(
        _norm_proj_kernel,
        grid=(m // ROW_TILE,),
        in_specs=in_specs,
        out_specs=out_specs,
        out_shape=out_shape,
        scratch_shapes=[pltpu.VMEM((D_MODEL // LANES, ROW_TILE, LANES), F32)],
        compiler_params=pltpu.CompilerParams(
            dimension_semantics=("arbitrary",), vmem_limit_bytes=VMEM_LIMIT),
        name="norm_proj",
    )(x, g, w)


def _band_bias(tq, win, off, half, slopes):
    col = lax.broadcasted_iota(jnp.int32, (tq, win), 1)
    row = lax.broadcasted_iota(jnp.int32, (tq, win), 0)
    dist = jnp.abs(col - row + off).astype(F32)
    inside = dist <= float(half)
    return jnp.concatenate([jnp.where(inside, dist * (-s * LOG2E), NEG_INF) for s in slopes], axis=0)


def _ones_blockdiag(win):
    left = lax.broadcasted_iota(jnp.int32, (2 * win, LANES), 1) < HEAD_DIM
    top = jnp.where(lax.broadcasted_iota(jnp.int32, (2 * win, LANES), 0) < win, 1.0, 0.0)
    return jnp.where(left, top, 1.0 - top).astype(BF16)


def _pair_scores(q_tiles, k_win, bias):
    left = _left_lanes()
    zero = jnp.zeros((), BF16)
    rows = []
    for q in q_tiles:
        rows += [jnp.where(left, q, zero), jnp.where(left, zero, q)]
    return _dot_nt(jnp.concatenate(rows, axis=0), k_win) + bias


def _pair_values(s_ref, v_win, ones_bd, sink_ref, p_ref, aux_ref):
    win = v_win.shape[0]
    n = s_ref.shape[0] // (2 * Q_TILE)
    left = _left_lanes()
    zero = jnp.zeros((), BF16)
    for i in range(n):
        ms = []
        for side in range(2):
            rows = pl.ds((2 * i + side) * Q_TILE, Q_TILE)
            s = s_ref[rows, :]
            m = jnp.broadcast_to(jnp.max(s, axis=-1, keepdims=True), (Q_TILE, LANES))
            if sink_ref is not None:
                m = jnp.maximum(m, sink_ref[rows, :])
            p = jnp.exp2(s - jnp.concatenate([m] * (win // LANES), axis=1)).astype(BF16)
            p_ref[i * Q_TILE:(i + 1) * Q_TILE, side * win:(side + 1) * win] = p
            ms.append(m)
        m_pair = jnp.where(left, ms[0], ms[1])
        if sink_ref is not None:
            sink_pair = jnp.where(left, sink_ref[pl.ds(2 * i * Q_TILE, Q_TILE), :],
                                  sink_ref[pl.ds((2 * i + 1) * Q_TILE, Q_TILE), :])
            aux_ref[i * Q_TILE:(i + 1) * Q_TILE, :] = jnp.exp2(sink_pair - m_pair)
        else:
            aux_ref[i * Q_TILE:(i + 1) * Q_TILE, :] = m_pair
    vbd = jnp.concatenate([jnp.where(left, v_win, zero), jnp.where(left, zero, v_win)], axis=0)
    on = _dot(p_ref[...], jnp.concatenate([vbd, ones_bd], axis=1))
    return on[:, :LANES], on[:, LANES:], aux_ref[...]


def _tile_variant(t, n_tiles):
    return jnp.where(t == 0, 0, jnp.where(t == n_tiles - 1, 2, 1))


def _pipelined_tiles(n_tiles, per_body, score_stage, value_stage):
    score_stage(0, 0)

    def body(i, carry):
        t = per_body * i
        for j in range(per_body):
            score_stage(jnp.minimum(t + j + 1, n_tiles - 1), (j + 1) % per_body)
            value_stage(t + j, j)
        return carry

    lax.fori_loop(0, n_tiles // per_body, body, 0)


def _dil_geometry(seq_full, grp):
    r = DIL_PATTERNS[grp][1]
    seq = seq_full // r
    win = min(seq, Q_TILE + 2 * DIL_HALF)
    n_tiles = seq // Q_TILE
    pad = (win - Q_TILE) // 2
    offs = [0] if n_tiles == 1 else [0, -pad, -2 * pad]
    return r, seq, win, n_tiles, pad, offs


def _dilated_kernel(seq_full, a0_ref, a1_ref, a2_ref, out_ref, o_nat, l_nat, b0, b1, b2, ones_ref, ones2_ref,
                    s_wide, s_narrow, p_wide, p_narrow, aux_ref):
    a_refs = (a0_ref, a1_ref, a2_ref)
    bias_refs = (b0, b1, b2)
    s_refs = (s_wide, s_wide, s_narrow)
    p_refs = (p_wide, p_wide, p_narrow)
    slopes = _alibi_slopes(N_DIL_SUB)

    @pl.when(pl.program_id(0) == 0)
    def _():
        for grp in range(3):
            r, seq, win, n_tiles, pad, offs = _dil_geometry(seq_full, grp)
            for v, off in enumerate(offs):
                for pr in range(DIL_PAIRS):
                    hs = slopes[grp * DIL_HEADS + 2 * pr:grp * DIL_HEADS + 2 * pr + 2]
                    bias_refs[grp][v, pr] = _band_bias(Q_TILE, win, off, DIL_HALF, [s * r for s in hs])
        ones_ref[...] = _ones_blockdiag(_dil_geometry(seq_full, 0)[2])
        ones2_ref[...] = _ones_blockdiag(_dil_geometry(seq_full, 2)[2])

    def pair_tile(grp, c, t, pr, slot):
        r, seq, win, n_tiles, pad, offs = _dil_geometry(seq_full, grp)
        a_ref = a_refs[grp]
        ones_bd_ref = ones2_ref if grp == 2 else ones_ref
        if n_tiles == 1:
            q0, ks, var = 0, 0, 0
        else:
            q0 = pl.multiple_of(t * Q_TILE, Q_TILE)
            ks = pl.multiple_of(jnp.clip(q0 - pad, 0, seq - win), DIL_HALF)
            var = _tile_variant(t, n_tiles)
        lo = c * GROUP_COLS + pr * LANES
        q = a_ref[pl.ds(q0, Q_TILE), lo:lo + LANES]
        k = a_ref[pl.ds(ks, win), DIL_OUT + lo:DIL_OUT + lo + LANES]
        v = a_ref[pl.ds(ks, win), 2 * DIL_OUT + lo:2 * DIL_OUT + lo + LANES]
        s_ref = s_refs[grp].at[slot]
        s_ref[...] = _pair_scores([q], k, bias_refs[grp][var, pr])
        num, den, m = _pair_values(s_ref, v, ones_bd_ref[...], None, p_refs[grp].at[slot], aux_ref.at[slot])
        rows = pl.ds(q0, Q_TILE) if r == 1 else pl.ds(c + r * q0, Q_TILE, stride=r)
        o_nat[grp, pr, rows, :] = num / den
        l_nat[grp, pr, rows, :] = m + jnp.log2(den)

    for grp in range(3):
        r, seq, win, n_tiles, pad, offs = _dil_geometry(seq_full, grp)
        per_body = DIL_INFLIGHT // DIL_PAIRS
        if n_tiles == 1:
            for c in range(r):
                for pr in range(DIL_PAIRS):
                    pair_tile(grp, c, 0, pr, c * DIL_PAIRS + pr)
        elif r >= per_body:
            for c0 in range(0, r, per_body):
                def body(t, carry, grp=grp, c0=c0):
                    for j in range(per_body):
                        for pr in range(DIL_PAIRS):
                            pair_tile(grp, c0 + j, t, pr, j * DIL_PAIRS + pr)
                    return carry
                lax.fori_loop(0, n_tiles, body, 0)
        else:
            tiles_per_body = per_body // r
            def body(i, carry, grp=grp, r=r, tiles_per_body=tiles_per_body):
                for j in range(tiles_per_body):
                    for c in range(r):
                        for pr in range(DIL_PAIRS):
                            pair_tile(grp, c, i * tiles_per_body + j, pr, (j * r + c) * DIL_PAIRS + pr)
                return carry
            lax.fori_loop(0, n_tiles // tiles_per_body, body, 0)

    def combine(i, carry):
        rows = pl.ds(pl.multiple_of(i * ROW_TILE, ROW_TILE), ROW_TILE)
        for pr in range(DIL_PAIRS):
            ls = [l_nat[g, pr, rows, :] for g in range(3)]
            mx = jnp.maximum(jnp.maximum(ls[0], ls[1]), ls[2])
            es = [jnp.exp2(l - mx) for l in ls]
            num = es[0] * o_nat[0, pr, rows, :] + es[1] * o_nat[1, pr, rows, :] + es[2] * o_nat[2, pr, rows, :]
            out_ref[rows, pr * LANES:(pr + 1) * LANES] = (num / (es[0] + es[1] + es[2])).astype(BF16)
        return carry

    lax.fori_loop(0, seq_full // ROW_TILE, combine, 0)


def _dilated_attention(a0, a1, a2, batch, seq_full):
    views, in_specs, bias_shapes = [], [], []
    for grp, a in enumerate((a0, a1, a2)):
        r, seq, win, n_tiles, pad, offs = _dil_geometry(seq_full, grp)
        views.append(a.reshape(batch, seq, r * GROUP_COLS))
        in_specs.append(pl.BlockSpec((None, seq, r * GROUP_COLS), lambda b: (b, 0, 0)))
        bias_shapes.append(pltpu.VMEM((len(offs), DIL_PAIRS, 2 * Q_TILE, win), F32))
    win0 = _dil_geometry(seq_full, 0)[2]
    win2 = _dil_geometry(seq_full, 2)[2]
    n_straight = DIL_PATTERNS[2][1] * DIL_PAIRS
    return pl.pallas_call(
        functools.partial(_dilated_kernel, seq_full),
        grid=(batch,),
        in_specs=in_specs,
        out_specs=pl.BlockSpec((None, seq_full, DIL_OUT), lambda b: (b, 0, 0)),
        out_shape=jax.ShapeDtypeStruct((batch, seq_full, DIL_OUT), BF16),
        scratch_shapes=[pltpu.VMEM((3, DIL_PAIRS, seq_full, LANES), F32),
                        pltpu.VMEM((3, DIL_PAIRS, seq_full, LANES), F32)]
        + bias_shapes + [pltpu.VMEM((2 * win0, LANES), BF16), pltpu.VMEM((2 * win2, LANES), BF16)]
        + [pltpu.VMEM((n, 2 * Q_TILE, w), F32) for n, w in ((DIL_INFLIGHT, win0), (n_straight, win2))]
        + [pltpu.VMEM((n, Q_TILE, 2 * w), BF16) for n, w in ((DIL_INFLIGHT, win0), (n_straight, win2))]
        + [pltpu.VMEM((max(DIL_INFLIGHT, n_straight), Q_TILE, LANES), F32)],
        compiler_params=pltpu.CompilerParams(
            dimension_semantics=("arbitrary",), vmem_limit_bytes=VMEM_LIMIT),
        name="dilated",
    )(*views)


WIN_WINDOW = Q_TILE + 2 * WIN_HALF


def _window_pair_heads():
    group = WIN_Q_HEADS // WIN_KV_HEADS
    pairs = []
    for u in range(WIN_UNITS):
        for i in range(WIN_UNIT_PAIRS):
            pairs.append((2 * u * group + i, (2 * u + 1) * group + i))
    return pairs


def _window_kernel(seq, q_ref, kv_ref, sink_ref, o_ref, bias_ref, ones_ref, sink_rows_ref, s_ref, p_ref,
                   aux_ref):
    n_tiles = seq // Q_TILE
    slopes = _alibi_slopes(WIN_Q_HEADS)
    pair_heads = _window_pair_heads()
    unit_rows = 2 * WIN_UNIT_PAIRS * Q_TILE

    @pl.when(pl.program_id(0) == 0)
    def _():
        for u in range(WIN_UNITS):
            heads = [h for pr in pair_heads[u * WIN_UNIT_PAIRS:(u + 1) * WIN_UNIT_PAIRS] for h in pr]
            for v, off in enumerate((0, -WIN_HALF, -2 * WIN_HALF)):
                bias_ref[v, u] = _band_bias(Q_TILE, WIN_WINDOW, off, WIN_HALF, [slopes[h] for h in heads])
            sink_rows_ref[u] = jnp.concatenate(
                [jnp.broadcast_to(sink_ref[0:1, h:h + 1] * LOG2E, (Q_TILE, LANES)) for h in heads], axis=0)
        ones_ref[...] = _ones_blockdiag(WIN_WINDOW)

    def rows_of(t):
        q0 = pl.multiple_of(t * Q_TILE, Q_TILE)
        ks = pl.multiple_of(jnp.clip(q0 - WIN_HALF, 0, seq - WIN_WINDOW), Q_TILE)
        return q0, ks

    def unit_cols(u):
        return [(u * WIN_UNIT_PAIRS + i) * LANES for i in range(WIN_UNIT_PAIRS)]

    def score_stage(t, k_set):
        q0, ks = rows_of(t)
        var = _tile_variant(t, n_tiles)
        for u in range(WIN_UNITS):
            qs = [q_ref[pl.ds(q0, Q_TILE), c:c + LANES] for c in unit_cols(u)]
            k = kv_ref[pl.ds(ks, WIN_WINDOW), u * LANES:(u + 1) * LANES]
            s_ref[k_set * WIN_UNITS + u] = _pair_scores(qs, k, bias_ref[var, u])

    def value_stage(t, k_set):
        q0, ks = rows_of(t)
        for u in range(WIN_UNITS):
            slot = k_set * WIN_UNITS + u
            v = kv_ref[pl.ds(ks, WIN_WINDOW), WIN_KV + u * LANES:WIN_KV + (u + 1) * LANES]
            num, den, sink_term = _pair_values(s_ref.at[slot], v, ones_ref[...], sink_rows_ref.at[u],
                                               p_ref.at[slot], aux_ref.at[slot])
            o = (num / (den + sink_term)).astype(BF16)
            for i, c in enumerate(unit_cols(u)):
                o_ref[pl.ds(q0, Q_TILE), c:c + LANES] = o[i * Q_TILE:(i + 1) * Q_TILE]

    _pipelined_tiles(n_tiles, WIN_TILES_PER_BODY, score_stage, value_stage)


def _window_attention(q, kv, sink, batch, seq):
    unit_rows = 2 * WIN_UNIT_PAIRS * Q_TILE
    return pl.pallas_call(
        functools.partial(_window_kernel, seq),
        grid=(batch,),
        in_specs=[pl.BlockSpec((None, seq, WIN_Q), lambda b: (b, 0, 0)),
                  pl.BlockSpec((None, seq, 2 * WIN_KV), lambda b: (b, 0, 0)),
                  pl.BlockSpec((1, WIN_Q_HEADS), lambda b: (0, 0))],
        out_specs=pl.BlockSpec((None, seq, WIN_Q), lambda b: (b, 0, 0)),
        out_shape=jax.ShapeDtypeStruct((batch, seq, WIN_Q), BF16),
        scratch_shapes=[pltpu.VMEM((3, WIN_UNITS, unit_rows, WIN_WINDOW), F32),
                        pltpu.VMEM((2 * WIN_WINDOW, LANES), BF16),
                        pltpu.VMEM((WIN_UNITS, unit_rows, LANES), F32),
                        pltpu.VMEM((WIN_TILES_PER_BODY * WIN_UNITS, unit_rows, WIN_WINDOW), F32),
                        pltpu.VMEM((WIN_TILES_PER_BODY * WIN_UNITS, unit_rows // 2, 2 * WIN_WINDOW), BF16),
                        pltpu.VMEM((WIN_TILES_PER_BODY * WIN_UNITS, unit_rows // 2, LANES), F32)],
        compiler_params=pltpu.CompilerParams(
            dimension_semantics=("arbitrary",), vmem_limit_bytes=VMEM_LIMIT),
        name="window",
    )(q.reshape(batch, seq, WIN_Q), kv.reshape(batch, seq, 2 * WIN_KV), sink)


def _merge_kernel(oa_ref, ow_ref, h_ref, g_ref, wgate_ref, bgate_ref, wa_ref, wb_ref, wo_ref, out_ref):
    h = h_ref[...]
    xb = _rms(h, g_ref[...]).astype(BF16)
    merged = None
    for br, (o_ref, w_ref) in enumerate(((oa_ref, wa_ref), (ow_ref, wb_ref))):
        cols = slice(br * D_MODEL, (br + 1) * D_MODEL)
        gate = jax.nn.sigmoid(_dot(xb, wgate_ref[:, cols]) + bgate_ref[:, cols])
        term = gate * _dot(o_ref[...], w_ref[...])
        merged = term if merged is None else merged + term
    out_ref[...] = h + _dot(merged.astype(BF16), wo_ref[...])


def _merge_out(oa, ow, h, g, wgate, bgate, wa, wb, wo):
    m = h.shape[0]
    row = lambda c: pl.BlockSpec((MERGE_TILE, c), lambda i: (i, 0))
    full = lambda a: pl.BlockSpec(a.shape, lambda i: (0, 0), pipeline_mode=pl.Buffered(1))
    return pl.pallas_call(
        _merge_kernel,
        grid=(m // MERGE_TILE,),
        in_specs=[row(DIL_OUT), row(WIN_Q), row(D_MODEL), full(g), full(wgate), full(bgate),
                  full(wa), full(wb), full(wo)],
        out_specs=row(D_MODEL),
        out_shape=jax.ShapeDtypeStruct((m, D_MODEL), F32),
        compiler_params=pltpu.CompilerParams(
            dimension_semantics=("arbitrary",), vmem_limit_bytes=VMEM_LIMIT),
        name="merge_out",
    )(oa, ow, h, g, wgate, bgate, wa, wb, wo)


TOKEN_TILE = 256
CUM_CHUNK = TOKEN_TILE
SLOT_WINDOW = 64
SLOT_ALIGN = 16


def _prefix_exclusive(mask_f, tri):
    e, s = mask_f.shape
    carry = jnp.zeros((e, 1), F32)
    parts, carries = [], []
    for j in range(0, s, CUM_CHUNK):
        blk = mask_f[:, j:j + CUM_CHUNK]
        carries.append(carry)
        parts.append(_dot(blk.astype(BF16), tri) + carry)
        carry = carry + jnp.sum(blk, axis=-1, keepdims=True)
    return jnp.concatenate(parts, axis=-1), carries + [carry]


def _route_kernel(cap, h_ref, g_ref, whi_ref, wlo_ref, hn_ref, rank_ref, aff_ref, starts_ref):
    seq = h_ref.shape[0]
    hn = _rms(h_ref[...], g_ref[...])
    hn_hi = hn.astype(BF16)
    hn_lo = (hn - hn_hi.astype(F32)).astype(BF16)
    hn_ref[...] = hn_hi
    both = _dot_nt(jnp.concatenate([whi_ref[...], wlo_ref[...]], axis=0), hn_hi)
    logits = both[:N_EXPERTS] + both[N_EXPERTS:] + _dot_nt(whi_ref[...], hn_lo)
    mx = jnp.max(logits, axis=0, keepdims=True)
    ex = jnp.exp(logits - mx)
    aff = ex / jnp.sum(ex, axis=0, keepdims=True)
    bits = pltpu.bitcast(aff, jnp.int32)

    def enough(t):
        return jnp.sum(jnp.where(bits >= t, 1.0, 0.0), axis=-1, keepdims=True) >= float(cap)

    def search4(_, c):
        lo, hi = c
        q = (hi - lo) >> 2
        m1, m2, m3 = lo + q, lo + 2 * q, lo + 3 * q
        ok1, ok2, ok3 = enough(m1), enough(m2), enough(m3)
        return (jnp.where(ok3, m3, jnp.where(ok2, m2, jnp.where(ok1, m1, lo))),
                jnp.where(ok3, hi, jnp.where(ok2, m3, jnp.where(ok1, m2, m1))))

    def search2(_, c):
        lo, hi = c
        mid = lo + ((hi - lo) >> 1)
        ok = enough(mid)
        return jnp.where(ok, mid, lo), jnp.where(ok, hi, mid)

    lo0 = jnp.zeros((N_EXPERTS, 1), jnp.int32)
    hi0 = jnp.full((N_EXPERTS, 1), 0x3F800001, jnp.int32)
    thr, _ = lax.fori_loop(0, 3, search2, lax.fori_loop(0, 15, search4, (lo0, hi0)))

    r_i = lax.broadcasted_iota(jnp.int32, (CUM_CHUNK, CUM_CHUNK), 0)
    c_i = lax.broadcasted_iota(jnp.int32, (CUM_CHUNK, CUM_CHUNK), 1)
    tri = jnp.where(r_i < c_i, 1.0, 0.0).astype(BF16)
    gt = jnp.where(bits > thr, 1.0, 0.0)
    eq = jnp.where(bits == thr, 1.0, 0.0)
    need = float(cap) - jnp.sum(gt, axis=-1, keepdims=True)
    tie_rank, _ = _prefix_exclusive(eq, tri)
    sel = gt + eq * jnp.where(tie_rank < need, 1.0, 0.0)
    slot, starts = _prefix_exclusive(sel, tri)
    rank = jnp.where(sel > 0.0, slot, -1.0)
    for t in range(seq // TOKEN_TILE):
        rank_ref[t] = rank[:, t * TOKEN_TILE:(t + 1) * TOKEN_TILE]
        aff_ref[t] = aff[:, t * TOKEN_TILE:(t + 1) * TOKEN_TILE]
    lane = lax.broadcasted_iota(jnp.int32, (N_EXPERTS, LANES), 1)
    acc = jnp.zeros((N_EXPERTS, LANES), F32)
    for t, c in enumerate(starts):
        acc = acc + jnp.where(lane == t, c, 0.0)
    starts_ref[...] = acc


def _route(h, g, w_hi, w_lo, batch, seq, cap):
    n_tiles = seq // TOKEN_TILE
    return pl.pallas_call(
        functools.partial(_route_kernel, cap),
        grid=(batch,),
        in_specs=[pl.BlockSpec((None, seq, D_MODEL), lambda b: (b, 0, 0)),
                  pl.BlockSpec((1, D_MODEL), lambda b: (0, 0)),
                  pl.BlockSpec((N_EXPERTS, D_MODEL), lambda b: (0, 0)),
                  pl.BlockSpec((N_EXPERTS, D_MODEL), lambda b: (0, 0))],
        out_specs=[pl.BlockSpec((None, seq, D_MODEL), lambda b: (b, 0, 0)),
                   pl.BlockSpec((None, n_tiles, N_EXPERTS, TOKEN_TILE), lambda b: (b, 0, 0, 0)),
                   pl.BlockSpec((None, n_tiles, N_EXPERTS, TOKEN_TILE), lambda b: (b, 0, 0, 0)),
                   pl.BlockSpec((None, N_EXPERTS, LANES), lambda b: (b, 0, 0))],
        out_shape=[jax.ShapeDtypeStruct((batch, seq, D_MODEL), BF16),
                   jax.ShapeDtypeStruct((batch, n_tiles, N_EXPERTS, TOKEN_TILE), F32),
                   jax.ShapeDtypeStruct((batch, n_tiles, N_EXPERTS, TOKEN_TILE), F32),
                   jax.ShapeDtypeStruct((batch, N_EXPERTS, LANES), F32)],
        compiler_params=pltpu.CompilerParams(
            dimension_semantics=("arbitrary",), vmem_limit_bytes=VMEM_LIMIT),
        name="route",
    )(h.reshape(batch, seq, D_MODEL), g, w_hi, w_lo)


def _slot_rows(cap):
    return cap + SLOT_WINDOW


def _window_starts(win_ref, b, t, p, cap):
    base = b * (N_EXPERTS * LANES) + t
    return [pl.multiple_of(jnp.minimum(win_ref[base + e * LANES] + p * SLOT_WINDOW, cap), SLOT_ALIGN)
            for e in range(N_EXPERTS)]


def _window_hits(rank_tile, starts):
    rows = lax.broadcasted_iota(jnp.int32, (SLOT_WINDOW, TOKEN_TILE), 0).astype(F32)
    return [(rank_tile[e:e + 1, :] - starts[e].astype(F32)) == rows for e in range(N_EXPERTS)]


def _one_hot(hits):
    return jnp.concatenate([jnp.where(h, 1.0, 0.0).astype(BF16) for h in hits], axis=0)


def _gather_kernel(cap, win_ref, npass_ref, hn_ref, rank_ref, aff_ref, xe_ref, gate_ref):
    b = pl.program_id(0)
    xe_ref[...] = jnp.zeros_like(xe_ref)
    gate_ref[...] = jnp.zeros_like(gate_ref)
    for t in range(hn_ref.shape[0] // TOKEN_TILE):

        def one_pass(p, carry, t=t):
            starts = _window_starts(win_ref, b, t, p, cap)
            hits = _window_hits(rank_ref[t], starts)
            rows = _dot(_one_hot(hits), hn_ref[t * TOKEN_TILE:(t + 1) * TOKEN_TILE, :]).astype(BF16)
            aff_tile = aff_ref[t]
            for e in range(N_EXPERTS):
                win = pl.ds(starts[e], SLOT_WINDOW)
                xe_ref[e, win, :] += rows[e * SLOT_WINDOW:(e + 1) * SLOT_WINDOW]
                g = jnp.sum(jnp.where(hits[e], aff_tile[e:e + 1, :], 0.0), axis=-1, keepdims=True)
                gate_ref[e, win, :] += jnp.broadcast_to(g, (SLOT_WINDOW, LANES))
            return carry

        lax.fori_loop(0, npass_ref[b * LANES + t], one_pass, 0)


def _gather(hn, rank, aff, win, npass, cap):
    batch, seq, _ = hn.shape
    n_tiles = seq // TOKEN_TILE
    grid_spec = pltpu.PrefetchScalarGridSpec(
        num_scalar_prefetch=2,
        grid=(batch,),
        in_specs=[pl.BlockSpec((None, seq, D_MODEL), lambda b, *_: (b, 0, 0)),
                  pl.BlockSpec((None, n_tiles, N_EXPERTS, TOKEN_TILE), lambda b, *_: (b, 0, 0, 0)),
                  pl.BlockSpec((None, n_tiles, N_EXPERTS, TOKEN_TILE), lambda b, *_: (b, 0, 0, 0))],
        out_specs=[pl.BlockSpec((None, N_EXPERTS, _slot_rows(cap), D_MODEL), lambda b, *_: (b, 0, 0, 0)),
                   pl.BlockSpec((None, N_EXPERTS, _slot_rows(cap), LANES), lambda b, *_: (b, 0, 0, 0))])
    return pl.pallas_call(
        functools.partial(_gather_kernel, cap),
        grid_spec=grid_spec,
        out_shape=[jax.ShapeDtypeStruct((batch, N_EXPERTS, _slot_rows(cap), D_MODEL), BF16),
                   jax.ShapeDtypeStruct((batch, N_EXPERTS, _slot_rows(cap), LANES), F32)],
        compiler_params=pltpu.CompilerParams(
            dimension_semantics=("arbitrary",), vmem_limit_bytes=VMEM_LIMIT),
        name="gather",
    )(win, npass, hn, rank, aff)


FFN_SEQS = 4


def _ffn_kernel(cap, xe_ref, gate_ref, wg_ref, wu_ref, wd_ref, y_ref, wg_b, wu_b, wd_b):
    d = xe_ref.shape[-1]

    @pl.when(pl.program_id(1) == 0)
    def _():
        for src, dst in ((wg_ref, wg_b), (wu_ref, wu_b), (wd_ref, wd_b)):
            dst[...] = src[...].astype(BF16)

    xe = xe_ref[:, :cap, :].reshape(FFN_SEQS * cap, d)
    a = _dot(xe, wg_b[...])
    u = _dot(xe, wu_b[...])
    y = _dot((jax.nn.silu(a) * u).astype(BF16), wd_b[...])
    gate = gate_ref[:, :cap, :].reshape(FFN_SEQS * cap, LANES)
    y = y * jnp.concatenate([gate] * (d // LANES), axis=1)
    y_ref[:, :cap, :] = y.astype(BF16).reshape(FFN_SEQS, cap, d)
    y_ref[:, cap:, :] = jnp.zeros((FFN_SEQS, SLOT_WINDOW, d), BF16)


def _ffn(xe, gate, wg, wu, wd, layer, cap):
    batch = xe.shape[0]
    d_exp = wg.shape[-1]
    slots = pl.BlockSpec((FFN_SEQS, None, _slot_rows(cap), D_MODEL), lambda e, i: (i, e, 0, 0))
    return pl.pallas_call(
        functools.partial(_ffn_kernel, cap),
        grid=(N_EXPERTS, batch // FFN_SEQS),
        in_specs=[slots,
                  pl.BlockSpec((FFN_SEQS, None, _slot_rows(cap), LANES), lambda e, i: (i, e, 0, 0)),
                  pl.BlockSpec((None, None, D_MODEL, d_exp), lambda e, i: (layer, e, 0, 0)),
                  pl.BlockSpec((None, None, D_MODEL, d_exp), lambda e, i: (layer, e, 0, 0)),
                  pl.BlockSpec((None, None, d_exp, D_MODEL), lambda e, i: (layer, e, 0, 0))],
        out_specs=slots,
        out_shape=jax.ShapeDtypeStruct(xe.shape, BF16),
        scratch_shapes=[pltpu.VMEM((D_MODEL, d_exp), BF16), pltpu.VMEM((D_MODEL, d_exp), BF16),
                        pltpu.VMEM((d_exp, D_MODEL), BF16)],
        compiler_params=pltpu.CompilerParams(
            dimension_semantics=("arbitrary", "arbitrary"), vmem_limit_bytes=VMEM_LIMIT),
        name="ffn",
    )(xe, gate, wg, wu, wd)


TILES_PER_ROW_TILE = ROW_TILE // TOKEN_TILE


def _scatter_kernel(cap, project, win_ref, npass_ref, y_ref, rank_ref, h_ref, g_ref, *rest):
    if project:
        w_ref, out_ref, a0_ref, a1_ref, a2_ref, qw_ref, kv_ref, xs_ref = rest
    else:
        out_ref, = rest
    b = pl.program_id(0)
    j = pl.program_id(1)
    out_ref[...] = h_ref[...]
    for tt in range(TILES_PER_ROW_TILE):
        t = j * TILES_PER_ROW_TILE + tt
        rows = slice(tt * TOKEN_TILE, (tt + 1) * TOKEN_TILE)

        def one_pass(p, carry, t=t, tt=tt, rows=rows):
            starts = _window_starts(win_ref, b, t, p, cap)
            put = _one_hot(_window_hits(rank_ref[tt], starts))
            yw = jnp.concatenate([y_ref[e, pl.ds(starts[e], SLOT_WINDOW), :] for e in range(N_EXPERTS)],
                                 axis=0)
            out_ref[rows, :] += lax.dot_general(put, yw, (((0,), (0,)), ((), ())), preferred_element_type=F32)
            return carry

        lax.fori_loop(0, npass_ref[b * LANES + t], one_pass, 0)
    if project:
        _project(out_ref[...], g_ref, w_ref, a0_ref, a1_ref, a2_ref, qw_ref, kv_ref, xs_ref)
    else:
        out_ref[...] = _rms(out_ref[...], g_ref[...])


def _scatter(y, rank, h, win, npass, cap, g, w_next=None):
    batch, n_tiles = rank.shape[:2]
    seq = h.shape[1]
    steps = n_tiles // TILES_PER_ROW_TILE
    project = w_next is not None
    const = lambda a: pl.BlockSpec(a.shape, lambda b, j, *_: (0, 0), pipeline_mode=pl.Buffered(1))
    in_specs = [pl.BlockSpec((None, N_EXPERTS, _slot_rows(cap), D_MODEL), lambda b, j, *_: (b, 0, 0, 0)),
                pl.BlockSpec((None, TILES_PER_ROW_TILE, N_EXPERTS, TOKEN_TILE), lambda b, j, *_: (b, j, 0, 0)),
                pl.BlockSpec((None, ROW_TILE, D_MODEL), lambda b, j, *_: (b, j, 0)),
                const(g)]
    out_shape = [jax.ShapeDtypeStruct(h.shape, F32)]
    out_specs = [pl.BlockSpec((None, ROW_TILE, D_MODEL), lambda b, j, *_: (b, j, 0))]
    args = [win, npass, y, rank, h, g]
    scratch = []
    if project:
        in_specs.append(const(w_next))
        args.append(w_next)
        proj_shape, proj_specs = _proj_outputs(batch * seq, lambda b, j, *_: (b * steps + j, 0))
        out_shape += proj_shape
        out_specs += proj_specs
        scratch = [pltpu.VMEM((D_MODEL // LANES, ROW_TILE, LANES), F32)]
    grid_spec = pltpu.PrefetchScalarGridSpec(
        num_scalar_prefetch=2, grid=(batch, steps), in_specs=in_specs, out_specs=out_specs,
        scratch_shapes=scratch)
    return pl.pallas_call(
        functools.partial(_scatter_kernel, cap, project),
        grid_spec=grid_spec,
        out_shape=out_shape,
        compiler_params=pltpu.CompilerParams(
            dimension_semantics=("arbitrary", "arbitrary"), vmem_limit_bytes=VMEM_LIMIT),
        name="scatter_proj" if project else "scatter_final",
    )(*args)


def _slot_windows(starts, n_tiles, cap):
    s = starts.astype(jnp.int32)
    first = (s // SLOT_ALIGN) * SLOT_ALIGN
    span = s[:, :, 1:n_tiles + 1] - first[:, :, :n_tiles]
    npass = jnp.maximum(jnp.max(-(-span // SLOT_WINDOW), axis=1), 1)
    npass = jnp.pad(npass, ((0, 0), (0, LANES - n_tiles)))
    return first.reshape(-1), npass.reshape(-1)


def _arrange_w_in(w):
    scale = LOG2E * HEAD_DIM ** -0.5
    qa, ka, va = w[:, :DIL_QKV], w[:, DIL_QKV:2 * DIL_QKV], w[:, 2 * DIL_QKV:3 * DIL_QKV]
    rest = w[:, 3 * DIL_QKV:]
    parts = []
    for g in range(len(DIL_PATTERNS)):
        sl = slice(g * DIL_OUT, (g + 1) * DIL_OUT)
        parts += [qa[:, sl] * scale, ka[:, sl], va[:, sl]]
    for pr in _window_pair_heads():
        parts += [rest[:, h * HEAD_DIM:(h + 1) * HEAD_DIM] * scale for h in pr]
    parts.append(rest[:, WIN_Q:WIN_Q + 2 * WIN_KV])
    return jnp.concatenate(parts, axis=1).astype(BF16), rest[:, WIN_Q + 2 * WIN_KV:].astype(BF16)


def _arrange_w_branch_b(w):
    return jnp.concatenate([w[h * HEAD_DIM:(h + 1) * HEAD_DIM] for pr in _window_pair_heads() for h in pr],
                           axis=0).astype(BF16)


def kernel(x, norm_mix, w_in, w_branch_a, w_branch_b, b_gate, sink_logit, w_out, norm_ffn, w_router,
           w_expert_gate, w_expert_up, w_expert_down, norm_final):
    batch, seq, d = x.shape
    depth = w_in.shape[0]
    cap = CAPACITY_FACTOR * seq // N_EXPERTS
    m = batch * seq
    h = x.reshape(m, d)
    w_attn, w_gates = _arrange_w_in(w_in[0])
    proj = _norm_proj(h, norm_mix[0][None, :], w_attn)
    for l in range(depth):
        a0, a1, a2, qw, kv = proj
        oa = _dilated_attention(a0, a1, a2, batch, seq).reshape(m, DIL_OUT)
        ow = _window_attention(qw, kv, sink_logit[l][None, :], batch, seq).reshape(m, WIN_Q)
        h = _merge_out(oa, ow, h, norm_mix[l][None, :], w_gates, b_gate[l][None, :],
                       w_branch_a[l].astype(BF16), _arrange_w_branch_b(w_branch_b[l]), w_out[l].astype(BF16))
        wr = w_router[l].T
        wr_hi = wr.astype(BF16)
        wr_lo = (wr - wr_hi.astype(F32)).astype(BF16)
        hn, rank, aff, starts = _route(h, norm_ffn[l][None, :], wr_hi, wr_lo, batch, seq, cap)
        win, npass = _slot_windows(starts, seq // TOKEN_TILE, cap)
        xe, gate = _gather(hn, rank, aff, win, npass, cap)
        y = _ffn(xe, gate, w_expert_gate, w_expert_up, w_expert_down, l, cap)
        h3 = h.reshape(batch, seq, d)
        if l + 1 == depth:
            return _scatter(y, rank, h3, win, npass, cap, norm_final[None, :])[0]
        w_attn, w_gates = _arrange_w_in(w_in[l + 1])
        h3, *proj = _scatter(y, rank, h3, win, npass, cap, norm_mix[l + 1][None, :], w_attn)
        h = h3.reshape(m, d)
```

```python
import functools

import jax
import jax.numpy as jnp
from jax import lax
from jax.experimental import pallas as pl
from jax.experimental.pallas import tpu as pltpu

D_MODEL = 1024
HEAD_DIM = 64
DIL_PATTERNS = ((128, 1), (512, 4), (2048, 16))
DIL_HEADS = 4
N_DIL_SUB = DIL_HEADS * len(DIL_PATTERNS)
DIL_QKV = N_DIL_SUB * HEAD_DIM
DIL_OUT = DIL_HEADS * HEAD_DIM
DIL_HALF = 64
WIN_HALF = 128
WIN_Q = D_MODEL
WIN_Q_HEADS = WIN_Q // HEAD_DIM
WIN_KV_HEADS = 4
WIN_KV = WIN_KV_HEADS * HEAD_DIM
N_EXPERTS = 16
CAPACITY_FACTOR = 2
RMS_EPS = 1e-6
NEG_INF = -1e30
LOG2E = 1.4426950408889634

LANES = 128
GROUP_COLS = 3 * DIL_OUT
DIL_PAIRS = DIL_OUT // LANES
WIN_UNITS = WIN_KV_HEADS // 2
WIN_UNIT_PAIRS = WIN_Q_HEADS // (2 * WIN_UNITS)
Q_TILE = 128
WIN_TILES_PER_BODY = 4
DIL_INFLIGHT = 16
ROW_TILE = 512
MERGE_TILE = 1024
VMEM_LIMIT = 56 * 1024 * 1024

F32 = jnp.float32
BF16 = jnp.bfloat16


def _alibi_slopes(n):
    return [float(2.0 ** (-8.0 * i / n)) for i in range(1, n + 1)]


def _rms(x, g):
    return x * lax.rsqrt(jnp.mean(x * x, axis=-1, keepdims=True) + RMS_EPS) * g


def _dot(a, b):
    return jnp.dot(a, b, preferred_element_type=F32)


def _dot_nt(a, b):
    return lax.dot_general(a, b, (((1,), (1,)), ((), ())), preferred_element_type=F32)


def _left_lanes():
    return lax.broadcasted_iota(jnp.int32, (1, LANES), 1) < HEAD_DIM


def _layer_block(a, layer):
    return pl.BlockSpec((None,) + a.shape[1:], lambda *_: (layer, 0, 0), pipeline_mode=pl.Buffered(1))


_C_A = (0, GROUP_COLS, 2 * GROUP_COLS)
_C_QW = 3 * GROUP_COLS
_C_KV = _C_QW + WIN_Q
_COL_CHUNK = 512


def _proj_store(ref, xb, w_ref, c0, cw):
    for j in range(0, cw, _COL_CHUNK):
        jw = min(_COL_CHUNK, cw - j)
        ref[:, j:j + jw] = _dot(xb, w_ref[:, c0 + j:c0 + j + jw]).astype(BF16)


def _norm_proj_kernel(x_ref, g_ref, w_ref, a0_ref, a1_ref, a2_ref, qw_ref, kv_ref, xs_ref):
    _project(x_ref[...], g_ref, w_ref, a0_ref, a1_ref, a2_ref, qw_ref, kv_ref, xs_ref)


def _project(x, g_ref, w_ref, a0_ref, a1_ref, a2_ref, qw_ref, kv_ref, xs_ref):
    xn = _rms(x, g_ref[...])
    n_lane_tiles = D_MODEL // LANES
    for j in range(n_lane_tiles):
        xs_ref[j] = xn[:, j * LANES:(j + 1) * LANES]
    xb = xn.astype(BF16)
    _proj_store(a0_ref, xb, w_ref, _C_A[0], GROUP_COLS)
    for grp, ref in ((1, a1_ref), (2, a2_ref)):
        r = DIL_PATTERNS[grp][1]
        n = ROW_TILE // r
        xp = jnp.concatenate(
            [jnp.concatenate([xs_ref[j, pl.ds(c, n, stride=r), :] for j in range(n_lane_tiles)], axis=1)
             for c in range(r)], axis=0).astype(BF16)
        res = _dot(xp, w_ref[:, _C_A[grp]:_C_A[grp] + GROUP_COLS])
        for c in range(r):
            ref[:, c * GROUP_COLS:(c + 1) * GROUP_COLS] = res[c * n:(c + 1) * n].astype(BF16)
    _proj_store(qw_ref, xb, w_ref, _C_QW, WIN_Q)
    _proj_store(kv_ref, xb, w_ref, _C_KV, 2 * WIN_KV)


def _proj_outputs(m, row_index):
    out_shape, out_specs = [], []
    for _, r in DIL_PATTERNS:
        out_shape.append(jax.ShapeDtypeStruct((m // r, r * GROUP_COLS), BF16))
        out_specs.append(pl.BlockSpec((ROW_TILE // r, r * GROUP_COLS), row_index))
    for c in (WIN_Q, 2 * WIN_KV):
        out_shape.append(jax.ShapeDtypeStruct((m, c), BF16))
        out_specs.append(pl.BlockSpec((ROW_TILE, c), row_index))
    return out_shape, out_specs


def _norm_proj(x, g, w, layer):
    m = x.shape[0]
    in_specs = [pl.BlockSpec((ROW_TILE, D_MODEL), lambda i: (i, 0)), _layer_block(g, layer), _layer_block(w, layer)]
    out_shape, out_specs = _proj_outputs(m, lambda i: (i, 0))
    return pl.pallas_call(
        _norm_proj_kernel,
        grid=(m // ROW_TILE,),
        in_specs=in_specs,
        out_specs=out_specs,
        out_shape=out_shape,
        scratch_shapes=[pltpu.VMEM((D_MODEL // LANES, ROW_TILE, LANES), F32)],
        compiler_params=pltpu.CompilerParams(
            dimension_semantics=("arbitrary",), vmem_limit_bytes=VMEM_LIMIT),
        name="norm_proj",
    )(x, g, w)


def _band_bias(tq, win, off, half, slopes):
    col = lax.broadcasted_iota(jnp.int32, (tq, win), 1)
    row = lax.broadcasted_iota(jnp.int32, (tq, win), 0)
    dist = jnp.abs(col - row + off).astype(F32)
    inside = dist <= float(half)
    return jnp.concatenate([jnp.where(inside, dist * (-s * LOG2E), NEG_INF) for s in slopes], axis=0)


def _ones_blockdiag(win):
    left = lax.broadcasted_iota(jnp.int32, (2 * win, LANES), 1) < HEAD_DIM
    top = jnp.where(lax.broadcasted_iota(jnp.int32, (2 * win, LANES), 0) < win, 1.0, 0.0)
    return jnp.where(left, top, 1.0 - top).astype(BF16)


def _pair_scores(q_tiles, k_win, bias):
    left = _left_lanes()
    zero = jnp.zeros((), BF16)
    rows = []
    for q in q_tiles:
        rows += [jnp.where(left, q, zero), jnp.where(left, zero, q)]
    return _dot_nt(jnp.concatenate(rows, axis=0), k_win) + bias


def _pair_values(s_ref, v_win, ones_bd, sink_ref, p_ref, aux_ref):
    win = v_win.shape[0]
    n = s_ref.shape[0] // (2 * Q_TILE)
    left = _left_lanes()
    zero = jnp.zeros((), BF16)
    for i in range(n):
        ms = []
        for side in range(2):
            rows = pl.ds((2 * i + side) * Q_TILE, Q_TILE)
            s = s_ref[rows, :]
            m = jnp.broadcast_to(jnp.max(s, axis=-1, keepdims=True), (Q_TILE, LANES))
            if sink_ref is not None:
                m = jnp.maximum(m, sink_ref[rows, :])
            p = jnp.exp2(s - jnp.concatenate([m] * (win // LANES), axis=1)).astype(BF16)
            p_ref[i * Q_TILE:(i + 1) * Q_TILE, side * win:(side + 1) * win] = p
            ms.append(m)
        m_pair = jnp.where(left, ms[0], ms[1])
        if sink_ref is not None:
            sink_pair = jnp.where(left, sink_ref[pl.ds(2 * i * Q_TILE, Q_TILE), :],
                                  sink_ref[pl.ds((2 * i + 1) * Q_TILE, Q_TILE), :])
            aux_ref[i * Q_TILE:(i + 1) * Q_TILE, :] = jnp.exp2(sink_pair - m_pair)
        else:
            aux_ref[i * Q_TILE:(i + 1) * Q_TILE, :] = m_pair
    vbd = jnp.concatenate([jnp.where(left, v_win, zero), jnp.where(left, zero, v_win)], axis=0)
    on = _dot(p_ref[...], jnp.concatenate([vbd, ones_bd], axis=1))
    return on[:, :LANES], on[:, LANES:], aux_ref[...]


def _tile_variant(t, n_tiles):
    return jnp.where(t == 0, 0, jnp.where(t == n_tiles - 1, 2, 1))


def _pipelined_tiles(n_tiles, per_body, score_stage, value_stage):
    score_stage(0, 0)

    def body(i, carry):
        t = per_body * i
        for j in range(per_body):
            score_stage(jnp.minimum(t + j + 1, n_tiles - 1), (j + 1) % per_body)
            value_stage(t + j, j)
        return carry

    lax.fori_loop(0, n_tiles // per_body, body, 0)


def _dil_geometry(seq_full, grp):
    r = DIL_PATTERNS[grp][1]
    seq = seq_full // r
    win = min(seq, Q_TILE + 2 * DIL_HALF)
    n_tiles = seq // Q_TILE
    pad = (win - Q_TILE) // 2
    offs = [0] if n_tiles == 1 else [0, -pad, -2 * pad]
    return r, seq, win, n_tiles, pad, offs


def _dilated_kernel(seq_full, a0_ref, a1_ref, a2_ref, out_ref, o_nat, l_nat, b0, b1, b2, ones_ref, ones2_ref,
                    s_wide, s_narrow, p_wide, p_narrow, aux_ref):
    a_refs = (a0_ref, a1_ref, a2_ref)
    bias_refs = (b0, b1, b2)
    s_refs = (s_wide, s_wide, s_narrow)
    p_refs = (p_wide, p_wide, p_narrow)
    slopes = _alibi_slopes(N_DIL_SUB)

    @pl.when(pl.program_id(0) == 0)
    def _():
        for grp in range(3):
            r, seq, win, n_tiles, pad, offs = _dil_geometry(seq_full, grp)
            for v, off in enumerate(offs):
                for pr in range(DIL_PAIRS):
                    hs = slopes[grp * DIL_HEADS + 2 * pr:grp * DIL_HEADS + 2 * pr + 2]
                    bias_refs[grp][v, pr] = _band_bias(Q_TILE, win, off, DIL_HALF, [s * r for s in hs])
        ones_ref[...] = _ones_blockdiag(_dil_geometry(seq_full, 0)[2])
        ones2_ref[...] = _ones_blockdiag(_dil_geometry(seq_full, 2)[2])

    def pair_tile(grp, c, t, pr, slot):
        r, seq, win, n_tiles, pad, offs = _dil_geometry(seq_full, grp)
        a_ref = a_refs[grp]
        ones_bd_ref = ones2_ref if grp == 2 else ones_ref
        if n_tiles == 1:
            q0, ks, var = 0, 0, 0
        else:
            q0 = pl.multiple_of(t * Q_TILE, Q_TILE)
            ks = pl.multiple_of(jnp.clip(q0 - pad, 0, seq - win), DIL_HALF)
            var = _tile_variant(t, n_tiles)
        lo = c * GROUP_COLS + pr * LANES
        q = a_ref[pl.ds(q0, Q_TILE), lo:lo + LANES]
        k = a_ref[pl.ds(ks, win), DIL_OUT + lo:DIL_OUT + lo + LANES]
        v = a_ref[pl.ds(ks, win), 2 * DIL_OUT + lo:2 * DIL_OUT + lo + LANES]
        s_ref = s_refs[grp].at[slot]
        s_ref[...] = _pair_scores([q], k, bias_refs[grp][var, pr])
        num, den, m = _pair_values(s_ref, v, ones_bd_ref[...], None, p_refs[grp].at[slot], aux_ref.at[slot])
        rows = pl.ds(q0, Q_TILE) if r == 1 else pl.ds(c + r * q0, Q_TILE, stride=r)
        o_nat[grp, pr, rows, :] = num / den
        l_nat[grp, pr, rows, :] = m + jnp.log2(den)

    for grp in range(3):
        r, seq, win, n_tiles, pad, offs = _dil_geometry(seq_full, grp)
        per_body = DIL_INFLIGHT // DIL_PAIRS
        if n_tiles == 1:
            for c in range(r):
                for pr in range(DIL_PAIRS):
                    pair_tile(grp, c, 0, pr, c * DIL_PAIRS + pr)
        elif r >= per_body:
            for c0 in range(0, r, per_body):
                def body(t, carry, grp=grp, c0=c0):
                    for j in range(per_body):
                        for pr in range(DIL_PAIRS):
                            pair_tile(grp, c0 + j, t, pr, j * DIL_PAIRS + pr)
                    return carry
                lax.fori_loop(0, n_tiles, body, 0)
        else:
            tiles_per_body = per_body // r
            def body(i, carry, grp=grp, r=r, tiles_per_body=tiles_per_body):
                for j in range(tiles_per_body):
                    for c in range(r):
                        for pr in range(DIL_PAIRS):
                            pair_tile(grp, c, i * tiles_per_body + j, pr, (j * r + c) * DIL_PAIRS + pr)
                return carry
            lax.fori_loop(0, n_tiles // tiles_per_body, body, 0)

    def combine(i, carry):
        rows = pl.ds(pl.multiple_of(i * ROW_TILE, ROW_TILE), ROW_TILE)
        for pr in range(DIL_PAIRS):
            ls = [l_nat[g, pr, rows, :] for g in range(3)]
            mx = jnp.maximum(jnp.maximum(ls[0], ls[1]), ls[2])
            es = [jnp.exp2(l - mx) for l in ls]
            num = es[0] * o_nat[0, pr, rows, :] + es[1] * o_nat[1, pr, rows, :] + es[2] * o_nat[2, pr, rows, :]
            out_ref[rows, pr * LANES:(pr + 1) * LANES] = (num / (es[0] + es[1] + es[2])).astype(BF16)
        return carry

    lax.fori_loop(0, seq_full // ROW_TILE, combine, 0)


def _dilated_attention(a0, a1, a2, batch, seq_full):
    views, in_specs, bias_shapes = [], [], []
    for grp, a in enumerate((a0, a1, a2)):
        r, seq, win, n_tiles, pad, offs = _dil_geometry(seq_full, grp)
        views.append(a.reshape(batch, seq, r * GROUP_COLS))
        in_specs.append(pl.BlockSpec((None, seq, r * GROUP_COLS), lambda b: (b, 0, 0)))
        bias_shapes.append(pltpu.VMEM((len(offs), DIL_PAIRS, 2 * Q_TILE, win), F32))
    win0 = _dil_geometry(seq_full, 0)[2]
    win2 = _dil_geometry(seq_full, 2)[2]
    n_straight = DIL_PATTERNS[2][1] * DIL_PAIRS
    return pl.pallas_call(
        functools.partial(_dilated_kernel, seq_full),
        grid=(batch,),
        in_specs=in_specs,
        out_specs=pl.BlockSpec((None, seq_full, DIL_OUT), lambda b: (b, 0, 0)),
        out_shape=jax.ShapeDtypeStruct((batch, seq_full, DIL_OUT), BF16),
        scratch_shapes=[pltpu.VMEM((3, DIL_PAIRS, seq_full, LANES), F32),
                        pltpu.VMEM((3, DIL_PAIRS, seq_full, LANES), F32)]
        + bias_shapes + [pltpu.VMEM((2 * win0, LANES), BF16), pltpu.VMEM((2 * win2, LANES), BF16)]
        + [pltpu.VMEM((n, 2 * Q_TILE, w), F32) for n, w in ((DIL_INFLIGHT, win0), (n_straight, win2))]
        + [pltpu.VMEM((n, Q_TILE, 2 * w), BF16) for n, w in ((DIL_INFLIGHT, win0), (n_straight, win2))]
        + [pltpu.VMEM((max(DIL_INFLIGHT, n_straight), Q_TILE, LANES), F32)],
        compiler_params=pltpu.CompilerParams(
            dimension_semantics=("arbitrary",), vmem_limit_bytes=VMEM_LIMIT),
        name="dilated",
    )(*views)


WIN_WINDOW = Q_TILE + 2 * WIN_HALF


def _window_pair_heads():
    group = WIN_Q_HEADS // WIN_KV_HEADS
    pairs = []
    for u in range(WIN_UNITS):
        for i in range(WIN_UNIT_PAIRS):
            pairs.append((2 * u * group + i, (2 * u + 1) * group + i))
    return pairs


def _window_kernel(seq, q_ref, kv_ref, sink_ref, o_ref, bias_ref, ones_ref, sink_rows_ref, s_ref, p_ref,
                   aux_ref):
    n_tiles = seq // Q_TILE
    slopes = _alibi_slopes(WIN_Q_HEADS)
    pair_heads = _window_pair_heads()

    @pl.when(pl.program_id(0) == 0)
    def _():
        for u in range(WIN_UNITS):
            heads = [h for pr in pair_heads[u * WIN_UNIT_PAIRS:(u + 1) * WIN_UNIT_PAIRS] for h in pr]
            for v, off in enumerate((0, -WIN_HALF, -2 * WIN_HALF)):
                bias_ref[v, u] = _band_bias(Q_TILE, WIN_WINDOW, off, WIN_HALF, [slopes[h] for h in heads])
            sink_rows_ref[u] = jnp.concatenate(
                [jnp.broadcast_to(sink_ref[0:1, h:h + 1] * LOG2E, (Q_TILE, LANES)) for h in heads], axis=0)
        ones_ref[...] = _ones_blockdiag(WIN_WINDOW)

    def rows_of(t):
        q0 = pl.multiple_of(t * Q_TILE, Q_TILE)
        ks = pl.multiple_of(jnp.clip(q0 - WIN_HALF, 0, seq - WIN_WINDOW), Q_TILE)
        return q0, ks

    def unit_cols(u):
        return [(u * WIN_UNIT_PAIRS + i) * LANES for i in range(WIN_UNIT_PAIRS)]

    def score_stage(t, k_set):
        q0, ks = rows_of(t)
        var = _tile_variant(t, n_tiles)
        for u in range(WIN_UNITS):
            qs = [q_ref[pl.ds(q0, Q_TILE), c:c + LANES] for c in unit_cols(u)]
            k = kv_ref[pl.ds(ks, WIN_WINDOW), u * LANES:(u + 1) * LANES]
            s_ref[k_set * WIN_UNITS + u] = _pair_scores(qs, k, bias_ref[var, u])

    def value_stage(t, k_set):
        q0, ks = rows_of(t)
        for u in range(WIN_UNITS):
            slot = k_set * WIN_UNITS + u
            v = kv_ref[pl.ds(ks, WIN_WINDOW), WIN_KV + u * LANES:WIN_KV + (u + 1) * LANES]
            num, den, sink_term = _pair_values(s_ref.at[slot], v, ones_ref[...], sink_rows_ref.at[u],
                                               p_ref.at[slot], aux_ref.at[slot])
            o = (num / (den + sink_term)).astype(BF16)
            for i, c in enumerate(unit_cols(u)):
                o_ref[pl.ds(q0, Q_TILE), c:c + LANES] = o[i * Q_TILE:(i + 1) * Q_TILE]

    _pipelined_tiles(n_tiles, WIN_TILES_PER_BODY, score_stage, value_stage)


def _window_attention(q, kv, sink, batch, seq):
    unit_rows = 2 * WIN_UNIT_PAIRS * Q_TILE
    return pl.pallas_call(
        functools.partial(_window_kernel, seq),
        grid=(batch,),
        in_specs=[pl.BlockSpec((None, seq, WIN_Q), lambda b: (b, 0, 0)),
                  pl.BlockSpec((None, seq, 2 * WIN_KV), lambda b: (b, 0, 0)),
                  pl.BlockSpec((1, WIN_Q_HEADS), lambda b: (0, 0))],
        out_specs=pl.BlockSpec((None, seq, WIN_Q), lambda b: (b, 0, 0)),
        out_shape=jax.ShapeDtypeStruct((batch, seq, WIN_Q), BF16),
        scratch_shapes=[pltpu.VMEM((3, WIN_UNITS, unit_rows, WIN_WINDOW), F32),
                        pltpu.VMEM((2 * WIN_WINDOW, LANES), BF16),
                        pltpu.VMEM((WIN_UNITS, unit_rows, LANES), F32),
                        pltpu.VMEM((WIN_TILES_PER_BODY * WIN_UNITS, unit_rows, WIN_WINDOW), F32),
                        pltpu.VMEM((WIN_TILES_PER_BODY * WIN_UNITS, unit_rows // 2, 2 * WIN_WINDOW), BF16),
                        pltpu.VMEM((WIN_TILES_PER_BODY * WIN_UNITS, unit_rows // 2, LANES), F32)],
        compiler_params=pltpu.CompilerParams(
            dimension_semantics=("arbitrary",), vmem_limit_bytes=VMEM_LIMIT),
        name="window",
    )(q.reshape(batch, seq, WIN_Q), kv.reshape(batch, seq, 2 * WIN_KV), sink)


def _merge_kernel(oa_ref, ow_ref, h_ref, g_ref, wgate_ref, bgate_ref, wa_ref, wb_ref, wo_ref, out_ref):
    h = h_ref[...]
    xb = _rms(h, g_ref[...]).astype(BF16)
    merged = None
    for br, (o_ref, w_ref) in enumerate(((oa_ref, wa_ref), (ow_ref, wb_ref))):
        cols = slice(br * D_MODEL, (br + 1) * D_MODEL)
        gate = jax.nn.sigmoid(_dot(xb, wgate_ref[:, cols]) + bgate_ref[:, cols])
        term = gate * _dot(o_ref[...], w_ref[...])
        merged = term if merged is None else merged + term
    out_ref[...] = h + _dot(merged.astype(BF16), wo_ref[...])


def _merge_out(oa, ow, h, g, wgate, bgate, wa, wb, wo, layer):
    m = h.shape[0]
    row = lambda c: pl.BlockSpec((MERGE_TILE, c), lambda i: (i, 0))
    return pl.pallas_call(
        _merge_kernel,
        grid=(m // MERGE_TILE,),
        in_specs=[row(DIL_OUT), row(WIN_Q), row(D_MODEL)]
        + [_layer_block(a, layer) for a in (g, wgate, bgate, wa, wb, wo)],
        out_specs=row(D_MODEL),
        out_shape=jax.ShapeDtypeStruct((m, D_MODEL), F32),
        compiler_params=pltpu.CompilerParams(
            dimension_semantics=("arbitrary",), vmem_limit_bytes=VMEM_LIMIT),
        name="merge_out",
    )(oa, ow, h, g, wgate, bgate, wa, wb, wo)


TOKEN_TILE = 256
CUM_CHUNK = TOKEN_TILE
SLOT_WINDOW = 64
SLOT_ALIGN = 16


def _prefix_exclusive(mask_f, tri):
    e, s = mask_f.shape
    carry = jnp.zeros((e, 1), F32)
    parts, carries = [], []
    for j in range(0, s, CUM_CHUNK):
        blk = mask_f[:, j:j + CUM_CHUNK]
        carries.append(carry)
        parts.append(_dot(blk.astype(BF16), tri) + carry)
        carry = carry + jnp.sum(blk, axis=-1, keepdims=True)
    return jnp.concatenate(parts, axis=-1), carries + [carry]


def _route_kernel(cap, h_ref, g_ref, whi_ref, wlo_ref, hn_ref, rank_ref, aff_ref, starts_ref):
    seq = h_ref.shape[0]
    hn = _rms(h_ref[...], g_ref[...])
    hn_hi = hn.astype(BF16)
    hn_lo = (hn - hn_hi.astype(F32)).astype(BF16)
    hn_ref[...] = hn_hi
    both = _dot_nt(jnp.concatenate([whi_ref[...], wlo_ref[...]], axis=0), hn_hi)
    logits = both[:N_EXPERTS] + both[N_EXPERTS:] + _dot_nt(whi_ref[...], hn_lo)
    mx = jnp.max(logits, axis=0, keepdims=True)
    ex = jnp.exp(logits - mx)
    aff = ex / jnp.sum(ex, axis=0, keepdims=True)
    bits = pltpu.bitcast(aff, jnp.int32)

    def enough(t):
        return jnp.sum(jnp.where(bits >= t, 1.0, 0.0), axis=-1, keepdims=True) >= float(cap)

    def search4(_, c):
        lo, hi = c
        q = (hi - lo) >> 2
        m1, m2, m3 = lo + q, lo + 2 * q, lo + 3 * q
        ok1, ok2, ok3 = enough(m1), enough(m2), enough(m3)
        return (jnp.where(ok3, m3, jnp.where(ok2, m2, jnp.where(ok1, m1, lo))),
                jnp.where(ok3, hi, jnp.where(ok2, m3, jnp.where(ok1, m2, m1))))

    def search2(_, c):
        lo, hi = c
        mid = lo + ((hi - lo) >> 1)
        ok = enough(mid)
        return jnp.where(ok, mid, lo), jnp.where(ok, hi, mid)

    lo0 = jnp.zeros((N_EXPERTS, 1), jnp.int32)
    hi0 = jnp.full((N_EXPERTS, 1), 0x3F800001, jnp.int32)
    thr, _ = lax.fori_loop(0, 3, search2, lax.fori_loop(0, 15, search4, (lo0, hi0)))

    r_i = lax.broadcasted_iota(jnp.int32, (CUM_CHUNK, CUM_CHUNK), 0)
    c_i = lax.broadcasted_iota(jnp.int32, (CUM_CHUNK, CUM_CHUNK), 1)
    tri = jnp.where(r_i < c_i, 1.0, 0.0).astype(BF16)
    gt = jnp.where(bits > thr, 1.0, 0.0)
    eq = jnp.where(bits == thr, 1.0, 0.0)
    need = float(cap) - jnp.sum(gt, axis=-1, keepdims=True)
    tie_rank, _ = _prefix_exclusive(eq, tri)
    sel = gt + eq * jnp.where(tie_rank < need, 1.0, 0.0)
    slot, starts = _prefix_exclusive(sel, tri)
    rank = jnp.where(sel > 0.0, slot, -1.0)
    for t in range(seq // TOKEN_TILE):
        rank_ref[t] = rank[:, t * TOKEN_TILE:(t + 1) * TOKEN_TILE]
        aff_ref[t] = aff[:, t * TOKEN_TILE:(t + 1) * TOKEN_TILE]
    lane = lax.broadcasted_iota(jnp.int32, (N_EXPERTS, LANES), 1)
    acc = jnp.zeros((N_EXPERTS, LANES), F32)
    for t, c in enumerate(starts):
        acc = acc + jnp.where(lane == t, c, 0.0)
    starts_ref[...] = acc


def _route(h, g, w_hi, w_lo, layer, batch, seq, cap):
    n_tiles = seq // TOKEN_TILE
    return pl.pallas_call(
        functools.partial(_route_kernel, cap),
        grid=(batch,),
        in_specs=[pl.BlockSpec((None, seq, D_MODEL), lambda b: (b, 0, 0))]
        + [_layer_block(a, layer) for a in (g, w_hi, w_lo)],
        out_specs=[pl.BlockSpec((None, seq, D_MODEL), lambda b: (b, 0, 0)),
                   pl.BlockSpec((None, n_tiles, N_EXPERTS, TOKEN_TILE), lambda b: (b, 0, 0, 0)),
                   pl.BlockSpec((None, n_tiles, N_EXPERTS, TOKEN_TILE), lambda b: (b, 0, 0, 0)),
                   pl.BlockSpec((None, N_EXPERTS, LANES), lambda b: (b, 0, 0))],
        out_shape=[jax.ShapeDtypeStruct((batch, seq, D_MODEL), BF16),
                   jax.ShapeDtypeStruct((batch, n_tiles, N_EXPERTS, TOKEN_TILE), F32),
                   jax.ShapeDtypeStruct((batch, n_tiles, N_EXPERTS, TOKEN_TILE), F32),
                   jax.ShapeDtypeStruct((batch, N_EXPERTS, LANES), F32)],
        compiler_params=pltpu.CompilerParams(
            dimension_semantics=("arbitrary",), vmem_limit_bytes=VMEM_LIMIT),
        name="route",
    )(h.reshape(batch, seq, D_MODEL), g, w_hi, w_lo)


def _slot_rows(cap):
    return cap + SLOT_WINDOW


def _window_starts(win_ref, b, t, p, cap):
    base = b * (N_EXPERTS * LANES) + t
    return [pl.multiple_of(jnp.minimum(win_ref[base + e * LANES] + p * SLOT_WINDOW, cap), SLOT_ALIGN)
            for e in range(N_EXPERTS)]


def _window_hits(rank_tile, starts):
    rows = lax.broadcasted_iota(jnp.int32, (SLOT_WINDOW, TOKEN_TILE), 0).astype(F32)
    return [(rank_tile[e:e + 1, :] - starts[e].astype(F32)) == rows for e in range(N_EXPERTS)]


def _one_hot(hits):
    return jnp.concatenate([jnp.where(h, 1.0, 0.0).astype(BF16) for h in hits], axis=0)


def _gather_kernel(cap, win_ref, npass_ref, hn_ref, rank_ref, aff_ref, xe_ref, gate_ref):
    b = pl.program_id(0)
    xe_ref[...] = jnp.zeros_like(xe_ref)
    gate_ref[...] = jnp.zeros_like(gate_ref)
    for t in range(hn_ref.shape[0] // TOKEN_TILE):

        def one_pass(p, carry, t=t):
            starts = _window_starts(win_ref, b, t, p, cap)
            hits = _window_hits(rank_ref[t], starts)
            rows = _dot(_one_hot(hits), hn_ref[t * TOKEN_TILE:(t + 1) * TOKEN_TILE, :]).astype(BF16)
            aff_tile = aff_ref[t]
            for e in range(N_EXPERTS):
                win = pl.ds(starts[e], SLOT_WINDOW)
                xe_ref[e, win, :] += rows[e * SLOT_WINDOW:(e + 1) * SLOT_WINDOW]
                g = jnp.sum(jnp.where(hits[e], aff_tile[e:e + 1, :], 0.0), axis=-1, keepdims=True)
                gate_ref[e, win, :] += jnp.broadcast_to(g, (SLOT_WINDOW, LANES))
            return carry

        lax.fori_loop(0, npass_ref[b * LANES + t], one_pass, 0)


def _gather(hn, rank, aff, win, npass, cap):
    batch, seq, _ = hn.shape
    n_tiles = seq // TOKEN_TILE
    grid_spec = pltpu.PrefetchScalarGridSpec(
        num_scalar_prefetch=2,
        grid=(batch,),
        in_specs=[pl.BlockSpec((None, seq, D_MODEL), lambda b, *_: (b, 0, 0)),
                  pl.BlockSpec((None, n_tiles, N_EXPERTS, TOKEN_TILE), lambda b, *_: (b, 0, 0, 0)),
                  pl.BlockSpec((None, n_tiles, N_EXPERTS, TOKEN_TILE), lambda b, *_: (b, 0, 0, 0))],
        out_specs=[pl.BlockSpec((None, N_EXPERTS, _slot_rows(cap), D_MODEL), lambda b, *_: (b, 0, 0, 0)),
                   pl.BlockSpec((None, N_EXPERTS, _slot_rows(cap), LANES), lambda b, *_: (b, 0, 0, 0))])
    return pl.pallas_call(
        functools.partial(_gather_kernel, cap),
        grid_spec=grid_spec,
        out_shape=[jax.ShapeDtypeStruct((batch, N_EXPERTS, _slot_rows(cap), D_MODEL), BF16),
                   jax.ShapeDtypeStruct((batch, N_EXPERTS, _slot_rows(cap), LANES), F32)],
        compiler_params=pltpu.CompilerParams(
            dimension_semantics=("arbitrary",), vmem_limit_bytes=VMEM_LIMIT),
        name="gather",
    )(win, npass, hn, rank, aff)


FFN_SEQS = 4


def _ffn_kernel(cap, xe_ref, gate_ref, wg_ref, wu_ref, wd_ref, y_ref, wg_b, wu_b, wd_b):
    d = xe_ref.shape[-1]

    @pl.when(pl.program_id(1) == 0)
    def _():
        for src, dst in ((wg_ref, wg_b), (wu_ref, wu_b), (wd_ref, wd_b)):
            dst[...] = src[...].astype(BF16)

    xe = xe_ref[:, :cap, :].reshape(FFN_SEQS * cap, d)
    a = _dot(xe, wg_b[...])
    u = _dot(xe, wu_b[...])
    y = _dot((jax.nn.silu(a) * u).astype(BF16), wd_b[...])
    gate = gate_ref[:, :cap, :].reshape(FFN_SEQS * cap, LANES)
    y = y * jnp.concatenate([gate] * (d // LANES), axis=1)
    y_ref[:, :cap, :] = y.astype(BF16).reshape(FFN_SEQS, cap, d)
    y_ref[:, cap:, :] = jnp.zeros((FFN_SEQS, SLOT_WINDOW, d), BF16)


def _ffn(xe, gate, wg, wu, wd, layer, cap):
    batch = xe.shape[0]
    d_exp = wg.shape[-1]
    slots = pl.BlockSpec((FFN_SEQS, None, _slot_rows(cap), D_MODEL), lambda e, i: (i, e, 0, 0))
    return pl.pallas_call(
        functools.partial(_ffn_kernel, cap),
        grid=(N_EXPERTS, batch // FFN_SEQS),
        in_specs=[slots,
                  pl.BlockSpec((FFN_SEQS, None, _slot_rows(cap), LANES), lambda e, i: (i, e, 0, 0)),
                  pl.BlockSpec((None, None, D_MODEL, d_exp), lambda e, i: (layer, e, 0, 0)),
                  pl.BlockSpec((None, None, D_MODEL, d_exp), lambda e, i: (layer, e, 0, 0)),
                  pl.BlockSpec((None, None, d_exp, D_MODEL), lambda e, i: (layer, e, 0, 0))],
        out_specs=slots,
        out_shape=jax.ShapeDtypeStruct(xe.shape, BF16),
        scratch_shapes=[pltpu.VMEM((D_MODEL, d_exp), BF16), pltpu.VMEM((D_MODEL, d_exp), BF16),
                        pltpu.VMEM((d_exp, D_MODEL), BF16)],
        compiler_params=pltpu.CompilerParams(
            dimension_semantics=("arbitrary", "arbitrary"), vmem_limit_bytes=VMEM_LIMIT),
        name="ffn",
    )(xe, gate, wg, wu, wd)


TILES_PER_ROW_TILE = ROW_TILE // TOKEN_TILE


def _scatter_kernel(cap, project, win_ref, npass_ref, y_ref, rank_ref, h_ref, g_ref, *rest):
    if project:
        w_ref, out_ref, a0_ref, a1_ref, a2_ref, qw_ref, kv_ref, xs_ref = rest
    else:
        out_ref, = rest
    b = pl.program_id(0)
    j = pl.program_id(1)
    out_ref[...] = h_ref[...]
    for tt in range(TILES_PER_ROW_TILE):
        t = j * TILES_PER_ROW_TILE + tt
        rows = slice(tt * TOKEN_TILE, (tt + 1) * TOKEN_TILE)

        def one_pass(p, carry, t=t, tt=tt, rows=rows):
            starts = _window_starts(win_ref, b, t, p, cap)
            put = _one_hot(_window_hits(rank_ref[tt], starts))
            yw = jnp.concatenate([y_ref[e, pl.ds(starts[e], SLOT_WINDOW), :] for e in range(N_EXPERTS)],
                                 axis=0)
            out_ref[rows, :] += lax.dot_general(put, yw, (((0,), (0,)), ((), ())), preferred_element_type=F32)
            return carry

        lax.fori_loop(0, npass_ref[b * LANES + t], one_pass, 0)
    if project:
        _project(out_ref[...], g_ref, w_ref, a0_ref, a1_ref, a2_ref, qw_ref, kv_ref, xs_ref)
    else:
        out_ref[...] = _rms(out_ref[...], g_ref[...])


def _scatter(y, rank, h, win, npass, cap, g, layer, w_next=None):
    batch, n_tiles = rank.shape[:2]
    seq = h.shape[1]
    steps = n_tiles // TILES_PER_ROW_TILE
    project = w_next is not None
    in_specs = [pl.BlockSpec((None, N_EXPERTS, _slot_rows(cap), D_MODEL), lambda b, j, *_: (b, 0, 0, 0)),
                pl.BlockSpec((None, TILES_PER_ROW_TILE, N_EXPERTS, TOKEN_TILE), lambda b, j, *_: (b, j, 0, 0)),
                pl.BlockSpec((None, ROW_TILE, D_MODEL), lambda b, j, *_: (b, j, 0)),
                _layer_block(g, layer)]
    out_shape = [jax.ShapeDtypeStruct(h.shape, F32)]
    out_specs = [pl.BlockSpec((None, ROW_TILE, D_MODEL), lambda b, j, *_: (b, j, 0))]
    args = [win, npass, y, rank, h, g]
    scratch = []
    if project:
        in_specs.append(_layer_block(w_next, layer))
        args.append(w_next)
        proj_shape, proj_specs = _proj_outputs(batch * seq, lambda b, j, *_: (b * steps + j, 0))
        out_shape += proj_shape
        out_specs += proj_specs
        scratch = [pltpu.VMEM((D_MODEL // LANES, ROW_TILE, LANES), F32)]
    grid_spec = pltpu.PrefetchScalarGridSpec(
        num_scalar_prefetch=2, grid=(batch, steps), in_specs=in_specs, out_specs=out_specs,
        scratch_shapes=scratch)
    return pl.pallas_call(
        functools.partial(_scatter_kernel, cap, project),
        grid_spec=grid_spec,
        out_shape=out_shape,
        compiler_params=pltpu.CompilerParams(
            dimension_semantics=("arbitrary", "arbitrary"), vmem_limit_bytes=VMEM_LIMIT),
        name="scatter_proj" if project else "scatter_final",
    )(*args)


def _slot_windows(starts, n_tiles, cap):
    s = starts.astype(jnp.int32)
    first = (s // SLOT_ALIGN) * SLOT_ALIGN
    span = s[:, :, 1:n_tiles + 1] - first[:, :, :n_tiles]
    npass = jnp.maximum(jnp.max(-(-span // SLOT_WINDOW), axis=1), 1)
    npass = jnp.pad(npass, ((0, 0), (0, LANES - n_tiles)))
    return first.reshape(-1), npass.reshape(-1)


def _arrange_w_in(w):
    scale = LOG2E * HEAD_DIM ** -0.5
    qa, ka, va = w[..., :DIL_QKV], w[..., DIL_QKV:2 * DIL_QKV], w[..., 2 * DIL_QKV:3 * DIL_QKV]
    rest = w[..., 3 * DIL_QKV:]
    parts = []
    for g in range(len(DIL_PATTERNS)):
        sl = slice(g * DIL_OUT, (g + 1) * DIL_OUT)
        parts += [qa[..., sl] * scale, ka[..., sl], va[..., sl]]
    for pr in _window_pair_heads():
        parts += [rest[..., h * HEAD_DIM:(h + 1) * HEAD_DIM] * scale for h in pr]
    parts.append(rest[..., WIN_Q:WIN_Q + 2 * WIN_KV])
    return jnp.concatenate(parts, axis=-1).astype(BF16), rest[..., WIN_Q + 2 * WIN_KV:].astype(BF16)


def _arrange_w_branch_b(w):
    return jnp.concatenate([w[:, h * HEAD_DIM:(h + 1) * HEAD_DIM] for pr in _window_pair_heads() for h in pr],
                           axis=1).astype(BF16)


def kernel(x, norm_mix, w_in, w_branch_a, w_branch_b, b_gate, sink_logit, w_out, norm_ffn, w_router,
           w_expert_gate, w_expert_up, w_expert_down, norm_final):
    batch, seq, d = x.shape
    depth = w_in.shape[0]
    cap = CAPACITY_FACTOR * seq // N_EXPERTS
    m = batch * seq
    w_attn, w_gates = _arrange_w_in(w_in)
    wa, wb, wo = w_branch_a.astype(BF16), _arrange_w_branch_b(w_branch_b), w_out.astype(BF16)
    g_mix, g_ffn, bg = norm_mix[:, None, :], norm_ffn[:, None, :], b_gate[:, None, :]
    wr = jnp.swapaxes(w_router, 1, 2)
    wr_hi = wr.astype(BF16)
    wr_lo = (wr - wr_hi.astype(F32)).astype(BF16)
    h = x.reshape(m, d)
    proj = _norm_proj(h, g_mix, w_attn, 0)
    for l in range(depth):
        a0, a1, a2, qw, kv = proj
        oa = _dilated_attention(a0, a1, a2, batch, seq).reshape(m, DIL_OUT)
        ow = _window_attention(qw, kv, sink_logit[l][None, :], batch, seq).reshape(m, WIN_Q)
        h = _merge_out(oa, ow, h, g_mix, w_gates, bg, wa, wb, wo, l)
        hn, rank, aff, starts = _route(h, g_ffn, wr_hi, wr_lo, l, batch, seq, cap)
        win, npass = _slot_windows(starts, seq // TOKEN_TILE, cap)
        xe, gate = _gather(hn, rank, aff, win, npass, cap)
        y = _ffn(xe, gate, w_expert_gate, w_expert_up, w_expert_down, l, cap)
        h3 = h.reshape(batch, seq, d)
        if l + 1 == depth:
            return _scatter(y, rank, h3, win, npass, cap, norm_final[None, None, :], 0)[0]
        h3, *proj = _scatter(y, rank, h3, win, npass, cap, g_mix, l + 1, w_attn)
        h = h3.reshape(m, d)
```

```python
import functools

import jax
import jax.numpy as jnp
from jax import lax
from jax.experimental import pallas as pl
from jax.experimental.pallas import tpu as pltpu

D_MODEL = 1024
HEAD_DIM = 64
DIL_PATTERNS = ((128, 1), (512, 4), (2048, 16))
DIL_HEADS = 4
N_DIL_SUB = DIL_HEADS * len(DIL_PATTERNS)
DIL_QKV = N_DIL_SUB * HEAD_DIM
DIL_OUT = DIL_HEADS * HEAD_DIM
DIL_HALF = 64
WIN_HALF = 128
WIN_Q = D_MODEL
WIN_Q_HEADS = WIN_Q // HEAD_DIM
WIN_KV_HEADS = 4
WIN_KV = WIN_KV_HEADS * HEAD_DIM
N_EXPERTS = 16
CAPACITY_FACTOR = 2
RMS_EPS = 1e-6
NEG_INF = -1e30
LOG2E = 1.4426950408889634

LANES = 128
GROUP_COLS = 3 * DIL_OUT
DIL_PAIRS = DIL_OUT // LANES
WIN_UNITS = WIN_KV_HEADS // 2
WIN_UNIT_PAIRS = WIN_Q_HEADS // (2 * WIN_UNITS)
Q_TILE = 128
WIN_TILES_PER_BODY = 4
DIL_INFLIGHT = 16
ROW_TILE = 512
MERGE_TILE = 1024
VMEM_LIMIT = 56 * 1024 * 1024

F32 = jnp.float32
BF16 = jnp.bfloat16


def _alibi_slopes(n):
    return [float(2.0 ** (-8.0 * i / n)) for i in range(1, n + 1)]


def _rms(x, g):
    return x * lax.rsqrt(jnp.mean(x * x, axis=-1, keepdims=True) + RMS_EPS) * g


def _dot(a, b):
    return jnp.dot(a, b, preferred_element_type=F32)


def _dot_nt(a, b):
    return lax.dot_general(a, b, (((1,), (1,)), ((), ())), preferred_element_type=F32)


def _left_lanes():
    return lax.broadcasted_iota(jnp.int32, (1, LANES), 1) < HEAD_DIM


def _layer_block(a, layer):
    return pl.BlockSpec((None,) + a.shape[1:], lambda *_: (layer, 0, 0), pipeline_mode=pl.Buffered(1))


_C_A = (0, GROUP_COLS, 2 * GROUP_COLS)
_C_QW = 3 * GROUP_COLS
_C_KV = _C_QW + WIN_Q
_COL_CHUNK = 512


def _proj_store(ref, xb, w_ref, c0, cw):
    for j in range(0, cw, _COL_CHUNK):
        jw = min(_COL_CHUNK, cw - j)
        ref[:, j:j + jw] = _dot(xb, w_ref[:, c0 + j:c0 + j + jw]).astype(BF16)


def _norm_proj_kernel(x_ref, g_ref, w_ref, a0_ref, a1_ref, a2_ref, qw_ref, kv_ref, xs_ref):
    _project(x_ref[...], g_ref, w_ref, a0_ref, a1_ref, a2_ref, qw_ref, kv_ref, xs_ref)


def _project(x, g_ref, w_ref, a0_ref, a1_ref, a2_ref, qw_ref, kv_ref, xs_ref):
    xn = _rms(x, g_ref[...])
    n_lane_tiles = D_MODEL // LANES
    for j in range(n_lane_tiles):
        xs_ref[j] = xn[:, j * LANES:(j + 1) * LANES]
    xb = xn.astype(BF16)
    _proj_store(a0_ref, xb, w_ref, _C_A[0], GROUP_COLS)
    for grp, ref in ((1, a1_ref), (2, a2_ref)):
        r = DIL_PATTERNS[grp][1]
        n = ROW_TILE // r
        xp = jnp.concatenate(
            [jnp.concatenate([xs_ref[j, pl.ds(c, n, stride=r), :] for j in range(n_lane_tiles)], axis=1)
             for c in range(r)], axis=0).astype(BF16)
        res = _dot(xp, w_ref[:, _C_A[grp]:_C_A[grp] + GROUP_COLS])
        for c in range(r):
            ref[:, c * GROUP_COLS:(c + 1) * GROUP_COLS] = res[c * n:(c + 1) * n].astype(BF16)
    _proj_store(qw_ref, xb, w_ref, _C_QW, WIN_Q)
    _proj_store(kv_ref, xb, w_ref, _C_KV, 2 * WIN_KV)


def _proj_outputs(m, row_index):
    out_shape, out_specs = [], []
    for _, r in DIL_PATTERNS:
        out_shape.append(jax.ShapeDtypeStruct((m // r, r * GROUP_COLS), BF16))
        out_specs.append(pl.BlockSpec((ROW_TILE // r, r * GROUP_COLS), row_index))
    for c in (WIN_Q, 2 * WIN_KV):
        out_shape.append(jax.ShapeDtypeStruct((m, c), BF16))
        out_specs.append(pl.BlockSpec((ROW_TILE, c), row_index))
    return out_shape, out_specs


def _norm_proj(x, g, w, layer):
    m = x.shape[0]
    in_specs = [pl.BlockSpec((ROW_TILE, D_MODEL), lambda i: (i, 0)), _layer_block(g, layer), _layer_block(w, layer)]
    out_shape, out_specs = _proj_outputs(m, lambda i: (i, 0))
    return pl.pallas_call(
        _norm_proj_kernel,
        grid=(m // ROW_TILE,),
        in_specs=in_specs,
        out_specs=out_specs,
        out_shape=out_shape,
        scratch_shapes=[pltpu.VMEM((D_MODEL // LANES, ROW_TILE, LANES), F32)],
        compiler_params=pltpu.CompilerParams(
            dimension_semantics=("arbitrary",), vmem_limit_bytes=VMEM_LIMIT),
        name="norm_proj",
    )(x, g, w)


def _band_bias(tq, win, off, half, slopes):
    col = lax.broadcasted_iota(jnp.int32, (tq, win), 1)
    row = lax.broadcasted_iota(jnp.int32, (tq, win), 0)
    dist = jnp.abs(col - row + off).astype(F32)
    inside = dist <= float(half)
    return jnp.concatenate([jnp.where(inside, dist * (-s * LOG2E), NEG_INF) for s in slopes], axis=0)


def _ones_blockdiag(win):
    left = lax.broadcasted_iota(jnp.int32, (2 * win, LANES), 1) < HEAD_DIM
    top = jnp.where(lax.broadcasted_iota(jnp.int32, (2 * win, LANES), 0) < win, 1.0, 0.0)
    return jnp.where(left, top, 1.0 - top).astype(BF16)


def _pair_scores(q_tiles, k_win, bias):
    left = _left_lanes()
    zero = jnp.zeros((), BF16)
    rows = []
    for q in q_tiles:
        rows += [jnp.where(left, q, zero), jnp.where(left, zero, q)]
    return _dot_nt(jnp.concatenate(rows, axis=0), k_win) + bias


def _pair_values(s_ref, v_win, ones_bd, sink_ref, p_ref, aux_ref):
    win = v_win.shape[0]
    n = s_ref.shape[0] // (2 * Q_TILE)
    left = _left_lanes()
    zero = jnp.zeros((), BF16)
    for i in range(n):
        ms = []
        for side in range(2):
            rows = pl.ds((2 * i + side) * Q_TILE, Q_TILE)
            s = s_ref[rows, :]
            m = jnp.broadcast_to(jnp.max(s, axis=-1, keepdims=True), (Q_TILE, LANES))
            if sink_ref is not None:
                m = jnp.maximum(m, sink_ref[rows, :])
            p = jnp.exp2(s - jnp.concatenate([m] * (win // LANES), axis=1)).astype(BF16)
            p_ref[i * Q_TILE:(i + 1) * Q_TILE, side * win:(side + 1) * win] = p
            ms.append(m)
        m_pair = jnp.where(left, ms[0], ms[1])
        if sink_ref is not None:
            sink_pair = jnp.where(left, sink_ref[pl.ds(2 * i * Q_TILE, Q_TILE), :],
                                  sink_ref[pl.ds((2 * i + 1) * Q_TILE, Q_TILE), :])
            aux_ref[i * Q_TILE:(i + 1) * Q_TILE, :] = jnp.exp2(sink_pair - m_pair)
        else:
            aux_ref[i * Q_TILE:(i + 1) * Q_TILE, :] = m_pair
    vbd = jnp.concatenate([jnp.where(left, v_win, zero), jnp.where(left, zero, v_win)], axis=0)
    on = _dot(p_ref[...], jnp.concatenate([vbd, ones_bd], axis=1))
    return on[:, :LANES], on[:, LANES:], aux_ref[...]


def _tile_variant(t, n_tiles):
    return jnp.where(t == 0, 0, jnp.where(t == n_tiles - 1, 2, 1))


def _pipelined_tiles(n_tiles, per_body, score_stage, value_stage):
    score_stage(0, 0)

    def body(i, carry):
        t = per_body * i
        for j in range(per_body):
            score_stage(jnp.minimum(t + j + 1, n_tiles - 1), (j + 1) % per_body)
            value_stage(t + j, j)
        return carry

    lax.fori_loop(0, n_tiles // per_body, body, 0)


def _dil_geometry(seq_full, grp):
    r = DIL_PATTERNS[grp][1]
    seq = seq_full // r
    win = min(seq, Q_TILE + 2 * DIL_HALF)
    n_tiles = seq // Q_TILE
    pad = (win - Q_TILE) // 2
    offs = [0] if n_tiles == 1 else [0, -pad, -2 * pad]
    return r, seq, win, n_tiles, pad, offs


def _dilated_kernel(seq_full, a0_ref, a1_ref, a2_ref, out_ref, o_nat, l_nat, b0, b1, b2, ones_ref, ones2_ref,
                    s_wide, s_narrow, p_wide, p_narrow, aux_ref):
    a_refs = (a0_ref, a1_ref, a2_ref)
    bias_refs = (b0, b1, b2)
    s_refs = (s_wide, s_wide, s_narrow)
    p_refs = (p_wide, p_wide, p_narrow)
    slopes = _alibi_slopes(N_DIL_SUB)

    @pl.when(pl.program_id(0) == 0)
    def _():
        for grp in range(3):
            r, seq, win, n_tiles, pad, offs = _dil_geometry(seq_full, grp)
            for v, off in enumerate(offs):
                for pr in range(DIL_PAIRS):
                    hs = slopes[grp * DIL_HEADS + 2 * pr:grp * DIL_HEADS + 2 * pr + 2]
                    bias_refs[grp][v, pr] = _band_bias(Q_TILE, win, off, DIL_HALF, [s * r for s in hs])
        ones_ref[...] = _ones_blockdiag(_dil_geometry(seq_full, 0)[2])
        ones2_ref[...] = _ones_blockdiag(_dil_geometry(seq_full, 2)[2])

    def pair_tile(grp, c, t, pr, slot):
        r, seq, win, n_tiles, pad, offs = _dil_geometry(seq_full, grp)
        a_ref = a_refs[grp]
        ones_bd_ref = ones2_ref if grp == 2 else ones_ref
        if n_tiles == 1:
            q0, ks, var = 0, 0, 0
        else:
            q0 = pl.multiple_of(t * Q_TILE, Q_TILE)
            ks = pl.multiple_of(jnp.clip(q0 - pad, 0, seq - win), DIL_HALF)
            var = _tile_variant(t, n_tiles)
        lo = c * GROUP_COLS + pr * LANES
        q = a_ref[pl.ds(q0, Q_TILE), lo:lo + LANES]
        k = a_ref[pl.ds(ks, win), DIL_OUT + lo:DIL_OUT + lo + LANES]
        v = a_ref[pl.ds(ks, win), 2 * DIL_OUT + lo:2 * DIL_OUT + lo + LANES]
        s_ref = s_refs[grp].at[slot]
        s_ref[...] = _pair_scores([q], k, bias_refs[grp][var, pr])
        num, den, m = _pair_values(s_ref, v, ones_bd_ref[...], None, p_refs[grp].at[slot], aux_ref.at[slot])
        rows = pl.ds(q0, Q_TILE) if r == 1 else pl.ds(c + r * q0, Q_TILE, stride=r)
        o_nat[grp, pr, rows, :] = num / den
        l_nat[grp, pr, rows, :] = m + jnp.log2(den)

    for grp in range(3):
        r, seq, win, n_tiles, pad, offs = _dil_geometry(seq_full, grp)
        per_body = DIL_INFLIGHT // DIL_PAIRS
        if n_tiles == 1:
            for c in range(r):
                for pr in range(DIL_PAIRS):
                    pair_tile(grp, c, 0, pr, c * DIL_PAIRS + pr)
        elif r >= per_body:
            for c0 in range(0, r, per_body):
                def body(t, carry, grp=grp, c0=c0):
                    for j in range(per_body):
                        for pr in range(DIL_PAIRS):
                            pair_tile(grp, c0 + j, t, pr, j * DIL_PAIRS + pr)
                    return carry
                lax.fori_loop(0, n_tiles, body, 0)
        else:
            tiles_per_body = per_body // r
            def body(i, carry, grp=grp, r=r, tiles_per_body=tiles_per_body):
                for j in range(tiles_per_body):
                    for c in range(r):
                        for pr in range(DIL_PAIRS):
                            pair_tile(grp, c, i * tiles_per_body + j, pr, (j * r + c) * DIL_PAIRS + pr)
                return carry
            lax.fori_loop(0, n_tiles // tiles_per_body, body, 0)

    def combine(i, carry):
        rows = pl.ds(pl.multiple_of(i * ROW_TILE, ROW_TILE), ROW_TILE)
        for pr in range(DIL_PAIRS):
            ls = [l_nat[g, pr, rows, :] for g in range(3)]
            mx = jnp.maximum(jnp.maximum(ls[0], ls[1]), ls[2])
            es = [jnp.exp2(l - mx) for l in ls]
            num = es[0] * o_nat[0, pr, rows, :] + es[1] * o_nat[1, pr, rows, :] + es[2] * o_nat[2, pr, rows, :]
            out_ref[rows, pr * LANES:(pr + 1) * LANES] = (num / (es[0] + es[1] + es[2])).astype(BF16)
        return carry

    lax.fori_loop(0, seq_full // ROW_TILE, combine, 0)


def _dilated_attention(a0, a1, a2, batch, seq_full):
    views, in_specs, bias_shapes = [], [], []
    for grp, a in enumerate((a0, a1, a2)):
        r, seq, win, n_tiles, pad, offs = _dil_geometry(seq_full, grp)
        views.append(a.reshape(batch, seq, r * GROUP_COLS))
        in_specs.append(pl.BlockSpec((None, seq, r * GROUP_COLS), lambda b: (b, 0, 0)))
        bias_shapes.append(pltpu.VMEM((len(offs), DIL_PAIRS, 2 * Q_TILE, win), F32))
    win0 = _dil_geometry(seq_full, 0)[2]
    win2 = _dil_geometry(seq_full, 2)[2]
    n_straight = DIL_PATTERNS[2][1] * DIL_PAIRS
    return pl.pallas_call(
        functools.partial(_dilated_kernel, seq_full),
        grid=(batch,),
        in_specs=in_specs,
        out_specs=pl.BlockSpec((None, seq_full, DIL_OUT), lambda b: (b, 0, 0)),
        out_shape=jax.ShapeDtypeStruct((batch, seq_full, DIL_OUT), BF16),
        scratch_shapes=[pltpu.VMEM((3, DIL_PAIRS, seq_full, LANES), F32),
                        pltpu.VMEM((3, DIL_PAIRS, seq_full, LANES), F32)]
        + bias_shapes + [pltpu.VMEM((2 * win0, LANES), BF16), pltpu.VMEM((2 * win2, LANES), BF16)]
        + [pltpu.VMEM((n, 2 * Q_TILE, w), F32) for n, w in ((DIL_INFLIGHT, win0), (n_straight, win2))]
        + [pltpu.VMEM((n, Q_TILE, 2 * w), BF16) for n, w in ((DIL_INFLIGHT, win0), (n_straight, win2))]
        + [pltpu.VMEM((max(DIL_INFLIGHT, n_straight), Q_TILE, LANES), F32)],
        compiler_params=pltpu.CompilerParams(
            dimension_semantics=("arbitrary",), vmem_limit_bytes=VMEM_LIMIT),
        name="dilated",
    )(*views)


WIN_WINDOW = Q_TILE + 2 * WIN_HALF


def _window_pair_heads():
    group = WIN_Q_HEADS // WIN_KV_HEADS
    pairs = []
    for u in range(WIN_UNITS):
        for i in range(WIN_UNIT_PAIRS):
            pairs.append((2 * u * group + i, (2 * u + 1) * group + i))
    return pairs


def _window_kernel(seq, q_ref, kv_ref, sink_ref, o_ref, bias_ref, ones_ref, sink_rows_ref, s_ref, p_ref,
                   aux_ref):
    n_tiles = seq // Q_TILE
    slopes = _alibi_slopes(WIN_Q_HEADS)
    pair_heads = _window_pair_heads()

    @pl.when(pl.program_id(0) == 0)
    def _():
        for u in range(WIN_UNITS):
            heads = [h for pr in pair_heads[u * WIN_UNIT_PAIRS:(u + 1) * WIN_UNIT_PAIRS] for h in pr]
            for v, off in enumerate((0, -WIN_HALF, -2 * WIN_HALF)):
                bias_ref[v, u] = _band_bias(Q_TILE, WIN_WINDOW, off, WIN_HALF, [slopes[h] for h in heads])
            sink_rows_ref[u] = jnp.concatenate(
                [jnp.broadcast_to(sink_ref[0:1, h:h + 1] * LOG2E, (Q_TILE, LANES)) for h in heads], axis=0)
        ones_ref[...] = _ones_blockdiag(WIN_WINDOW)

    def rows_of(t):
        q0 = pl.multiple_of(t * Q_TILE, Q_TILE)
        ks = pl.multiple_of(jnp.clip(q0 - WIN_HALF, 0, seq - WIN_WINDOW), Q_TILE)
        return q0, ks

    def unit_cols(u):
        return [(u * WIN_UNIT_PAIRS + i) * LANES for i in range(WIN_UNIT_PAIRS)]

    def score_stage(t, k_set):
        q0, ks = rows_of(t)
        var = _tile_variant(t, n_tiles)
        for u in range(WIN_UNITS):
            qs = [q_ref[pl.ds(q0, Q_TILE), c:c + LANES] for c in unit_cols(u)]
            k = kv_ref[pl.ds(ks, WIN_WINDOW), u * LANES:(u + 1) * LANES]
            s_ref[k_set * WIN_UNITS + u] = _pair_scores(qs, k, bias_ref[var, u])

    def value_stage(t, k_set):
        q0, ks = rows_of(t)
        for u in range(WIN_UNITS):
            slot = k_set * WIN_UNITS + u
            v = kv_ref[pl.ds(ks, WIN_WINDOW), WIN_KV + u * LANES:WIN_KV + (u + 1) * LANES]
            num, den, sink_term = _pair_values(s_ref.at[slot], v, ones_ref[...], sink_rows_ref.at[u],
                                               p_ref.at[slot], aux_ref.at[slot])
            o = (num / (den + sink_term)).astype(BF16)
            for i, c in enumerate(unit_cols(u)):
                o_ref[pl.ds(q0, Q_TILE), c:c + LANES] = o[i * Q_TILE:(i + 1) * Q_TILE]

    _pipelined_tiles(n_tiles, WIN_TILES_PER_BODY, score_stage, value_stage)


def _window_attention(q, kv, sink, batch, seq):
    unit_rows = 2 * WIN_UNIT_PAIRS * Q_TILE
    return pl.pallas_call(
        functools.partial(_window_kernel, seq),
        grid=(batch,),
        in_specs=[pl.BlockSpec((None, seq, WIN_Q), lambda b: (b, 0, 0)),
                  pl.BlockSpec((None, seq, 2 * WIN_KV), lambda b: (b, 0, 0)),
                  pl.BlockSpec((1, WIN_Q_HEADS), lambda b: (0, 0))],
        out_specs=pl.BlockSpec((None, seq, WIN_Q), lambda b: (b, 0, 0)),
        out_shape=jax.ShapeDtypeStruct((batch, seq, WIN_Q), BF16),
        scratch_shapes=[pltpu.VMEM((3, WIN_UNITS, unit_rows, WIN_WINDOW), F32),
                        pltpu.VMEM((2 * WIN_WINDOW, LANES), BF16),
                        pltpu.VMEM((WIN_UNITS, unit_rows, LANES), F32),
                        pltpu.VMEM((WIN_TILES_PER_BODY * WIN_UNITS, unit_rows, WIN_WINDOW), F32),
                        pltpu.VMEM((WIN_TILES_PER_BODY * WIN_UNITS, unit_rows // 2, 2 * WIN_WINDOW), BF16),
                        pltpu.VMEM((WIN_TILES_PER_BODY * WIN_UNITS, unit_rows // 2, LANES), F32)],
        compiler_params=pltpu.CompilerParams(
            dimension_semantics=("arbitrary",), vmem_limit_bytes=VMEM_LIMIT),
        name="window",
    )(q.reshape(batch, seq, WIN_Q), kv.reshape(batch, seq, 2 * WIN_KV), sink)


def _merge_kernel(oa_ref, ow_ref, h_ref, g_ref, wgate_ref, bgate_ref, wa_ref, wb_ref, wo_ref, out_ref):
    h = h_ref[...]
    xb = _rms(h, g_ref[...]).astype(BF16)
    merged = None
    for br, (o_ref, w_ref) in enumerate(((oa_ref, wa_ref), (ow_ref, wb_ref))):
        cols = slice(br * D_MODEL, (br + 1) * D_MODEL)
        gate = jax.nn.sigmoid(_dot(xb, wgate_ref[:, cols]) + bgate_ref[:, cols])
        term = gate * _dot(o_ref[...], w_ref[...])
        merged = term if merged is None else merged + term
    out_ref[...] = h + _dot(merged.astype(BF16), wo_ref[...])


def _merge_out(oa, ow, h, g, wgate, bgate, wa, wb, wo, layer):
    m = h.shape[0]
    row = lambda c: pl.BlockSpec((MERGE_TILE, c), lambda i: (i, 0))
    return pl.pallas_call(
        _merge_kernel,
        grid=(m // MERGE_TILE,),
        in_specs=[row(DIL_OUT), row(WIN_Q), row(D_MODEL)]
        + [_layer_block(a, layer) for a in (g, wgate, bgate, wa, wb, wo)],
        out_specs=row(D_MODEL),
        out_shape=jax.ShapeDtypeStruct((m, D_MODEL), F32),
        compiler_params=pltpu.CompilerParams(
            dimension_semantics=("arbitrary",), vmem_limit_bytes=VMEM_LIMIT),
        name="merge_out",
    )(oa, ow, h, g, wgate, bgate, wa, wb, wo)


TOKEN_TILE = 256
CUM_CHUNK = TOKEN_TILE
SLOT_WINDOW = 64
SLOT_ALIGN = 16


def _prefix_exclusive(mask_f, tri):
    e, s = mask_f.shape
    carry = jnp.zeros((e, 1), F32)
    parts, carries = [], []
    for j in range(0, s, CUM_CHUNK):
        blk = mask_f[:, j:j + CUM_CHUNK]
        carries.append(carry)
        parts.append(_dot(blk.astype(BF16), tri) + carry)
        carry = carry + jnp.sum(blk, axis=-1, keepdims=True)
    return jnp.concatenate(parts, axis=-1), carries + [carry]


def _route_kernel(cap, h_ref, g_ref, whi_ref, wlo_ref, hn_ref, rank_ref, aff_ref, starts_ref):
    seq = h_ref.shape[0]
    hn = _rms(h_ref[...], g_ref[...])
    hn_hi = hn.astype(BF16)
    hn_lo = (hn - hn_hi.astype(F32)).astype(BF16)
    hn_ref[...] = hn_hi
    both = _dot_nt(jnp.concatenate([whi_ref[...], wlo_ref[...]], axis=0), hn_hi)
    logits = both[:N_EXPERTS] + both[N_EXPERTS:] + _dot_nt(whi_ref[...], hn_lo)
    mx = jnp.max(logits, axis=0, keepdims=True)
    ex = jnp.exp(logits - mx)
    aff = ex / jnp.sum(ex, axis=0, keepdims=True)
    bits = pltpu.bitcast(aff, jnp.int32)

    def enough(t):
        return jnp.sum(jnp.where(bits >= t, 1.0, 0.0), axis=-1, keepdims=True) >= float(cap)

    def search4(_, c):
        lo, hi = c
        q = (hi - lo) >> 2
        m1, m2, m3 = lo + q, lo + 2 * q, lo + 3 * q
        ok1, ok2, ok3 = enough(m1), enough(m2), enough(m3)
        return (jnp.where(ok3, m3, jnp.where(ok2, m2, jnp.where(ok1, m1, lo))),
                jnp.where(ok3, hi, jnp.where(ok2, m3, jnp.where(ok1, m2, m1))))

    def search2(_, c):
        lo, hi = c
        mid = lo + ((hi - lo) >> 1)
        ok = enough(mid)
        return jnp.where(ok, mid, lo), jnp.where(ok, hi, mid)

    lo0 = jnp.zeros((N_EXPERTS, 1), jnp.int32)
    hi0 = jnp.full((N_EXPERTS, 1), 0x3F800001, jnp.int32)
    thr, _ = lax.fori_loop(0, 3, search2, lax.fori_loop(0, 15, search4, (lo0, hi0)))

    r_i = lax.broadcasted_iota(jnp.int32, (CUM_CHUNK, CUM_CHUNK), 0)
    c_i = lax.broadcasted_iota(jnp.int32, (CUM_CHUNK, CUM_CHUNK), 1)
    tri = jnp.where(r_i < c_i, 1.0, 0.0).astype(BF16)
    gt = jnp.where(bits > thr, 1.0, 0.0)
    eq = jnp.where(bits == thr, 1.0, 0.0)
    need = float(cap) - jnp.sum(gt, axis=-1, keepdims=True)
    tie_rank, _ = _prefix_exclusive(eq, tri)
    sel = gt + eq * jnp.where(tie_rank < need, 1.0, 0.0)
    slot, starts = _prefix_exclusive(sel, tri)
    rank = jnp.where(sel > 0.0, slot, -1.0)
    for t in range(seq // TOKEN_TILE):
        rank_ref[t] = rank[:, t * TOKEN_TILE:(t + 1) * TOKEN_TILE]
        aff_ref[t] = aff[:, t * TOKEN_TILE:(t + 1) * TOKEN_TILE]
    lane = lax.broadcasted_iota(jnp.int32, (N_EXPERTS, LANES), 1)
    acc = jnp.zeros((N_EXPERTS, LANES), F32)
    for t, c in enumerate(starts):
        acc = acc + jnp.where(lane == t, c, 0.0)
    starts_ref[...] = acc


def _route(h, g, w_hi, w_lo, layer, batch, seq, cap):
    n_tiles = seq // TOKEN_TILE
    return pl.pallas_call(
        functools.partial(_route_kernel, cap),
        grid=(batch,),
        in_specs=[pl.BlockSpec((None, seq, D_MODEL), lambda b: (b, 0, 0))]
        + [_layer_block(a, layer) for a in (g, w_hi, w_lo)],
        out_specs=[pl.BlockSpec((None, seq, D_MODEL), lambda b: (b, 0, 0)),
                   pl.BlockSpec((None, n_tiles, N_EXPERTS, TOKEN_TILE), lambda b: (b, 0, 0, 0)),
                   pl.BlockSpec((None, n_tiles, N_EXPERTS, TOKEN_TILE), lambda b: (b, 0, 0, 0)),
                   pl.BlockSpec((None, N_EXPERTS, LANES), lambda b: (b, 0, 0))],
        out_shape=[jax.ShapeDtypeStruct((batch, seq, D_MODEL), BF16),
                   jax.ShapeDtypeStruct((batch, n_tiles, N_EXPERTS, TOKEN_TILE), F32),
                   jax.ShapeDtypeStruct((batch, n_tiles, N_EXPERTS, TOKEN_TILE), F32),
                   jax.ShapeDtypeStruct((batch, N_EXPERTS, LANES), F32)],
        compiler_params=pltpu.CompilerParams(
            dimension_semantics=("arbitrary",), vmem_limit_bytes=VMEM_LIMIT),
        name="route",
    )(h.reshape(batch, seq, D_MODEL), g, w_hi, w_lo)


def _slot_rows(cap):
    return cap + SLOT_WINDOW


def _window_starts(win_ref, b, t, p, cap):
    base = b * (N_EXPERTS * LANES) + t
    return [pl.multiple_of(jnp.minimum(win_ref[base + e * LANES] + p * SLOT_WINDOW, cap), SLOT_ALIGN)
            for e in range(N_EXPERTS)]


def _window_hits(rank_tile, starts):
    rows = lax.broadcasted_iota(jnp.int32, (SLOT_WINDOW, TOKEN_TILE), 0).astype(F32)
    return [(rank_tile[e:e + 1, :] - starts[e].astype(F32)) == rows for e in range(N_EXPERTS)]


def _one_hot(hits):
    return jnp.concatenate([jnp.where(h, 1.0, 0.0).astype(BF16) for h in hits], axis=0)


def _gather_kernel(cap, win_ref, npass_ref, hn_ref, rank_ref, aff_ref, xe_ref, gate_ref):
    b = pl.program_id(0)
    xe_ref[...] = jnp.zeros_like(xe_ref)
    gate_ref[...] = jnp.zeros_like(gate_ref)
    for t in range(hn_ref.shape[0] // TOKEN_TILE):

        def one_pass(p, carry, t=t):
            starts = _window_starts(win_ref, b, t, p, cap)
            hits = _window_hits(rank_ref[t], starts)
            rows = _dot(_one_hot(hits), hn_ref[t * TOKEN_TILE:(t + 1) * TOKEN_TILE, :]).astype(BF16)
            aff_tile = aff_ref[t]
            for e in range(N_EXPERTS):
                win = pl.ds(starts[e], SLOT_WINDOW)
                xe_ref[e, win, :] += rows[e * SLOT_WINDOW:(e + 1) * SLOT_WINDOW]
                g = jnp.sum(jnp.where(hits[e], aff_tile[e:e + 1, :], 0.0), axis=-1, keepdims=True)
                gate_ref[e, win, :] += jnp.broadcast_to(g, (SLOT_WINDOW, LANES))
            return carry

        lax.fori_loop(0, npass_ref[b * LANES + t], one_pass, 0)


def _gather(hn, rank, aff, win, npass, cap):
    batch, seq, _ = hn.shape
    n_tiles = seq // TOKEN_TILE
    grid_spec = pltpu.PrefetchScalarGridSpec(
        num_scalar_prefetch=2,
        grid=(batch,),
        in_specs=[pl.BlockSpec((None, seq, D_MODEL), lambda b, *_: (b, 0, 0)),
                  pl.BlockSpec((None, n_tiles, N_EXPERTS, TOKEN_TILE), lambda b, *_: (b, 0, 0, 0)),
                  pl.BlockSpec((None, n_tiles, N_EXPERTS, TOKEN_TILE), lambda b, *_: (b, 0, 0, 0))],
        out_specs=[pl.BlockSpec((None, N_EXPERTS, _slot_rows(cap), D_MODEL), lambda b, *_: (b, 0, 0, 0)),
                   pl.BlockSpec((None, N_EXPERTS, _slot_rows(cap), LANES), lambda b, *_: (b, 0, 0, 0))])
    return pl.pallas_call(
        functools.partial(_gather_kernel, cap),
        grid_spec=grid_spec,
        out_shape=[jax.ShapeDtypeStruct((batch, N_EXPERTS, _slot_rows(cap), D_MODEL), BF16),
                   jax.ShapeDtypeStruct((batch, N_EXPERTS, _slot_rows(cap), LANES), F32)],
        compiler_params=pltpu.CompilerParams(
            dimension_semantics=("arbitrary",), vmem_limit_bytes=VMEM_LIMIT),
        name="gather",
    )(win, npass, hn, rank, aff)


FFN_SEQS = 4


def _ffn_kernel(cap, xe_ref, gate_ref, wg_ref, wu_ref, wd_ref, y_ref, wg_b, wu_b, wd_b):
    d = xe_ref.shape[-1]

    @pl.when(pl.program_id(1) == 0)
    def _():
        for src, dst in ((wg_ref, wg_b), (wu_ref, wu_b), (wd_ref, wd_b)):
            dst[...] = src[...].astype(BF16)

    xe = xe_ref[:, :cap, :].reshape(FFN_SEQS * cap, d)
    a = _dot(xe, wg_b[...])
    u = _dot(xe, wu_b[...])
    y = _dot((jax.nn.silu(a) * u).astype(BF16), wd_b[...])
    gate = gate_ref[:, :cap, :].reshape(FFN_SEQS * cap, LANES)
    y = y * jnp.concatenate([gate] * (d // LANES), axis=1)
    y_ref[:, :cap, :] = y.astype(BF16).reshape(FFN_SEQS, cap, d)
    y_ref[:, cap:, :] = jnp.zeros((FFN_SEQS, SLOT_WINDOW, d), BF16)


def _ffn(xe, gate, wg, wu, wd, layer, cap):
    batch = xe.shape[0]
    d_exp = wg.shape[-1]
    slots = pl.BlockSpec((FFN_SEQS, None, _slot_rows(cap), D_MODEL), lambda e, i: (i, e, 0, 0))
    return pl.pallas_call(
        functools.partial(_ffn_kernel, cap),
        grid=(N_EXPERTS, batch // FFN_SEQS),
        in_specs=[slots,
                  pl.BlockSpec((FFN_SEQS, None, _slot_rows(cap), LANES), lambda e, i: (i, e, 0, 0)),
                  pl.BlockSpec((None, None, D_MODEL, d_exp), lambda e, i: (layer, e, 0, 0)),
                  pl.BlockSpec((None, None, D_MODEL, d_exp), lambda e, i: (layer, e, 0, 0)),
                  pl.BlockSpec((None, None, d_exp, D_MODEL), lambda e, i: (layer, e, 0, 0))],
        out_specs=slots,
        out_shape=jax.ShapeDtypeStruct(xe.shape, BF16),
        scratch_shapes=[pltpu.VMEM((D_MODEL, d_exp), BF16), pltpu.VMEM((D_MODEL, d_exp), BF16),
                        pltpu.VMEM((d_exp, D_MODEL), BF16)],
        compiler_params=pltpu.CompilerParams(
            dimension_semantics=("arbitrary", "arbitrary"), vmem_limit_bytes=VMEM_LIMIT),
        name="ffn",
    )(xe, gate, wg, wu, wd)


TILES_PER_ROW_TILE = ROW_TILE // TOKEN_TILE


def _scatter_kernel(cap, project, win_ref, npass_ref, y_hbm, rank_ref, h_ref, g_ref, *rest):
    if project:
        w_ref, out_ref, a0_ref, a1_ref, a2_ref, qw_ref, kv_ref, xs_ref, y_buf, y_sem = rest
    else:
        out_ref, y_buf, y_sem = rest
    b = pl.program_id(0)
    j = pl.program_id(1)
    slot = b % 2

    def y_copy(seq_idx, buf_slot):
        return pltpu.make_async_copy(y_hbm.at[seq_idx], y_buf.at[buf_slot], y_sem.at[buf_slot])

    @pl.when((b == 0) & (j == 0))
    def _():
        y_copy(0, 0).start()

    @pl.when(j == 0)
    def _():
        y_copy(b, slot).wait()

    @pl.when((j == 0) & (b + 1 < pl.num_programs(0)))
    def _():
        y_copy(b + 1, 1 - slot).start()

    y_ref = y_buf.at[slot]
    out_ref[...] = h_ref[...]
    for tt in range(TILES_PER_ROW_TILE):
        t = j * TILES_PER_ROW_TILE + tt
        rows = slice(tt * TOKEN_TILE, (tt + 1) * TOKEN_TILE)

        def one_pass(p, carry, t=t, tt=tt, rows=rows):
            starts = _window_starts(win_ref, b, t, p, cap)
            put = _one_hot(_window_hits(rank_ref[tt], starts))
            yw = jnp.concatenate([y_ref[e, pl.ds(starts[e], SLOT_WINDOW), :] for e in range(N_EXPERTS)],
                                 axis=0)
            out_ref[rows, :] += lax.dot_general(put, yw, (((0,), (0,)), ((), ())), preferred_element_type=F32)
            return carry

        lax.fori_loop(0, npass_ref[b * LANES + t], one_pass, 0)
    if project:
        _project(out_ref[...], g_ref, w_ref, a0_ref, a1_ref, a2_ref, qw_ref, kv_ref, xs_ref)
    else:
        out_ref[...] = _rms(out_ref[...], g_ref[...])


def _scatter(y, rank, h, win, npass, cap, g, layer, w_next=None):
    batch, n_tiles = rank.shape[:2]
    seq = h.shape[1]
    steps = n_tiles // TILES_PER_ROW_TILE
    project = w_next is not None
    in_specs = [pl.BlockSpec(memory_space=pl.ANY),
                pl.BlockSpec((None, TILES_PER_ROW_TILE, N_EXPERTS, TOKEN_TILE), lambda b, j, *_: (b, j, 0, 0)),
                pl.BlockSpec((None, ROW_TILE, D_MODEL), lambda b, j, *_: (b, j, 0)),
                _layer_block(g, layer)]
    out_shape = [jax.ShapeDtypeStruct(h.shape, F32)]
    out_specs = [pl.BlockSpec((None, ROW_TILE, D_MODEL), lambda b, j, *_: (b, j, 0))]
    args = [win, npass, y, rank, h, g]
    scratch = []
    if project:
        in_specs.append(_layer_block(w_next, layer))
        args.append(w_next)
        proj_shape, proj_specs = _proj_outputs(batch * seq, lambda b, j, *_: (b * steps + j, 0))
        out_shape += proj_shape
        out_specs += proj_specs
        scratch = [pltpu.VMEM((D_MODEL // LANES, ROW_TILE, LANES), F32)]
    scratch += [pltpu.VMEM((2,) + y.shape[1:], BF16), pltpu.SemaphoreType.DMA((2,))]
    grid_spec = pltpu.PrefetchScalarGridSpec(
        num_scalar_prefetch=2, grid=(batch, steps), in_specs=in_specs, out_specs=out_specs,
        scratch_shapes=scratch)
    return pl.pallas_call(
        functools.partial(_scatter_kernel, cap, project),
        grid_spec=grid_spec,
        out_shape=out_shape,
        compiler_params=pltpu.CompilerParams(
            dimension_semantics=("arbitrary", "arbitrary"), vmem_limit_bytes=VMEM_LIMIT),
        name="scatter_proj" if project else "scatter_final",
    )(*args)


def _slot_windows(starts, n_tiles, cap):
    s = starts.astype(jnp.int32)
    first = (s // SLOT_ALIGN) * SLOT_ALIGN
    span = s[:, :, 1:n_tiles + 1] - first[:, :, :n_tiles]
    npass = jnp.maximum(jnp.max(-(-span // SLOT_WINDOW), axis=1), 1)
    npass = jnp.pad(npass, ((0, 0), (0, LANES - n_tiles)))
    return first.reshape(-1), npass.reshape(-1)


def _arrange_w_in(w):
    scale = LOG2E * HEAD_DIM ** -0.5
    qa, ka, va = w[..., :DIL_QKV], w[..., DIL_QKV:2 * DIL_QKV], w[..., 2 * DIL_QKV:3 * DIL_QKV]
    rest = w[..., 3 * DIL_QKV:]
    parts = []
    for g in range(len(DIL_PATTERNS)):
        sl = slice(g * DIL_OUT, (g + 1) * DIL_OUT)
        parts += [qa[..., sl] * scale, ka[..., sl], va[..., sl]]
    for pr in _window_pair_heads():
        parts += [rest[..., h * HEAD_DIM:(h + 1) * HEAD_DIM] * scale for h in pr]
    parts.append(rest[..., WIN_Q:WIN_Q + 2 * WIN_KV])
    return jnp.concatenate(parts, axis=-1).astype(BF16), rest[..., WIN_Q + 2 * WIN_KV:].astype(BF16)


def _arrange_w_branch_b(w):
    return jnp.concatenate([w[:, h * HEAD_DIM:(h + 1) * HEAD_DIM] for pr in _window_pair_heads() for h in pr],
                           axis=1).astype(BF16)


def kernel(x, norm_mix, w_in, w_branch_a, w_branch_b, b_gate, sink_logit, w_out, norm_ffn, w_router,
           w_expert_gate, w_expert_up, w_expert_down, norm_final):
    batch, seq, d = x.shape
    depth = w_in.shape[0]
    cap = CAPACITY_FACTOR * seq // N_EXPERTS
    m = batch * seq
    w_attn, w_gates = _arrange_w_in(w_in)
    wa, wb, wo = w_branch_a.astype(BF16), _arrange_w_branch_b(w_branch_b), w_out.astype(BF16)
    g_mix, g_ffn, bg = norm_mix[:, None, :], norm_ffn[:, None, :], b_gate[:, None, :]
    wr = jnp.swapaxes(w_router, 1, 2)
    wr_hi = wr.astype(BF16)
    wr_lo = (wr - wr_hi.astype(F32)).astype(BF16)
    h = x.reshape(m, d)
    proj = _norm_proj(h, g_mix, w_attn, 0)
    for l in range(depth):
        a0, a1, a2, qw, kv = proj
        oa = _dilated_attention(a0, a1, a2, batch, seq).reshape(m, DIL_OUT)
        ow = _window_attention(qw, kv, sink_logit[l][None, :], batch, seq).reshape(m, WIN_Q)
        h = _merge_out(oa, ow, h, g_mix, w_gates, bg, wa, wb, wo, l)
        hn, rank, aff, starts = _route(h, g_ffn, wr_hi, wr_lo, l, batch, seq, cap)
        win, npass = _slot_windows(starts, seq // TOKEN_TILE, cap)
        xe, gate = _gather(hn, rank, aff, win, npass, cap)
        y = _ffn(xe, gate, w_expert_gate, w_expert_up, w_expert_down, l, cap)
        h3 = h.reshape(batch, seq, d)
        if l + 1 == depth:
            return _scatter(y, rank, h3, win, npass, cap, norm_final[None, None, :], 0)[0]
        h3, *proj = _scatter(y, rank, h3, win, npass, cap, g_mix, l + 1, w_attn)
        h = h3.reshape(m, d)
```

```python
import functools

import jax
import jax.numpy as jnp
from jax import lax
from jax.experimental import pallas as pl
from jax.experimental.pallas import tpu as pltpu

D_MODEL = 1024
HEAD_DIM = 64
DIL_PATTERNS = ((128, 1), (512, 4), (2048, 16))
DIL_HEADS = 4
N_DIL_SUB = DIL_HEADS * len(DIL_PATTERNS)
DIL_QKV = N_DIL_SUB * HEAD_DIM
DIL_OUT = DIL_HEADS * HEAD_DIM
DIL_HALF = 64
WIN_HALF = 128
WIN_Q = D_MODEL
WIN_Q_HEADS = WIN_Q // HEAD_DIM
WIN_KV_HEADS = 4
WIN_KV = WIN_KV_HEADS * HEAD_DIM
N_EXPERTS = 16
CAPACITY_FACTOR = 2
RMS_EPS = 1e-6
NEG_INF = -1e30
LOG2E = 1.4426950408889634

LANES = 128
GROUP_COLS = 3 * DIL_OUT
DIL_PAIRS = DIL_OUT // LANES
WIN_UNITS = WIN_KV_HEADS // 2
WIN_UNIT_PAIRS = WIN_Q_HEADS // (2 * WIN_UNITS)
Q_TILE = 128
WIN_TILES_PER_BODY = 4
DIL_INFLIGHT = 16
ROW_TILE = 512
MERGE_TILE = 1024
VMEM_LIMIT = 56 * 1024 * 1024

F32 = jnp.float32
BF16 = jnp.bfloat16


def _alibi_slopes(n):
    return [float(2.0 ** (-8.0 * i / n)) for i in range(1, n + 1)]


def _rms(x, g):
    return x * lax.rsqrt(jnp.mean(x * x, axis=-1, keepdims=True) + RMS_EPS) * g


def _dot(a, b):
    return jnp.dot(a, b, preferred_element_type=F32)


def _dot_nt(a, b):
    return lax.dot_general(a, b, (((1,), (1,)), ((), ())), preferred_element_type=F32)


def _left_lanes():
    return lax.broadcasted_iota(jnp.int32, (1, LANES), 1) < HEAD_DIM


def _layer_block(a, layer):
    return pl.BlockSpec((None,) + a.shape[1:], lambda *_: (layer, 0, 0), pipeline_mode=pl.Buffered(1))


_C_A = (0, GROUP_COLS, 2 * GROUP_COLS)
_C_QW = 3 * GROUP_COLS
_C_KV = _C_QW + WIN_Q
_COL_CHUNK = 512


def _proj_store(ref, xb, w_ref, c0, cw):
    for j in range(0, cw, _COL_CHUNK):
        jw = min(_COL_CHUNK, cw - j)
        ref[:, j:j + jw] = _dot(xb, w_ref[:, c0 + j:c0 + j + jw]).astype(BF16)


def _norm_proj_kernel(x_ref, g_ref, w_ref, a0_ref, a1_ref, a2_ref, qw_ref, kv_ref, xs_ref):
    _project(x_ref[...], g_ref, w_ref, a0_ref, a1_ref, a2_ref, qw_ref, kv_ref, xs_ref)


def _project(x, g_ref, w_ref, a0_ref, a1_ref, a2_ref, qw_ref, kv_ref, xs_ref):
    xn = _rms(x, g_ref[...])
    n_lane_tiles = D_MODEL // LANES
    for j in range(n_lane_tiles):
        xs_ref[j] = xn[:, j * LANES:(j + 1) * LANES]
    xb = xn.astype(BF16)
    _proj_store(a0_ref, xb, w_ref, _C_A[0], GROUP_COLS)
    for grp, ref in ((1, a1_ref), (2, a2_ref)):
        r = DIL_PATTERNS[grp][1]
        n = ROW_TILE // r
        xp = jnp.concatenate(
            [jnp.concatenate([xs_ref[j, pl.ds(c, n, stride=r), :] for j in range(n_lane_tiles)], axis=1)
             for c in range(r)], axis=0).astype(BF16)
        res = _dot(xp, w_ref[:, _C_A[grp]:_C_A[grp] + GROUP_COLS])
        for c in range(r):
            ref[:, c * GROUP_COLS:(c + 1) * GROUP_COLS] = res[c * n:(c + 1) * n].astype(BF16)
    _proj_store(qw_ref, xb, w_ref, _C_QW, WIN_Q)
    _proj_store(kv_ref, xb, w_ref, _C_KV, 2 * WIN_KV)


def _proj_outputs(m, row_index):
    out_shape, out_specs = [], []
    for _, r in DIL_PATTERNS:
        out_shape.append(jax.ShapeDtypeStruct((m // r, r * GROUP_COLS), BF16))
        out_specs.append(pl.BlockSpec((ROW_TILE // r, r * GROUP_COLS), row_index))
    for c in (WIN_Q, 2 * WIN_KV):
        out_shape.append(jax.ShapeDtypeStruct((m, c), BF16))
        out_specs.append(pl.BlockSpec((ROW_TILE, c), row_index))
    return out_shape, out_specs


def _norm_proj(x, g, w, layer):
    m = x.shape[0]
    in_specs = [pl.BlockSpec((ROW_TILE, D_MODEL), lambda i: (i, 0)), _layer_block(g, layer), _layer_block(w, layer)]
    out_shape, out_specs = _proj_outputs(m, lambda i: (i, 0))
    return pl.pallas_call(
        _norm_proj_kernel,
        grid=(m // ROW_TILE,),
        in_specs=in_specs,
        out_specs=out_specs,
        out_shape=out_shape,
        scratch_shapes=[pltpu.VMEM((D_MODEL // LANES, ROW_TILE, LANES), F32)],
        compiler_params=pltpu.CompilerParams(
            dimension_semantics=("arbitrary",), vmem_limit_bytes=VMEM_LIMIT),
        name="norm_proj",
    )(x, g, w)


def _band_bias(tq, win, off, half, slopes):
    col = lax.broadcasted_iota(jnp.int32, (tq, win), 1)
    row = lax.broadcasted_iota(jnp.int32, (tq, win), 0)
    dist = jnp.abs(col - row + off).astype(F32)
    inside = dist <= float(half)
    return jnp.concatenate([jnp.where(inside, dist * (-s * LOG2E), NEG_INF) for s in slopes], axis=0)


def _ones_blockdiag(win):
    left = lax.broadcasted_iota(jnp.int32, (2 * win, LANES), 1) < HEAD_DIM
    top = jnp.where(lax.broadcasted_iota(jnp.int32, (2 * win, LANES), 0) < win, 1.0, 0.0)
    return jnp.where(left, top, 1.0 - top).astype(BF16)


def _pair_scores(q_tiles, k_win, bias):
    left = _left_lanes()
    zero = jnp.zeros((), BF16)
    rows = []
    for q in q_tiles:
        rows += [jnp.where(left, q, zero), jnp.where(left, zero, q)]
    return _dot_nt(jnp.concatenate(rows, axis=0), k_win) + bias


def _pair_values(s_ref, v_win, ones_bd, sink_ref, p_ref, aux_ref):
    win = v_win.shape[0]
    n = s_ref.shape[0] // (2 * Q_TILE)
    left = _left_lanes()
    zero = jnp.zeros((), BF16)
    for i in range(n):
        ms = []
        for side in range(2):
            rows = pl.ds((2 * i + side) * Q_TILE, Q_TILE)
            s = s_ref[rows, :]
            m = jnp.broadcast_to(jnp.max(s, axis=-1, keepdims=True), (Q_TILE, LANES))
            if sink_ref is not None:
                m = jnp.maximum(m, sink_ref[rows, :])
            p = jnp.exp2(s - jnp.concatenate([m] * (win // LANES), axis=1)).astype(BF16)
            p_ref[i * Q_TILE:(i + 1) * Q_TILE, side * win:(side + 1) * win] = p
            ms.append(m)
        m_pair = jnp.where(left, ms[0], ms[1])
        if sink_ref is not None:
            sink_pair = jnp.where(left, sink_ref[pl.ds(2 * i * Q_TILE, Q_TILE), :],
                                  sink_ref[pl.ds((2 * i + 1) * Q_TILE, Q_TILE), :])
            aux_ref[i * Q_TILE:(i + 1) * Q_TILE, :] = jnp.exp2(sink_pair - m_pair)
        else:
            aux_ref[i * Q_TILE:(i + 1) * Q_TILE, :] = m_pair
    vbd = jnp.concatenate([jnp.where(left, v_win, zero), jnp.where(left, zero, v_win)], axis=0)
    on = _dot(p_ref[...], jnp.concatenate([vbd, ones_bd], axis=1))
    return on[:, :LANES], on[:, LANES:], aux_ref[...]


def _tile_variant(t, n_tiles):
    if isinstance(t, int):
        return 0 if t == 0 else (2 if t == n_tiles - 1 else 1)
    return jnp.where(t == 0, 0, jnp.where(t == n_tiles - 1, 2, 1))


def _dil_geometry(seq_full, grp):
    r = DIL_PATTERNS[grp][1]
    seq = seq_full // r
    win = min(seq, Q_TILE + 2 * DIL_HALF)
    n_tiles = seq // Q_TILE
    pad = (win - Q_TILE) // 2
    offs = [0] if n_tiles == 1 else [0, -pad, -2 * pad]
    return r, seq, win, n_tiles, pad, offs


def _dilated_kernel(seq_full, a0_ref, a1_ref, a2_ref, out_ref, o_nat, l_nat, b0, b1, b2, ones_ref, ones2_ref,
                    s_wide, s_narrow, p_wide, p_narrow, aux_ref):
    a_refs = (a0_ref, a1_ref, a2_ref)
    bias_refs = (b0, b1, b2)
    s_refs = (s_wide, s_wide, s_narrow)
    p_refs = (p_wide, p_wide, p_narrow)
    slopes = _alibi_slopes(N_DIL_SUB)

    @pl.when(pl.program_id(0) == 0)
    def _():
        for grp in range(3):
            r, seq, win, n_tiles, pad, offs = _dil_geometry(seq_full, grp)
            for v, off in enumerate(offs):
                for pr in range(DIL_PAIRS):
                    hs = slopes[grp * DIL_HEADS + 2 * pr:grp * DIL_HEADS + 2 * pr + 2]
                    bias_refs[grp][v, pr] = _band_bias(Q_TILE, win, off, DIL_HALF, [s * r for s in hs])
        ones_ref[...] = _ones_blockdiag(_dil_geometry(seq_full, 0)[2])
        ones2_ref[...] = _ones_blockdiag(_dil_geometry(seq_full, 2)[2])

    def pair_tile(grp, c, t, pr, slot):
        r, seq, win, n_tiles, pad, offs = _dil_geometry(seq_full, grp)
        a_ref = a_refs[grp]
        ones_bd_ref = ones2_ref if grp == 2 else ones_ref
        if n_tiles == 1:
            q0, ks, var = 0, 0, 0
        else:
            q0 = pl.multiple_of(t * Q_TILE, Q_TILE)
            ks = pl.multiple_of(jnp.clip(q0 - pad, 0, seq - win), DIL_HALF)
            var = _tile_variant(t, n_tiles)
        lo = c * GROUP_COLS + pr * LANES
        q = a_ref[pl.ds(q0, Q_TILE), lo:lo + LANES]
        k = a_ref[pl.ds(ks, win), DIL_OUT + lo:DIL_OUT + lo + LANES]
        v = a_ref[pl.ds(ks, win), 2 * DIL_OUT + lo:2 * DIL_OUT + lo + LANES]
        s_ref = s_refs[grp].at[slot]
        s_ref[...] = _pair_scores([q], k, bias_refs[grp][var, pr])
        num, den, m = _pair_values(s_ref, v, ones_bd_ref[...], None, p_refs[grp].at[slot], aux_ref.at[slot])
        rows = pl.ds(q0, Q_TILE) if r == 1 else pl.ds(c + r * q0, Q_TILE, stride=r)
        o_nat[grp, pr, rows, :] = num / den
        l_nat[grp, pr, rows, :] = m + jnp.log2(den)

    for grp in range(3):
        r, seq, win, n_tiles, pad, offs = _dil_geometry(seq_full, grp)
        per_body = DIL_INFLIGHT // DIL_PAIRS
        if n_tiles == 1:
            for c in range(r):
                for pr in range(DIL_PAIRS):
                    pair_tile(grp, c, 0, pr, c * DIL_PAIRS + pr)
        elif r >= per_body:
            for c0 in range(0, r, per_body):
                def body(t, carry, grp=grp, c0=c0):
                    for j in range(per_body):
                        for pr in range(DIL_PAIRS):
                            pair_tile(grp, c0 + j, t, pr, j * DIL_PAIRS + pr)
                    return carry
                lax.fori_loop(0, n_tiles, body, 0)
        else:
            tiles_per_body = per_body // r
            def body(i, carry, grp=grp, r=r, tiles_per_body=tiles_per_body):
                for j in range(tiles_per_body):
                    for c in range(r):
                        for pr in range(DIL_PAIRS):
                            pair_tile(grp, c, i * tiles_per_body + j, pr, (j * r + c) * DIL_PAIRS + pr)
                return carry
            lax.fori_loop(0, n_tiles // tiles_per_body, body, 0)

    def combine(i, carry):
        rows = pl.ds(pl.multiple_of(i * ROW_TILE, ROW_TILE), ROW_TILE)
        for pr in range(DIL_PAIRS):
            ls = [l_nat[g, pr, rows, :] for g in range(3)]
            mx = jnp.maximum(jnp.maximum(ls[0], ls[1]), ls[2])
            es = [jnp.exp2(l - mx) for l in ls]
            num = es[0] * o_nat[0, pr, rows, :] + es[1] * o_nat[1, pr, rows, :] + es[2] * o_nat[2, pr, rows, :]
            out_ref[rows, pr * LANES:(pr + 1) * LANES] = (num / (es[0] + es[1] + es[2])).astype(BF16)
        return carry

    lax.fori_loop(0, seq_full // ROW_TILE, combine, 0)


def _dilated_attention(a0, a1, a2, batch, seq_full):
    views, in_specs, bias_shapes = [], [], []
    for grp, a in enumerate((a0, a1, a2)):
        r, seq, win, n_tiles, pad, offs = _dil_geometry(seq_full, grp)
        views.append(a.reshape(batch, seq, r * GROUP_COLS))
        in_specs.append(pl.BlockSpec((None, seq, r * GROUP_COLS), lambda b: (b, 0, 0)))
        bias_shapes.append(pltpu.VMEM((len(offs), DIL_PAIRS, 2 * Q_TILE, win), F32))
    win0 = _dil_geometry(seq_full, 0)[2]
    win2 = _dil_geometry(seq_full, 2)[2]
    n_straight = DIL_PATTERNS[2][1] * DIL_PAIRS
    return pl.pallas_call(
        functools.partial(_dilated_kernel, seq_full),
        grid=(batch,),
        in_specs=in_specs,
        out_specs=pl.BlockSpec((None, seq_full, DIL_OUT), lambda b: (b, 0, 0)),
        out_shape=jax.ShapeDtypeStruct((batch, seq_full, DIL_OUT), BF16),
        scratch_shapes=[pltpu.VMEM((3, DIL_PAIRS, seq_full, LANES), F32),
                        pltpu.VMEM((3, DIL_PAIRS, seq_full, LANES), F32)]
        + bias_shapes + [pltpu.VMEM((2 * win0, LANES), BF16), pltpu.VMEM((2 * win2, LANES), BF16)]
        + [pltpu.VMEM((n, 2 * Q_TILE, w), F32) for n, w in ((DIL_INFLIGHT, win0), (n_straight, win2))]
        + [pltpu.VMEM((n, Q_TILE, 2 * w), BF16) for n, w in ((DIL_INFLIGHT, win0), (n_straight, win2))]
        + [pltpu.VMEM((max(DIL_INFLIGHT, n_straight), Q_TILE, LANES), F32)],
        compiler_params=pltpu.CompilerParams(
            dimension_semantics=("arbitrary",), vmem_limit_bytes=VMEM_LIMIT),
        name="dilated",
    )(*views)


WIN_WINDOW = Q_TILE + 2 * WIN_HALF


def _window_pair_heads():
    group = WIN_Q_HEADS // WIN_KV_HEADS
    pairs = []
    for u in range(WIN_UNITS):
        for i in range(WIN_UNIT_PAIRS):
            pairs.append((2 * u * group + i, (2 * u + 1) * group + i))
    return pairs


def _window_kernel(seq, q_hbm, kv_hbm, sink_ref, o_ref, bias_ref, ones_ref, sink_rows_ref, s_ref, p_ref,
                   aux_ref, q_buf, kv_buf, sems):
    n_tiles = seq // Q_TILE
    slopes = _alibi_slopes(WIN_Q_HEADS)
    pair_heads = _window_pair_heads()
    b = pl.program_id(0)
    last_seq = b + 1 == pl.num_programs(0)
    slot = b % 2

    def fetch(seq_idx, sl):
        return (pltpu.make_async_copy(q_hbm.at[seq_idx], q_buf.at[sl], sems.at[0, sl]),
                pltpu.make_async_copy(kv_hbm.at[seq_idx], kv_buf.at[sl], sems.at[1, sl]))

    @pl.when(b == 0)
    def _():
        for u in range(WIN_UNITS):
            heads = [h for pr in pair_heads[u * WIN_UNIT_PAIRS:(u + 1) * WIN_UNIT_PAIRS] for h in pr]
            for v, off in enumerate((0, -WIN_HALF, -2 * WIN_HALF)):
                bias_ref[v, u] = _band_bias(Q_TILE, WIN_WINDOW, off, WIN_HALF, [slopes[h] for h in heads])
            sink_rows_ref[u] = jnp.concatenate(
                [jnp.broadcast_to(sink_ref[0:1, h:h + 1] * LOG2E, (Q_TILE, LANES)) for h in heads], axis=0)
        ones_ref[...] = _ones_blockdiag(WIN_WINDOW)

    def rows_of(t):
        if isinstance(t, int):
            return t * Q_TILE, min(max(t * Q_TILE - WIN_HALF, 0), seq - WIN_WINDOW)
        q0 = pl.multiple_of(t * Q_TILE, Q_TILE)
        ks = pl.multiple_of(jnp.clip(q0 - WIN_HALF, 0, seq - WIN_WINDOW), Q_TILE)
        return q0, ks

    def unit_cols(u):
        return [(u * WIN_UNIT_PAIRS + i) * LANES for i in range(WIN_UNIT_PAIRS)]

    def score_stage(sl, t, k_set):
        q0, ks = rows_of(t)
        var = _tile_variant(t, n_tiles)
        for u in range(WIN_UNITS):
            qs = [q_buf[sl, pl.ds(q0, Q_TILE), c:c + LANES] for c in unit_cols(u)]
            k = kv_buf[sl, pl.ds(ks, WIN_WINDOW), u * LANES:(u + 1) * LANES]
            s_ref[k_set * WIN_UNITS + u] = _pair_scores(qs, k, bias_ref[var, u])

    def value_stage(sl, t, k_set):
        q0, ks = rows_of(t)
        for u in range(WIN_UNITS):
            work = k_set * WIN_UNITS + u
            v = kv_buf[sl, pl.ds(ks, WIN_WINDOW), WIN_KV + u * LANES:WIN_KV + (u + 1) * LANES]
            num, den, sink_term = _pair_values(s_ref.at[work], v, ones_ref[...], sink_rows_ref.at[u],
                                               p_ref.at[work], aux_ref.at[work])
            o = (num / (den + sink_term)).astype(BF16)
            for i, c in enumerate(unit_cols(u)):
                o_ref[pl.ds(q0, Q_TILE), c:c + LANES] = o[i * Q_TILE:(i + 1) * Q_TILE]

    per_body = WIN_TILES_PER_BODY

    @pl.when(b == 0)
    def _():
        for cp in fetch(0, 0):
            cp.start()
        for cp in fetch(0, 0):
            cp.wait()
        score_stage(0, 0, 0)

    @pl.when(jnp.logical_not(last_seq))
    def _():
        for cp in fetch(b + 1, 1 - slot):
            cp.start()

    def body(i, carry):
        t = per_body * i
        for j in range(per_body):
            score_stage(slot, t + j + 1, (j + 1) % per_body)
            value_stage(slot, t + j, j)
        return carry

    lax.fori_loop(0, n_tiles // per_body - 1, body, 0)

    @pl.when(jnp.logical_not(last_seq))
    def _():
        for cp in fetch(b + 1, 1 - slot):
            cp.wait()

    t = n_tiles - per_body
    for j in range(per_body):
        if j + 1 < per_body:
            score_stage(slot, t + j + 1, j + 1)
        else:
            score_stage(1 - slot, 0, 0)
        value_stage(slot, t + j, j)


def _window_attention(q, kv, sink, batch, seq):
    unit_rows = 2 * WIN_UNIT_PAIRS * Q_TILE
    return pl.pallas_call(
        functools.partial(_window_kernel, seq),
        grid=(batch,),
        in_specs=[pl.BlockSpec(memory_space=pl.ANY),
                  pl.BlockSpec(memory_space=pl.ANY),
                  pl.BlockSpec((1, WIN_Q_HEADS), lambda b: (0, 0))],
        out_specs=pl.BlockSpec((None, seq, WIN_Q), lambda b: (b, 0, 0)),
        out_shape=jax.ShapeDtypeStruct((batch, seq, WIN_Q), BF16),
        scratch_shapes=[pltpu.VMEM((3, WIN_UNITS, unit_rows, WIN_WINDOW), F32),
                        pltpu.VMEM((2 * WIN_WINDOW, LANES), BF16),
                        pltpu.VMEM((WIN_UNITS, unit_rows, LANES), F32),
                        pltpu.VMEM((WIN_TILES_PER_BODY * WIN_UNITS, unit_rows, WIN_WINDOW), F32),
                        pltpu.VMEM((WIN_TILES_PER_BODY * WIN_UNITS, unit_rows // 2, 2 * WIN_WINDOW), BF16),
                        pltpu.VMEM((WIN_TILES_PER_BODY * WIN_UNITS, unit_rows // 2, LANES), F32),
                        pltpu.VMEM((2, seq, WIN_Q), BF16),
                        pltpu.VMEM((2, seq, 2 * WIN_KV), BF16),
                        pltpu.SemaphoreType.DMA((2, 2))],
        compiler_params=pltpu.CompilerParams(
            dimension_semantics=("arbitrary",), vmem_limit_bytes=VMEM_LIMIT),
        name="window",
    )(q.reshape(batch, seq, WIN_Q), kv.reshape(batch, seq, 2 * WIN_KV), sink)


def _merge_kernel(oa_ref, ow_ref, h_ref, g_ref, wgate_ref, bgate_ref, wa_ref, wb_ref, wo_ref, out_ref):
    h = h_ref[...]
    xb = _rms(h, g_ref[...]).astype(BF16)
    merged = None
    for br, (o_ref, w_ref) in enumerate(((oa_ref, wa_ref), (ow_ref, wb_ref))):
        cols = slice(br * D_MODEL, (br + 1) * D_MODEL)
        gate = jax.nn.sigmoid(_dot(xb, wgate_ref[:, cols]) + bgate_ref[:, cols])
        term = gate * _dot(o_ref[...], w_ref[...])
        merged = term if merged is None else merged + term
    out_ref[...] = h + _dot(merged.astype(BF16), wo_ref[...])


def _merge_out(oa, ow, h, g, wgate, bgate, wa, wb, wo, layer):
    m = h.shape[0]
    row = lambda c: pl.BlockSpec((MERGE_TILE, c), lambda i: (i, 0))
    return pl.pallas_call(
        _merge_kernel,
        grid=(m // MERGE_TILE,),
        in_specs=[row(DIL_OUT), row(WIN_Q), row(D_MODEL)]
        + [_layer_block(a, layer) for a in (g, wgate, bgate, wa, wb, wo)],
        out_specs=row(D_MODEL),
        out_shape=jax.ShapeDtypeStruct((m, D_MODEL), F32),
        compiler_params=pltpu.CompilerParams(
            dimension_semantics=("arbitrary",), vmem_limit_bytes=VMEM_LIMIT),
        name="merge_out",
    )(oa, ow, h, g, wgate, bgate, wa, wb, wo)


TOKEN_TILE = 256
CUM_CHUNK = TOKEN_TILE
SLOT_WINDOW = 64
SLOT_ALIGN = 16


def _prefix_exclusive(mask_f, tri):
    e, s = mask_f.shape
    carry = jnp.zeros((e, 1), F32)
    parts, carries = [], []
    for j in range(0, s, CUM_CHUNK):
        blk = mask_f[:, j:j + CUM_CHUNK]
        carries.append(carry)
        parts.append(_dot(blk.astype(BF16), tri) + carry)
        carry = carry + jnp.sum(blk, axis=-1, keepdims=True)
    return jnp.concatenate(parts, axis=-1), carries + [carry]


def _route_kernel(cap, h_ref, g_ref, whi_ref, wlo_ref, hn_ref, rank_ref, aff_ref, starts_ref):
    seq = h_ref.shape[0]
    hn = _rms(h_ref[...], g_ref[...])
    hn_hi = hn.astype(BF16)
    hn_lo = (hn - hn_hi.astype(F32)).astype(BF16)
    hn_ref[...] = hn_hi
    both = _dot_nt(jnp.concatenate([whi_ref[...], wlo_ref[...]], axis=0), hn_hi)
    logits = both[:N_EXPERTS] + both[N_EXPERTS:] + _dot_nt(whi_ref[...], hn_lo)
    mx = jnp.max(logits, axis=0, keepdims=True)
    ex = jnp.exp(logits - mx)
    aff = ex / jnp.sum(ex, axis=0, keepdims=True)
    bits = pltpu.bitcast(aff, jnp.int32)

    def enough(t):
        return jnp.sum(jnp.where(bits >= t, 1.0, 0.0), axis=-1, keepdims=True) >= float(cap)

    def search4(_, c):
        lo, hi = c
        q = (hi - lo) >> 2
        m1, m2, m3 = lo + q, lo + 2 * q, lo + 3 * q
        ok1, ok2, ok3 = enough(m1), enough(m2), enough(m3)
        return (jnp.where(ok3, m3, jnp.where(ok2, m2, jnp.where(ok1, m1, lo))),
                jnp.where(ok3, hi, jnp.where(ok2, m3, jnp.where(ok1, m2, m1))))

    def search2(_, c):
        lo, hi = c
        mid = lo + ((hi - lo) >> 1)
        ok = enough(mid)
        return jnp.where(ok, mid, lo), jnp.where(ok, hi, mid)

    lo0 = jnp.zeros((N_EXPERTS, 1), jnp.int32)
    hi0 = jnp.full((N_EXPERTS, 1), 0x3F800001, jnp.int32)
    thr, _ = lax.fori_loop(0, 3, search2, lax.fori_loop(0, 15, search4, (lo0, hi0)))

    r_i = lax.broadcasted_iota(jnp.int32, (CUM_CHUNK, CUM_CHUNK), 0)
    c_i = lax.broadcasted_iota(jnp.int32, (CUM_CHUNK, CUM_CHUNK), 1)
    tri = jnp.where(r_i < c_i, 1.0, 0.0).astype(BF16)
    gt = jnp.where(bits > thr, 1.0, 0.0)
    eq = jnp.where(bits == thr, 1.0, 0.0)
    need = float(cap) - jnp.sum(gt, axis=-1, keepdims=True)
    tie_rank, _ = _prefix_exclusive(eq, tri)
    sel = gt + eq * jnp.where(tie_rank < need, 1.0, 0.0)
    slot, starts = _prefix_exclusive(sel, tri)
    rank = jnp.where(sel > 0.0, slot, -1.0)
    for t in range(seq // TOKEN_TILE):
        rank_ref[t] = rank[:, t * TOKEN_TILE:(t + 1) * TOKEN_TILE]
        aff_ref[t] = aff[:, t * TOKEN_TILE:(t + 1) * TOKEN_TILE]
    lane = lax.broadcasted_iota(jnp.int32, (N_EXPERTS, LANES), 1)
    acc = jnp.zeros((N_EXPERTS, LANES), F32)
    for t, c in enumerate(starts):
        acc = acc + jnp.where(lane == t, c, 0.0)
    starts_ref[...] = acc


def _route(h, g, w_hi, w_lo, layer, batch, seq, cap):
    n_tiles = seq // TOKEN_TILE
    return pl.pallas_call(
        functools.partial(_route_kernel, cap),
        grid=(batch,),
        in_specs=[pl.BlockSpec((None, seq, D_MODEL), lambda b: (b, 0, 0))]
        + [_layer_block(a, layer) for a in (g, w_hi, w_lo)],
        out_specs=[pl.BlockSpec((None, seq, D_MODEL), lambda b: (b, 0, 0)),
                   pl.BlockSpec((None, n_tiles, N_EXPERTS, TOKEN_TILE), lambda b: (b, 0, 0, 0)),
                   pl.BlockSpec((None, n_tiles, N_EXPERTS, TOKEN_TILE), lambda b: (b, 0, 0, 0)),
                   pl.BlockSpec((None, N_EXPERTS, LANES), lambda b: (b, 0, 0))],
        out_shape=[jax.ShapeDtypeStruct((batch, seq, D_MODEL), BF16),
                   jax.ShapeDtypeStruct((batch, n_tiles, N_EXPERTS, TOKEN_TILE), F32),
                   jax.ShapeDtypeStruct((batch, n_tiles, N_EXPERTS, TOKEN_TILE), F32),
                   jax.ShapeDtypeStruct((batch, N_EXPERTS, LANES), F32)],
        compiler_params=pltpu.CompilerParams(
            dimension_semantics=("arbitrary",), vmem_limit_bytes=VMEM_LIMIT),
        name="route",
    )(h.reshape(batch, seq, D_MODEL), g, w_hi, w_lo)


def _slot_rows(cap):
    return cap + SLOT_WINDOW


def _window_starts(win_ref, b, t, p, cap):
    base = b * (N_EXPERTS * LANES) + t
    return [pl.multiple_of(jnp.minimum(win_ref[base + e * LANES] + p * SLOT_WINDOW, cap), SLOT_ALIGN)
            for e in range(N_EXPERTS)]


def _window_hits(rank_tile, starts):
    rows = lax.broadcasted_iota(jnp.int32, (SLOT_WINDOW, TOKEN_TILE), 0).astype(F32)
    return [(rank_tile[e:e + 1, :] - starts[e].astype(F32)) == rows for e in range(N_EXPERTS)]


def _one_hot(hits):
    return jnp.concatenate([jnp.where(h, 1.0, 0.0).astype(BF16) for h in hits], axis=0)


def _gather_kernel(cap, win_ref, npass_ref, hn_ref, rank_ref, aff_ref, xe_ref, gate_ref):
    b = pl.program_id(0)
    xe_ref[...] = jnp.zeros_like(xe_ref)
    gate_ref[...] = jnp.zeros_like(gate_ref)
    for t in range(hn_ref.shape[0] // TOKEN_TILE):

        def one_pass(p, carry, t=t):
            starts = _window_starts(win_ref, b, t, p, cap)
            hits = _window_hits(rank_ref[t], starts)
            rows = _dot(_one_hot(hits), hn_ref[t * TOKEN_TILE:(t + 1) * TOKEN_TILE, :]).astype(BF16)
            aff_tile = aff_ref[t]
            for e in range(N_EXPERTS):
                win = pl.ds(starts[e], SLOT_WINDOW)
                xe_ref[e, win, :] += rows[e * SLOT_WINDOW:(e + 1) * SLOT_WINDOW]
                g = jnp.sum(jnp.where(hits[e], aff_tile[e:e + 1, :], 0.0), axis=-1, keepdims=True)
                gate_ref[e, win, :] += jnp.broadcast_to(g, (SLOT_WINDOW, LANES))
            return carry

        lax.fori_loop(0, npass_ref[b * LANES + t], one_pass, 0)


def _gather(hn, rank, aff, win, npass, cap):
    batch, seq, _ = hn.shape
    n_tiles = seq // TOKEN_TILE
    grid_spec = pltpu.PrefetchScalarGridSpec(
        num_scalar_prefetch=2,
        grid=(batch,),
        in_specs=[pl.BlockSpec((None, seq, D_MODEL), lambda b, *_: (b, 0, 0)),
                  pl.BlockSpec((None, n_tiles, N_EXPERTS, TOKEN_TILE), lambda b, *_: (b, 0, 0, 0)),
                  pl.BlockSpec((None, n_tiles, N_EXPERTS, TOKEN_TILE), lambda b, *_: (b, 0, 0, 0))],
        out_specs=[pl.BlockSpec((None, N_EXPERTS, _slot_rows(cap), D_MODEL), lambda b, *_: (b, 0, 0, 0)),
                   pl.BlockSpec((None, N_EXPERTS, _slot_rows(cap), LANES), lambda b, *_: (b, 0, 0, 0))])
    return pl.pallas_call(
        functools.partial(_gather_kernel, cap),
        grid_spec=grid_spec,
        out_shape=[jax.ShapeDtypeStruct((batch, N_EXPERTS, _slot_rows(cap), D_MODEL), BF16),
                   jax.ShapeDtypeStruct((batch, N_EXPERTS, _slot_rows(cap), LANES), F32)],
        compiler_params=pltpu.CompilerParams(
            dimension_semantics=("arbitrary",), vmem_limit_bytes=VMEM_LIMIT),
        name="gather",
    )(win, npass, hn, rank, aff)


FFN_SEQS = 4


def _ffn_kernel(cap, xe_ref, gate_ref, wg_ref, wu_ref, wd_ref, y_ref, wg_b, wu_b, wd_b):
    d = xe_ref.shape[-1]

    @pl.when(pl.program_id(1) == 0)
    def _():
        for src, dst in ((wg_ref, wg_b), (wu_ref, wu_b), (wd_ref, wd_b)):
            dst[...] = src[...].astype(BF16)

    xe = xe_ref[:, :cap, :].reshape(FFN_SEQS * cap, d)
    a = _dot(xe, wg_b[...])
    u = _dot(xe, wu_b[...])
    y = _dot((jax.nn.silu(a) * u).astype(BF16), wd_b[...])
    gate = gate_ref[:, :cap, :].reshape(FFN_SEQS * cap, LANES)
    y = y * jnp.concatenate([gate] * (d // LANES), axis=1)
    y_ref[:, :cap, :] = y.astype(BF16).reshape(FFN_SEQS, cap, d)
    y_ref[:, cap:, :] = jnp.zeros((FFN_SEQS, SLOT_WINDOW, d), BF16)


def _ffn(xe, gate, wg, wu, wd, layer, cap):
    batch = xe.shape[0]
    d_exp = wg.shape[-1]
    slots = pl.BlockSpec((FFN_SEQS, None, _slot_rows(cap), D_MODEL), lambda e, i: (i, e, 0, 0))
    return pl.pallas_call(
        functools.partial(_ffn_kernel, cap),
        grid=(N_EXPERTS, batch // FFN_SEQS),
        in_specs=[slots,
                  pl.BlockSpec((FFN_SEQS, None, _slot_rows(cap), LANES), lambda e, i: (i, e, 0, 0)),
                  pl.BlockSpec((None, None, D_MODEL, d_exp), lambda e, i: (layer, e, 0, 0)),
                  pl.BlockSpec((None, None, D_MODEL, d_exp), lambda e, i: (layer, e, 0, 0)),
                  pl.BlockSpec((None, None, d_exp, D_MODEL), lambda e, i: (layer, e, 0, 0))],
        out_specs=slots,
        out_shape=jax.ShapeDtypeStruct(xe.shape, BF16),
        scratch_shapes=[pltpu.VMEM((D_MODEL, d_exp), BF16), pltpu.VMEM((D_MODEL, d_exp), BF16),
                        pltpu.VMEM((d_exp, D_MODEL), BF16)],
        compiler_params=pltpu.CompilerParams(
            dimension_semantics=("arbitrary", "arbitrary"), vmem_limit_bytes=VMEM_LIMIT),
        name="ffn",
    )(xe, gate, wg, wu, wd)


TILES_PER_ROW_TILE = ROW_TILE // TOKEN_TILE


def _scatter_kernel(cap, project, win_ref, npass_ref, y_hbm, rank_ref, h_ref, g_ref, *rest):
    if project:
        w_ref, out_ref, a0_ref, a1_ref, a2_ref, qw_ref, kv_ref, xs_ref, y_buf, y_sem = rest
    else:
        out_ref, y_buf, y_sem = rest
    b = pl.program_id(0)
    j = pl.program_id(1)
    slot = b % 2

    def y_copy(seq_idx, buf_slot):
        return pltpu.make_async_copy(y_hbm.at[seq_idx], y_buf.at[buf_slot], y_sem.at[buf_slot])

    @pl.when((b == 0) & (j == 0))
    def _():
        y_copy(0, 0).start()

    @pl.when(j == 0)
    def _():
        y_copy(b, slot).wait()

    @pl.when((j == 0) & (b + 1 < pl.num_programs(0)))
    def _():
        y_copy(b + 1, 1 - slot).start()

    y_ref = y_buf.at[slot]
    out_ref[...] = h_ref[...]
    for tt in range(TILES_PER_ROW_TILE):
        t = j * TILES_PER_ROW_TILE + tt
        rows = slice(tt * TOKEN_TILE, (tt + 1) * TOKEN_TILE)

        def one_pass(p, carry, t=t, tt=tt, rows=rows):
            starts = _window_starts(win_ref, b, t, p, cap)
            put = _one_hot(_window_hits(rank_ref[tt], starts))
            yw = jnp.concatenate([y_ref[e, pl.ds(starts[e], SLOT_WINDOW), :] for e in range(N_EXPERTS)],
                                 axis=0)
            out_ref[rows, :] += lax.dot_general(put, yw, (((0,), (0,)), ((), ())), preferred_element_type=F32)
            return carry

        lax.fori_loop(0, npass_ref[b * LANES + t], one_pass, 0)
    if project:
        _project(out_ref[...], g_ref, w_ref, a0_ref, a1_ref, a2_ref, qw_ref, kv_ref, xs_ref)
    else:
        out_ref[...] = _rms(out_ref[...], g_ref[...])


def _scatter(y, rank, h, win, npass, cap, g, layer, w_next=None):
    batch, n_tiles = rank.shape[:2]
    seq = h.shape[1]
    steps = n_tiles // TILES_PER_ROW_TILE
    project = w_next is not None
    in_specs = [pl.BlockSpec(memory_space=pl.ANY),
                pl.BlockSpec((None, TILES_PER_ROW_TILE, N_EXPERTS, TOKEN_TILE), lambda b, j, *_: (b, j, 0, 0)),
                pl.BlockSpec((None, ROW_TILE, D_MODEL), lambda b, j, *_: (b, j, 0)),
                _layer_block(g, layer)]
    out_shape = [jax.ShapeDtypeStruct(h.shape, F32)]
    out_specs = [pl.BlockSpec((None, ROW_TILE, D_MODEL), lambda b, j, *_: (b, j, 0))]
    args = [win, npass, y, rank, h, g]
    scratch = []
    if project:
        in_specs.append(_layer_block(w_next, layer))
        args.append(w_next)
        proj_shape, proj_specs = _proj_outputs(batch * seq, lambda b, j, *_: (b * steps + j, 0))
        out_shape += proj_shape
        out_specs += proj_specs
        scratch = [pltpu.VMEM((D_MODEL // LANES, ROW_TILE, LANES), F32)]
    scratch += [pltpu.VMEM((2,) + y.shape[1:], BF16), pltpu.SemaphoreType.DMA((2,))]
    grid_spec = pltpu.PrefetchScalarGridSpec(
        num_scalar_prefetch=2, grid=(batch, steps), in_specs=in_specs, out_specs=out_specs,
        scratch_shapes=scratch)
    return pl.pallas_call(
        functools.partial(_scatter_kernel, cap, project),
        grid_spec=grid_spec,
        out_shape=out_shape,
        compiler_params=pltpu.CompilerParams(
            dimension_semantics=("arbitrary", "arbitrary"), vmem_limit_bytes=VMEM_LIMIT),
        name="scatter_proj" if project else "scatter_final",
    )(*args)


def _slot_windows(starts, n_tiles, cap):
    s = starts.astype(jnp.int32)
    first = (s // SLOT_ALIGN) * SLOT_ALIGN
    span = s[:, :, 1:n_tiles + 1] - first[:, :, :n_tiles]
    npass = jnp.maximum(jnp.max(-(-span // SLOT_WINDOW), axis=1), 1)
    npass = jnp.pad(npass, ((0, 0), (0, LANES - n_tiles)))
    return first.reshape(-1), npass.reshape(-1)


def _arrange_w_in(w):
    scale = LOG2E * HEAD_DIM ** -0.5
    qa, ka, va = w[..., :DIL_QKV], w[..., DIL_QKV:2 * DIL_QKV], w[..., 2 * DIL_QKV:3 * DIL_QKV]
    rest = w[..., 3 * DIL_QKV:]
    parts = []
    for g in range(len(DIL_PATTERNS)):
        sl = slice(g * DIL_OUT, (g + 1) * DIL_OUT)
        parts += [qa[..., sl] * scale, ka[..., sl], va[..., sl]]
    for pr in _window_pair_heads():
        parts += [rest[..., h * HEAD_DIM:(h + 1) * HEAD_DIM] * scale for h in pr]
    parts.append(rest[..., WIN_Q:WIN_Q + 2 * WIN_KV])
    return jnp.concatenate(parts, axis=-1).astype(BF16), rest[..., WIN_Q + 2 * WIN_KV:].astype(BF16)


def _arrange_w_branch_b(w):
    return jnp.concatenate([w[:, h * HEAD_DIM:(h + 1) * HEAD_DIM] for pr in _window_pair_heads() for h in pr],
                           axis=1).astype(BF16)


def kernel(x, norm_mix, w_in, w_branch_a, w_branch_b, b_gate, sink_logit, w_out, norm_ffn, w_router,
           w_expert_gate, w_expert_up, w_expert_down, norm_final):
    batch, seq, d = x.shape
    depth = w_in.shape[0]
    cap = CAPACITY_FACTOR * seq // N_EXPERTS
    m = batch * seq
    w_attn, w_gates = _arrange_w_in(w_in)
    wa, wb, wo = w_branch_a.astype(BF16), _arrange_w_branch_b(w_branch_b), w_out.astype(BF16)
    g_mix, g_ffn, bg = norm_mix[:, None, :], norm_ffn[:, None, :], b_gate[:, None, :]
    wr = jnp.swapaxes(w_router, 1, 2)
    wr_hi = wr.astype(BF16)
    wr_lo = (wr - wr_hi.astype(F32)).astype(BF16)
    h = x.reshape(m, d)
    proj = _norm_proj(h, g_mix, w_attn, 0)
    for l in range(depth):
        a0, a1, a2, qw, kv = proj
        oa = _dilated_attention(a0, a1, a2, batch, seq).reshape(m, DIL_OUT)
        ow = _window_attention(qw, kv, sink_logit[l][None, :], batch, seq).reshape(m, WIN_Q)
        h = _merge_out(oa, ow, h, g_mix, w_gates, bg, wa, wb, wo, l)
        hn, rank, aff, starts = _route(h, g_ffn, wr_hi, wr_lo, l, batch, seq, cap)
        win, npass = _slot_windows(starts, seq // TOKEN_TILE, cap)
        xe, gate = _gather(hn, rank, aff, win, npass, cap)
        y = _ffn(xe, gate, w_expert_gate, w_expert_up, w_expert_down, l, cap)
        h3 = h.reshape(batch, seq, d)
        if l + 1 == depth:
            return _scatter(y, rank, h3, win, npass, cap, norm_final[None, None, :], 0)[0]
        h3, *proj = _scatter(y, rank, h3, win, npass, cap, g_mix, l + 1, w_attn)
        h = h3.reshape(m, d)
```

```python
import functools

import jax
import jax.numpy as jnp
from jax import lax
from jax.experimental import pallas as pl
from jax.experimental.pallas import tpu as pltpu

D_MODEL = 1024
HEAD_DIM = 64
DIL_PATTERNS = ((128, 1), (512, 4), (2048, 16))
DIL_HEADS = 4
N_DIL_SUB = DIL_HEADS * len(DIL_PATTERNS)
DIL_QKV = N_DIL_SUB * HEAD_DIM
DIL_OUT = DIL_HEADS * HEAD_DIM
DIL_HALF = 64
WIN_HALF = 128
WIN_Q = D_MODEL
WIN_Q_HEADS = WIN_Q // HEAD_DIM
WIN_KV_HEADS = 4
WIN_KV = WIN_KV_HEADS * HEAD_DIM
N_EXPERTS = 16
CAPACITY_FACTOR = 2
RMS_EPS = 1e-6
NEG_INF = -1e30
LOG2E = 1.4426950408889634

LANES = 128
GROUP_COLS = 3 * DIL_OUT
DIL_PAIRS = DIL_OUT // LANES
WIN_UNITS = WIN_KV_HEADS // 2
WIN_UNIT_PAIRS = WIN_Q_HEADS // (2 * WIN_UNITS)
Q_TILE = 128
WIN_TILES_PER_BODY = 4
DIL_INFLIGHT = 16
ROW_TILE = 512
MERGE_TILE = 1024
VMEM_LIMIT = 56 * 1024 * 1024

F32 = jnp.float32
BF16 = jnp.bfloat16


def _alibi_slopes(n):
    return [float(2.0 ** (-8.0 * i / n)) for i in range(1, n + 1)]


def _rms(x, g):
    return x * lax.rsqrt(jnp.mean(x * x, axis=-1, keepdims=True) + RMS_EPS) * g


def _dot(a, b):
    return jnp.dot(a, b, preferred_element_type=F32)


def _dot_nt(a, b):
    return lax.dot_general(a, b, (((1,), (1,)), ((), ())), preferred_element_type=F32)


def _left_lanes():
    return lax.broadcasted_iota(jnp.int32, (1, LANES), 1) < HEAD_DIM


def _layer_block(a, layer):
    return pl.BlockSpec((None,) + a.shape[1:], lambda *_: (layer, 0, 0), pipeline_mode=pl.Buffered(1))


_C_A = (0, GROUP_COLS, 2 * GROUP_COLS)
_C_QW = 3 * GROUP_COLS
_C_KV = _C_QW + WIN_Q
_COL_CHUNK = 512


def _proj_store(ref, xb, w_ref, c0, cw):
    for j in range(0, cw, _COL_CHUNK):
        jw = min(_COL_CHUNK, cw - j)
        ref[:, j:j + jw] = _dot(xb, w_ref[:, c0 + j:c0 + j + jw]).astype(BF16)


def _norm_proj_kernel(x_ref, g_ref, w_ref, a0_ref, a1_ref, a2_ref, qw_ref, kv_ref, xs_ref):
    _project(x_ref[...], g_ref, w_ref, a0_ref, a1_ref, a2_ref, qw_ref, kv_ref, xs_ref)


def _project(x, g_ref, w_ref, a0_ref, a1_ref, a2_ref, qw_ref, kv_ref, xs_ref):
    xn = _rms(x, g_ref[...])
    n_lane_tiles = D_MODEL // LANES
    for j in range(n_lane_tiles):
        xs_ref[j] = xn[:, j * LANES:(j + 1) * LANES]
    xb = xn.astype(BF16)
    _proj_store(a0_ref, xb, w_ref, _C_A[0], GROUP_COLS)
    for grp, ref in ((1, a1_ref), (2, a2_ref)):
        r = DIL_PATTERNS[grp][1]
        n = ROW_TILE // r
        xp = jnp.concatenate(
            [jnp.concatenate([xs_ref[j, pl.ds(c, n, stride=r), :] for j in range(n_lane_tiles)], axis=1)
             for c in range(r)], axis=0).astype(BF16)
        res = _dot(xp, w_ref[:, _C_A[grp]:_C_A[grp] + GROUP_COLS])
        for c in range(r):
            ref[:, c * GROUP_COLS:(c + 1) * GROUP_COLS] = res[c * n:(c + 1) * n].astype(BF16)
    _proj_store(qw_ref, xb, w_ref, _C_QW, WIN_Q)
    _proj_store(kv_ref, xb, w_ref, _C_KV, 2 * WIN_KV)


def _proj_outputs(m, row_index):
    out_shape, out_specs = [], []
    for _, r in DIL_PATTERNS:
        out_shape.append(jax.ShapeDtypeStruct((m // r, r * GROUP_COLS), BF16))
        out_specs.append(pl.BlockSpec((ROW_TILE // r, r * GROUP_COLS), row_index))
    for c in (WIN_Q, 2 * WIN_KV):
        out_shape.append(jax.ShapeDtypeStruct((m, c), BF16))
        out_specs.append(pl.BlockSpec((ROW_TILE, c), row_index))
    return out_shape, out_specs


def _norm_proj(x, g, w, layer):
    m = x.shape[0]
    in_specs = [pl.BlockSpec((ROW_TILE, D_MODEL), lambda i: (i, 0)), _layer_block(g, layer), _layer_block(w, layer)]
    out_shape, out_specs = _proj_outputs(m, lambda i: (i, 0))
    return pl.pallas_call(
        _norm_proj_kernel,
        grid=(m // ROW_TILE,),
        in_specs=in_specs,
        out_specs=out_specs,
        out_shape=out_shape,
        scratch_shapes=[pltpu.VMEM((D_MODEL // LANES, ROW_TILE, LANES), F32)],
        compiler_params=pltpu.CompilerParams(
            dimension_semantics=("arbitrary",), vmem_limit_bytes=VMEM_LIMIT),
        name="norm_proj",
    )(x, g, w)


def _band_bias(tq, win, off, half, slopes):
    col = lax.broadcasted_iota(jnp.int32, (tq, win), 1)
    row = lax.broadcasted_iota(jnp.int32, (tq, win), 0)
    dist = jnp.abs(col - row + off).astype(F32)
    inside = dist <= float(half)
    return jnp.concatenate([jnp.where(inside, dist * (-s * LOG2E), NEG_INF) for s in slopes], axis=0)


def _ones_blockdiag(win):
    left = lax.broadcasted_iota(jnp.int32, (2 * win, LANES), 1) < HEAD_DIM
    top = jnp.where(lax.broadcasted_iota(jnp.int32, (2 * win, LANES), 0) < win, 1.0, 0.0)
    return jnp.where(left, top, 1.0 - top).astype(BF16)


def _pair_scores(q_tiles, k_win, bias):
    left = _left_lanes()
    zero = jnp.zeros((), BF16)
    rows = []
    for q in q_tiles:
        rows += [jnp.where(left, q, zero), jnp.where(left, zero, q)]
    return _dot_nt(jnp.concatenate(rows, axis=0), k_win) + bias


def _pair_values(s_ref, v_win, ones_bd, sink_ref, p_ref, aux_ref):
    win = v_win.shape[0]
    n = s_ref.shape[0] // (2 * Q_TILE)
    left = _left_lanes()
    zero = jnp.zeros((), BF16)
    for i in range(n):
        ms = []
        for side in range(2):
            rows = pl.ds((2 * i + side) * Q_TILE, Q_TILE)
            s = s_ref[rows, :]
            m = jnp.broadcast_to(jnp.max(s, axis=-1, keepdims=True), (Q_TILE, LANES))
            if sink_ref is not None:
                m = jnp.maximum(m, sink_ref[rows, :])
            p = jnp.exp2(s - jnp.concatenate([m] * (win // LANES), axis=1)).astype(BF16)
            p_ref[i * Q_TILE:(i + 1) * Q_TILE, side * win:(side + 1) * win] = p
            ms.append(m)
        m_pair = jnp.where(left, ms[0], ms[1])
        if sink_ref is not None:
            sink_pair = jnp.where(left, sink_ref[pl.ds(2 * i * Q_TILE, Q_TILE), :],
                                  sink_ref[pl.ds((2 * i + 1) * Q_TILE, Q_TILE), :])
            aux_ref[i * Q_TILE:(i + 1) * Q_TILE, :] = jnp.exp2(sink_pair - m_pair)
        else:
            aux_ref[i * Q_TILE:(i + 1) * Q_TILE, :] = m_pair
    vbd = jnp.concatenate([jnp.where(left, v_win, zero), jnp.where(left, zero, v_win)], axis=0)
    on = _dot(p_ref[...], jnp.concatenate([vbd, ones_bd], axis=1))
    return on[:, :LANES], on[:, LANES:], aux_ref[...]


def _tile_variant(t, n_tiles):
    if isinstance(t, int):
        return 0 if t == 0 else (2 if t == n_tiles - 1 else 1)
    return jnp.where(t == 0, 0, jnp.where(t == n_tiles - 1, 2, 1))


def _dil_geometry(seq_full, grp):
    r = DIL_PATTERNS[grp][1]
    seq = seq_full // r
    win = min(seq, Q_TILE + 2 * DIL_HALF)
    n_tiles = seq // Q_TILE
    pad = (win - Q_TILE) // 2
    offs = [0] if n_tiles == 1 else [0, -pad, -2 * pad]
    return r, seq, win, n_tiles, pad, offs


def _dilated_kernel(seq_full, a0_ref, a1_ref, a2_ref, out_ref, o_nat, l_nat, b0, b1, b2, ones_ref, ones2_ref,
                    s_wide, s_narrow, p_wide, p_narrow, aux_ref):
    a_refs = (a0_ref, a1_ref, a2_ref)
    bias_refs = (b0, b1, b2)
    s_refs = (s_wide, s_wide, s_narrow)
    p_refs = (p_wide, p_wide, p_narrow)
    slopes = _alibi_slopes(N_DIL_SUB)

    @pl.when(pl.program_id(0) == 0)
    def _():
        for grp in range(3):
            r, seq, win, n_tiles, pad, offs = _dil_geometry(seq_full, grp)
            for v, off in enumerate(offs):
                for pr in range(DIL_PAIRS):
                    hs = slopes[grp * DIL_HEADS + 2 * pr:grp * DIL_HEADS + 2 * pr + 2]
                    bias_refs[grp][v, pr] = _band_bias(Q_TILE, win, off, DIL_HALF, [s * r for s in hs])
        ones_ref[...] = _ones_blockdiag(_dil_geometry(seq_full, 0)[2])
        ones2_ref[...] = _ones_blockdiag(_dil_geometry(seq_full, 2)[2])

    def pair_tile(grp, c, t, pr, slot):
        r, seq, win, n_tiles, pad, offs = _dil_geometry(seq_full, grp)
        a_ref = a_refs[grp]
        ones_bd_ref = ones2_ref if grp == 2 else ones_ref
        if n_tiles == 1:
            q0, ks, var = 0, 0, 0
        else:
            q0 = pl.multiple_of(t * Q_TILE, Q_TILE)
            ks = pl.multiple_of(jnp.clip(q0 - pad, 0, seq - win), DIL_HALF)
            var = _tile_variant(t, n_tiles)
        lo = c * GROUP_COLS + pr * LANES
        q = a_ref[pl.ds(q0, Q_TILE), lo:lo + LANES]
        k = a_ref[pl.ds(ks, win), DIL_OUT + lo:DIL_OUT + lo + LANES]
        v = a_ref[pl.ds(ks, win), 2 * DIL_OUT + lo:2 * DIL_OUT + lo + LANES]
        s_ref = s_refs[grp].at[slot]
        s_ref[...] = _pair_scores([q], k, bias_refs[grp][var, pr])
        num, den, m = _pair_values(s_ref, v, ones_bd_ref[...], None, p_refs[grp].at[slot], aux_ref.at[slot])
        rows = pl.ds(q0, Q_TILE) if r == 1 else pl.ds(c + r * q0, Q_TILE, stride=r)
        o_nat[grp, pr, rows, :] = num / den
        l_nat[grp, pr, rows, :] = m + jnp.log2(den)

    for grp in range(3):
        r, seq, win, n_tiles, pad, offs = _dil_geometry(seq_full, grp)
        per_body = DIL_INFLIGHT // DIL_PAIRS
        if n_tiles == 1:
            for c in range(r):
                for pr in range(DIL_PAIRS):
                    pair_tile(grp, c, 0, pr, c * DIL_PAIRS + pr)
        elif r >= per_body:
            for c0 in range(0, r, per_body):
                def body(t, carry, grp=grp, c0=c0):
                    for j in range(per_body):
                        for pr in range(DIL_PAIRS):
                            pair_tile(grp, c0 + j, t, pr, j * DIL_PAIRS + pr)
                    return carry
                lax.fori_loop(0, n_tiles, body, 0)
        else:
            tiles_per_body = per_body // r
            def body(i, carry, grp=grp, r=r, tiles_per_body=tiles_per_body):
                for j in range(tiles_per_body):
                    for c in range(r):
                        for pr in range(DIL_PAIRS):
                            pair_tile(grp, c, i * tiles_per_body + j, pr, (j * r + c) * DIL_PAIRS + pr)
                return carry
            lax.fori_loop(0, n_tiles // tiles_per_body, body, 0)

    def combine(i, carry):
        rows = pl.ds(pl.multiple_of(i * ROW_TILE, ROW_TILE), ROW_TILE)
        for pr in range(DIL_PAIRS):
            ls = [l_nat[g, pr, rows, :] for g in range(3)]
            mx = jnp.maximum(jnp.maximum(ls[0], ls[1]), ls[2])
            es = [jnp.exp2(l - mx) for l in ls]
            num = es[0] * o_nat[0, pr, rows, :] + es[1] * o_nat[1, pr, rows, :] + es[2] * o_nat[2, pr, rows, :]
            out_ref[rows, pr * LANES:(pr + 1) * LANES] = (num / (es[0] + es[1] + es[2])).astype(BF16)
        return carry

    lax.fori_loop(0, seq_full // ROW_TILE, combine, 0)


def _dilated_attention(a0, a1, a2, batch, seq_full):
    views, in_specs, bias_shapes = [], [], []
    for grp, a in enumerate((a0, a1, a2)):
        r, seq, win, n_tiles, pad, offs = _dil_geometry(seq_full, grp)
        views.append(a.reshape(batch, seq, r * GROUP_COLS))
        in_specs.append(pl.BlockSpec((None, seq, r * GROUP_COLS), lambda b: (b, 0, 0)))
        bias_shapes.append(pltpu.VMEM((len(offs), DIL_PAIRS, 2 * Q_TILE, win), F32))
    win0 = _dil_geometry(seq_full, 0)[2]
    win2 = _dil_geometry(seq_full, 2)[2]
    n_straight = DIL_PATTERNS[2][1] * DIL_PAIRS
    return pl.pallas_call(
        functools.partial(_dilated_kernel, seq_full),
        grid=(batch,),
        in_specs=in_specs,
        out_specs=pl.BlockSpec((None, seq_full, DIL_OUT), lambda b: (b, 0, 0)),
        out_shape=jax.ShapeDtypeStruct((batch, seq_full, DIL_OUT), BF16),
        scratch_shapes=[pltpu.VMEM((3, DIL_PAIRS, seq_full, LANES), F32),
                        pltpu.VMEM((3, DIL_PAIRS, seq_full, LANES), F32)]
        + bias_shapes + [pltpu.VMEM((2 * win0, LANES), BF16), pltpu.VMEM((2 * win2, LANES), BF16)]
        + [pltpu.VMEM((n, 2 * Q_TILE, w), F32) for n, w in ((DIL_INFLIGHT, win0), (n_straight, win2))]
        + [pltpu.VMEM((n, Q_TILE, 2 * w), BF16) for n, w in ((DIL_INFLIGHT, win0), (n_straight, win2))]
        + [pltpu.VMEM((max(DIL_INFLIGHT, n_straight), Q_TILE, LANES), F32)],
        compiler_params=pltpu.CompilerParams(
            dimension_semantics=("arbitrary",), vmem_limit_bytes=VMEM_LIMIT),
        name="dilated",
    )(*views)


WIN_WINDOW = Q_TILE + 2 * WIN_HALF


def _window_pair_heads():
    group = WIN_Q_HEADS // WIN_KV_HEADS
    pairs = []
    for u in range(WIN_UNITS):
        for i in range(WIN_UNIT_PAIRS):
            pairs.append((2 * u * group + i, (2 * u + 1) * group + i))
    return pairs


def _window_kernel(seq, q_hbm, kv_hbm, sink_ref, o_ref, bias_ref, ones_ref, sink_rows_ref, s_ref, p_ref,
                   aux_ref, q_buf, kv_buf, sems):
    n_tiles = seq // Q_TILE
    slopes = _alibi_slopes(WIN_Q_HEADS)
    pair_heads = _window_pair_heads()
    b = pl.program_id(0)
    last_seq = b + 1 == pl.num_programs(0)
    slot = b % 2

    def fetch(seq_idx, sl):
        return (pltpu.make_async_copy(q_hbm.at[seq_idx], q_buf.at[sl], sems.at[0, sl]),
                pltpu.make_async_copy(kv_hbm.at[seq_idx], kv_buf.at[sl], sems.at[1, sl]))

    @pl.when(b == 0)
    def _():
        for u in range(WIN_UNITS):
            heads = [h for pr in pair_heads[u * WIN_UNIT_PAIRS:(u + 1) * WIN_UNIT_PAIRS] for h in pr]
            for v, off in enumerate((0, -WIN_HALF, -2 * WIN_HALF)):
                bias_ref[v, u] = _band_bias(Q_TILE, WIN_WINDOW, off, WIN_HALF, [slopes[h] for h in heads])
            sink_rows_ref[u] = jnp.concatenate(
                [jnp.broadcast_to(sink_ref[0:1, h:h + 1] * LOG2E, (Q_TILE, LANES)) for h in heads], axis=0)
        ones_ref[...] = _ones_blockdiag(WIN_WINDOW)

    def rows_of(t):
        if isinstance(t, int):
            return t * Q_TILE, min(max(t * Q_TILE - WIN_HALF, 0), seq - WIN_WINDOW)
        q0 = pl.multiple_of(t * Q_TILE, Q_TILE)
        ks = pl.multiple_of(jnp.clip(q0 - WIN_HALF, 0, seq - WIN_WINDOW), Q_TILE)
        return q0, ks

    def unit_cols(u):
        return [(u * WIN_UNIT_PAIRS + i) * LANES for i in range(WIN_UNIT_PAIRS)]

    def score_stage(sl, t, k_set):
        q0, ks = rows_of(t)
        var = _tile_variant(t, n_tiles)
        for u in range(WIN_UNITS):
            qs = [q_buf[sl, pl.ds(q0, Q_TILE), c:c + LANES] for c in unit_cols(u)]
            k = kv_buf[sl, pl.ds(ks, WIN_WINDOW), u * LANES:(u + 1) * LANES]
            s_ref[k_set * WIN_UNITS + u] = _pair_scores(qs, k, bias_ref[var, u])

    def value_stage(sl, t, k_set):
        q0, ks = rows_of(t)
        for u in range(WIN_UNITS):
            work = k_set * WIN_UNITS + u
            v = kv_buf[sl, pl.ds(ks, WIN_WINDOW), WIN_KV + u * LANES:WIN_KV + (u + 1) * LANES]
            num, den, sink_term = _pair_values(s_ref.at[work], v, ones_ref[...], sink_rows_ref.at[u],
                                               p_ref.at[work], aux_ref.at[work])
            o = (num / (den + sink_term)).astype(BF16)
            for i, c in enumerate(unit_cols(u)):
                o_ref[pl.ds(q0, Q_TILE), c:c + LANES] = o[i * Q_TILE:(i + 1) * Q_TILE]

    per_body = WIN_TILES_PER_BODY

    @pl.when(b == 0)
    def _():
        for cp in fetch(0, 0):
            cp.start()
        for cp in fetch(0, 0):
            cp.wait()
        score_stage(0, 0, 0)

    @pl.when(jnp.logical_not(last_seq))
    def _():
        for cp in fetch(b + 1, 1 - slot):
            cp.start()

    def body(i, carry):
        t = per_body * i
        for j in range(per_body):
            score_stage(slot, t + j + 1, (j + 1) % per_body)
            value_stage(slot, t + j, j)
        return carry

    lax.fori_loop(0, n_tiles // per_body - 1, body, 0)

    @pl.when(jnp.logical_not(last_seq))
    def _():
        for cp in fetch(b + 1, 1 - slot):
            cp.wait()

    t = n_tiles - per_body
    for j in range(per_body):
        if j + 1 < per_body:
            score_stage(slot, t + j + 1, j + 1)
        else:
            score_stage(1 - slot, 0, 0)
        value_stage(slot, t + j, j)


def _window_attention(q, kv, sink, batch, seq):
    unit_rows = 2 * WIN_UNIT_PAIRS * Q_TILE
    return pl.pallas_call(
        functools.partial(_window_kernel, seq),
        grid=(batch,),
        in_specs=[pl.BlockSpec(memory_space=pl.ANY),
                  pl.BlockSpec(memory_space=pl.ANY),
                  pl.BlockSpec((1, WIN_Q_HEADS), lambda b: (0, 0))],
        out_specs=pl.BlockSpec((None, seq, WIN_Q), lambda b: (b, 0, 0)),
        out_shape=jax.ShapeDtypeStruct((batch, seq, WIN_Q), BF16),
        scratch_shapes=[pltpu.VMEM((3, WIN_UNITS, unit_rows, WIN_WINDOW), F32),
                        pltpu.VMEM((2 * WIN_WINDOW, LANES), BF16),
                        pltpu.VMEM((WIN_UNITS, unit_rows, LANES), F32),
                        pltpu.VMEM((WIN_TILES_PER_BODY * WIN_UNITS, unit_rows, WIN_WINDOW), F32),
                        pltpu.VMEM((WIN_TILES_PER_BODY * WIN_UNITS, unit_rows // 2, 2 * WIN_WINDOW), BF16),
                        pltpu.VMEM((WIN_TILES_PER_BODY * WIN_UNITS, unit_rows // 2, LANES), F32),
                        pltpu.VMEM((2, seq, WIN_Q), BF16),
                        pltpu.VMEM((2, seq, 2 * WIN_KV), BF16),
                        pltpu.SemaphoreType.DMA((2, 2))],
        compiler_params=pltpu.CompilerParams(
            dimension_semantics=("arbitrary",), vmem_limit_bytes=VMEM_LIMIT),
        name="window",
    )(q.reshape(batch, seq, WIN_Q), kv.reshape(batch, seq, 2 * WIN_KV), sink)


def _merge_kernel(oa_ref, ow_ref, h_ref, g_ref, wgate_ref, bgate_ref, wa_ref, wb_ref, wo_ref, out_ref):
    h = h_ref[...]
    xb = _rms(h, g_ref[...]).astype(BF16)
    merged = None
    for br, (o_ref, w_ref) in enumerate(((oa_ref, wa_ref), (ow_ref, wb_ref))):
        cols = slice(br * D_MODEL, (br + 1) * D_MODEL)
        gate = jax.nn.sigmoid(_dot(xb, wgate_ref[:, cols]) + bgate_ref[:, cols])
        term = gate * _dot(o_ref[...], w_ref[...])
        merged = term if merged is None else merged + term
    out_ref[...] = h + _dot(merged.astype(BF16), wo_ref[...])


def _merge_out(oa, ow, h, g, wgate, bgate, wa, wb, wo, layer):
    m = h.shape[0]
    row = lambda c: pl.BlockSpec((MERGE_TILE, c), lambda i: (i, 0))
    return pl.pallas_call(
        _merge_kernel,
        grid=(m // MERGE_TILE,),
        in_specs=[row(DIL_OUT), row(WIN_Q), row(D_MODEL)]
        + [_layer_block(a, layer) for a in (g, wgate, bgate, wa, wb, wo)],
        out_specs=row(D_MODEL),
        out_shape=jax.ShapeDtypeStruct((m, D_MODEL), F32),
        compiler_params=pltpu.CompilerParams(
            dimension_semantics=("arbitrary",), vmem_limit_bytes=VMEM_LIMIT),
        name="merge_out",
    )(oa, ow, h, g, wgate, bgate, wa, wb, wo)


TOKEN_TILE = 256
CUM_CHUNK = TOKEN_TILE
SLOT_WINDOW = 64
SLOT_ALIGN = 16


def _prefix_exclusive(mask_f, tri):
    e, s = mask_f.shape
    carry = jnp.zeros((e, 1), F32)
    parts, carries = [], []
    for j in range(0, s, CUM_CHUNK):
        blk = mask_f[:, j:j + CUM_CHUNK]
        carries.append(carry)
        parts.append(_dot(blk.astype(BF16), tri) + carry)
        carry = carry + jnp.sum(blk, axis=-1, keepdims=True)
    return jnp.concatenate(parts, axis=-1), carries + [carry]


def _route_kernel(cap, h_ref, g_ref, whi_ref, wlo_ref, hn_ref, rank_ref, aff_ref, starts_ref):
    seq = h_ref.shape[0]
    hn = _rms(h_ref[...], g_ref[...])
    hn_hi = hn.astype(BF16)
    hn_lo = (hn - hn_hi.astype(F32)).astype(BF16)
    hn_ref[...] = hn_hi
    both = _dot_nt(jnp.concatenate([whi_ref[...], wlo_ref[...]], axis=0), hn_hi)
    logits = both[:N_EXPERTS] + both[N_EXPERTS:] + _dot_nt(whi_ref[...], hn_lo)
    mx = jnp.max(logits, axis=0, keepdims=True)
    ex = jnp.exp(logits - mx)
    aff = ex / jnp.sum(ex, axis=0, keepdims=True)
    bits = pltpu.bitcast(aff, jnp.int32)

    def enough(t):
        return jnp.sum(jnp.where(bits >= t, 1.0, 0.0), axis=-1, keepdims=True) >= float(cap)

    def search4(_, c):
        lo, hi = c
        q = (hi - lo) >> 2
        m1, m2, m3 = lo + q, lo + 2 * q, lo + 3 * q
        ok1, ok2, ok3 = enough(m1), enough(m2), enough(m3)
        return (jnp.where(ok3, m3, jnp.where(ok2, m2, jnp.where(ok1, m1, lo))),
                jnp.where(ok3, hi, jnp.where(ok2, m3, jnp.where(ok1, m2, m1))))

    def search2(_, c):
        lo, hi = c
        mid = lo + ((hi - lo) >> 1)
        ok = enough(mid)
        return jnp.where(ok, mid, lo), jnp.where(ok, hi, mid)

    lo0 = jnp.zeros((N_EXPERTS, 1), jnp.int32)
    hi0 = jnp.full((N_EXPERTS, 1), 0x3F800001, jnp.int32)
    thr, _ = lax.fori_loop(0, 3, search2, lax.fori_loop(0, 15, search4, (lo0, hi0)))

    r_i = lax.broadcasted_iota(jnp.int32, (CUM_CHUNK, CUM_CHUNK), 0)
    c_i = lax.broadcasted_iota(jnp.int32, (CUM_CHUNK, CUM_CHUNK), 1)
    tri = jnp.where(r_i < c_i, 1.0, 0.0).astype(BF16)
    gt = jnp.where(bits > thr, 1.0, 0.0)
    eq = jnp.where(bits == thr, 1.0, 0.0)
    need = float(cap) - jnp.sum(gt, axis=-1, keepdims=True)
    tie_rank, _ = _prefix_exclusive(eq, tri)
    sel = gt + eq * jnp.where(tie_rank < need, 1.0, 0.0)
    slot, starts = _prefix_exclusive(sel, tri)
    rank = jnp.where(sel > 0.0, slot, -1.0)
    for t in range(seq // TOKEN_TILE):
        rank_ref[t] = rank[:, t * TOKEN_TILE:(t + 1) * TOKEN_TILE]
        aff_ref[t] = aff[:, t * TOKEN_TILE:(t + 1) * TOKEN_TILE]
    lane = lax.broadcasted_iota(jnp.int32, (N_EXPERTS, LANES), 1)
    acc = jnp.zeros((N_EXPERTS, LANES), F32)
    for t, c in enumerate(starts):
        acc = acc + jnp.where(lane == t, c, 0.0)
    starts_ref[...] = acc


def _route(h, g, w_hi, w_lo, layer, batch, seq, cap):
    n_tiles = seq // TOKEN_TILE
    return pl.pallas_call(
        functools.partial(_route_kernel, cap),
        grid=(batch,),
        in_specs=[pl.BlockSpec((None, seq, D_MODEL), lambda b: (b, 0, 0))]
        + [_layer_block(a, layer) for a in (g, w_hi, w_lo)],
        out_specs=[pl.BlockSpec((None, seq, D_MODEL), lambda b: (b, 0, 0)),
                   pl.BlockSpec((None, n_tiles, N_EXPERTS, TOKEN_TILE), lambda b: (b, 0, 0, 0)),
                   pl.BlockSpec((None, n_tiles, N_EXPERTS, TOKEN_TILE), lambda b: (b, 0, 0, 0)),
                   pl.BlockSpec((None, N_EXPERTS, LANES), lambda b: (b, 0, 0))],
        out_shape=[jax.ShapeDtypeStruct((batch, seq, D_MODEL), BF16),
                   jax.ShapeDtypeStruct((batch, n_tiles, N_EXPERTS, TOKEN_TILE), F32),
                   jax.ShapeDtypeStruct((batch, n_tiles, N_EXPERTS, TOKEN_TILE), F32),
                   jax.ShapeDtypeStruct((batch, N_EXPERTS, LANES), F32)],
        compiler_params=pltpu.CompilerParams(
            dimension_semantics=("arbitrary",), vmem_limit_bytes=VMEM_LIMIT),
        name="route",
    )(h.reshape(batch, seq, D_MODEL), g, w_hi, w_lo)


def _slot_rows(cap):
    return cap + SLOT_WINDOW


def _window_starts(win_ref, b, t, p, cap):
    base = b * (N_EXPERTS * LANES) + t
    return [pl.multiple_of(jnp.minimum(win_ref[base + e * LANES] + p * SLOT_WINDOW, cap), SLOT_ALIGN)
            for e in range(N_EXPERTS)]


def _window_hits(rank_tile, starts):
    rows = lax.broadcasted_iota(jnp.int32, (SLOT_WINDOW, TOKEN_TILE), 0).astype(F32)
    return [(rank_tile[e:e + 1, :] - starts[e].astype(F32)) == rows for e in range(N_EXPERTS)]


def _one_hot(hits):
    return jnp.concatenate([jnp.where(h, 1.0, 0.0).astype(BF16) for h in hits], axis=0)


def _gather_kernel(cap, win_ref, npass_ref, hn_ref, rank_ref, aff_ref, xe_ref, gate_ref):
    b = pl.program_id(0)
    xe_ref[...] = jnp.zeros_like(xe_ref)
    gate_ref[...] = jnp.zeros_like(gate_ref)
    n_tiles = hn_ref.shape[0] // TOKEN_TILE

    def one_pass(p, t):
        starts = _window_starts(win_ref, b, t, p, cap)
        hits = _window_hits(rank_ref[t], starts)
        rows = _dot(_one_hot(hits), hn_ref[t * TOKEN_TILE:(t + 1) * TOKEN_TILE, :]).astype(BF16)
        aff_tile = aff_ref[t]
        for e in range(N_EXPERTS):
            win = pl.ds(starts[e], SLOT_WINDOW)
            xe_ref[e, win, :] += rows[e * SLOT_WINDOW:(e + 1) * SLOT_WINDOW]
            g = jnp.sum(jnp.where(hits[e], aff_tile[e:e + 1, :], 0.0), axis=-1, keepdims=True)
            gate_ref[e, win, :] += jnp.broadcast_to(g, (SLOT_WINDOW, LANES))

    for t in range(n_tiles):
        lax.fori_loop(1, npass_ref[b * LANES + t], lambda p, c, t=t: (one_pass(p, t), c)[1], 0)
    for t in range(n_tiles):
        one_pass(0, t)


def _gather(hn, rank, aff, win, npass, cap):
    batch, seq, _ = hn.shape
    n_tiles = seq // TOKEN_TILE
    grid_spec = pltpu.PrefetchScalarGridSpec(
        num_scalar_prefetch=2,
        grid=(batch,),
        in_specs=[pl.BlockSpec((None, seq, D_MODEL), lambda b, *_: (b, 0, 0)),
                  pl.BlockSpec((None, n_tiles, N_EXPERTS, TOKEN_TILE), lambda b, *_: (b, 0, 0, 0)),
                  pl.BlockSpec((None, n_tiles, N_EXPERTS, TOKEN_TILE), lambda b, *_: (b, 0, 0, 0))],
        out_specs=[pl.BlockSpec((None, N_EXPERTS, _slot_rows(cap), D_MODEL), lambda b, *_: (b, 0, 0, 0)),
                   pl.BlockSpec((None, N_EXPERTS, _slot_rows(cap), LANES), lambda b, *_: (b, 0, 0, 0))])
    return pl.pallas_call(
        functools.partial(_gather_kernel, cap),
        grid_spec=grid_spec,
        out_shape=[jax.ShapeDtypeStruct((batch, N_EXPERTS, _slot_rows(cap), D_MODEL), BF16),
                   jax.ShapeDtypeStruct((batch, N_EXPERTS, _slot_rows(cap), LANES), F32)],
        compiler_params=pltpu.CompilerParams(
            dimension_semantics=("arbitrary",), vmem_limit_bytes=VMEM_LIMIT),
        name="gather",
    )(win, npass, hn, rank, aff)


FFN_SEQS = 4


def _ffn_kernel(cap, xe_ref, gate_ref, wg_ref, wu_ref, wd_ref, y_ref, wg_b, wu_b, wd_b):
    d = xe_ref.shape[-1]

    @pl.when(pl.program_id(1) == 0)
    def _():
        for src, dst in ((wg_ref, wg_b), (wu_ref, wu_b), (wd_ref, wd_b)):
            dst[...] = src[...].astype(BF16)

    xe = xe_ref[:, :cap, :].reshape(FFN_SEQS * cap, d)
    a = _dot(xe, wg_b[...])
    u = _dot(xe, wu_b[...])
    y = _dot((jax.nn.silu(a) * u).astype(BF16), wd_b[...])
    gate = gate_ref[:, :cap, :].reshape(FFN_SEQS * cap, LANES)
    y = y * jnp.concatenate([gate] * (d // LANES), axis=1)
    y_ref[:, :cap, :] = y.astype(BF16).reshape(FFN_SEQS, cap, d)
    y_ref[:, cap:, :] = jnp.zeros((FFN_SEQS, SLOT_WINDOW, d), BF16)


def _ffn(xe, gate, wg, wu, wd, layer, cap):
    batch = xe.shape[0]
    d_exp = wg.shape[-1]
    slots = pl.BlockSpec((FFN_SEQS, None, _slot_rows(cap), D_MODEL), lambda e, i: (i, e, 0, 0))
    return pl.pallas_call(
        functools.partial(_ffn_kernel, cap),
        grid=(N_EXPERTS, batch // FFN_SEQS),
        in_specs=[slots,
                  pl.BlockSpec((FFN_SEQS, None, _slot_rows(cap), LANES), lambda e, i: (i, e, 0, 0)),
                  pl.BlockSpec((None, None, D_MODEL, d_exp), lambda e, i: (layer, e, 0, 0)),
                  pl.BlockSpec((None, None, D_MODEL, d_exp), lambda e, i: (layer, e, 0, 0)),
                  pl.BlockSpec((None, None, d_exp, D_MODEL), lambda e, i: (layer, e, 0, 0))],
        out_specs=slots,
        out_shape=jax.ShapeDtypeStruct(xe.shape, BF16),
        scratch_shapes=[pltpu.VMEM((D_MODEL, d_exp), BF16), pltpu.VMEM((D_MODEL, d_exp), BF16),
                        pltpu.VMEM((d_exp, D_MODEL), BF16)],
        compiler_params=pltpu.CompilerParams(
            dimension_semantics=("arbitrary", "arbitrary"), vmem_limit_bytes=VMEM_LIMIT),
        name="ffn",
    )(xe, gate, wg, wu, wd)


TILES_PER_ROW_TILE = ROW_TILE // TOKEN_TILE


def _scatter_kernel(cap, project, win_ref, npass_ref, y_hbm, rank_ref, h_ref, g_ref, *rest):
    if project:
        w_ref, out_ref, a0_ref, a1_ref, a2_ref, qw_ref, kv_ref, xs_ref, y_buf, y_sem = rest
    else:
        out_ref, y_buf, y_sem = rest
    b = pl.program_id(0)
    j = pl.program_id(1)
    slot = b % 2

    def y_copy(seq_idx, buf_slot):
        return pltpu.make_async_copy(y_hbm.at[seq_idx], y_buf.at[buf_slot], y_sem.at[buf_slot])

    @pl.when((b == 0) & (j == 0))
    def _():
        y_copy(0, 0).start()

    @pl.when(j == 0)
    def _():
        y_copy(b, slot).wait()

    @pl.when((j == 0) & (b + 1 < pl.num_programs(0)))
    def _():
        y_copy(b + 1, 1 - slot).start()

    y_ref = y_buf.at[slot]
    out_ref[...] = h_ref[...]

    def one_pass(p, tt):
        t = j * TILES_PER_ROW_TILE + tt
        rows = slice(tt * TOKEN_TILE, (tt + 1) * TOKEN_TILE)
        starts = _window_starts(win_ref, b, t, p, cap)
        put = _one_hot(_window_hits(rank_ref[tt], starts))
        yw = jnp.concatenate([y_ref[e, pl.ds(starts[e], SLOT_WINDOW), :] for e in range(N_EXPERTS)], axis=0)
        out_ref[rows, :] += lax.dot_general(put, yw, (((0,), (0,)), ((), ())), preferred_element_type=F32)

    for tt in range(TILES_PER_ROW_TILE):
        n_pass = npass_ref[b * LANES + j * TILES_PER_ROW_TILE + tt]
        lax.fori_loop(1, n_pass, lambda p, c, tt=tt: (one_pass(p, tt), c)[1], 0)
    for tt in range(TILES_PER_ROW_TILE):
        one_pass(0, tt)
    if project:
        _project(out_ref[...], g_ref, w_ref, a0_ref, a1_ref, a2_ref, qw_ref, kv_ref, xs_ref)
    else:
        out_ref[...] = _rms(out_ref[...], g_ref[...])


def _scatter(y, rank, h, win, npass, cap, g, layer, w_next=None):
    batch, n_tiles = rank.shape[:2]
    seq = h.shape[1]
    steps = n_tiles // TILES_PER_ROW_TILE
    project = w_next is not None
    in_specs = [pl.BlockSpec(memory_space=pl.ANY),
                pl.BlockSpec((None, TILES_PER_ROW_TILE, N_EXPERTS, TOKEN_TILE), lambda b, j, *_: (b, j, 0, 0)),
                pl.BlockSpec((None, ROW_TILE, D_MODEL), lambda b, j, *_: (b, j, 0)),
                _layer_block(g, layer)]
    out_shape = [jax.ShapeDtypeStruct(h.shape, F32)]
    out_specs = [pl.BlockSpec((None, ROW_TILE, D_MODEL), lambda b, j, *_: (b, j, 0))]
    args = [win, npass, y, rank, h, g]
    scratch = []
    if project:
        in_specs.append(_layer_block(w_next, layer))
        args.append(w_next)
        proj_shape, proj_specs = _proj_outputs(batch * seq, lambda b, j, *_: (b * steps + j, 0))
        out_shape += proj_shape
        out_specs += proj_specs
        scratch = [pltpu.VMEM((D_MODEL // LANES, ROW_TILE, LANES), F32)]
    scratch += [pltpu.VMEM((2,) + y.shape[1:], BF16), pltpu.SemaphoreType.DMA((2,))]
    grid_spec = pltpu.PrefetchScalarGridSpec(
        num_scalar_prefetch=2, grid=(batch, steps), in_specs=in_specs, out_specs=out_specs,
        scratch_shapes=scratch)
    return pl.pallas_call(
        functools.partial(_scatter_kernel, cap, project),
        grid_spec=grid_spec,
        out_shape=out_shape,
        compiler_params=pltpu.CompilerParams(
            dimension_semantics=("arbitrary", "arbitrary"), vmem_limit_bytes=VMEM_LIMIT),
        name="scatter_proj" if project else "scatter_final",
    )(*args)


def _slot_windows(starts, n_tiles, cap):
    s = starts.astype(jnp.int32)
    first = (s // SLOT_ALIGN) * SLOT_ALIGN
    span = s[:, :, 1:n_tiles + 1] - first[:, :, :n_tiles]
    npass = jnp.maximum(jnp.max(-(-span // SLOT_WINDOW), axis=1), 1)
    npass = jnp.pad(npass, ((0, 0), (0, LANES - n_tiles)))
    return first.reshape(-1), npass.reshape(-1)


def _arrange_w_in(w):
    scale = LOG2E * HEAD_DIM ** -0.5
    qa, ka, va = w[..., :DIL_QKV], w[..., DIL_QKV:2 * DIL_QKV], w[..., 2 * DIL_QKV:3 * DIL_QKV]
    rest = w[..., 3 * DIL_QKV:]
    parts = []
    for g in range(len(DIL_PATTERNS)):
        sl = slice(g * DIL_OUT, (g + 1) * DIL_OUT)
        parts += [qa[..., sl] * scale, ka[..., sl], va[..., sl]]
    for pr in _window_pair_heads():
        parts += [rest[..., h * HEAD_DIM:(h + 1) * HEAD_DIM] * scale for h in pr]
    parts.append(rest[..., WIN_Q:WIN_Q + 2 * WIN_KV])
    return jnp.concatenate(parts, axis=-1).astype(BF16), rest[..., WIN_Q + 2 * WIN_KV:].astype(BF16)


def _arrange_w_branch_b(w):
    return jnp.concatenate([w[:, h * HEAD_DIM:(h + 1) * HEAD_DIM] for pr in _window_pair_heads() for h in pr],
                           axis=1).astype(BF16)


def kernel(x, norm_mix, w_in, w_branch_a, w_branch_b, b_gate, sink_logit, w_out, norm_ffn, w_router,
           w_expert_gate, w_expert_up, w_expert_down, norm_final):
    batch, seq, d = x.shape
    depth = w_in.shape[0]
    cap = CAPACITY_FACTOR * seq // N_EXPERTS
    m = batch * seq
    w_attn, w_gates = _arrange_w_in(w_in)
    wa, wb, wo = w_branch_a.astype(BF16), _arrange_w_branch_b(w_branch_b), w_out.astype(BF16)
    g_mix, g_ffn, bg = norm_mix[:, None, :], norm_ffn[:, None, :], b_gate[:, None, :]
    wr = jnp.swapaxes(w_router, 1, 2)
    wr_hi = wr.astype(BF16)
    wr_lo = (wr - wr_hi.astype(F32)).astype(BF16)
    h = x.reshape(m, d)
    proj = _norm_proj(h, g_mix, w_attn, 0)
    for l in range(depth):
        a0, a1, a2, qw, kv = proj
        oa = _dilated_attention(a0, a1, a2, batch, seq).reshape(m, DIL_OUT)
        ow = _window_attention(qw, kv, sink_logit[l][None, :], batch, seq).reshape(m, WIN_Q)
        h = _merge_out(oa, ow, h, g_mix, w_gates, bg, wa, wb, wo, l)
        hn, rank, aff, starts = _route(h, g_ffn, wr_hi, wr_lo, l, batch, seq, cap)
        win, npass = _slot_windows(starts, seq // TOKEN_TILE, cap)
        xe, gate = _gather(hn, rank, aff, win, npass, cap)
        y = _ffn(xe, gate, w_expert_gate, w_expert_up, w_expert_down, l, cap)
        h3 = h.reshape(batch, seq, d)
        if l + 1 == depth:
            return _scatter(y, rank, h3, win, npass, cap, norm_final[None, None, :], 0)[0]
        h3, *proj = _scatter(y, rank, h3, win, npass, cap, g_mix, l + 1, w_attn)
        h = h3.reshape(m, d)
```

```python
import functools

import jax
import jax.numpy as jnp
from jax import lax
from jax.experimental import pallas as pl
from jax.experimental.pallas import tpu as pltpu

D_MODEL = 1024
HEAD_DIM = 64
DIL_PATTERNS = ((128, 1), (512, 4), (2048, 16))
DIL_HEADS = 4
N_DIL_SUB = DIL_HEADS * len(DIL_PATTERNS)
DIL_QKV = N_DIL_SUB * HEAD_DIM
DIL_OUT = DIL_HEADS * HEAD_DIM
DIL_HALF = 64
WIN_HALF = 128
WIN_Q = D_MODEL
WIN_Q_HEADS = WIN_Q // HEAD_DIM
WIN_KV_HEADS = 4
WIN_KV = WIN_KV_HEADS * HEAD_DIM
N_EXPERTS = 16
CAPACITY_FACTOR = 2
RMS_EPS = 1e-6
NEG_INF = -1e30
LOG2E = 1.4426950408889634

LANES = 128
GROUP_COLS = 3 * DIL_OUT
DIL_PAIRS = DIL_OUT // LANES
WIN_UNITS = WIN_KV_HEADS // 2
WIN_UNIT_PAIRS = WIN_Q_HEADS // (2 * WIN_UNITS)
Q_TILE = 128
WIN_TILES_PER_BODY = 4
DIL_INFLIGHT = 16
ROW_TILE = 512
MERGE_TILE = 1024
MERGE_CHUNK = 256
VMEM_LIMIT = 56 * 1024 * 1024

F32 = jnp.float32
BF16 = jnp.bfloat16


def _alibi_slopes(n):
    return [float(2.0 ** (-8.0 * i / n)) for i in range(1, n + 1)]


def _rms(x, g):
    return x * lax.rsqrt(jnp.mean(x * x, axis=-1, keepdims=True) + RMS_EPS) * g


def _dot(a, b):
    return jnp.dot(a, b, preferred_element_type=F32)


def _dot_nt(a, b):
    return lax.dot_general(a, b, (((1,), (1,)), ((), ())), preferred_element_type=F32)


def _left_lanes():
    return lax.broadcasted_iota(jnp.int32, (1, LANES), 1) < HEAD_DIM


def _layer_block(a, layer):
    return pl.BlockSpec((None,) + a.shape[1:], lambda *_: (layer, 0, 0), pipeline_mode=pl.Buffered(1))


_C_A = (0, GROUP_COLS, 2 * GROUP_COLS)
_C_QW = 3 * GROUP_COLS
_C_KV = _C_QW + WIN_Q
_COL_CHUNK = 512
_ROW_CHUNK = 256


def _proj_store(ref, xb, w_ref, c0, cw):
    for r0 in range(0, xb.shape[0], _ROW_CHUNK):
        for j in range(0, cw, _COL_CHUNK):
            jw = min(_COL_CHUNK, cw - j)
            ref[r0:r0 + _ROW_CHUNK, j:j + jw] = _dot(xb[r0:r0 + _ROW_CHUNK],
                                                     w_ref[:, c0 + j:c0 + j + jw]).astype(BF16)


def _norm_proj_kernel(x_ref, g_ref, w_ref, a0_ref, a1_ref, a2_ref, qw_ref, kv_ref, xs_ref):
    _project(x_ref[...], g_ref, w_ref, a0_ref, a1_ref, a2_ref, qw_ref, kv_ref, xs_ref)


def _project(x, g_ref, w_ref, a0_ref, a1_ref, a2_ref, qw_ref, kv_ref, xs_ref):
    xn = _rms(x, g_ref[...])
    n_lane_tiles = D_MODEL // LANES
    for j in range(n_lane_tiles):
        xs_ref[j] = xn[:, j * LANES:(j + 1) * LANES]
    xb = xn.astype(BF16)
    _proj_store(a0_ref, xb, w_ref, _C_A[0], GROUP_COLS)
    for grp, ref in ((1, a1_ref), (2, a2_ref)):
        r = DIL_PATTERNS[grp][1]
        n = ROW_TILE // r
        xp = jnp.concatenate(
            [jnp.concatenate([xs_ref[j, pl.ds(c, n, stride=r), :] for j in range(n_lane_tiles)], axis=1)
             for c in range(r)], axis=0).astype(BF16)
        res = _dot(xp, w_ref[:, _C_A[grp]:_C_A[grp] + GROUP_COLS])
        for c in range(r):
            ref[:, c * GROUP_COLS:(c + 1) * GROUP_COLS] = res[c * n:(c + 1) * n].astype(BF16)
    _proj_store(qw_ref, xb, w_ref, _C_QW, WIN_Q)
    _proj_store(kv_ref, xb, w_ref, _C_KV, 2 * WIN_KV)


def _proj_outputs(m, row_index):
    out_shape, out_specs = [], []
    for _, r in DIL_PATTERNS:
        out_shape.append(jax.ShapeDtypeStruct((m // r, r * GROUP_COLS), BF16))
        out_specs.append(pl.BlockSpec((ROW_TILE // r, r * GROUP_COLS), row_index))
    for c in (WIN_Q, 2 * WIN_KV):
        out_shape.append(jax.ShapeDtypeStruct((m, c), BF16))
        out_specs.append(pl.BlockSpec((ROW_TILE, c), row_index))
    return out_shape, out_specs


def _norm_proj(x, g, w, layer):
    m = x.shape[0]
    in_specs = [pl.BlockSpec((ROW_TILE, D_MODEL), lambda i: (i, 0)), _layer_block(g, layer), _layer_block(w, layer)]
    out_shape, out_specs = _proj_outputs(m, lambda i: (i, 0))
    return pl.pallas_call(
        _norm_proj_kernel,
        grid=(m // ROW_TILE,),
        in_specs=in_specs,
        out_specs=out_specs,
        out_shape=out_shape,
        scratch_shapes=[pltpu.VMEM((D_MODEL // LANES, ROW_TILE, LANES), F32)],
        compiler_params=pltpu.CompilerParams(
            dimension_semantics=("arbitrary",), vmem_limit_bytes=VMEM_LIMIT),
        name="norm_proj",
    )(x, g, w)


def _band_bias(tq, win, off, half, slopes):
    col = lax.broadcasted_iota(jnp.int32, (tq, win), 1)
    row = lax.broadcasted_iota(jnp.int32, (tq, win), 0)
    dist = jnp.abs(col - row + off).astype(F32)
    inside = dist <= float(half)
    return jnp.concatenate([jnp.where(inside, dist * (-s * LOG2E), NEG_INF) for s in slopes], axis=0)


def _ones_blockdiag(win):
    left = lax.broadcasted_iota(jnp.int32, (2 * win, LANES), 1) < HEAD_DIM
    top = jnp.where(lax.broadcasted_iota(jnp.int32, (2 * win, LANES), 0) < win, 1.0, 0.0)
    return jnp.where(left, top, 1.0 - top).astype(BF16)


def _pair_scores(q_tiles, k_win, bias):
    left = _left_lanes()
    zero = jnp.zeros((), BF16)
    rows = []
    for q in q_tiles:
        rows += [jnp.where(left, q, zero), jnp.where(left, zero, q)]
    return _dot_nt(jnp.concatenate(rows, axis=0), k_win) + bias


def _pair_values(s_ref, v_win, ones_bd, sink_ref, p_ref, aux_ref):
    win = v_win.shape[0]
    n = s_ref.shape[0] // (2 * Q_TILE)
    left = _left_lanes()
    zero = jnp.zeros((), BF16)
    for i in range(n):
        ms = []
        for side in range(2):
            rows = pl.ds((2 * i + side) * Q_TILE, Q_TILE)
            s = s_ref[rows, :]
            m = jnp.broadcast_to(jnp.max(s, axis=-1, keepdims=True), (Q_TILE, LANES))
            if sink_ref is not None:
                m = jnp.maximum(m, sink_ref[rows, :])
            p = jnp.exp2(s - jnp.concatenate([m] * (win // LANES), axis=1)).astype(BF16)
            p_ref[i * Q_TILE:(i + 1) * Q_TILE, side * win:(side + 1) * win] = p
            ms.append(m)
        m_pair = jnp.where(left, ms[0], ms[1])
        if sink_ref is not None:
            sink_pair = jnp.where(left, sink_ref[pl.ds(2 * i * Q_TILE, Q_TILE), :],
                                  sink_ref[pl.ds((2 * i + 1) * Q_TILE, Q_TILE), :])
            aux_ref[i * Q_TILE:(i + 1) * Q_TILE, :] = jnp.exp2(sink_pair - m_pair)
        else:
            aux_ref[i * Q_TILE:(i + 1) * Q_TILE, :] = m_pair
    vbd = jnp.concatenate([jnp.where(left, v_win, zero), jnp.where(left, zero, v_win)], axis=0)
    on = _dot(p_ref[...], jnp.concatenate([vbd, ones_bd], axis=1))
    return on[:, :LANES], on[:, LANES:], aux_ref[...]


def _tile_variant(t, n_tiles):
    if isinstance(t, int):
        return 0 if t == 0 else (2 if t == n_tiles - 1 else 1)
    return jnp.where(t == 0, 0, jnp.where(t == n_tiles - 1, 2, 1))


def _dil_geometry(seq_full, grp):
    r = DIL_PATTERNS[grp][1]
    seq = seq_full // r
    win = min(seq, Q_TILE + 2 * DIL_HALF)
    n_tiles = seq // Q_TILE
    pad = (win - Q_TILE) // 2
    offs = [0] if n_tiles == 1 else [0, -pad, -2 * pad]
    return r, seq, win, n_tiles, pad, offs


def _dilated_kernel(seq_full, a0_ref, a1_ref, a2_ref, out_ref, o_nat, l_nat, b0, b1, b2, ones_ref, ones2_ref,
                    s_wide, s_narrow, p_wide, p_narrow, aux_ref):
    a_refs = (a0_ref, a1_ref, a2_ref)
    bias_refs = (b0, b1, b2)
    s_refs = (s_wide, s_wide, s_narrow)
    p_refs = (p_wide, p_wide, p_narrow)
    slopes = _alibi_slopes(N_DIL_SUB)

    @pl.when(pl.program_id(0) == 0)
    def _():
        for grp in range(3):
            r, seq, win, n_tiles, pad, offs = _dil_geometry(seq_full, grp)
            for v, off in enumerate(offs):
                for pr in range(DIL_PAIRS):
                    hs = slopes[grp * DIL_HEADS + 2 * pr:grp * DIL_HEADS + 2 * pr + 2]
                    bias_refs[grp][v, pr] = _band_bias(Q_TILE, win, off, DIL_HALF, [s * r for s in hs])
        ones_ref[...] = _ones_blockdiag(_dil_geometry(seq_full, 0)[2])
        ones2_ref[...] = _ones_blockdiag(_dil_geometry(seq_full, 2)[2])

    def pair_tile(grp, c, t, pr, slot):
        r, seq, win, n_tiles, pad, offs = _dil_geometry(seq_full, grp)
        a_ref = a_refs[grp]
        ones_bd_ref = ones2_ref if grp == 2 else ones_ref
        if n_tiles == 1:
            q0, ks, var = 0, 0, 0
        else:
            q0 = pl.multiple_of(t * Q_TILE, Q_TILE)
            ks = pl.multiple_of(jnp.clip(q0 - pad, 0, seq - win), DIL_HALF)
            var = _tile_variant(t, n_tiles)
        lo = c * GROUP_COLS + pr * LANES
        q = a_ref[pl.ds(q0, Q_TILE), lo:lo + LANES]
        k = a_ref[pl.ds(ks, win), DIL_OUT + lo:DIL_OUT + lo + LANES]
        v = a_ref[pl.ds(ks, win), 2 * DIL_OUT + lo:2 * DIL_OUT + lo + LANES]
        s_ref = s_refs[grp].at[slot]
        s_ref[...] = _pair_scores([q], k, bias_refs[grp][var, pr])
        num, den, m = _pair_values(s_ref, v, ones_bd_ref[...], None, p_refs[grp].at[slot], aux_ref.at[slot])
        rows = pl.ds(q0, Q_TILE) if r == 1 else pl.ds(c + r * q0, Q_TILE, stride=r)
        o_nat[grp, pr, rows, :] = num / den
        l_nat[grp, pr, rows, :] = m + jnp.log2(den)

    for grp in range(3):
        r, seq, win, n_tiles, pad, offs = _dil_geometry(seq_full, grp)
        per_body = DIL_INFLIGHT // DIL_PAIRS
        if n_tiles == 1:
            for c in range(r):
                for pr in range(DIL_PAIRS):
                    pair_tile(grp, c, 0, pr, c * DIL_PAIRS + pr)
        elif r >= per_body:
            for c0 in range(0, r, per_body):
                def body(t, carry, grp=grp, c0=c0):
                    for j in range(per_body):
                        for pr in range(DIL_PAIRS):
                            pair_tile(grp, c0 + j, t, pr, j * DIL_PAIRS + pr)
                    return carry
                lax.fori_loop(0, n_tiles, body, 0)
        else:
            tiles_per_body = per_body // r
            def body(i, carry, grp=grp, r=r, tiles_per_body=tiles_per_body):
                for j in range(tiles_per_body):
                    for c in range(r):
                        for pr in range(DIL_PAIRS):
                            pair_tile(grp, c, i * tiles_per_body + j, pr, (j * r + c) * DIL_PAIRS + pr)
                return carry
            lax.fori_loop(0, n_tiles // tiles_per_body, body, 0)

    def combine(i, carry):
        rows = pl.ds(pl.multiple_of(i * ROW_TILE, ROW_TILE), ROW_TILE)
        for pr in range(DIL_PAIRS):
            ls = [l_nat[g, pr, rows, :] for g in range(3)]
            mx = jnp.maximum(jnp.maximum(ls[0], ls[1]), ls[2])
            es = [jnp.exp2(l - mx) for l in ls]
            num = es[0] * o_nat[0, pr, rows, :] + es[1] * o_nat[1, pr, rows, :] + es[2] * o_nat[2, pr, rows, :]
            out_ref[rows, pr * LANES:(pr + 1) * LANES] = (num / (es[0] + es[1] + es[2])).astype(BF16)
        return carry

    lax.fori_loop(0, seq_full // ROW_TILE, combine, 0)


def _dilated_attention(a0, a1, a2, batch, seq_full):
    views, in_specs, bias_shapes = [], [], []
    for grp, a in enumerate((a0, a1, a2)):
        r, seq, win, n_tiles, pad, offs = _dil_geometry(seq_full, grp)
        views.append(a.reshape(batch, seq, r * GROUP_COLS))
        in_specs.append(pl.BlockSpec((None, seq, r * GROUP_COLS), lambda b: (b, 0, 0)))
        bias_shapes.append(pltpu.VMEM((len(offs), DIL_PAIRS, 2 * Q_TILE, win), F32))
    win0 = _dil_geometry(seq_full, 0)[2]
    win2 = _dil_geometry(seq_full, 2)[2]
    n_straight = DIL_PATTERNS[2][1] * DIL_PAIRS
    return pl.pallas_call(
        functools.partial(_dilated_kernel, seq_full),
        grid=(batch,),
        in_specs=in_specs,
        out_specs=pl.BlockSpec((None, seq_full, DIL_OUT), lambda b: (b, 0, 0)),
        out_shape=jax.ShapeDtypeStruct((batch, seq_full, DIL_OUT), BF16),
        scratch_shapes=[pltpu.VMEM((3, DIL_PAIRS, seq_full, LANES), F32),
                        pltpu.VMEM((3, DIL_PAIRS, seq_full, LANES), F32)]
        + bias_shapes + [pltpu.VMEM((2 * win0, LANES), BF16), pltpu.VMEM((2 * win2, LANES), BF16)]
        + [pltpu.VMEM((n, 2 * Q_TILE, w), F32) for n, w in ((DIL_INFLIGHT, win0), (n_straight, win2))]
        + [pltpu.VMEM((n, Q_TILE, 2 * w), BF16) for n, w in ((DIL_INFLIGHT, win0), (n_straight, win2))]
        + [pltpu.VMEM((max(DIL_INFLIGHT, n_straight), Q_TILE, LANES), F32)],
        compiler_params=pltpu.CompilerParams(
            dimension_semantics=("arbitrary",), vmem_limit_bytes=VMEM_LIMIT),
        name="dilated",
    )(*views)


WIN_WINDOW = Q_TILE + 2 * WIN_HALF


def _window_pair_heads():
    group = WIN_Q_HEADS // WIN_KV_HEADS
    pairs = []
    for u in range(WIN_UNITS):
        for i in range(WIN_UNIT_PAIRS):
            pairs.append((2 * u * group + i, (2 * u + 1) * group + i))
    return pairs


def _window_kernel(seq, q_hbm, kv_hbm, sink_ref, o_ref, bias_ref, ones_ref, sink_rows_ref, s_ref, p_ref,
                   aux_ref, q_buf, kv_buf, sems):
    n_tiles = seq // Q_TILE
    slopes = _alibi_slopes(WIN_Q_HEADS)
    pair_heads = _window_pair_heads()
    b = pl.program_id(0)
    last_seq = b + 1 == pl.num_programs(0)
    slot = b % 2

    def fetch(seq_idx, sl):
        return (pltpu.make_async_copy(q_hbm.at[seq_idx], q_buf.at[sl], sems.at[0, sl]),
                pltpu.make_async_copy(kv_hbm.at[seq_idx], kv_buf.at[sl], sems.at[1, sl]))

    @pl.when(b == 0)
    def _():
        for u in range(WIN_UNITS):
            heads = [h for pr in pair_heads[u * WIN_UNIT_PAIRS:(u + 1) * WIN_UNIT_PAIRS] for h in pr]
            for v, off in enumerate((0, -WIN_HALF, -2 * WIN_HALF)):
                bias_ref[v, u] = _band_bias(Q_TILE, WIN_WINDOW, off, WIN_HALF, [slopes[h] for h in heads])
            sink_rows_ref[u] = jnp.concatenate(
                [jnp.broadcast_to(sink_ref[0:1, h:h + 1] * LOG2E, (Q_TILE, LANES)) for h in heads], axis=0)
        ones_ref[...] = _ones_blockdiag(WIN_WINDOW)

    def rows_of(t):
        if isinstance(t, int):
            return t * Q_TILE, min(max(t * Q_TILE - WIN_HALF, 0), seq - WIN_WINDOW)
        q0 = pl.multiple_of(t * Q_TILE, Q_TILE)
        ks = pl.multiple_of(jnp.clip(q0 - WIN_HALF, 0, seq - WIN_WINDOW), Q_TILE)
        return q0, ks

    def unit_cols(u):
        return [(u * WIN_UNIT_PAIRS + i) * LANES for i in range(WIN_UNIT_PAIRS)]

    def score_stage(sl, t, k_set):
        q0, ks = rows_of(t)
        var = _tile_variant(t, n_tiles)
        for u in range(WIN_UNITS):
            qs = [q_buf[sl, pl.ds(q0, Q_TILE), c:c + LANES] for c in unit_cols(u)]
            k = kv_buf[sl, pl.ds(ks, WIN_WINDOW), u * LANES:(u + 1) * LANES]
            s_ref[k_set * WIN_UNITS + u] = _pair_scores(qs, k, bias_ref[var, u])

    def value_stage(sl, t, k_set):
        q0, ks = rows_of(t)
        for u in range(WIN_UNITS):
            work = k_set * WIN_UNITS + u
            v = kv_buf[sl, pl.ds(ks, WIN_WINDOW), WIN_KV + u * LANES:WIN_KV + (u + 1) * LANES]
            num, den, sink_term = _pair_values(s_ref.at[work], v, ones_ref[...], sink_rows_ref.at[u],
                                               p_ref.at[work], aux_ref.at[work])
            o = (num / (den + sink_term)).astype(BF16)
            for i, c in enumerate(unit_cols(u)):
                o_ref[pl.ds(q0, Q_TILE), c:c + LANES] = o[i * Q_TILE:(i + 1) * Q_TILE]

    per_body = WIN_TILES_PER_BODY

    @pl.when(b == 0)
    def _():
        for cp in fetch(0, 0):
            cp.start()
        for cp in fetch(0, 0):
            cp.wait()
        score_stage(0, 0, 0)

    @pl.when(jnp.logical_not(last_seq))
    def _():
        for cp in fetch(b + 1, 1 - slot):
            cp.start()

    def body(i, carry):
        t = per_body * i
        for j in range(per_body):
            score_stage(slot, t + j + 1, (j + 1) % per_body)
            value_stage(slot, t + j, j)
        return carry

    lax.fori_loop(0, n_tiles // per_body - 1, body, 0)

    @pl.when(jnp.logical_not(last_seq))
    def _():
        for cp in fetch(b + 1, 1 - slot):
            cp.wait()

    t = n_tiles - per_body
    for j in range(per_body):
        if j + 1 < per_body:
            score_stage(slot, t + j + 1, j + 1)
        else:
            score_stage(1 - slot, 0, 0)
        value_stage(slot, t + j, j)


def _window_attention(q, kv, sink, batch, seq):
    unit_rows = 2 * WIN_UNIT_PAIRS * Q_TILE
    return pl.pallas_call(
        functools.partial(_window_kernel, seq),
        grid=(batch,),
        in_specs=[pl.BlockSpec(memory_space=pl.ANY),
                  pl.BlockSpec(memory_space=pl.ANY),
                  pl.BlockSpec((1, WIN_Q_HEADS), lambda b: (0, 0))],
        out_specs=pl.BlockSpec((None, seq, WIN_Q), lambda b: (b, 0, 0)),
        out_shape=jax.ShapeDtypeStruct((batch, seq, WIN_Q), BF16),
        scratch_shapes=[pltpu.VMEM((3, WIN_UNITS, unit_rows, WIN_WINDOW), F32),
                        pltpu.VMEM((2 * WIN_WINDOW, LANES), BF16),
                        pltpu.VMEM((WIN_UNITS, unit_rows, LANES), F32),
                        pltpu.VMEM((WIN_TILES_PER_BODY * WIN_UNITS, unit_rows, WIN_WINDOW), F32),
                        pltpu.VMEM((WIN_TILES_PER_BODY * WIN_UNITS, unit_rows // 2, 2 * WIN_WINDOW), BF16),
                        pltpu.VMEM((WIN_TILES_PER_BODY * WIN_UNITS, unit_rows // 2, LANES), F32),
                        pltpu.VMEM((2, seq, WIN_Q), BF16),
                        pltpu.VMEM((2, seq, 2 * WIN_KV), BF16),
                        pltpu.SemaphoreType.DMA((2, 2))],
        compiler_params=pltpu.CompilerParams(
            dimension_semantics=("arbitrary",), vmem_limit_bytes=VMEM_LIMIT),
        name="window",
    )(q.reshape(batch, seq, WIN_Q), kv.reshape(batch, seq, 2 * WIN_KV), sink)


def _merge_kernel(oa_ref, ow_ref, h_ref, g_ref, wgate_ref, bgate_ref, wa_ref, wb_ref, wo_ref, out_ref):
    for r0 in range(0, h_ref.shape[0], MERGE_CHUNK):
        rows = slice(r0, r0 + MERGE_CHUNK)
        h = h_ref[rows, :]
        xb = _rms(h, g_ref[...]).astype(BF16)
        merged = None
        for br, (o_ref, w_ref) in enumerate(((oa_ref, wa_ref), (ow_ref, wb_ref))):
            cols = slice(br * D_MODEL, (br + 1) * D_MODEL)
            gate = jax.nn.sigmoid(_dot(xb, wgate_ref[:, cols]) + bgate_ref[:, cols])
            term = gate * _dot(o_ref[rows, :], w_ref[...])
            merged = term if merged is None else merged + term
        out_ref[rows, :] = h + _dot(merged.astype(BF16), wo_ref[...])


def _merge_out(oa, ow, h, g, wgate, bgate, wa, wb, wo, layer):
    m = h.shape[0]
    row = lambda c: pl.BlockSpec((MERGE_TILE, c), lambda i: (i, 0))
    return pl.pallas_call(
        _merge_kernel,
        grid=(m // MERGE_TILE,),
        in_specs=[row(DIL_OUT), row(WIN_Q), row(D_MODEL)]
        + [_layer_block(a, layer) for a in (g, wgate, bgate, wa, wb, wo)],
        out_specs=row(D_MODEL),
        out_shape=jax.ShapeDtypeStruct((m, D_MODEL), F32),
        compiler_params=pltpu.CompilerParams(
            dimension_semantics=("arbitrary",), vmem_limit_bytes=VMEM_LIMIT),
        name="merge_out",
    )(oa, ow, h, g, wgate, bgate, wa, wb, wo)


TOKEN_TILE = 256
CUM_CHUNK = TOKEN_TILE
SLOT_WINDOW = 64
SLOT_ALIGN = 16


def _prefix_exclusive(mask_f, tri):
    e, s = mask_f.shape
    carry = jnp.zeros((e, 1), F32)
    parts, carries = [], []
    for j in range(0, s, CUM_CHUNK):
        blk = mask_f[:, j:j + CUM_CHUNK]
        carries.append(carry)
        parts.append(_dot(blk.astype(BF16), tri) + carry)
        carry = carry + jnp.sum(blk, axis=-1, keepdims=True)
    return jnp.concatenate(parts, axis=-1), carries + [carry]


def _route_kernel(cap, h_ref, g_ref, whi_ref, wlo_ref, hn_ref, rank_ref, aff_ref, starts_ref):
    seq = h_ref.shape[0]
    hn = _rms(h_ref[...], g_ref[...])
    hn_hi = hn.astype(BF16)
    hn_lo = (hn - hn_hi.astype(F32)).astype(BF16)
    hn_ref[...] = hn_hi
    both = _dot_nt(jnp.concatenate([whi_ref[...], wlo_ref[...]], axis=0), hn_hi)
    logits = both[:N_EXPERTS] + both[N_EXPERTS:] + _dot_nt(whi_ref[...], hn_lo)
    mx = jnp.max(logits, axis=0, keepdims=True)
    ex = jnp.exp(logits - mx)
    aff = ex / jnp.sum(ex, axis=0, keepdims=True)
    bits = pltpu.bitcast(aff, jnp.int32)

    def enough(t):
        return jnp.sum(jnp.where(bits >= t, 1.0, 0.0), axis=-1, keepdims=True) >= float(cap)

    def search4(_, c):
        lo, hi = c
        q = (hi - lo) >> 2
        m1, m2, m3 = lo + q, lo + 2 * q, lo + 3 * q
        ok1, ok2, ok3 = enough(m1), enough(m2), enough(m3)
        return (jnp.where(ok3, m3, jnp.where(ok2, m2, jnp.where(ok1, m1, lo))),
                jnp.where(ok3, hi, jnp.where(ok2, m3, jnp.where(ok1, m2, m1))))

    def search2(_, c):
        lo, hi = c
        mid = lo + ((hi - lo) >> 1)
        ok = enough(mid)
        return jnp.where(ok, mid, lo), jnp.where(ok, hi, mid)

    lo0 = jnp.zeros((N_EXPERTS, 1), jnp.int32)
    hi0 = jnp.full((N_EXPERTS, 1), 0x3F800001, jnp.int32)
    thr, _ = lax.fori_loop(0, 3, search2, lax.fori_loop(0, 15, search4, (lo0, hi0)))

    r_i = lax.broadcasted_iota(jnp.int32, (CUM_CHUNK, CUM_CHUNK), 0)
    c_i = lax.broadcasted_iota(jnp.int32, (CUM_CHUNK, CUM_CHUNK), 1)
    tri = jnp.where(r_i < c_i, 1.0, 0.0).astype(BF16)
    gt = jnp.where(bits > thr, 1.0, 0.0)
    eq = jnp.where(bits == thr, 1.0, 0.0)
    need = float(cap) - jnp.sum(gt, axis=-1, keepdims=True)
    tie_rank, _ = _prefix_exclusive(eq, tri)
    sel = gt + eq * jnp.where(tie_rank < need, 1.0, 0.0)
    slot, starts = _prefix_exclusive(sel, tri)
    rank = jnp.where(sel > 0.0, slot, -1.0)
    for t in range(seq // TOKEN_TILE):
        rank_ref[t] = rank[:, t * TOKEN_TILE:(t + 1) * TOKEN_TILE]
        aff_ref[t] = aff[:, t * TOKEN_TILE:(t + 1) * TOKEN_TILE]
    lane = lax.broadcasted_iota(jnp.int32, (N_EXPERTS, LANES), 1)
    acc = jnp.zeros((N_EXPERTS, LANES), F32)
    for t, c in enumerate(starts):
        acc = acc + jnp.where(lane == t, c, 0.0)
    starts_ref[...] = acc


def _route(h, g, w_hi, w_lo, layer, batch, seq, cap):
    n_tiles = seq // TOKEN_TILE
    return pl.pallas_call(
        functools.partial(_route_kernel, cap),
        grid=(batch,),
        in_specs=[pl.BlockSpec((None, seq, D_MODEL), lambda b: (b, 0, 0))]
        + [_layer_block(a, layer) for a in (g, w_hi, w_lo)],
        out_specs=[pl.BlockSpec((None, seq, D_MODEL), lambda b: (b, 0, 0)),
                   pl.BlockSpec((None, n_tiles, N_EXPERTS, TOKEN_TILE), lambda b: (b, 0, 0, 0)),
                   pl.BlockSpec((None, n_tiles, N_EXPERTS, TOKEN_TILE), lambda b: (b, 0, 0, 0)),
                   pl.BlockSpec((None, N_EXPERTS, LANES), lambda b: (b, 0, 0))],
        out_shape=[jax.ShapeDtypeStruct((batch, seq, D_MODEL), BF16),
                   jax.ShapeDtypeStruct((batch, n_tiles, N_EXPERTS, TOKEN_TILE), F32),
                   jax.ShapeDtypeStruct((batch, n_tiles, N_EXPERTS, TOKEN_TILE), F32),
                   jax.ShapeDtypeStruct((batch, N_EXPERTS, LANES), F32)],
        compiler_params=pltpu.CompilerParams(
            dimension_semantics=("arbitrary",), vmem_limit_bytes=VMEM_LIMIT),
        name="route",
    )(h.reshape(batch, seq, D_MODEL), g, w_hi, w_lo)


def _slot_rows(cap):
    return cap + SLOT_WINDOW


def _window_starts(win_ref, b, t, p, cap):
    base = b * (N_EXPERTS * LANES) + t
    return [pl.multiple_of(jnp.minimum(win_ref[base + e * LANES] + p * SLOT_WINDOW, cap), SLOT_ALIGN)
            for e in range(N_EXPERTS)]


def _window_hits(rank_tile, starts):
    rows = lax.broadcasted_iota(jnp.int32, (SLOT_WINDOW, TOKEN_TILE), 0).astype(F32)
    return [(rank_tile[e:e + 1, :] - starts[e].astype(F32)) == rows for e in range(N_EXPERTS)]


def _one_hot(hits):
    return jnp.concatenate([jnp.where(h, 1.0, 0.0).astype(BF16) for h in hits], axis=0)


def _gather_kernel(cap, win_ref, npass_ref, hn_ref, rank_ref, aff_ref, xe_ref, gate_ref):
    b = pl.program_id(0)
    xe_ref[...] = jnp.zeros_like(xe_ref)
    gate_ref[...] = jnp.zeros_like(gate_ref)
    n_tiles = hn_ref.shape[0] // TOKEN_TILE

    def one_pass(p, t):
        starts = _window_starts(win_ref, b, t, p, cap)
        hits = _window_hits(rank_ref[t], starts)
        rows = _dot(_one_hot(hits), hn_ref[t * TOKEN_TILE:(t + 1) * TOKEN_TILE, :]).astype(BF16)
        aff_tile = aff_ref[t]
        for e in range(N_EXPERTS):
            win = pl.ds(starts[e], SLOT_WINDOW)
            xe_ref[e, win, :] += rows[e * SLOT_WINDOW:(e + 1) * SLOT_WINDOW]
            g = jnp.sum(jnp.where(hits[e], aff_tile[e:e + 1, :], 0.0), axis=-1, keepdims=True)
            gate_ref[e, win, :] += jnp.broadcast_to(g, (SLOT_WINDOW, LANES))

    for t in range(n_tiles):
        lax.fori_loop(1, npass_ref[b * LANES + t], lambda p, c, t=t: (one_pass(p, t), c)[1], 0)
    for t in range(n_tiles):
        one_pass(0, t)


def _gather(hn, rank, aff, win, npass, cap):
    batch, seq, _ = hn.shape
    n_tiles = seq // TOKEN_TILE
    grid_spec = pltpu.PrefetchScalarGridSpec(
        num_scalar_prefetch=2,
        grid=(batch,),
        in_specs=[pl.BlockSpec((None, seq, D_MODEL), lambda b, *_: (b, 0, 0)),
                  pl.BlockSpec((None, n_tiles, N_EXPERTS, TOKEN_TILE), lambda b, *_: (b, 0, 0, 0)),
                  pl.BlockSpec((None, n_tiles, N_EXPERTS, TOKEN_TILE), lambda b, *_: (b, 0, 0, 0))],
        out_specs=[pl.BlockSpec((None, N_EXPERTS, _slot_rows(cap), D_MODEL), lambda b, *_: (b, 0, 0, 0)),
                   pl.BlockSpec((None, N_EXPERTS, _slot_rows(cap), LANES), lambda b, *_: (b, 0, 0, 0))])
    return pl.pallas_call(
        functools.partial(_gather_kernel, cap),
        grid_spec=grid_spec,
        out_shape=[jax.ShapeDtypeStruct((batch, N_EXPERTS, _slot_rows(cap), D_MODEL), BF16),
                   jax.ShapeDtypeStruct((batch, N_EXPERTS, _slot_rows(cap), LANES), F32)],
        compiler_params=pltpu.CompilerParams(
            dimension_semantics=("arbitrary",), vmem_limit_bytes=VMEM_LIMIT),
        name="gather",
    )(win, npass, hn, rank, aff)


FFN_SEQS = 4


def _ffn_kernel(cap, xe_ref, gate_ref, wg_ref, wu_ref, wd_ref, y_ref, wg_b, wu_b, wd_b):
    d = xe_ref.shape[-1]

    @pl.when(pl.program_id(1) == 0)
    def _():
        for src, dst in ((wg_ref, wg_b), (wu_ref, wu_b), (wd_ref, wd_b)):
            dst[...] = src[...].astype(BF16)

    xe = xe_ref[:, :cap, :].reshape(FFN_SEQS * cap, d)
    a = _dot(xe, wg_b[...])
    u = _dot(xe, wu_b[...])
    y = _dot((jax.nn.silu(a) * u).astype(BF16), wd_b[...])
    gate = gate_ref[:, :cap, :].reshape(FFN_SEQS * cap, LANES)
    y = y * jnp.concatenate([gate] * (d // LANES), axis=1)
    y_ref[:, :cap, :] = y.astype(BF16).reshape(FFN_SEQS, cap, d)
    y_ref[:, cap:, :] = jnp.zeros((FFN_SEQS, SLOT_WINDOW, d), BF16)


def _ffn(xe, gate, wg, wu, wd, layer, cap):
    batch = xe.shape[0]
    d_exp = wg.shape[-1]
    slots = pl.BlockSpec((FFN_SEQS, None, _slot_rows(cap), D_MODEL), lambda e, i: (i, e, 0, 0))
    return pl.pallas_call(
        functools.partial(_ffn_kernel, cap),
        grid=(N_EXPERTS, batch // FFN_SEQS),
        in_specs=[slots,
                  pl.BlockSpec((FFN_SEQS, None, _slot_rows(cap), LANES), lambda e, i: (i, e, 0, 0)),
                  pl.BlockSpec((None, None, D_MODEL, d_exp), lambda e, i: (layer, e, 0, 0)),
                  pl.BlockSpec((None, None, D_MODEL, d_exp), lambda e, i: (layer, e, 0, 0)),
                  pl.BlockSpec((None, None, d_exp, D_MODEL), lambda e, i: (layer, e, 0, 0))],
        out_specs=slots,
        out_shape=jax.ShapeDtypeStruct(xe.shape, BF16),
        scratch_shapes=[pltpu.VMEM((D_MODEL, d_exp), BF16), pltpu.VMEM((D_MODEL, d_exp), BF16),
                        pltpu.VMEM((d_exp, D_MODEL), BF16)],
        compiler_params=pltpu.CompilerParams(
            dimension_semantics=("arbitrary", "arbitrary"), vmem_limit_bytes=VMEM_LIMIT),
        name="ffn",
    )(xe, gate, wg, wu, wd)


TILES_PER_ROW_TILE = ROW_TILE // TOKEN_TILE


def _scatter_kernel(cap, project, win_ref, npass_ref, y_hbm, rank_ref, h_ref, g_ref, *rest):
    if project:
        w_ref, out_ref, a0_ref, a1_ref, a2_ref, qw_ref, kv_ref, xs_ref, y_buf, y_sem = rest
    else:
        out_ref, y_buf, y_sem = rest
    b = pl.program_id(0)
    j = pl.program_id(1)
    slot = b % 2

    def y_copy(seq_idx, buf_slot):
        return pltpu.make_async_copy(y_hbm.at[seq_idx], y_buf.at[buf_slot], y_sem.at[buf_slot])

    @pl.when((b == 0) & (j == 0))
    def _():
        y_copy(0, 0).start()

    @pl.when(j == 0)
    def _():
        y_copy(b, slot).wait()

    @pl.when((j == 0) & (b + 1 < pl.num_programs(0)))
    def _():
        y_copy(b + 1, 1 - slot).start()

    y_ref = y_buf.at[slot]
    out_ref[...] = h_ref[...]

    def one_pass(p, tt):
        t = j * TILES_PER_ROW_TILE + tt
        rows = slice(tt * TOKEN_TILE, (tt + 1) * TOKEN_TILE)
        starts = _window_starts(win_ref, b, t, p, cap)
        put = _one_hot(_window_hits(rank_ref[tt], starts))
        yw = jnp.concatenate([y_ref[e, pl.ds(starts[e], SLOT_WINDOW), :] for e in range(N_EXPERTS)], axis=0)
        out_ref[rows, :] += lax.dot_general(put, yw, (((0,), (0,)), ((), ())), preferred_element_type=F32)

    for tt in range(TILES_PER_ROW_TILE):
        n_pass = npass_ref[b * LANES + j * TILES_PER_ROW_TILE + tt]
        lax.fori_loop(1, n_pass, lambda p, c, tt=tt: (one_pass(p, tt), c)[1], 0)
    for tt in range(TILES_PER_ROW_TILE):
        one_pass(0, tt)
    if project:
        _project(out_ref[...], g_ref, w_ref, a0_ref, a1_ref, a2_ref, qw_ref, kv_ref, xs_ref)
    else:
        out_ref[...] = _rms(out_ref[...], g_ref[...])


def _scatter(y, rank, h, win, npass, cap, g, layer, w_next=None):
    batch, n_tiles = rank.shape[:2]
    seq = h.shape[1]
    steps = n_tiles // TILES_PER_ROW_TILE
    project = w_next is not None
    in_specs = [pl.BlockSpec(memory_space=pl.ANY),
                pl.BlockSpec((None, TILES_PER_ROW_TILE, N_EXPERTS, TOKEN_TILE), lambda b, j, *_: (b, j, 0, 0)),
                pl.BlockSpec((None, ROW_TILE, D_MODEL), lambda b, j, *_: (b, j, 0)),
                _layer_block(g, layer)]
    out_shape = [jax.ShapeDtypeStruct(h.shape, F32)]
    out_specs = [pl.BlockSpec((None, ROW_TILE, D_MODEL), lambda b, j, *_: (b, j, 0))]
    args = [win, npass, y, rank, h, g]
    scratch = []
    if project:
        in_specs.append(_layer_block(w_next, layer))
        args.append(w_next)
        proj_shape, proj_specs = _proj_outputs(batch * seq, lambda b, j, *_: (b * steps + j, 0))
        out_shape += proj_shape
        out_specs += proj_specs
        scratch = [pltpu.VMEM((D_MODEL // LANES, ROW_TILE, LANES), F32)]
    scratch += [pltpu.VMEM((2,) + y.shape[1:], BF16), pltpu.SemaphoreType.DMA((2,))]
    grid_spec = pltpu.PrefetchScalarGridSpec(
        num_scalar_prefetch=2, grid=(batch, steps), in_specs=in_specs, out_specs=out_specs,
        scratch_shapes=scratch)
    return pl.pallas_call(
        functools.partial(_scatter_kernel, cap, project),
        grid_spec=grid_spec,
        out_shape=out_shape,
        compiler_params=pltpu.CompilerParams(
            dimension_semantics=("arbitrary", "arbitrary"), vmem_limit_bytes=VMEM_LIMIT),
        name="scatter_proj" if project else "scatter_final",
    )(*args)


def _slot_windows(starts, n_tiles, cap):
    s = starts.astype(jnp.int32)
    first = (s // SLOT_ALIGN) * SLOT_ALIGN
    span = s[:, :, 1:n_tiles + 1] - first[:, :, :n_tiles]
    npass = jnp.maximum(jnp.max(-(-span // SLOT_WINDOW), axis=1), 1)
    npass = jnp.pad(npass, ((0, 0), (0, LANES - n_tiles)))
    return first.reshape(-1), npass.reshape(-1)


def _arrange_w_in(w):
    scale = LOG2E * HEAD_DIM ** -0.5
    qa, ka, va = w[..., :DIL_QKV], w[..., DIL_QKV:2 * DIL_QKV], w[..., 2 * DIL_QKV:3 * DIL_QKV]
    rest = w[..., 3 * DIL_QKV:]
    parts = []
    for g in range(len(DIL_PATTERNS)):
        sl = slice(g * DIL_OUT, (g + 1) * DIL_OUT)
        parts += [qa[..., sl] * scale, ka[..., sl], va[..., sl]]
    for pr in _window_pair_heads():
        parts += [rest[..., h * HEAD_DIM:(h + 1) * HEAD_DIM] * scale for h in pr]
    parts.append(rest[..., WIN_Q:WIN_Q + 2 * WIN_KV])
    return jnp.concatenate(parts, axis=-1).astype(BF16), rest[..., WIN_Q + 2 * WIN_KV:].astype(BF16)


def _arrange_w_branch_b(w):
    return jnp.concatenate([w[:, h * HEAD_DIM:(h + 1) * HEAD_DIM] for pr in _window_pair_heads() for h in pr],
                           axis=1).astype(BF16)


def kernel(x, norm_mix, w_in, w_branch_a, w_branch_b, b_gate, sink_logit, w_out, norm_ffn, w_router,
           w_expert_gate, w_expert_up, w_expert_down, norm_final):
    batch, seq, d = x.shape
    depth = w_in.shape[0]
    cap = CAPACITY_FACTOR * seq // N_EXPERTS
    m = batch * seq
    w_attn, w_gates = _arrange_w_in(w_in)
    wa, wb, wo = w_branch_a.astype(BF16), _arrange_w_branch_b(w_branch_b), w_out.astype(BF16)
    g_mix, g_ffn, bg = norm_mix[:, None, :], norm_ffn[:, None, :], b_gate[:, None, :]
    wr = jnp.swapaxes(w_router, 1, 2)
    wr_hi = wr.astype(BF16)
    wr_lo = (wr - wr_hi.astype(F32)).astype(BF16)
    h = x.reshape(m, d)
    proj = _norm_proj(h, g_mix, w_attn, 0)
    for l in range(depth):
        a0, a1, a2, qw, kv = proj
        oa = _dilated_attention(a0, a1, a2, batch, seq).reshape(m, DIL_OUT)
        ow = _window_attention(qw, kv, sink_logit[l][None, :], batch, seq).reshape(m, WIN_Q)
        h = _merge_out(oa, ow, h, g_mix, w_gates, bg, wa, wb, wo, l)
        hn, rank, aff, starts = _route(h, g_ffn, wr_hi, wr_lo, l, batch, seq, cap)
        win, npass = _slot_windows(starts, seq // TOKEN_TILE, cap)
        xe, gate = _gather(hn, rank, aff, win, npass, cap)
        y = _ffn(xe, gate, w_expert_gate, w_expert_up, w_expert_down, l, cap)
        h3 = h.reshape(batch, seq, d)
        if l + 1 == depth:
            return _scatter(y, rank, h3, win, npass, cap, norm_final[None, None, :], 0)[0]
        h3, *proj = _scatter(y, rank, h3, win, npass, cap, g_mix, l + 1, w_attn)
        h = h3.reshape(m, d)
```

```python
import functools

import jax
import jax.numpy as jnp
from jax import lax
from jax.experimental import pallas as pl
from jax.experimental.pallas import tpu as pltpu

D_MODEL = 1024
HEAD_DIM = 64
DIL_PATTERNS = ((128, 1), (512, 4), (2048, 16))
DIL_HEADS = 4
N_DIL_SUB = DIL_HEADS * len(DIL_PATTERNS)
DIL_QKV = N_DIL_SUB * HEAD_DIM
DIL_OUT = DIL_HEADS * HEAD_DIM
DIL_HALF = 64
WIN_HALF = 128
WIN_Q = D_MODEL
WIN_Q_HEADS = WIN_Q // HEAD_DIM
WIN_KV_HEADS = 4
WIN_KV = WIN_KV_HEADS * HEAD_DIM
N_EXPERTS = 16
CAPACITY_FACTOR = 2
RMS_EPS = 1e-6
NEG_INF = -1e30
LOG2E = 1.4426950408889634

LANES = 128
GROUP_COLS = 3 * DIL_OUT
DIL_PAIRS = DIL_OUT // LANES
WIN_UNITS = WIN_KV_HEADS // 2
WIN_UNIT_PAIRS = WIN_Q_HEADS // (2 * WIN_UNITS)
Q_TILE = 128
WIN_TILES_PER_BODY = 4
DIL_INFLIGHT = 16
ROW_TILE = 512
MERGE_TILE = 1024
MERGE_CHUNK = 256
VMEM_LIMIT = 56 * 1024 * 1024

F32 = jnp.float32
BF16 = jnp.bfloat16


def _alibi_slopes(n):
    return [float(2.0 ** (-8.0 * i / n)) for i in range(1, n + 1)]


def _rms(x, g):
    return x * lax.rsqrt(jnp.mean(x * x, axis=-1, keepdims=True) + RMS_EPS) * g


def _dot(a, b):
    return jnp.dot(a, b, preferred_element_type=F32)


def _dot_nt(a, b):
    return lax.dot_general(a, b, (((1,), (1,)), ((), ())), preferred_element_type=F32)


def _left_lanes():
    return lax.broadcasted_iota(jnp.int32, (1, LANES), 1) < HEAD_DIM


def _layer_block(a, layer):
    return pl.BlockSpec((None,) + a.shape[1:], lambda *_: (layer, 0, 0), pipeline_mode=pl.Buffered(1))


_C_A = (0, GROUP_COLS, 2 * GROUP_COLS)
_C_QW = 3 * GROUP_COLS
_C_KV = _C_QW + WIN_Q
_COL_CHUNK = 512
_ROW_CHUNK = 256


def _proj_store(ref, xb, w_ref, c0, cw):
    for r0 in range(0, xb.shape[0], _ROW_CHUNK):
        for j in range(0, cw, _COL_CHUNK):
            jw = min(_COL_CHUNK, cw - j)
            ref[r0:r0 + _ROW_CHUNK, j:j + jw] = _dot(xb[r0:r0 + _ROW_CHUNK],
                                                     w_ref[:, c0 + j:c0 + j + jw]).astype(BF16)


def _norm_proj_kernel(x_ref, g_ref, w_ref, a0_ref, a1_ref, a2_ref, qw_ref, kv_ref, xs_ref):
    _project(x_ref[...], g_ref, w_ref, a0_ref, a1_ref, a2_ref, qw_ref, kv_ref, xs_ref)


def _project(x, g_ref, w_ref, a0_ref, a1_ref, a2_ref, qw_ref, kv_ref, xs_ref):
    xn = _rms(x, g_ref[...])
    n_lane_tiles = D_MODEL // LANES
    for j in range(n_lane_tiles):
        xs_ref[j] = xn[:, j * LANES:(j + 1) * LANES]
    xb = xn.astype(BF16)
    _proj_store(a0_ref, xb, w_ref, _C_A[0], GROUP_COLS)
    for grp, ref in ((1, a1_ref), (2, a2_ref)):
        r = DIL_PATTERNS[grp][1]
        n = ROW_TILE // r
        xp = jnp.concatenate(
            [jnp.concatenate([xs_ref[j, pl.ds(c, n, stride=r), :] for j in range(n_lane_tiles)], axis=1)
             for c in range(r)], axis=0).astype(BF16)
        res = _dot(xp, w_ref[:, _C_A[grp]:_C_A[grp] + GROUP_COLS])
        for c in range(r):
            ref[:, c * GROUP_COLS:(c + 1) * GROUP_COLS] = res[c * n:(c + 1) * n].astype(BF16)
    _proj_store(qw_ref, xb, w_ref, _C_QW, WIN_Q)
    _proj_store(kv_ref, xb, w_ref, _C_KV, 2 * WIN_KV)


def _proj_outputs(m, row_index):
    out_shape, out_specs = [], []
    for _, r in DIL_PATTERNS:
        out_shape.append(jax.ShapeDtypeStruct((m // r, r * GROUP_COLS), BF16))
        out_specs.append(pl.BlockSpec((ROW_TILE // r, r * GROUP_COLS), row_index))
    for c in (WIN_Q, 2 * WIN_KV):
        out_shape.append(jax.ShapeDtypeStruct((m, c), BF16))
        out_specs.append(pl.BlockSpec((ROW_TILE, c), row_index))
    return out_shape, out_specs


def _norm_proj(x, g, w, layer):
    m = x.shape[0]
    in_specs = [pl.BlockSpec((ROW_TILE, D_MODEL), lambda i: (i, 0)), _layer_block(g, layer), _layer_block(w, layer)]
    out_shape, out_specs = _proj_outputs(m, lambda i: (i, 0))
    return pl.pallas_call(
        _norm_proj_kernel,
        grid=(m // ROW_TILE,),
        in_specs=in_specs,
        out_specs=out_specs,
        out_shape=out_shape,
        scratch_shapes=[pltpu.VMEM((D_MODEL // LANES, ROW_TILE, LANES), F32)],
        compiler_params=pltpu.CompilerParams(
            dimension_semantics=("arbitrary",), vmem_limit_bytes=VMEM_LIMIT),
        name="norm_proj",
    )(x, g, w)


def _band_bias(tq, win, off, half, slopes):
    col = lax.broadcasted_iota(jnp.int32, (tq, win), 1)
    row = lax.broadcasted_iota(jnp.int32, (tq, win), 0)
    dist = jnp.abs(col - row + off).astype(F32)
    inside = dist <= float(half)
    return jnp.concatenate([jnp.where(inside, dist * (-s * LOG2E), NEG_INF) for s in slopes], axis=0)


def _ones_blockdiag(win):
    left = lax.broadcasted_iota(jnp.int32, (2 * win, LANES), 1) < HEAD_DIM
    top = jnp.where(lax.broadcasted_iota(jnp.int32, (2 * win, LANES), 0) < win, 1.0, 0.0)
    return jnp.where(left, top, 1.0 - top).astype(BF16)


def _pair_scores(q_tiles, k_win, bias):
    left = _left_lanes()
    zero = jnp.zeros((), BF16)
    rows = []
    for q in q_tiles:
        rows += [jnp.where(left, q, zero), jnp.where(left, zero, q)]
    return _dot_nt(jnp.concatenate(rows, axis=0), k_win) + bias


def _pair_values(s_ref, v_win, ones_bd, sink_ref, p_ref, aux_ref):
    win = v_win.shape[0]
    n = s_ref.shape[0] // (2 * Q_TILE)
    left = _left_lanes()
    zero = jnp.zeros((), BF16)
    for i in range(n):
        ms = []
        for side in range(2):
            rows = pl.ds((2 * i + side) * Q_TILE, Q_TILE)
            s = s_ref[rows, :]
            m = jnp.broadcast_to(jnp.max(s, axis=-1, keepdims=True), (Q_TILE, LANES))
            if sink_ref is not None:
                m = jnp.maximum(m, sink_ref[rows, :])
            p = jnp.exp2(s - jnp.concatenate([m] * (win // LANES), axis=1)).astype(BF16)
            p_ref[i * Q_TILE:(i + 1) * Q_TILE, side * win:(side + 1) * win] = p
            ms.append(m)
        m_pair = jnp.where(left, ms[0], ms[1])
        if sink_ref is not None:
            sink_pair = jnp.where(left, sink_ref[pl.ds(2 * i * Q_TILE, Q_TILE), :],
                                  sink_ref[pl.ds((2 * i + 1) * Q_TILE, Q_TILE), :])
            aux_ref[i * Q_TILE:(i + 1) * Q_TILE, :] = jnp.exp2(sink_pair - m_pair)
        else:
            aux_ref[i * Q_TILE:(i + 1) * Q_TILE, :] = m_pair
    vbd = jnp.concatenate([jnp.where(left, v_win, zero), jnp.where(left, zero, v_win)], axis=0)
    on = _dot(p_ref[...], jnp.concatenate([vbd, ones_bd], axis=1))
    return on[:, :LANES], on[:, LANES:], aux_ref[...]


def _tile_variant(t, n_tiles):
    if isinstance(t, int):
        return 0 if t == 0 else (2 if t == n_tiles - 1 else 1)
    return jnp.where(t == 0, 0, jnp.where(t == n_tiles - 1, 2, 1))


def _dil_geometry(seq_full, grp):
    r = DIL_PATTERNS[grp][1]
    seq = seq_full // r
    win = min(seq, Q_TILE + 2 * DIL_HALF)
    n_tiles = seq // Q_TILE
    pad = (win - Q_TILE) // 2
    offs = [0] if n_tiles == 1 else [0, -pad, -2 * pad]
    return r, seq, win, n_tiles, pad, offs


def _dilated_kernel(seq_full, a0_ref, a1_ref, a2_ref, out_ref, o_nat, l_nat, b0, b1, b2, ones_ref, ones2_ref,
                    s_wide, s_narrow, p_wide, p_narrow, aux_ref):
    a_refs = (a0_ref, a1_ref, a2_ref)
    bias_refs = (b0, b1, b2)
    s_refs = (s_wide, s_wide, s_narrow)
    p_refs = (p_wide, p_wide, p_narrow)
    slopes = _alibi_slopes(N_DIL_SUB)

    @pl.when(pl.program_id(0) == 0)
    def _():
        for grp in range(3):
            r, seq, win, n_tiles, pad, offs = _dil_geometry(seq_full, grp)
            for v, off in enumerate(offs):
                for pr in range(DIL_PAIRS):
                    hs = slopes[grp * DIL_HEADS + 2 * pr:grp * DIL_HEADS + 2 * pr + 2]
                    bias_refs[grp][v, pr] = _band_bias(Q_TILE, win, off, DIL_HALF, [s * r for s in hs])
        ones_ref[...] = _ones_blockdiag(_dil_geometry(seq_full, 0)[2])
        ones2_ref[...] = _ones_blockdiag(_dil_geometry(seq_full, 2)[2])

    def pair_tile(grp, c, t, pr, slot):
        r, seq, win, n_tiles, pad, offs = _dil_geometry(seq_full, grp)
        a_ref = a_refs[grp]
        ones_bd_ref = ones2_ref if grp == 2 else ones_ref
        if n_tiles == 1:
            q0, ks, var = 0, 0, 0
        else:
            q0 = pl.multiple_of(t * Q_TILE, Q_TILE)
            ks = pl.multiple_of(jnp.clip(q0 - pad, 0, seq - win), DIL_HALF)
            var = _tile_variant(t, n_tiles)
        lo = c * GROUP_COLS + pr * LANES
        q = a_ref[pl.ds(q0, Q_TILE), lo:lo + LANES]
        k = a_ref[pl.ds(ks, win), DIL_OUT + lo:DIL_OUT + lo + LANES]
        v = a_ref[pl.ds(ks, win), 2 * DIL_OUT + lo:2 * DIL_OUT + lo + LANES]
        s_ref = s_refs[grp].at[slot]
        s_ref[...] = _pair_scores([q], k, bias_refs[grp][var, pr])
        num, den, m = _pair_values(s_ref, v, ones_bd_ref[...], None, p_refs[grp].at[slot], aux_ref.at[slot])
        rows = pl.ds(q0, Q_TILE) if r == 1 else pl.ds(c + r * q0, Q_TILE, stride=r)
        o_nat[grp, pr, rows, :] = num / den
        l_nat[grp, pr, rows, :] = m + jnp.log2(den)

    def combine_rows(rows):
        for pr in range(DIL_PAIRS):
            ls = [l_nat[g, pr, rows, :] for g in range(3)]
            mx = jnp.maximum(jnp.maximum(ls[0], ls[1]), ls[2])
            es = [jnp.exp2(l - mx) for l in ls]
            num = es[0] * o_nat[0, pr, rows, :] + es[1] * o_nat[1, pr, rows, :] + es[2] * o_nat[2, pr, rows, :]
            out_ref[rows, pr * LANES:(pr + 1) * LANES] = (num / (es[0] + es[1] + es[2])).astype(BF16)

    for grp in (2, 1, 0):
        r, seq, win, n_tiles, pad, offs = _dil_geometry(seq_full, grp)
        per_body = DIL_INFLIGHT // DIL_PAIRS
        if n_tiles == 1:
            for c in range(r):
                for pr in range(DIL_PAIRS):
                    pair_tile(grp, c, 0, pr, c * DIL_PAIRS + pr)
        elif r >= per_body:
            for c0 in range(0, r, per_body):
                def body(t, carry, grp=grp, c0=c0):
                    for j in range(per_body):
                        for pr in range(DIL_PAIRS):
                            pair_tile(grp, c0 + j, t, pr, j * DIL_PAIRS + pr)
                    return carry
                lax.fori_loop(0, n_tiles, body, 0)
        else:
            tiles_per_body = per_body // r
            body_rows = tiles_per_body * Q_TILE if r == 1 else 0
            assert (r == 1) == (grp == 0)
            def body(i, carry, grp=grp, r=r, tiles_per_body=tiles_per_body, body_rows=body_rows):
                for j in range(tiles_per_body):
                    for c in range(r):
                        for pr in range(DIL_PAIRS):
                            pair_tile(grp, c, i * tiles_per_body + j, pr, (j * r + c) * DIL_PAIRS + pr)
                for k in range(body_rows // ROW_TILE):
                    combine_rows(pl.ds(pl.multiple_of(i * body_rows + k * ROW_TILE, ROW_TILE), ROW_TILE))
                return carry
            lax.fori_loop(0, n_tiles // tiles_per_body, body, 0)


def _dilated_attention(a0, a1, a2, batch, seq_full):
    views, in_specs, bias_shapes = [], [], []
    for grp, a in enumerate((a0, a1, a2)):
        r, seq, win, n_tiles, pad, offs = _dil_geometry(seq_full, grp)
        views.append(a.reshape(batch, seq, r * GROUP_COLS))
        in_specs.append(pl.BlockSpec((None, seq, r * GROUP_COLS), lambda b: (b, 0, 0)))
        bias_shapes.append(pltpu.VMEM((len(offs), DIL_PAIRS, 2 * Q_TILE, win), F32))
    win0 = _dil_geometry(seq_full, 0)[2]
    win2 = _dil_geometry(seq_full, 2)[2]
    n_straight = DIL_PATTERNS[2][1] * DIL_PAIRS
    return pl.pallas_call(
        functools.partial(_dilated_kernel, seq_full),
        grid=(batch,),
        in_specs=in_specs,
        out_specs=pl.BlockSpec((None, seq_full, DIL_OUT), lambda b: (b, 0, 0)),
        out_shape=jax.ShapeDtypeStruct((batch, seq_full, DIL_OUT), BF16),
        scratch_shapes=[pltpu.VMEM((3, DIL_PAIRS, seq_full, LANES), F32),
                        pltpu.VMEM((3, DIL_PAIRS, seq_full, LANES), F32)]
        + bias_shapes + [pltpu.VMEM((2 * win0, LANES), BF16), pltpu.VMEM((2 * win2, LANES), BF16)]
        + [pltpu.VMEM((n, 2 * Q_TILE, w), F32) for n, w in ((DIL_INFLIGHT, win0), (n_straight, win2))]
        + [pltpu.VMEM((n, Q_TILE, 2 * w), BF16) for n, w in ((DIL_INFLIGHT, win0), (n_straight, win2))]
        + [pltpu.VMEM((max(DIL_INFLIGHT, n_straight), Q_TILE, LANES), F32)],
        compiler_params=pltpu.CompilerParams(
            dimension_semantics=("arbitrary",), vmem_limit_bytes=VMEM_LIMIT),
        name="dilated",
    )(*views)


WIN_WINDOW = Q_TILE + 2 * WIN_HALF


def _window_pair_heads():
    group = WIN_Q_HEADS // WIN_KV_HEADS
    pairs = []
    for u in range(WIN_UNITS):
        for i in range(WIN_UNIT_PAIRS):
            pairs.append((2 * u * group + i, (2 * u + 1) * group + i))
    return pairs


def _window_kernel(seq, q_hbm, kv_hbm, sink_ref, o_ref, bias_ref, ones_ref, sink_rows_ref, s_ref, p_ref,
                   aux_ref, q_buf, kv_buf, sems):
    n_tiles = seq // Q_TILE
    slopes = _alibi_slopes(WIN_Q_HEADS)
    pair_heads = _window_pair_heads()
    b = pl.program_id(0)
    last_seq = b + 1 == pl.num_programs(0)
    slot = b % 2

    def fetch(seq_idx, sl):
        return (pltpu.make_async_copy(q_hbm.at[seq_idx], q_buf.at[sl], sems.at[0, sl]),
                pltpu.make_async_copy(kv_hbm.at[seq_idx], kv_buf.at[sl], sems.at[1, sl]))

    @pl.when(b == 0)
    def _():
        for u in range(WIN_UNITS):
            heads = [h for pr in pair_heads[u * WIN_UNIT_PAIRS:(u + 1) * WIN_UNIT_PAIRS] for h in pr]
            for v, off in enumerate((0, -WIN_HALF, -2 * WIN_HALF)):
                bias_ref[v, u] = _band_bias(Q_TILE, WIN_WINDOW, off, WIN_HALF, [slopes[h] for h in heads])
            sink_rows_ref[u] = jnp.concatenate(
                [jnp.broadcast_to(sink_ref[0:1, h:h + 1] * LOG2E, (Q_TILE, LANES)) for h in heads], axis=0)
        ones_ref[...] = _ones_blockdiag(WIN_WINDOW)

    def rows_of(t):
        if isinstance(t, int):
            return t * Q_TILE, min(max(t * Q_TILE - WIN_HALF, 0), seq - WIN_WINDOW)
        q0 = pl.multiple_of(t * Q_TILE, Q_TILE)
        ks = pl.multiple_of(jnp.clip(q0 - WIN_HALF, 0, seq - WIN_WINDOW), Q_TILE)
        return q0, ks

    def unit_cols(u):
        return [(u * WIN_UNIT_PAIRS + i) * LANES for i in range(WIN_UNIT_PAIRS)]

    def score_stage(sl, t, k_set):
        q0, ks = rows_of(t)
        var = _tile_variant(t, n_tiles)
        for u in range(WIN_UNITS):
            qs = [q_buf[sl, pl.ds(q0, Q_TILE), c:c + LANES] for c in unit_cols(u)]
            k = kv_buf[sl, pl.ds(ks, WIN_WINDOW), u * LANES:(u + 1) * LANES]
            s_ref[k_set * WIN_UNITS + u] = _pair_scores(qs, k, bias_ref[var, u])

    def value_stage(sl, t, k_set):
        q0, ks = rows_of(t)
        for u in range(WIN_UNITS):
            work = k_set * WIN_UNITS + u
            v = kv_buf[sl, pl.ds(ks, WIN_WINDOW), WIN_KV + u * LANES:WIN_KV + (u + 1) * LANES]
            num, den, sink_term = _pair_values(s_ref.at[work], v, ones_ref[...], sink_rows_ref.at[u],
                                               p_ref.at[work], aux_ref.at[work])
            o = (num / (den + sink_term)).astype(BF16)
            for i, c in enumerate(unit_cols(u)):
                o_ref[pl.ds(q0, Q_TILE), c:c + LANES] = o[i * Q_TILE:(i + 1) * Q_TILE]

    per_body = WIN_TILES_PER_BODY

    @pl.when(b == 0)
    def _():
        for cp in fetch(0, 0):
            cp.start()
        for cp in fetch(0, 0):
            cp.wait()
        score_stage(0, 0, 0)

    @pl.when(jnp.logical_not(last_seq))
    def _():
        for cp in fetch(b + 1, 1 - slot):
            cp.start()

    def body(i, carry):
        t = per_body * i
        for j in range(per_body):
            score_stage(slot, t + j + 1, (j + 1) % per_body)
            value_stage(slot, t + j, j)
        return carry

    lax.fori_loop(0, n_tiles // per_body - 1, body, 0)

    @pl.when(jnp.logical_not(last_seq))
    def _():
        for cp in fetch(b + 1, 1 - slot):
            cp.wait()

    t = n_tiles - per_body
    for j in range(per_body):
        if j + 1 < per_body:
            score_stage(slot, t + j + 1, j + 1)
        else:
            score_stage(1 - slot, 0, 0)
        value_stage(slot, t + j, j)


def _window_attention(q, kv, sink, batch, seq):
    unit_rows = 2 * WIN_UNIT_PAIRS * Q_TILE
    return pl.pallas_call(
        functools.partial(_window_kernel, seq),
        grid=(batch,),
        in_specs=[pl.BlockSpec(memory_space=pl.ANY),
                  pl.BlockSpec(memory_space=pl.ANY),
                  pl.BlockSpec((1, WIN_Q_HEADS), lambda b: (0, 0))],
        out_specs=pl.BlockSpec((None, seq, WIN_Q), lambda b: (b, 0, 0)),
        out_shape=jax.ShapeDtypeStruct((batch, seq, WIN_Q), BF16),
        scratch_shapes=[pltpu.VMEM((3, WIN_UNITS, unit_rows, WIN_WINDOW), F32),
                        pltpu.VMEM((2 * WIN_WINDOW, LANES), BF16),
                        pltpu.VMEM((WIN_UNITS, unit_rows, LANES), F32),
                        pltpu.VMEM((WIN_TILES_PER_BODY * WIN_UNITS, unit_rows, WIN_WINDOW), F32),
                        pltpu.VMEM((WIN_TILES_PER_BODY * WIN_UNITS, unit_rows // 2, 2 * WIN_WINDOW), BF16),
                        pltpu.VMEM((WIN_TILES_PER_BODY * WIN_UNITS, unit_rows // 2, LANES), F32),
                        pltpu.VMEM((2, seq, WIN_Q), BF16),
                        pltpu.VMEM((2, seq, 2 * WIN_KV), BF16),
                        pltpu.SemaphoreType.DMA((2, 2))],
        compiler_params=pltpu.CompilerParams(
            dimension_semantics=("arbitrary",), vmem_limit_bytes=VMEM_LIMIT),
        name="window",
    )(q.reshape(batch, seq, WIN_Q), kv.reshape(batch, seq, 2 * WIN_KV), sink)


def _merge_kernel(oa_ref, ow_ref, h_ref, g_ref, wgate_ref, bgate_ref, wa_ref, wb_ref, wo_ref, out_ref):
    for r0 in range(0, h_ref.shape[0], MERGE_CHUNK):
        rows = slice(r0, r0 + MERGE_CHUNK)
        h = h_ref[rows, :]
        xb = _rms(h, g_ref[...]).astype(BF16)
        merged = None
        for br, (o_ref, w_ref) in enumerate(((oa_ref, wa_ref), (ow_ref, wb_ref))):
            cols = slice(br * D_MODEL, (br + 1) * D_MODEL)
            gate = jax.nn.sigmoid(_dot(xb, wgate_ref[:, cols]) + bgate_ref[:, cols])
            term = gate * _dot(o_ref[rows, :], w_ref[...])
            merged = term if merged is None else merged + term
        out_ref[rows, :] = h + _dot(merged.astype(BF16), wo_ref[...])


def _merge_out(oa, ow, h, g, wgate, bgate, wa, wb, wo, layer):
    m = h.shape[0]
    row = lambda c: pl.BlockSpec((MERGE_TILE, c), lambda i: (i, 0))
    return pl.pallas_call(
        _merge_kernel,
        grid=(m // MERGE_TILE,),
        in_specs=[row(DIL_OUT), row(WIN_Q), row(D_MODEL)]
        + [_layer_block(a, layer) for a in (g, wgate, bgate, wa, wb, wo)],
        out_specs=row(D_MODEL),
        out_shape=jax.ShapeDtypeStruct((m, D_MODEL), F32),
        compiler_params=pltpu.CompilerParams(
            dimension_semantics=("arbitrary",), vmem_limit_bytes=VMEM_LIMIT),
        name="merge_out",
    )(oa, ow, h, g, wgate, bgate, wa, wb, wo)


TOKEN_TILE = 256
CUM_CHUNK = TOKEN_TILE
SLOT_WINDOW = 64
SLOT_ALIGN = 16


def _prefix_exclusive(mask_f, tri):
    e, s = mask_f.shape
    carry = jnp.zeros((e, 1), F32)
    parts, carries = [], []
    for j in range(0, s, CUM_CHUNK):
        blk = mask_f[:, j:j + CUM_CHUNK]
        carries.append(carry)
        parts.append(_dot(blk.astype(BF16), tri) + carry)
        carry = carry + jnp.sum(blk, axis=-1, keepdims=True)
    return jnp.concatenate(parts, axis=-1), carries + [carry]


def _route_kernel(cap, h_ref, g_ref, whi_ref, wlo_ref, hn_ref, rank_ref, aff_ref, starts_ref):
    seq = h_ref.shape[0]
    hn = _rms(h_ref[...], g_ref[...])
    hn_hi = hn.astype(BF16)
    hn_lo = (hn - hn_hi.astype(F32)).astype(BF16)
    hn_ref[...] = hn_hi
    both = _dot_nt(jnp.concatenate([whi_ref[...], wlo_ref[...]], axis=0), hn_hi)
    logits = both[:N_EXPERTS] + both[N_EXPERTS:] + _dot_nt(whi_ref[...], hn_lo)
    mx = jnp.max(logits, axis=0, keepdims=True)
    ex = jnp.exp(logits - mx)
    aff = ex / jnp.sum(ex, axis=0, keepdims=True)
    bits = pltpu.bitcast(aff, jnp.int32)

    def enough(t):
        return jnp.sum(jnp.where(bits >= t, 1.0, 0.0), axis=-1, keepdims=True) >= float(cap)

    def search4(_, c):
        lo, hi = c
        q = (hi - lo) >> 2
        m1, m2, m3 = lo + q, lo + 2 * q, lo + 3 * q
        ok1, ok2, ok3 = enough(m1), enough(m2), enough(m3)
        return (jnp.where(ok3, m3, jnp.where(ok2, m2, jnp.where(ok1, m1, lo))),
                jnp.where(ok3, hi, jnp.where(ok2, m3, jnp.where(ok1, m2, m1))))

    def search2(_, c):
        lo, hi = c
        mid = lo + ((hi - lo) >> 1)
        ok = enough(mid)
        return jnp.where(ok, mid, lo), jnp.where(ok, hi, mid)

    lo0 = jnp.zeros((N_EXPERTS, 1), jnp.int32)
    hi0 = jnp.full((N_EXPERTS, 1), 0x3F800001, jnp.int32)
    thr, _ = lax.fori_loop(0, 3, search2, lax.fori_loop(0, 15, search4, (lo0, hi0)))

    r_i = lax.broadcasted_iota(jnp.int32, (CUM_CHUNK, CUM_CHUNK), 0)
    c_i = lax.broadcasted_iota(jnp.int32, (CUM_CHUNK, CUM_CHUNK), 1)
    tri = jnp.where(r_i < c_i, 1.0, 0.0).astype(BF16)
    gt = jnp.where(bits > thr, 1.0, 0.0)
    eq = jnp.where(bits == thr, 1.0, 0.0)
    need = float(cap) - jnp.sum(gt, axis=-1, keepdims=True)
    tie_rank, _ = _prefix_exclusive(eq, tri)
    sel = gt + eq * jnp.where(tie_rank < need, 1.0, 0.0)
    slot, starts = _prefix_exclusive(sel, tri)
    rank = jnp.where(sel > 0.0, slot, -1.0)
    for t in range(seq // TOKEN_TILE):
        rank_ref[t] = rank[:, t * TOKEN_TILE:(t + 1) * TOKEN_TILE]
        aff_ref[t] = aff[:, t * TOKEN_TILE:(t + 1) * TOKEN_TILE]
    lane = lax.broadcasted_iota(jnp.int32, (N_EXPERTS, LANES), 1)
    acc = jnp.zeros((N_EXPERTS, LANES), F32)
    for t, c in enumerate(starts):
        acc = acc + jnp.where(lane == t, c, 0.0)
    starts_ref[...] = acc


def _route(h, g, w_hi, w_lo, layer, batch, seq, cap):
    n_tiles = seq // TOKEN_TILE
    return pl.pallas_call(
        functools.partial(_route_kernel, cap),
        grid=(batch,),
        in_specs=[pl.BlockSpec((None, seq, D_MODEL), lambda b: (b, 0, 0))]
        + [_layer_block(a, layer) for a in (g, w_hi, w_lo)],
        out_specs=[pl.BlockSpec((None, seq, D_MODEL), lambda b: (b, 0, 0)),
                   pl.BlockSpec((None, n_tiles, N_EXPERTS, TOKEN_TILE), lambda b: (b, 0, 0, 0)),
                   pl.BlockSpec((None, n_tiles, N_EXPERTS, TOKEN_TILE), lambda b: (b, 0, 0, 0)),
                   pl.BlockSpec((None, N_EXPERTS, LANES), lambda b: (b, 0, 0))],
        out_shape=[jax.ShapeDtypeStruct((batch, seq, D_MODEL), BF16),
                   jax.ShapeDtypeStruct((batch, n_tiles, N_EXPERTS, TOKEN_TILE), F32),
                   jax.ShapeDtypeStruct((batch, n_tiles, N_EXPERTS, TOKEN_TILE), F32),
                   jax.ShapeDtypeStruct((batch, N_EXPERTS, LANES), F32)],
        compiler_params=pltpu.CompilerParams(
            dimension_semantics=("arbitrary",), vmem_limit_bytes=VMEM_LIMIT),
        name="route",
    )(h.reshape(batch, seq, D_MODEL), g, w_hi, w_lo)


def _slot_rows(cap):
    return cap + SLOT_WINDOW


def _window_starts(win_ref, b, t, p, cap):
    base = b * (N_EXPERTS * LANES) + t
    return [pl.multiple_of(jnp.minimum(win_ref[base + e * LANES] + p * SLOT_WINDOW, cap), SLOT_ALIGN)
            for e in range(N_EXPERTS)]


def _window_hits(rank_tile, starts):
    rows = lax.broadcasted_iota(jnp.int32, (SLOT_WINDOW, TOKEN_TILE), 0).astype(F32)
    return [(rank_tile[e:e + 1, :] - starts[e].astype(F32)) == rows for e in range(N_EXPERTS)]


def _one_hot(hits):
    return jnp.concatenate([jnp.where(h, 1.0, 0.0).astype(BF16) for h in hits], axis=0)


def _gather_kernel(cap, win_ref, npass_ref, hn_ref, rank_ref, aff_ref, xe_ref, gate_ref):
    b = pl.program_id(0)
    xe_ref[...] = jnp.zeros_like(xe_ref)
    gate_ref[...] = jnp.zeros_like(gate_ref)
    n_tiles = hn_ref.shape[0] // TOKEN_TILE

    def one_pass(p, t):
        starts = _window_starts(win_ref, b, t, p, cap)
        hits = _window_hits(rank_ref[t], starts)
        rows = _dot(_one_hot(hits), hn_ref[t * TOKEN_TILE:(t + 1) * TOKEN_TILE, :]).astype(BF16)
        aff_tile = aff_ref[t]
        for e in range(N_EXPERTS):
            win = pl.ds(starts[e], SLOT_WINDOW)
            xe_ref[e, win, :] += rows[e * SLOT_WINDOW:(e + 1) * SLOT_WINDOW]
            g = jnp.sum(jnp.where(hits[e], aff_tile[e:e + 1, :], 0.0), axis=-1, keepdims=True)
            gate_ref[e, win, :] += jnp.broadcast_to(g, (SLOT_WINDOW, LANES))

    for t in range(n_tiles):
        lax.fori_loop(1, npass_ref[b * LANES + t], lambda p, c, t=t: (one_pass(p, t), c)[1], 0)
    for t in range(n_tiles):
        one_pass(0, t)


def _gather(hn, rank, aff, win, npass, cap):
    batch, seq, _ = hn.shape
    n_tiles = seq // TOKEN_TILE
    grid_spec = pltpu.PrefetchScalarGridSpec(
        num_scalar_prefetch=2,
        grid=(batch,),
        in_specs=[pl.BlockSpec((None, seq, D_MODEL), lambda b, *_: (b, 0, 0)),
                  pl.BlockSpec((None, n_tiles, N_EXPERTS, TOKEN_TILE), lambda b, *_: (b, 0, 0, 0)),
                  pl.BlockSpec((None, n_tiles, N_EXPERTS, TOKEN_TILE), lambda b, *_: (b, 0, 0, 0))],
        out_specs=[pl.BlockSpec((None, N_EXPERTS, _slot_rows(cap), D_MODEL), lambda b, *_: (b, 0, 0, 0)),
                   pl.BlockSpec((None, N_EXPERTS, _slot_rows(cap), LANES), lambda b, *_: (b, 0, 0, 0))])
    return pl.pallas_call(
        functools.partial(_gather_kernel, cap),
        grid_spec=grid_spec,
        out_shape=[jax.ShapeDtypeStruct((batch, N_EXPERTS, _slot_rows(cap), D_MODEL), BF16),
                   jax.ShapeDtypeStruct((batch, N_EXPERTS, _slot_rows(cap), LANES), F32)],
        compiler_params=pltpu.CompilerParams(
            dimension_semantics=("arbitrary",), vmem_limit_bytes=VMEM_LIMIT),
        name="gather",
    )(win, npass, hn, rank, aff)


FFN_SEQS = 4


def _ffn_kernel(cap, xe_ref, gate_ref, wg_ref, wu_ref, wd_ref, y_ref, wg_b, wu_b, wd_b):
    d = xe_ref.shape[-1]

    @pl.when(pl.program_id(1) == 0)
    def _():
        for src, dst in ((wg_ref, wg_b), (wu_ref, wu_b), (wd_ref, wd_b)):
            dst[...] = src[...].astype(BF16)

    xe = xe_ref[:, :cap, :].reshape(FFN_SEQS * cap, d)
    a = _dot(xe, wg_b[...])
    u = _dot(xe, wu_b[...])
    y = _dot((jax.nn.silu(a) * u).astype(BF16), wd_b[...])
    gate = gate_ref[:, :cap, :].reshape(FFN_SEQS * cap, LANES)
    y = y * jnp.concatenate([gate] * (d // LANES), axis=1)
    y_ref[:, :cap, :] = y.astype(BF16).reshape(FFN_SEQS, cap, d)
    y_ref[:, cap:, :] = jnp.zeros((FFN_SEQS, SLOT_WINDOW, d), BF16)


def _ffn(xe, gate, wg, wu, wd, layer, cap):
    batch = xe.shape[0]
    d_exp = wg.shape[-1]
    slots = pl.BlockSpec((FFN_SEQS, None, _slot_rows(cap), D_MODEL), lambda e, i: (i, e, 0, 0))
    return pl.pallas_call(
        functools.partial(_ffn_kernel, cap),
        grid=(N_EXPERTS, batch // FFN_SEQS),
        in_specs=[slots,
                  pl.BlockSpec((FFN_SEQS, None, _slot_rows(cap), LANES), lambda e, i: (i, e, 0, 0)),
                  pl.BlockSpec((None, None, D_MODEL, d_exp), lambda e, i: (layer, e, 0, 0)),
                  pl.BlockSpec((None, None, D_MODEL, d_exp), lambda e, i: (layer, e, 0, 0)),
                  pl.BlockSpec((None, None, d_exp, D_MODEL), lambda e, i: (layer, e, 0, 0))],
        out_specs=slots,
        out_shape=jax.ShapeDtypeStruct(xe.shape, BF16),
        scratch_shapes=[pltpu.VMEM((D_MODEL, d_exp), BF16), pltpu.VMEM((D_MODEL, d_exp), BF16),
                        pltpu.VMEM((d_exp, D_MODEL), BF16)],
        compiler_params=pltpu.CompilerParams(
            dimension_semantics=("arbitrary", "arbitrary"), vmem_limit_bytes=VMEM_LIMIT),
        name="ffn",
    )(xe, gate, wg, wu, wd)


TILES_PER_ROW_TILE = ROW_TILE // TOKEN_TILE


def _scatter_kernel(cap, project, win_ref, npass_ref, y_hbm, rank_ref, h_ref, g_ref, *rest):
    if project:
        w_ref, out_ref, a0_ref, a1_ref, a2_ref, qw_ref, kv_ref, xs_ref, y_buf, y_sem = rest
    else:
        out_ref, y_buf, y_sem = rest
    b = pl.program_id(0)
    j = pl.program_id(1)
    slot = b % 2

    def y_copy(seq_idx, buf_slot):
        return pltpu.make_async_copy(y_hbm.at[seq_idx], y_buf.at[buf_slot], y_sem.at[buf_slot])

    @pl.when((b == 0) & (j == 0))
    def _():
        y_copy(0, 0).start()

    @pl.when(j == 0)
    def _():
        y_copy(b, slot).wait()

    @pl.when((j == 0) & (b + 1 < pl.num_programs(0)))
    def _():
        y_copy(b + 1, 1 - slot).start()

    y_ref = y_buf.at[slot]
    out_ref[...] = h_ref[...]

    def one_pass(p, tt):
        t = j * TILES_PER_ROW_TILE + tt
        rows = slice(tt * TOKEN_TILE, (tt + 1) * TOKEN_TILE)
        starts = _window_starts(win_ref, b, t, p, cap)
        put = _one_hot(_window_hits(rank_ref[tt], starts))
        yw = jnp.concatenate([y_ref[e, pl.ds(starts[e], SLOT_WINDOW), :] for e in range(N_EXPERTS)], axis=0)
        out_ref[rows, :] += lax.dot_general(put, yw, (((0,), (0,)), ((), ())), preferred_element_type=F32)

    for tt in range(TILES_PER_ROW_TILE):
        n_pass = npass_ref[b * LANES + j * TILES_PER_ROW_TILE + tt]
        lax.fori_loop(1, n_pass, lambda p, c, tt=tt: (one_pass(p, tt), c)[1], 0)
    for tt in range(TILES_PER_ROW_TILE):
        one_pass(0, tt)
    if project:
        _project(out_ref[...], g_ref, w_ref, a0_ref, a1_ref, a2_ref, qw_ref, kv_ref, xs_ref)
    else:
        out_ref[...] = _rms(out_ref[...], g_ref[...])


def _scatter(y, rank, h, win, npass, cap, g, layer, w_next=None):
    batch, n_tiles = rank.shape[:2]
    seq = h.shape[1]
    steps = n_tiles // TILES_PER_ROW_TILE
    project = w_next is not None
    in_specs = [pl.BlockSpec(memory_space=pl.ANY),
                pl.BlockSpec((None, TILES_PER_ROW_TILE, N_EXPERTS, TOKEN_TILE), lambda b, j, *_: (b, j, 0, 0)),
                pl.BlockSpec((None, ROW_TILE, D_MODEL), lambda b, j, *_: (b, j, 0)),
                _layer_block(g, layer)]
    out_shape = [jax.ShapeDtypeStruct(h.shape, F32)]
    out_specs = [pl.BlockSpec((None, ROW_TILE, D_MODEL), lambda b, j, *_: (b, j, 0))]
    args = [win, npass, y, rank, h, g]
    scratch = []
    if project:
        in_specs.append(_layer_block(w_next, layer))
        args.append(w_next)
        proj_shape, proj_specs = _proj_outputs(batch * seq, lambda b, j, *_: (b * steps + j, 0))
        out_shape += proj_shape
        out_specs += proj_specs
        scratch = [pltpu.VMEM((D_MODEL // LANES, ROW_TILE, LANES), F32)]
    scratch += [pltpu.VMEM((2,) + y.shape[1:], BF16), pltpu.SemaphoreType.DMA((2,))]
    grid_spec = pltpu.PrefetchScalarGridSpec(
        num_scalar_prefetch=2, grid=(batch, steps), in_specs=in_specs, out_specs=out_specs,
        scratch_shapes=scratch)
    return pl.pallas_call(
        functools.partial(_scatter_kernel, cap, project),
        grid_spec=grid_spec,
        out_shape=out_shape,
        compiler_params=pltpu.CompilerParams(
            dimension_semantics=("arbitrary", "arbitrary"), vmem_limit_bytes=VMEM_LIMIT),
        name="scatter_proj" if project else "scatter_final",
    )(*args)


def _slot_windows(starts, n_tiles, cap):
    s = starts.astype(jnp.int32)
    first = (s // SLOT_ALIGN) * SLOT_ALIGN
    span = s[:, :, 1:n_tiles + 1] - first[:, :, :n_tiles]
    npass = jnp.maximum(jnp.max(-(-span // SLOT_WINDOW), axis=1), 1)
    npass = jnp.pad(npass, ((0, 0), (0, LANES - n_tiles)))
    return first.reshape(-1), npass.reshape(-1)


def _arrange_w_in(w):
    scale = LOG2E * HEAD_DIM ** -0.5
    qa, ka, va = w[..., :DIL_QKV], w[..., DIL_QKV:2 * DIL_QKV], w[..., 2 * DIL_QKV:3 * DIL_QKV]
    rest = w[..., 3 * DIL_QKV:]
    parts = []
    for g in range(len(DIL_PATTERNS)):
        sl = slice(g * DIL_OUT, (g + 1) * DIL_OUT)
        parts += [qa[..., sl] * scale, ka[..., sl], va[..., sl]]
    for pr in _window_pair_heads():
        parts += [rest[..., h * HEAD_DIM:(h + 1) * HEAD_DIM] * scale for h in pr]
    parts.append(rest[..., WIN_Q:WIN_Q + 2 * WIN_KV])
    return jnp.concatenate(parts, axis=-1).astype(BF16), rest[..., WIN_Q + 2 * WIN_KV:].astype(BF16)


def _arrange_w_branch_b(w):
    return jnp.concatenate([w[:, h * HEAD_DIM:(h + 1) * HEAD_DIM] for pr in _window_pair_heads() for h in pr],
                           axis=1).astype(BF16)


def kernel(x, norm_mix, w_in, w_branch_a, w_branch_b, b_gate, sink_logit, w_out, norm_ffn, w_router,
           w_expert_gate, w_expert_up, w_expert_down, norm_final):
    batch, seq, d = x.shape
    depth = w_in.shape[0]
    cap = CAPACITY_FACTOR * seq // N_EXPERTS
    m = batch * seq
    w_attn, w_gates = _arrange_w_in(w_in)
    wa, wb, wo = w_branch_a.astype(BF16), _arrange_w_branch_b(w_branch_b), w_out.astype(BF16)
    g_mix, g_ffn, bg = norm_mix[:, None, :], norm_ffn[:, None, :], b_gate[:, None, :]
    wr = jnp.swapaxes(w_router, 1, 2)
    wr_hi = wr.astype(BF16)
    wr_lo = (wr - wr_hi.astype(F32)).astype(BF16)
    h = x.reshape(m, d)
    proj = _norm_proj(h, g_mix, w_attn, 0)
    for l in range(depth):
        a0, a1, a2, qw, kv = proj
        oa = _dilated_attention(a0, a1, a2, batch, seq).reshape(m, DIL_OUT)
        ow = _window_attention(qw, kv, sink_logit[l][None, :], batch, seq).reshape(m, WIN_Q)
        h = _merge_out(oa, ow, h, g_mix, w_gates, bg, wa, wb, wo, l)
        hn, rank, aff, starts = _route(h, g_ffn, wr_hi, wr_lo, l, batch, seq, cap)
        win, npass = _slot_windows(starts, seq // TOKEN_TILE, cap)
        xe, gate = _gather(hn, rank, aff, win, npass, cap)
        y = _ffn(xe, gate, w_expert_gate, w_expert_up, w_expert_down, l, cap)
        h3 = h.reshape(batch, seq, d)
        if l + 1 == depth:
            return _scatter(y, rank, h3, win, npass, cap, norm_final[None, None, :], 0)[0]
        h3, *proj = _scatter(y, rank, h3, win, npass, cap, g_mix, l + 1, w_attn)
        h = h3.reshape(m, d)
```
